```python
import math
import jax, jax.numpy as jnp
from jax import lax
import numpy as np

D_MODEL = 1024
BATCH = 8
SEQ = 4096
DEPTH = 4

CTX_LEN = 256
GRID_W = 64
HEAD_DIM = 64
RWKV_HEADS = D_MODEL // 128
RWKV_WIDTH = RWKV_HEADS * HEAD_DIM
RWKV_W_RANK = 64
RWKV_A_RANK = 64
RWKV_G_RANK = 128
ATTN_Q_HEADS = D_MODEL // HEAD_DIM
ATTN_KV_HEADS = ATTN_Q_HEADS // 4
GQA_GROUP = ATTN_Q_HEADS // ATTN_KV_HEADS
ATTN_Q_WIDTH = ATTN_Q_HEADS * HEAD_DIM
ATTN_KV_WIDTH = ATTN_KV_HEADS * HEAD_DIM
ATTN_SCALE = HEAD_DIM ** -0.5
Q_BLOCK = 128
ROPE_THETA = 10000.0
ROPE_PAIRS = HEAD_DIM // 4
SSM_WIDTH = D_MODEL // 2
SSM_GROUP = 16
SSM_GROUPS = SSM_WIDTH // SSM_GROUP
SSM_STATE = 64
D_FF = 4 * D_MODEL
N_BRANCHES = 3
RWKV_COLS = 3 * RWKV_WIDTH + RWKV_W_RANK + RWKV_A_RANK + RWKV_G_RANK
ATTN_COLS = ATTN_Q_WIDTH + 2 * ATTN_KV_WIDTH
SSM_COLS = SSM_WIDTH
GATE_COLS = N_BRANCHES * D_MODEL
IN_COLS = RWKV_COLS + ATTN_COLS + SSM_COLS + GATE_COLS
COL_SPLITS = [RWKV_COLS, RWKV_COLS + ATTN_COLS, RWKV_COLS + ATTN_COLS + SSM_COLS]
RWKV_SPLITS = [RWKV_WIDTH, 2 * RWKV_WIDTH, 3 * RWKV_WIDTH, 3 * RWKV_WIDTH + RWKV_W_RANK, 3 * RWKV_WIDTH + RWKV_W_RANK + RWKV_A_RANK]
ALPHA = (2 * DEPTH) ** 0.25
BETA = (8 * DEPTH) ** -0.25
LN_EPS = 1e-5
RMS_EPS = 1e-6
GN_EPS = 64e-5

kernel_name = 'hybrid_rwkv7_gqa_s5_dit_block'


def layer_norm(x, g, b):
    xf = x.astype(jnp.float32)
    mu = xf.mean(-1, keepdims=True)
    var = jnp.square(xf - mu).mean(-1, keepdims=True)
    return ((xf - mu) * lax.rsqrt(var + LN_EPS) * g + b).astype(x.dtype)


def rms_norm(x, g):
    xf = x.astype(jnp.float32)
    return (xf * lax.rsqrt(jnp.square(xf).mean(-1, keepdims=True) + RMS_EPS) * g).astype(x.dtype)


def centred_shift(p):
    zero = jnp.zeros_like(p[:, :1])
    prev = jnp.concatenate([zero, p[:, :-1]], axis=1)
    nxt = jnp.concatenate([p[:, 1:], zero], axis=1)
    return 0.5 * (prev + nxt)


def bidir_shared(t):
    return jnp.stack([t, jnp.flip(t, axis=1)])


def bidir_split(t):
    return jnp.stack([t[0], jnp.flip(t[1], axis=1)])


def axial_rope_tables(n_tokens):
    rows = n_tokens // GRID_W
    row = jnp.repeat(jnp.arange(rows, dtype=jnp.float32), GRID_W)
    col = jnp.tile(jnp.arange(GRID_W, dtype=jnp.float32), rows)
    inv = ROPE_THETA ** (-jnp.arange(ROPE_PAIRS, dtype=jnp.float32) / ROPE_PAIRS)
    ang = jnp.stack([row, col], axis=-1)[:, :, None] * inv
    ang = jnp.broadcast_to(ang[:, :, None, :], (n_tokens, 2, 2, ROPE_PAIRS)).reshape(n_tokens, HEAD_DIM)
    return jnp.cos(ang), jnp.sin(ang)


def apply_axial_rope(x, cos, sin):
    xf = x.astype(jnp.float32)
    xr = xf.reshape(*x.shape[:-1], 2, 2, ROPE_PAIRS)
    rot = jnp.stack([-xr[..., 1, :], xr[..., 0, :]], axis=-2).reshape(x.shape)
    return (xf * cos[None, :, None, :] + rot * sin[None, :, None, :]).astype(x.dtype)


def rwkv7_scan(state0, r, decay, k, v, kk, a):
    def step(state, inp):
        r_t, w_t, k_t, v_t, kk_t, a_t = inp
        sa = jnp.einsum('dbhvk,dbhk->dbhv', state, -kk_t)
        state = (state * w_t[..., None, :] + sa[..., :, None] * (kk_t * a_t)[..., None, :]
                 + v_t[..., :, None] * k_t[..., None, :])
        return state, jnp.einsum('dbhvk,dbhk->dbhv', state, r_t)
    xs = tuple(jnp.moveaxis(t.astype(jnp.float32), 2, 0) for t in (r, decay, k, v, kk, a))
    state, ys = lax.scan(step, state0, xs)
    return state, jnp.moveaxis(ys, 0, 2)


def rwkv7_branch(p_lat, p_ctx, mu, w0, w_up, a0, a_up, g_up, k_k, k_a, r_k, gn_w, gn_b, need_ctx_out):
    r_k_h = r_k.reshape(RWKV_HEADS, HEAD_DIM)

    def heads(t):
        return t.reshape(*t.shape[:-1], RWKV_HEADS, HEAD_DIM)

    def prepare(p):
        p = p + mu * (centred_shift(p) - p)
        r, k, v, wd, ad, gd = jnp.split(p, RWKV_SPLITS, axis=-1)
        g = jax.nn.sigmoid(gd) @ g_up
        w_pre = w0[:, None, None, :] + jnp.einsum('btr,drc->dbtc', jnp.tanh(wd), w_up)
        decay = jnp.exp(-jnp.exp(-jax.nn.softplus(-w_pre.astype(jnp.float32)) - 0.5))
        a = jax.nn.sigmoid(a0[:, None, None, :] + jnp.einsum('btr,drc->dbtc', ad, a_up))
        kk = heads(k * k_k).astype(jnp.float32)
        kk = kk * lax.rsqrt(jnp.maximum(jnp.sum(jnp.square(kk), axis=-1, keepdims=True), 1e-24))
        k_dir = k[None] * (1.0 + (a - 1.0) * k_a)
        return heads(r), heads(k_dir), heads(v), kk, heads(decay), heads(a), g

    def readout(ys, r, k_dir, v, g):
        n_b, n_t = g.shape[:2]
        y = ys[0] + jnp.flip(ys[1], axis=1)
        mean = y.mean(-1, keepdims=True)
        var = jnp.square(y - mean).mean(-1, keepdims=True)
        y = ((y - mean) * lax.rsqrt(var + GN_EPS)).reshape(n_b, n_t, RWKV_WIDTH) * gn_w + gn_b
        bonus = jnp.sum(r[None] * k_dir * r_k_h, axis=(0, -1))[..., None] * v
        return ((y + bonus.reshape(n_b, n_t, RWKV_WIDTH)) * g).astype(g.dtype)

    rc, kc, vc, kkc, dc, ac, gc = prepare(p_ctx)
    state0 = jnp.zeros((2, p_ctx.shape[0], RWKV_HEADS, HEAD_DIM, HEAD_DIM), jnp.float32)
    state_ctx, ys_c = rwkv7_scan(state0, bidir_shared(rc), bidir_split(dc), bidir_split(kc),
                                 bidir_shared(vc), bidir_shared(kkc), bidir_split(ac))
    rl, kl, vl, kkl, dl, al, gl = prepare(p_lat)
    _, ys_l = rwkv7_scan(state_ctx, bidir_shared(rl), bidir_split(dl), bidir_split(kl),
                         bidir_shared(vl), bidir_shared(kkl), bidir_split(al))
    y_lat = readout(ys_l, rl, kl, vl, gl)
    y_ctx = readout(ys_c, rc, kc, vc, gc) if need_ctx_out else None
    return y_lat, y_ctx


def grouped_attend(q, k, v):
    s = jnp.einsum('bqhgd,bkhd->bhgqk', q, k).astype(jnp.float32) * ATTN_SCALE
    p = jax.nn.softmax(s, axis=-1).astype(v.dtype)
    return jnp.einsum('bhgqk,bkhd->bqhgd', p, v)


def gqa_axial_branch(p_lat, p_ctx, q_gain, k_gain, cos, sin, need_ctx_out):
    def qkv(p):
        n_b, n_t = p.shape[:2]
        q, k, v = jnp.split(p, [ATTN_Q_WIDTH, ATTN_Q_WIDTH + ATTN_KV_WIDTH], axis=-1)
        q = rms_norm(q.reshape(n_b, n_t, ATTN_Q_HEADS, HEAD_DIM), q_gain)
        k = rms_norm(k.reshape(n_b, n_t, ATTN_KV_HEADS, HEAD_DIM), k_gain)
        return q, k, v.reshape(n_b, n_t, ATTN_KV_HEADS, HEAD_DIM)

    q_c, k_c, v_c = qkv(p_ctx)
    q_l, k_l, v_l = qkv(p_lat)
    q_l = apply_axial_rope(q_l, cos, sin)
    k_l = apply_axial_rope(k_l, cos, sin)
    k_all = jnp.concatenate([k_c, k_l], axis=1)
    v_all = jnp.concatenate([v_c, v_l], axis=1)
    n_b, n_t = q_l.shape[:2]
    q_blocks = q_l.reshape(n_b, n_t // Q_BLOCK, Q_BLOCK, ATTN_KV_HEADS, GQA_GROUP, HEAD_DIM).swapaxes(0, 1)
    o = lax.map(lambda qb: grouped_attend(qb, k_all, v_all), q_blocks)
    y_lat = o.swapaxes(0, 1).reshape(n_b, n_t, ATTN_Q_WIDTH)
    y_ctx = None
    if need_ctx_out:
        n_c = q_c.shape[1]
        y_ctx = grouped_attend(q_c.reshape(n_b, n_c, ATTN_KV_HEADS, GQA_GROUP, HEAD_DIM), k_c, v_c).reshape(n_b, n_c, ATTN_Q_WIDTH)
    return y_lat, y_ctx


def complex_affine_combine(earlier, later):
    a1r, a1i, b1r, b1i = earlier
    a2r, a2i, b2r, b2i = later
    return (a2r * a1r - a2i * a1i, a2r * a1i + a2i * a1r,
            a2r * b1r - a2i * b1i + b2r, a2r * b1i + a2i * b1r + b2i)


def zoh_discretise(a_re, a_im, log_dt):
    lam_re = jnp.minimum(a_re.astype(jnp.float32), -1e-4)
    lam_im = a_im.astype(jnp.float32)
    dt = jnp.exp(log_dt.astype(jnp.float32))[:, None]
    mag = jnp.exp(lam_re * dt)
    abar_re, abar_im = mag * jnp.cos(lam_im * dt), mag * jnp.sin(lam_im * dt)
    nr, ni = abar_re - 1.0, abar_im
    den = jnp.square(lam_re) + jnp.square(lam_im)
    coef_re = (nr * lam_re + ni * lam_im) / den
    coef_im = (ni * lam_re - nr * lam_im) / den
    return abar_re, abar_im, coef_re, coef_im


def s5_scan(bu_re, bu_im, disc, h0, reverse):
    abar_re, abar_im, coef_re, coef_im = disc
    b_re = coef_re * bu_re - coef_im * bu_im
    b_im = coef_re * bu_im + coef_im * bu_re
    if reverse:
        b_re, b_im = jnp.flip(b_re, axis=1), jnp.flip(b_im, axis=1)
    if h0 is not None:
        h_re, h_im = h0
        b_re = b_re.at[:, 0].add(abar_re * h_re - abar_im * h_im)
        b_im = b_im.at[:, 0].add(abar_re * h_im + abar_im * h_re)
    n_t = b_re.shape[1]
    a_r = jnp.broadcast_to(abar_re, (1, n_t, SSM_GROUPS, SSM_STATE))
    a_i = jnp.broadcast_to(abar_im, (1, n_t, SSM_GROUPS, SSM_STATE))
    _, _, x_re, x_im = lax.associative_scan(complex_affine_combine, (a_r, a_i, b_re, b_im), axis=1)
    final = (x_re[:, -1], x_im[:, -1])
    if reverse:
        x_re, x_im = jnp.flip(x_re, axis=1), jnp.flip(x_im, axis=1)
    return x_re, x_im, final


def s5_branch(u_lat, u_ctx, a_re, a_im, log_dt, b_re, b_im, c_re, c_im, d_skip, glu_w, glu_b, need_ctx_out):
    def drive(u):
        ug = u.reshape(u.shape[0], u.shape[1], SSM_GROUPS, SSM_GROUP).astype(jnp.float32)
        return (jnp.einsum('btgi,gni->btgn', ug, b_re.astype(jnp.float32)),
                jnp.einsum('btgi,gni->btgn', ug, b_im.astype(jnp.float32)))

    def readout(x_re, x_im, u):
        y = (jnp.einsum('btgn,gin->btgi', x_re, c_re.astype(jnp.float32))
             - jnp.einsum('btgn,gin->btgi', x_im, c_im.astype(jnp.float32)))
        y = y.reshape(u.shape).astype(u.dtype) + d_skip * u
        y = jax.nn.gelu(y)
        return y * jax.nn.sigmoid(y @ glu_w + glu_b)

    bu_c = drive(u_ctx)
    bu_l = drive(u_lat)
    lat_states, ctx_states = [], []
    for d in range(2):
        disc = zoh_discretise(a_re[d], a_im[d], log_dt[d])
        xc_re, xc_im, h_ctx = s5_scan(bu_c[0], bu_c[1], disc, None, d == 1)
        xl_re, xl_im, _ = s5_scan(bu_l[0], bu_l[1], disc, h_ctx, d == 1)
        lat_states.append((xl_re, xl_im))
        ctx_states.append((xc_re, xc_im))
    y_lat = readout(lat_states[0][0] + lat_states[1][0], lat_states[0][1] + lat_states[1][1], u_lat)
    y_ctx = None
    if need_ctx_out:
        y_ctx = readout(ctx_states[0][0] + ctx_states[1][0], ctx_states[0][1] + ctx_states[1][1], u_ctx)
    return y_lat, y_ctx


def gated_merge(ya, yb, yc, pg, proj_a, proj_b, proj_c, w_out):
    ga, gb, gc = jnp.split(jax.nn.sigmoid(pg), N_BRANCHES, axis=-1)
    merged = ga * (ya @ proj_a) + gb * (yb @ proj_b) + gc * (yc @ proj_c)
    return merged @ w_out


def squared_relu_mlp(h, w1, w2):
    return jnp.square(jax.nn.relu(h @ w1)) @ w2


def setup_inputs(seed: int = 0) -> dict:
    key = jax.random.key(seed)
    ks = iter(jax.random.split(key, 48))

    def nrm(shape, scale):
        return scale * jax.random.normal(next(ks), shape, jnp.float32)

    def unif(shape, lo, hi):
        return jax.random.uniform(next(ks), shape, jnp.float32, minval=lo, maxval=hi)

    L, D = DEPTH, D_MODEL
    n_idx = jnp.arange(SSM_STATE, dtype=jnp.float32)
    return {
        'x': nrm((BATCH, SEQ, D), 1.0),
        'c': nrm((BATCH, D), 1.0),
        'ctx': nrm((BATCH, CTX_LEN, D), 1.0),
        'c_ctx': nrm((D,), 1.0),
        'mod_w': nrm((L, D, 6 * D), 0.5 * D ** -0.5),
        'mod_b': nrm((L, 6 * D), 0.02),
        'w_in': nrm((L, D, IN_COLS), D ** -0.5),
        'rwkv_mu': unif((L, RWKV_COLS), 0.0, 1.0),
        'rwkv_w0': unif((L, 2, RWKV_WIDTH), -6.0, -1.0),
        'rwkv_w_up': nrm((L, 2, RWKV_W_RANK, RWKV_WIDTH), 0.5 * RWKV_W_RANK ** -0.5),
        'rwkv_a0': nrm((L, 2, RWKV_WIDTH), 0.1),
        'rwkv_a_up': nrm((L, 2, RWKV_A_RANK, RWKV_WIDTH), RWKV_A_RANK ** -0.5),
        'rwkv_g_up': nrm((L, RWKV_G_RANK, RWKV_WIDTH), RWKV_G_RANK ** -0.5),
        'rwkv_k_k': 0.85 + nrm((L, RWKV_WIDTH), 0.02),
        'rwkv_k_a': 1.0 + nrm((L, RWKV_WIDTH), 0.02),
        'rwkv_r_k': nrm((L, RWKV_WIDTH), 0.1),
        'rwkv_gn_w': 1.0 + nrm((L, RWKV_WIDTH), 0.02),
        'rwkv_gn_b': nrm((L, RWKV_WIDTH), 0.02),
        'attn_q_gain': 1.0 + nrm((L, HEAD_DIM), 0.02),
        'attn_k_gain': 1.0 + nrm((L, HEAD_DIM), 0.02),
        'ssm_a_re': -0.5 + nrm((L, 2, SSM_GROUPS, SSM_STATE), 0.01),
        'ssm_a_im': jnp.pi * n_idx + nrm((L, 2, SSM_GROUPS, SSM_STATE), 0.01),
        'ssm_log_dt': unif((L, 2, SSM_GROUPS), math.log(1e-3), math.log(1e-1)),
        'ssm_b_re': nrm((L, SSM_GROUPS, SSM_STATE, SSM_GROUP), (2 * SSM_GROUP) ** -0.5),
        'ssm_b_im': nrm((L, SSM_GROUPS, SSM_STATE, SSM_GROUP), (2 * SSM_GROUP) ** -0.5),
        'ssm_c_re': nrm((L, SSM_GROUPS, SSM_GROUP, SSM_STATE), 0.5),
        'ssm_c_im': nrm((L, SSM_GROUPS, SSM_GROUP, SSM_STATE), 0.5),
        'ssm_d': nrm((L, SSM_WIDTH), 0.5),
        'ssm_glu_w': nrm((L, SSM_WIDTH, SSM_WIDTH), SSM_WIDTH ** -0.5),
        'ssm_glu_b': nrm((L, SSM_WIDTH), 0.02),
        'proj_a': nrm((L, RWKV_WIDTH, D), RWKV_WIDTH ** -0.5),
        'proj_b': nrm((L, ATTN_Q_WIDTH, D), ATTN_Q_WIDTH ** -0.5),
        'proj_c': nrm((L, SSM_WIDTH, D), SSM_WIDTH ** -0.5),
        'w_out': nrm((L, D, D), BETA * D ** -0.5),
        'ln1_g': 1.0 + nrm((L, D), 0.02),
        'ln1_b': nrm((L, D), 0.02),
        'ln2_g': 1.0 + nrm((L, D), 0.02),
        'ln2_b': nrm((L, D), 0.02),
        'mlp_w1': nrm((L, D, D_FF), D ** -0.5),
        'mlp_w2': nrm((L, D_FF, D), BETA * D_FF ** -0.5),
    }


def reference(x, c, ctx, c_ctx, mod_w, mod_b, w_in, rwkv_mu, rwkv_w0, rwkv_w_up, rwkv_a0, rwkv_a_up, rwkv_g_up,
              rwkv_k_k, rwkv_k_a, rwkv_r_k, rwkv_gn_w, rwkv_gn_b, attn_q_gain, attn_k_gain, ssm_a_re, ssm_a_im,
              ssm_log_dt, ssm_b_re, ssm_b_im, ssm_c_re, ssm_c_im, ssm_d, ssm_glu_w, ssm_glu_b, proj_a, proj_b, proj_c,
              w_out, ln1_g, ln1_b, ln2_g, ln2_b, mlp_w1, mlp_w2):
    cos, sin = axial_rope_tables(x.shape[1])
    c_silu = jax.nn.silu(c)
    cc_silu = jax.nn.silu(c_ctx)
    xc = ctx
    for l in range(DEPTH):
        update_ctx = l < DEPTH - 1
        sh1, sc1, gt1, sh2, sc2, gt2 = jnp.split((c_silu @ mod_w[l] + mod_b[l])[:, None, :], 6, axis=-1)
        sh1c, sc1c, gt1c, sh2c, sc2c, gt2c = jnp.split(cc_silu @ mod_w[l] + mod_b[l], 6, axis=-1)
        proj = (x * (1.0 + sc1) + sh1) @ w_in[l]
        proj_ctx = (xc * (1.0 + sc1c) + sh1c) @ w_in[l]
        pa, pb, pc, pg = jnp.split(proj, COL_SPLITS, axis=-1)
        pa_c, pb_c, pc_c, pg_c = jnp.split(proj_ctx, COL_SPLITS, axis=-1)
        ya, ya_c = rwkv7_branch(pa, pa_c, rwkv_mu[l], rwkv_w0[l], rwkv_w_up[l], rwkv_a0[l], rwkv_a_up[l],
                                rwkv_g_up[l], rwkv_k_k[l], rwkv_k_a[l], rwkv_r_k[l], rwkv_gn_w[l], rwkv_gn_b[l],
                                update_ctx)
        yb, yb_c = gqa_axial_branch(pb, pb_c, attn_q_gain[l], attn_k_gain[l], cos, sin, update_ctx)
        yc, yc_c = s5_branch(pc, pc_c, ssm_a_re[l], ssm_a_im[l], ssm_log_dt[l], ssm_b_re[l], ssm_b_im[l],
                             ssm_c_re[l], ssm_c_im[l], ssm_d[l], ssm_glu_w[l], ssm_glu_b[l], update_ctx)
        mix = gated_merge(ya, yb, yc, pg, proj_a[l], proj_b[l], proj_c[l], w_out[l])
        x_mid = layer_norm(ALPHA * x + gt1 * mix, ln1_g[l], ln1_b[l])
        ff = squared_relu_mlp(x_mid * (1.0 + sc2) + sh2, mlp_w1[l], mlp_w2[l])
        x = layer_norm(ALPHA * x_mid + gt2 * ff, ln2_g[l], ln2_b[l])
        if update_ctx:
            mix_c = gated_merge(ya_c, yb_c, yc_c, pg_c, proj_a[l], proj_b[l], proj_c[l], w_out[l])
            xc_mid = layer_norm(ALPHA * xc + gt1c * mix_c, ln1_g[l], ln1_b[l])
            ff_c = squared_relu_mlp(xc_mid * (1.0 + sc2c) + sh2c, mlp_w1[l], mlp_w2[l])
            xc = layer_norm(ALPHA * xc_mid + gt2c * ff_c, ln2_g[l], ln2_b[l])
    return x
```

```python
import functools
import math

import jax
import jax.numpy as jnp
from jax import lax
from jax.experimental import pallas as pl
from jax.experimental.pallas import tpu as pltpu

HEAD_DIM = 64
GRID_W = 64
ROPE_THETA = 10000.0
LN_EPS = 1e-5
RMS_EPS = 1e-6
GN_EPS = 64e-5
KK_EPS = 1e-24
LAMBDA_RE_MAX = -1e-4
GQA_GROUP = 4
SSM_GROUP = 16
SSM_STATE = 64

LANES = 128
SUBLANES = 8
VMEM_LIMIT = 56 * 1024 * 1024

ROW_TILE = 256
MLP_ROW_TILE = 512
MLP_FF_TILE = 1024
SCAN_CHUNK = 64
SSM_CHUNK = 64
SSM_LANE_TILE = 512
ATTN_TQ = 256
ATTN_TK = 512
PREP_TT = 256

F32 = jnp.float32
BF16 = jnp.bfloat16


def _cparams(*sem):
    return pltpu.CompilerParams(dimension_semantics=sem, vmem_limit_bytes=VMEM_LIMIT)


def _mm(a, b):
    return jnp.dot(a.astype(BF16), b.astype(BF16), preferred_element_type=F32)


def _mm_nt(a, b):
    return lax.dot_general(a.astype(BF16), b.astype(BF16), (((1,), (1,)), ((), ())),
                           preferred_element_type=F32)


def _mm_tn(a, b):
    return lax.dot_general(a.astype(BF16), b.astype(BF16), (((0,), (0,)), ((), ())),
                           preferred_element_type=F32)


def _segsum(x, ones_bd):
    hi = x.astype(BF16)
    lo = (x - hi.astype(F32)).astype(BF16)
    return (jnp.dot(hi, ones_bd, preferred_element_type=F32)
            + jnp.dot(lo, ones_bd, preferred_element_type=F32))


def _sigmoid(x):
    return 1.0 / (1.0 + jnp.exp(-x))


def _softplus(x):
    return jnp.maximum(x, 0.0) + jnp.log(1.0 + jnp.exp(-jnp.abs(x)))


def _layer_norm(z, g, b):
    mu = jnp.mean(z, axis=-1, keepdims=True)
    zc = z - mu
    var = jnp.mean(zc * zc, axis=-1, keepdims=True)
    return zc * lax.rsqrt(var + LN_EPS) * g + b


def _rows_mod(x, m):
    tm, n = x.shape
    return (x.reshape(tm // SUBLANES, SUBLANES, n) * m[None]).reshape(tm, n)


def _rows_add(x, m):
    tm, n = x.shape
    return (x.reshape(tm // SUBLANES, SUBLANES, n) + m[None]).reshape(tm, n)


def _mod_kernel(c_ref, w_ref, b_ref, o_ref):
    c = c_ref[...]
    s = c * _sigmoid(c)
    o_ref[0] = jnp.dot(s, w_ref[0], preferred_element_type=F32,
                       precision=lax.Precision.HIGHEST) + b_ref[0]


def _modulation(cvec, mod_w, mod_b):
    n_layers, d, n6 = mod_w.shape
    tn = 1024 if n6 % 1024 == 0 else n6
    return pl.pallas_call(
        _mod_kernel,
        out_shape=jax.ShapeDtypeStruct((n_layers, cvec.shape[0], n6), F32),
        grid=(n_layers, n6 // tn),
        in_specs=[pl.BlockSpec((cvec.shape[0], d), lambda l, j: (0, 0)),
                  pl.BlockSpec((1, d, tn), lambda l, j: (l, 0, j)),
                  pl.BlockSpec((1, 1, tn), lambda l, j: (l, 0, j))],
        out_specs=pl.BlockSpec((1, cvec.shape[0], tn), lambda l, j: (l, 0, j)),
        compiler_params=_cparams("parallel", "parallel"),
        name="modulation",
    )(cvec, mod_w, mod_b.reshape(n_layers, 1, n6))


def _inproj_kernel(splits, x_ref, sh_ref, sc_ref, w_ref, *o_refs):
    xm = _rows_add(_rows_mod(x_ref[...], 1.0 + sc_ref[0]), sh_ref[0]).astype(BF16)
    for (a, b), o_ref in zip(splits, o_refs):
        o_ref[...] = jnp.dot(xm, w_ref[:, a:b], preferred_element_type=F32)


def _in_projection(x_all, modtab, w_bf16, splits, n_ctx_rows):
    rows, d = x_all.shape
    tm = ROW_TILE
    nctx = n_ctx_rows // tm
    seg = lambda i: (i >= nctx).astype(jnp.int32)
    return pl.pallas_call(
        functools.partial(_inproj_kernel, splits),
        out_shape=[jax.ShapeDtypeStruct((rows, b - a), F32) for a, b in splits],
        grid=(rows // tm,),
        in_specs=[pl.BlockSpec((tm, d), lambda i: (i, 0)),
                  pl.BlockSpec((1, SUBLANES, d), lambda i: (seg(i), 0, 0)),
                  pl.BlockSpec((1, SUBLANES, d), lambda i: (seg(i), 0, 1)),
                  pl.BlockSpec(w_bf16.shape, lambda i: (0, 0), pipeline_mode=pl.Buffered(1))],
        out_specs=[pl.BlockSpec((tm, b - a), lambda i: (i, 0)) for a, b in splits],
        compiler_params=_cparams("parallel"),
        name="in_projection",
    )(x_all, modtab, modtab, w_bf16)


def _rwkv_prep_kernel(width, n_ctx_rows, n_rows,
                      p_ref, prev_ref, next_ref, mu_ref, w0_ref, wup_ref, a0_ref, aup_ref, gup_ref,
                      kk_ref, ka_ref, rk_ref, ones_ref,
                      r_out, v_out, kkn_out, g_out, bonus_out, lw_out, kd_out, bv_out):
    i = pl.program_id(0)
    tm = p_ref.shape[0]
    r0 = i * tm
    seg_start = jnp.logical_or(r0 == 0, r0 == n_ctx_rows)
    seg_end = jnp.logical_or(r0 + tm == n_ctx_rows, r0 + tm == n_rows)
    p = p_ref[...]
    halo_prev = jnp.where(seg_start, 0.0, prev_ref[...])
    halo_next = jnp.where(seg_end, 0.0, next_ref[...])
    prev = jnp.concatenate([halo_prev, p[:tm - SUBLANES]], axis=0)
    nxt = jnp.concatenate([p[SUBLANES:], halo_next], axis=0)
    p = p + mu_ref[...] * (0.5 * (prev + nxt) - p)

    w = width
    r, k, v = p[:, 0:w], p[:, w:2 * w], p[:, 2 * w:3 * w]
    wa = p[:, 3 * w:3 * w + LANES]
    gd = p[:, 3 * w + LANES:3 * w + 2 * LANES]
    ones_bd = ones_ref[...]

    g_out[...] = _mm(_sigmoid(gd), gup_ref[...])
    kk = k * kk_ref[...]
    kk = kk * lax.rsqrt(jnp.maximum(_segsum(kk * kk, ones_bd), KK_EPS))
    r_out[...] = r
    v_out[...] = v
    kkn_out[...] = kk
    tanh_wa = jnp.tanh(wa)
    rk_sum = jnp.zeros_like(r)
    for d in range(2):
        w_pre = w0_ref[d] + _mm(tanh_wa, wup_ref[d])
        lw_out[d] = -jnp.exp(-_softplus(-w_pre) - 0.5)
        a = _sigmoid(a0_ref[d] + _mm(wa, aup_ref[d]))
        k_dir = k * (1.0 + (a - 1.0) * ka_ref[...])
        kd_out[d] = k_dir
        bv_out[d] = kk * a
        rk_sum = rk_sum + r * k_dir
    bonus_out[...] = _segsum(rk_sum * rk_ref[...], ones_bd) * v


def _rwkv_prepare(pa, prm, n_ctx_rows):
    rows, cols = pa.shape
    width = prm["k_k"].shape[-1]
    tm = ROW_TILE
    hb = tm // SUBLANES
    last = rows // SUBLANES - 1
    full = lambda a: pl.BlockSpec(a.shape, lambda i: (0,) * a.ndim)
    row = lambda n: pl.BlockSpec((tm, n), lambda i: (i, 0))
    row2 = lambda n: pl.BlockSpec((2, tm, n), lambda i: (0, i, 0))
    consts = [prm["mu"], prm["w0"], prm["w_up"], prm["a0"], prm["a_up"], prm["g_up"],
              prm["k_k"], prm["k_a"], prm["r_k"], prm["ones_bd"]]
    one = jax.ShapeDtypeStruct((rows, width), F32)
    two = jax.ShapeDtypeStruct((2, rows, width), F32)
    return pl.pallas_call(
        functools.partial(_rwkv_prep_kernel, width, n_ctx_rows, rows),
        out_shape=[one, one, one, one, one, two, two, two],
        grid=(rows // tm,),
        in_specs=[row(cols),
                  pl.BlockSpec((SUBLANES, cols), lambda i: (jnp.maximum(i * hb - 1, 0), 0)),
                  pl.BlockSpec((SUBLANES, cols), lambda i: (jnp.minimum((i + 1) * hb, last), 0))]
                 + [full(a) for a in consts],
        out_specs=[row(width)] * 5 + [row2(width)] * 3,
        compiler_params=_cparams("parallel"),
        name="rwkv_prepare",
    )(pa, pa, pa, *consts)


def _scan_block(d, c, n_ctx_blk, n_blk):
    bwd = jnp.where(c < n_ctx_blk, n_ctx_blk - 1 - c, n_ctx_blk + (n_blk - 1 - c))
    return jnp.where(d == 0, c, bwd)


def _tri_inverse(a, blk_id_r, blk_id_c):
    n = a.shape[0]
    eye = (lax.broadcasted_iota(jnp.int32, (n, n), 0) == lax.broadcasted_iota(jnp.int32, (n, n), 1)).astype(F32)
    same = (blk_id_r // SUBLANES) == (blk_id_c // SUBLANES)
    x = -jnp.where(same, a, 0.0)
    x2 = _mm(x, x)
    x4 = _mm(x2, x2)
    t = eye + x
    t = t + _mm(t, x2)
    t = t + _mm(t, x4)
    size = SUBLANES
    while size < n:
        outer = (blk_id_r // (2 * size)) == (blk_id_c // (2 * size))
        inner = (blk_id_r // size) == (blk_id_c // size)
        low = jnp.where(jnp.logical_and(outer, jnp.logical_not(inner)), a, 0.0)
        t = t - _mm(t, _mm(low, t))
        size *= 2
    return t


def _rwkv_scan_kernel(r_ref, v_ref, kk_ref, lw_ref, kd_ref, bv_ref, y_ref, state_ref):
    d = pl.program_id(0)
    c = pl.program_id(2)
    n = r_ref.shape[0]
    n_pairs = r_ref.shape[1] // LANES

    @pl.when(c == 0)
    def _():
        state_ref[...] = jnp.zeros_like(state_ref)

    ri = lax.broadcasted_iota(jnp.int32, (n, n), 0)
    ci = lax.broadcasted_iota(jnp.int32, (n, n), 1)
    strict = (ri - ci) * (1 - 2 * d) > 0
    incl = jnp.logical_or(strict, ci == ri)
    incl_f = incl.astype(F32)
    lane = lax.broadcasted_iota(jnp.int32, (n, LANES), 1)
    head0 = lane < HEAD_DIM
    sr = lax.broadcasted_iota(jnp.int32, (LANES, LANES), 0) // HEAD_DIM
    sc = lax.broadcasted_iota(jnp.int32, (LANES, LANES), 1) // HEAD_DIM
    blockdiag = sr == sc

    for pidx in range(n_pairs):
        sl = slice(pidx * LANES, (pidx + 1) * LANES)
        r, v, kk = r_ref[:, sl], v_ref[:, sl], kk_ref[:, sl]
        lw, kd, bv = lw_ref[0, :, sl], kd_ref[0, :, sl], bv_ref[0, :, sl]
        state = state_ref[pidx]

        cum = jnp.dot(incl_f, lw, preferred_element_type=F32, precision=lax.Precision.HIGHEST)
        e_cum = jnp.exp(cum)
        e_neg = jnp.exp(-cum)
        kap = kk * jnp.exp(cum - lw)
        rt = r * e_cum
        bt = bv * e_neg
        kt = kd * e_neg
        total = jnp.exp(jnp.sum(lw, axis=0, keepdims=True))

        lhs = jnp.concatenate([jnp.where(head0, kap, 0.0), jnp.where(head0, rt, 0.0),
                               jnp.where(head0, 0.0, kap), jnp.where(head0, 0.0, rt)], axis=0)
        sb = _mm_nt(lhs, bt)
        sk = _mm_nt(lhs, kt)
        g0 = _mm_nt(jnp.concatenate([kap, rt], axis=0), state)

        us, ys = [], []
        for h in range(2):
            o = 2 * n * h
            a_b = jnp.where(strict, sb[o:o + n], 0.0)
            a_k = jnp.where(strict, sk[o:o + n], 0.0)
            a_rb = jnp.where(incl, sb[o + n:o + 2 * n], 0.0)
            a_rk = jnp.where(incl, sk[o + n:o + 2 * n], 0.0)
            t_inv = _tri_inverse(a_b, ri, ci)
            u = _mm(t_inv, g0[0:n] + _mm(a_k, v))
            us.append(u)
            ys.append(g0[n:2 * n] - _mm(a_rb, u) + _mm(a_rk, v))
        u = jnp.where(head0, us[0], us[1])
        y_ref[0, :, sl] = jnp.where(head0, ys[0], ys[1])

        ds = _mm_tn(jnp.concatenate([-u, v], axis=0), jnp.concatenate([bt, kt], axis=0))
        state_ref[pidx] = (state + jnp.where(blockdiag, ds, 0.0)) * total


def _rwkv_scan(r, v, kk, lw, kd, bv, n_batch, n_ctx):
    rows, width = r.shape
    tt = rows // n_batch
    n = SCAN_CHUNK
    n_blk, n_ctx_blk = tt // n, n_ctx // n
    view = lambda a: a.reshape(a.shape[:-2] + (tt, n_batch * width))
    blk = lambda d, b, c: _scan_block(d, c, n_ctx_blk, n_blk)
    shared = pl.BlockSpec((n, width), lambda d, b, c: (blk(d, b, c), b))
    split = pl.BlockSpec((1, n, width), lambda d, b, c: (d, blk(d, b, c), b))
    y = pl.pallas_call(
        _rwkv_scan_kernel,
        out_shape=jax.ShapeDtypeStruct((2, tt, n_batch * width), F32),
        grid=(2, n_batch, n_blk),
        in_specs=[shared, shared, shared, split, split, split],
        out_specs=split,
        scratch_shapes=[pltpu.VMEM((width // LANES, LANES, LANES), F32)],
        compiler_params=_cparams("parallel", "parallel", "arbitrary"),
        name="rwkv_scan",
    )(view(r), view(v), view(kk), view(lw), view(kd), view(bv))
    return y.reshape(2, rows, width)


def _rwkv_readout_kernel(y_ref, g_ref, bonus_ref, gnw_ref, gnb_ref, ones_ref, o_ref):
    y = y_ref[0] + y_ref[1]
    ones_bd = ones_ref[...]
    inv = 1.0 / HEAD_DIM
    mean = _segsum(y, ones_bd) * inv
    yc = y - mean
    var = _segsum(yc * yc, ones_bd) * inv
    yn = yc * lax.rsqrt(var + GN_EPS) * gnw_ref[...] + gnb_ref[...]
    o_ref[...] = (yn + bonus_ref[...]) * g_ref[...]


def _rwkv_readout(ys, g, bonus, gn_w, gn_b, ones_bd):
    rows, width = g.shape
    tm = ROW_TILE
    row = pl.BlockSpec((tm, width), lambda i: (i, 0))
    vec = pl.BlockSpec((1, width), lambda i: (0, 0))
    return pl.pallas_call(
        _rwkv_readout_kernel,
        out_shape=jax.ShapeDtypeStruct((rows, width), F32),
        grid=(rows // tm,),
        in_specs=[pl.BlockSpec((2, tm, width), lambda i: (0, i, 0)), row, row, vec, vec,
                  pl.BlockSpec(ones_bd.shape, lambda i: (0, 0))],
        out_specs=row,
        compiler_params=_cparams("parallel"),
        name="rwkv_readout",
    )(ys, g, bonus, gn_w, gn_b, ones_bd)


def _norm_rope(x, gain, cos, sin_a, sin_b, ones_bd):
    ms = _segsum(x * x, ones_bd) * (1.0 / HEAD_DIM)
    xn = x * lax.rsqrt(ms + RMS_EPS) * gain
    return (xn * cos + pltpu.roll(xn, LANES - HEAD_DIM // 4, 1) * sin_a
            + pltpu.roll(xn, HEAD_DIM // 4, 1) * sin_b)


def _attn_prep_kernel(scale, q_ref, k_ref, v_ref, cos_ref, sa_ref, sb_ref, qg_ref, kg_ref, ones_ref,
                      qo_ref, ko_ref, vo_ref):
    cos, sin_a, sin_b = cos_ref[...], sa_ref[...], sb_ref[...]
    ones_bd = ones_ref[...]
    for j in range(q_ref.shape[1] // LANES):
        y = _norm_rope(q_ref[:, j * LANES:(j + 1) * LANES], qg_ref[...], cos, sin_a, sin_b, ones_bd) * scale
        qo_ref[0, 2 * j] = y[:, :HEAD_DIM].astype(BF16)
        qo_ref[0, 2 * j + 1] = y[:, HEAD_DIM:].astype(BF16)
    for j in range(k_ref.shape[1] // LANES):
        y = _norm_rope(k_ref[:, j * LANES:(j + 1) * LANES], kg_ref[...], cos, sin_a, sin_b, ones_bd)
        ko_ref[0, 2 * j] = y[:, :HEAD_DIM].astype(BF16)
        ko_ref[0, 2 * j + 1] = y[:, HEAD_DIM:].astype(BF16)
        vv = v_ref[:, j * LANES:(j + 1) * LANES]
        vo_ref[0, 2 * j] = vv[:, :HEAD_DIM].astype(BF16)
        vo_ref[0, 2 * j + 1] = vv[:, HEAD_DIM:].astype(BF16)


def _attn_prepare(pq, pk, pv, cos, sin_a, sin_b, q_gain, k_gain, ones_pair, n_batch):
    rows, qw = pq.shape
    kw = pk.shape[1]
    tt_all = rows // n_batch
    tt = math.gcd(PREP_TT, tt_all)
    qh, kh = qw // HEAD_DIM, kw // HEAD_DIM
    view = lambda a: a.reshape(tt_all, n_batch * a.shape[1])
    tab = pl.BlockSpec((tt, LANES), lambda b, i: (i, 0))
    vec = pl.BlockSpec((1, LANES), lambda b, i: (0, 0))
    return pl.pallas_call(
        functools.partial(_attn_prep_kernel, HEAD_DIM ** -0.5),
        out_shape=[jax.ShapeDtypeStruct((n_batch, qh, tt_all, HEAD_DIM), BF16),
                   jax.ShapeDtypeStruct((n_batch, kh, tt_all, HEAD_DIM), BF16),
                   jax.ShapeDtypeStruct((n_batch, kh, tt_all, HEAD_DIM), BF16)],
        grid=(n_batch, tt_all // tt),
        in_specs=[pl.BlockSpec((tt, qw), lambda b, i: (i, b)),
                  pl.BlockSpec((tt, kw), lambda b, i: (i, b)),
                  pl.BlockSpec((tt, kw), lambda b, i: (i, b)),
                  tab, tab, tab, vec, vec,
                  pl.BlockSpec(ones_pair.shape, lambda b, i: (0, 0))],
        out_specs=[pl.BlockSpec((1, qh, tt, HEAD_DIM), lambda b, i: (b, 0, i, 0)),
                   pl.BlockSpec((1, kh, tt, HEAD_DIM), lambda b, i: (b, 0, i, 0)),
                   pl.BlockSpec((1, kh, tt, HEAD_DIM), lambda b, i: (b, 0, i, 0))],
        compiler_params=_cparams("parallel", "parallel"),
        name="attn_prepare",
    )(view(pq), view(pk), view(pv), cos, sin_a, sin_b, q_gain, k_gain, ones_pair)


def _flash_kernel(nq_ctx, nk_ctx, q_ref, k_ref, v_ref, o_ref, m_ref, l_ref, acc_ref):
    qi = pl.program_id(2)
    ki = pl.program_id(3)
    g, tq, dh = q_ref.shape[1], q_ref.shape[2], q_ref.shape[3]

    @pl.when(ki == 0)
    def _():
        m_ref[...] = jnp.full_like(m_ref, -jnp.inf)
        l_ref[...] = jnp.zeros_like(l_ref)
        acc_ref[...] = jnp.zeros_like(acc_ref)

    @pl.when(jnp.logical_or(qi >= nq_ctx, ki < nk_ctx))
    def _():
        q = q_ref[0].reshape(g * tq, dh)
        s = lax.dot_general(q, k_ref[0, 0], (((1,), (1,)), ((), ())), preferred_element_type=F32)
        m_prev = m_ref[...]
        m_new = jnp.maximum(m_prev, jnp.max(s, axis=-1, keepdims=True))
        alpha = jnp.exp(m_prev - m_new)
        p = jnp.exp(s - m_new)
        l_ref[...] = alpha * l_ref[...] + jnp.sum(p, axis=-1, keepdims=True)
        acc_ref[...] = alpha * acc_ref[...] + jnp.dot(p.astype(BF16), v_ref[0, 0], preferred_element_type=F32)
        m_ref[...] = m_new

    @pl.when(ki == pl.num_programs(3) - 1)
    def _():
        o = acc_ref[...] / l_ref[...]
        o_ref[...] = jnp.concatenate([o[h * tq:(h + 1) * tq] for h in range(g)], axis=1)


def _flash_attention(qh, kh, vh, n_ctx):
    n_batch, n_qh, tt_all, dh = qh.shape
    n_kvh = kh.shape[1]
    g = n_qh // n_kvh
    tq, tk = min(ATTN_TQ, n_ctx), min(ATTN_TK, n_ctx)
    nq_ctx, nk_ctx = n_ctx // tq, n_ctx // tk
    kidx = lambda qi, ki: jnp.where(qi < nq_ctx, jnp.minimum(ki, nk_ctx - 1), ki)
    kv_spec = pl.BlockSpec((1, 1, tk, dh), lambda b, h, qi, ki: (b, h, kidx(qi, ki), 0))
    return pl.pallas_call(
        functools.partial(_flash_kernel, nq_ctx, nk_ctx),
        out_shape=jax.ShapeDtypeStruct((tt_all, n_batch * n_qh * dh), F32),
        grid=(n_batch, n_kvh, tt_all // tq, tt_all // tk),
        in_specs=[pl.BlockSpec((1, g, tq, dh), lambda b, h, qi, ki: (b, h, qi, 0)), kv_spec, kv_spec],
        out_specs=pl.BlockSpec((tq, g * dh), lambda b, h, qi, ki: (qi, b * n_kvh + h)),
        scratch_shapes=[pltpu.VMEM((g * tq, 1), F32), pltpu.VMEM((g * tq, 1), F32),
                        pltpu.VMEM((g * tq, dh), F32)],
        compiler_params=_cparams("parallel", "parallel", "parallel", "arbitrary"),
        name="flash_attention",
    )(qh, kh, vh)


def _s5_param_kernel(are_ref, aim_ref, ldt_ref, bre_ref, bim_ref, abar_re, abar_im, bf_re, bf_im):
    lam_re = jnp.minimum(are_ref[0], LAMBDA_RE_MAX)
    lam_im = aim_ref[0]
    dt = jnp.exp(ldt_ref[0])
    mag = jnp.exp(lam_re * dt)
    ar, ai = mag * jnp.cos(lam_im * dt), mag * jnp.sin(lam_im * dt)
    nr, ni = ar - 1.0, ai
    den = lam_re * lam_re + lam_im * lam_im
    cr = (nr * lam_re + ni * lam_im) / den
    cim = (ni * lam_re - nr * lam_im) / den
    abar_re[0], abar_im[0] = ar, ai
    bre, bim = bre_ref[...], bim_ref[...]
    bf_re[0] = cr * bre - cim * bim
    bf_im[0] = cr * bim + cim * bre


def _s5_params(a_re, a_im, log_dt, b_re, b_im):
    _, n_g, n_s = a_re.shape
    n_i = b_re.shape[-1]
    gn = n_g * n_s
    flat = lambda a: a.reshape(2, 1, gn)
    dt = jnp.broadcast_to(log_dt[:, :, None], (2, n_g, n_s))
    bt = lambda b: b.reshape(gn, n_i).T
    vec = pl.BlockSpec((1, 1, gn), lambda d: (d, 0, 0))
    mat = pl.BlockSpec((n_i, gn), lambda d: (0, 0))
    omat = pl.BlockSpec((1, n_i, gn), lambda d: (d, 0, 0))
    return pl.pallas_call(
        _s5_param_kernel,
        out_shape=[jax.ShapeDtypeStruct((2, 1, gn), F32)] * 2 + [jax.ShapeDtypeStruct((2, n_i, gn), F32)] * 2,
        grid=(2,),
        in_specs=[vec, vec, vec, mat, mat],
        out_specs=[vec, vec, omat, omat],
        compiler_params=_cparams("parallel"),
        name="s5_params",
    )(flat(a_re), flat(a_im), flat(dt), bt(b_re), bt(b_im))


def _s5_scan_kernel(u_ref, ar_ref, ai_ref, bre_ref, bim_ref, cre_ref, cim_ref, y_ref,
                    xr_ref, xi_ref, hr_ref, hi_ref):
    d = pl.program_id(0)
    c = pl.program_id(1)
    steps = u_ref.shape[0] // SUBLANES
    width = xr_ref.shape[1]

    @pl.when(c == 0)
    def _():
        hr_ref[...] = jnp.zeros_like(hr_ref)
        hi_ref[...] = jnp.zeros_like(hi_ref)

    u = u_ref[...].astype(BF16)
    xr_ref[...] = jnp.dot(u, bre_ref[0], preferred_element_type=F32)
    xi_ref[...] = jnp.dot(u, bim_ref[0], preferred_element_type=F32)

    for j in range(width // SSM_LANE_TILE):
        sl = pl.ds(j * SSM_LANE_TILE, SSM_LANE_TILE)
        ar = jnp.broadcast_to(ar_ref[0, :, sl], (SUBLANES, SSM_LANE_TILE))
        ai = jnp.broadcast_to(ai_ref[0, :, sl], (SUBLANES, SSM_LANE_TILE))

        def step(i, carry):
            hr, hi = carry
            t = jnp.where(d == 0, i, steps - 1 - i)
            rows = pl.ds(pl.multiple_of(t * SUBLANES, SUBLANES), SUBLANES)
            nr = ar * hr - ai * hi + xr_ref[rows, sl]
            ni = ar * hi + ai * hr + xi_ref[rows, sl]
            xr_ref[rows, sl] = nr
            xi_ref[rows, sl] = ni
            return nr, ni

        hr, hi = lax.fori_loop(0, steps, step, (hr_ref[:, sl], hi_ref[:, sl]), unroll=8)
        hr_ref[:, sl] = hr
        hi_ref[:, sl] = hi

    y_ref[0] = (jnp.dot(xr_ref[...].astype(BF16), cre_ref[...], preferred_element_type=F32)
                - jnp.dot(xi_ref[...].astype(BF16), cim_ref[...], preferred_element_type=F32))


def _s5_scan(u, abar_re, abar_im, b_re, b_im, c_re, c_im, n_batch, n_ctx):
    rows, width = u.shape
    gn = abar_re.shape[-1]
    tm = SSM_CHUNK * n_batch
    n_blk, n_ctx_blk = rows // tm, n_ctx * n_batch // tm
    blk = lambda d, c: _scan_block(d, c, n_ctx_blk, n_blk)
    vec = pl.BlockSpec((1, 1, gn), lambda d, c: (d, 0, 0))
    bmat = pl.BlockSpec((1, width, gn), lambda d, c: (d, 0, 0))
    cmat = pl.BlockSpec((gn, width), lambda d, c: (0, 0))
    return pl.pallas_call(
        _s5_scan_kernel,
        out_shape=jax.ShapeDtypeStruct((2, rows, width), F32),
        grid=(2, n_blk),
        in_specs=[pl.BlockSpec((tm, width), lambda d, c: (blk(d, c), 0)), vec, vec, bmat, bmat, cmat, cmat],
        out_specs=pl.BlockSpec((1, tm, width), lambda d, c: (d, blk(d, c), 0)),
        scratch_shapes=[pltpu.VMEM((tm, gn), F32), pltpu.VMEM((tm, gn), F32),
                        pltpu.VMEM((SUBLANES, gn), F32), pltpu.VMEM((SUBLANES, gn), F32)],
        compiler_params=_cparams("parallel", "arbitrary"),
        name="s5_scan",
    )(u, abar_re, abar_im, b_re, b_im, c_re, c_im)


def _gelu_tanh(x):
    return 0.5 * x * (1.0 + jnp.tanh(math.sqrt(2.0 / math.pi) * (x + 0.044715 * (x * x * x))))


def _s5_readout_kernel(y_ref, u_ref, d_ref, w_ref, b_ref, o_ref):
    y = y_ref[0] + y_ref[1] + d_ref[...] * u_ref[...]
    y = _gelu_tanh(y)
    o_ref[...] = y * _sigmoid(jnp.dot(y.astype(BF16), w_ref[...], preferred_element_type=F32) + b_ref[...])


def _s5_readout(ys, u, d_skip, glu_w, glu_b):
    rows, width = u.shape
    tm = ROW_TILE
    row = pl.BlockSpec((tm, width), lambda i: (i, 0))
    vec = pl.BlockSpec((1, width), lambda i: (0, 0))
    return pl.pallas_call(
        _s5_readout_kernel,
        out_shape=jax.ShapeDtypeStruct((rows, width), F32),
        grid=(rows // tm,),
        in_specs=[pl.BlockSpec((2, tm, width), lambda i: (0, i, 0)), row, vec,
                  pl.BlockSpec(glu_w.shape, lambda i: (0, 0)), vec],
        out_specs=row,
        compiler_params=_cparams("parallel"),
        name="s5_readout",
    )(ys, u, d_skip, glu_w, glu_b)


def _merge_kernel(alpha, ya_ref, yb_ref, yc_ref, pg_ref, x_ref, gt_ref, pa_ref, pb_ref, pc_ref, wo_ref,
                  g_ref, b_ref, o_ref):
    d = x_ref.shape[1]
    pg = pg_ref[...]
    merged = (_sigmoid(pg[:, 0:d]) * _mm(ya_ref[...], pa_ref[...])
              + _sigmoid(pg[:, d:2 * d]) * _mm(yb_ref[...], pb_ref[...])
              + _sigmoid(pg[:, 2 * d:3 * d]) * _mm(yc_ref[...], pc_ref[...]))
    mix = _mm(merged, wo_ref[...])
    o_ref[...] = _layer_norm(alpha * x_ref[...] + _rows_mod(mix, gt_ref[0]), g_ref[...], b_ref[...])


def _merge(ya, yb, yc, pg, x_all, modtab, proj_a, proj_b, proj_c, w_out, ln_g, ln_b, alpha, n_ctx_rows):
    rows, d = x_all.shape
    tm = ROW_TILE
    nctx = n_ctx_rows // tm
    seg = lambda i: (i >= nctx).astype(jnp.int32)
    row = lambda a: pl.BlockSpec((tm, a.shape[1]), lambda i: (i, 0))
    full = lambda a: pl.BlockSpec(a.shape, lambda i: (0, 0), pipeline_mode=pl.Buffered(1))
    vec = pl.BlockSpec((1, d), lambda i: (0, 0))
    return pl.pallas_call(
        functools.partial(_merge_kernel, alpha),
        out_shape=jax.ShapeDtypeStruct((rows, d), F32),
        grid=(rows // tm,),
        in_specs=[row(ya), row(yb), row(yc), row(pg), row(x_all),
                  pl.BlockSpec((1, SUBLANES, d), lambda i: (seg(i), 0, 2)),
                  full(proj_a), full(proj_b), full(proj_c), full(w_out), vec, vec],
        out_specs=pl.BlockSpec((tm, d), lambda i: (i, 0)),
        compiler_params=_cparams("parallel"),
        name="merge_ln",
    )(ya, yb, yc, pg, x_all, modtab, proj_a, proj_b, proj_c, w_out, ln_g, ln_b)


def _mlp_kernel(alpha, x_ref, sh_ref, sc_ref, gt_ref, w1_ref, w2_ref, g_ref, b_ref, o_ref, h_ref, acc_ref):
    j = pl.program_id(1)

    @pl.when(j == 0)
    def _():
        h_ref[...] = _rows_add(_rows_mod(x_ref[...], 1.0 + sc_ref[0]), sh_ref[0]).astype(BF16)
        acc_ref[...] = jnp.zeros_like(acc_ref)

    a = jnp.maximum(jnp.dot(h_ref[...], w1_ref[...], preferred_element_type=F32), 0.0)
    acc_ref[...] += jnp.dot((a * a).astype(BF16), w2_ref[...], preferred_element_type=F32)

    @pl.when(j == pl.num_programs(1) - 1)
    def _():
        o_ref[...] = _layer_norm(alpha * x_ref[...] + _rows_mod(acc_ref[...], gt_ref[0]),
                                 g_ref[...], b_ref[...])


def _mlp(x_mid, modtab, w1, w2, ln_g, ln_b, alpha, n_ctx_rows):
    rows, d = x_mid.shape
    ff = w1.shape[1]
    tm = min(MLP_ROW_TILE, n_ctx_rows)
    tf = min(MLP_FF_TILE, ff)
    nctx = n_ctx_rows // tm
    seg = lambda i: (i >= nctx).astype(jnp.int32)
    mod = lambda col: pl.BlockSpec((1, SUBLANES, d), lambda i, j: (seg(i), 0, col))
    vec = pl.BlockSpec((1, d), lambda i, j: (0, 0))
    return pl.pallas_call(
        functools.partial(_mlp_kernel, alpha),
        out_shape=jax.ShapeDtypeStruct((rows, d), F32),
        grid=(rows // tm, ff // tf),
        in_specs=[pl.BlockSpec((tm, d), lambda i, j: (i, 0)), mod(3), mod(4), mod(5),
                  pl.BlockSpec((d, tf), lambda i, j: (0, j)),
                  pl.BlockSpec((tf, d), lambda i, j: (j, 0)), vec, vec],
        out_specs=pl.BlockSpec((tm, d), lambda i, j: (i, 0)),
        scratch_shapes=[pltpu.VMEM((tm, d), BF16), pltpu.VMEM((tm, d), F32)],
        compiler_params=_cparams("parallel", "arbitrary"),
        name="mlp_ln",
    )(x_mid, modtab, modtab, modtab, w1, w2, ln_g, ln_b)


def _block_diag_ones(n, blk):
    i = jnp.arange(n) // blk
    return (i[:, None] == i[None, :]).astype(BF16)


def _rope_tables(n_ctx, n_lat):
    pairs = HEAD_DIM // 4
    rows = n_lat // GRID_W
    row = jnp.repeat(jnp.arange(rows, dtype=F32), GRID_W)
    col = jnp.tile(jnp.arange(GRID_W, dtype=F32), rows)
    inv = ROPE_THETA ** (-jnp.arange(pairs, dtype=F32) / pairs)
    ang = jnp.stack([row, col], axis=-1)[:, :, None] * inv
    ang = jnp.broadcast_to(ang[:, :, None, :], (n_lat, 2, 2, pairs)).reshape(n_lat, HEAD_DIM)
    cos = jnp.concatenate([jnp.ones((n_ctx, HEAD_DIM), F32), jnp.cos(ang)], axis=0)
    sin = jnp.concatenate([jnp.zeros((n_ctx, HEAD_DIM), F32), jnp.sin(ang)], axis=0)
    first = (jnp.arange(HEAD_DIM) % (2 * pairs)) < pairs
    sin_a = jnp.where(first, -sin, 0.0)
    sin_b = jnp.where(first, 0.0, sin)
    two = lambda a: jnp.concatenate([a, a], axis=1)
    return two(cos), two(sin_a), two(sin_b)


def _block_diag(blocks):
    g, a, b = blocks.shape
    eye = jnp.eye(g, dtype=blocks.dtype)
    return (eye[:, None, :, None] * blocks[:, :, None, :]).reshape(g * a, g * b)


def kernel(x, c, ctx, c_ctx, mod_w, mod_b, w_in, rwkv_mu, rwkv_w0, rwkv_w_up, rwkv_a0, rwkv_a_up, rwkv_g_up, rwkv_k_k, rwkv_k_a, rwkv_r_k, rwkv_gn_w, rwkv_gn_b, attn_q_gain, attn_k_gain, ssm_a_re, ssm_a_im, ssm_log_dt, ssm_b_re, ssm_b_im, ssm_c_re, ssm_c_im, ssm_d, ssm_glu_w, ssm_glu_b, proj_a, proj_b, proj_c, w_out, ln1_g, ln1_b, ln2_g, ln2_b, mlp_w1, mlp_w2):
    n_batch, n_lat, d_model = x.shape
    n_ctx = ctx.shape[1]
    depth = mod_w.shape[0]
    assert n_batch == SUBLANES, "row layout packs the batch into one sublane tile"
    alpha = (2 * depth) ** 0.25
    n_ctx_rows = n_ctx * n_batch

    rwkv_w = rwkv_k_k.shape[1]
    q_w = proj_b.shape[1]
    kv_w = q_w // GQA_GROUP
    ssm_w = ssm_d.shape[1]
    rwkv_cols = rwkv_mu.shape[1]
    edges = [0, rwkv_cols, rwkv_cols + q_w, rwkv_cols + q_w + kv_w, rwkv_cols + q_w + 2 * kv_w,
             rwkv_cols + q_w + 2 * kv_w + ssm_w, w_in.shape[2]]
    splits = tuple(zip(edges[:-1], edges[1:]))

    to_rows = lambda a: a.transpose(1, 0, 2).reshape(a.shape[1] * n_batch, d_model)
    x_all = jnp.concatenate([to_rows(ctx), to_rows(x)], axis=0)

    cvec = jnp.concatenate([jnp.broadcast_to(c_ctx[None], (n_batch, d_model)), c], axis=0)
    modtab = _modulation(cvec, mod_w, mod_b).reshape(depth, 2, n_batch, 6 * d_model)

    ones_rwkv = _block_diag_ones(rwkv_w, HEAD_DIM)
    ones_pair = _block_diag_ones(LANES, HEAD_DIM)
    cos, sin_a, sin_b = _rope_tables(n_ctx, n_lat)
    rank = rwkv_w_up.shape[2]
    zpad = jnp.zeros((depth, 2, rank, rwkv_w), F32)

    for l in range(depth):
        mt = modtab[l]
        pa, pq, pk, pv, pc, pg = _in_projection(x_all, mt, w_in[l].astype(BF16), splits, n_ctx_rows)

        prm = dict(mu=rwkv_mu[l][None], w0=rwkv_w0[l][:, None], a0=rwkv_a0[l][:, None],
                   w_up=jnp.concatenate([rwkv_w_up[l], zpad[l]], axis=1).astype(BF16),
                   a_up=jnp.concatenate([zpad[l], rwkv_a_up[l]], axis=1).astype(BF16),
                   g_up=rwkv_g_up[l].astype(BF16), k_k=rwkv_k_k[l][None], k_a=rwkv_k_a[l][None],
                   r_k=rwkv_r_k[l][None], ones_bd=ones_rwkv)
        r, v, kk, g, bonus, lw, kd, bv = _rwkv_prepare(pa, prm, n_ctx_rows)
        ys = _rwkv_scan(r, v, kk, lw, kd, bv, n_batch, n_ctx)
        ya = _rwkv_readout(ys, g, bonus, rwkv_gn_w[l][None], rwkv_gn_b[l][None], ones_rwkv)

        tile2 = lambda a: jnp.tile(a, 2)[None]
        qh, kh, vh = _attn_prepare(pq, pk, pv, cos, sin_a, sin_b, tile2(attn_q_gain[l]), tile2(attn_k_gain[l]),
                                   ones_pair, n_batch)
        yb = _flash_attention(qh, kh, vh, n_ctx).reshape(x_all.shape[0], q_w)

        abar_re, abar_im, bf_re, bf_im = _s5_params(ssm_a_re[l], ssm_a_im[l], ssm_log_dt[l], ssm_b_re[l], ssm_b_im[l])
        n_g = ssm_a_re.shape[2]
        bmat = lambda bf: jnp.stack([_block_diag(bf[d].reshape(SSM_GROUP, n_g, SSM_STATE).transpose(1, 0, 2))
                                     for d in range(2)]).astype(BF16)
        cmat = lambda cc: _block_diag(cc.transpose(0, 2, 1)).astype(BF16)
        ysc = _s5_scan(pc, abar_re, abar_im, bmat(bf_re), bmat(bf_im), cmat(ssm_c_re[l]), cmat(ssm_c_im[l]),
                       n_batch, n_ctx)
        yc = _s5_readout(ysc, pc, ssm_d[l][None], ssm_glu_w[l].astype(BF16), ssm_glu_b[l][None])

        x_mid = _merge(ya, yb, yc, pg, x_all, mt, proj_a[l].astype(BF16), proj_b[l].astype(BF16),
                       proj_c[l].astype(BF16), w_out[l].astype(BF16), ln1_g[l][None], ln1_b[l][None],
                       alpha, n_ctx_rows)
        x_all = _mlp(x_mid, mt, mlp_w1[l].astype(BF16), mlp_w2[l].astype(BF16), ln2_g[l][None], ln2_b[l][None],
                     alpha, n_ctx_rows)

    out = x_all[n_ctx_rows:].reshape(n_lat, n_batch, d_model).transpose(1, 0, 2)
    return out
```

```python
import functools
import math

import jax
import jax.numpy as jnp
from jax import lax
from jax.experimental import pallas as pl
from jax.experimental.pallas import tpu as pltpu

HEAD_DIM = 64
GRID_W = 64
ROPE_THETA = 10000.0
LN_EPS = 1e-5
RMS_EPS = 1e-6
GN_EPS = 64e-5
KK_EPS = 1e-24
LAMBDA_RE_MAX = -1e-4
GQA_GROUP = 4
SSM_GROUP = 16
SSM_STATE = 64

LANES = 128
SUBLANES = 8
VMEM_LIMIT = 56 * 1024 * 1024

ROW_TILE = 256
MLP_ROW_TILE = 512
MLP_FF_TILE = 1024
SCAN_CHUNK = 64
SSM_CHUNK = 64
SSM_LANE_TILE = 512
ATTN_TQ = 256
ATTN_TK = 512
PREP_TT = 64

F32 = jnp.float32
BF16 = jnp.bfloat16


def _cparams(*sem):
    return pltpu.CompilerParams(dimension_semantics=sem, vmem_limit_bytes=VMEM_LIMIT)


def _mm(a, b):
    return jnp.dot(a.astype(BF16), b.astype(BF16), preferred_element_type=F32)


def _mm_nt(a, b):
    return lax.dot_general(a.astype(BF16), b.astype(BF16), (((1,), (1,)), ((), ())),
                           preferred_element_type=F32)


def _mm_tn(a, b):
    return lax.dot_general(a.astype(BF16), b.astype(BF16), (((0,), (0,)), ((), ())),
                           preferred_element_type=F32)


def _segsum(x, ones_bd):
    hi = x.astype(BF16)
    lo = (x - hi.astype(F32)).astype(BF16)
    return (jnp.dot(hi, ones_bd, preferred_element_type=F32)
            + jnp.dot(lo, ones_bd, preferred_element_type=F32))


def _sigmoid(x):
    return 1.0 / (1.0 + jnp.exp(-x))


def _softplus(x):
    return jnp.maximum(x, 0.0) + jnp.log(1.0 + jnp.exp(-jnp.abs(x)))


def _layer_norm(z, g, b):
    mu = jnp.mean(z, axis=-1, keepdims=True)
    zc = z - mu
    var = jnp.mean(zc * zc, axis=-1, keepdims=True)
    return zc * lax.rsqrt(var + LN_EPS) * g + b


def _rows_mod(x, m):
    tm, n = x.shape
    return (x.reshape(tm // SUBLANES, SUBLANES, n) * m[None]).reshape(tm, n)


def _rows_add(x, m):
    tm, n = x.shape
    return (x.reshape(tm // SUBLANES, SUBLANES, n) + m[None]).reshape(tm, n)


def _mod_kernel(c_ref, w_ref, b_ref, o_ref):
    c = c_ref[...]
    s = c * _sigmoid(c)
    o_ref[0] = jnp.dot(s, w_ref[0], preferred_element_type=F32,
                       precision=lax.Precision.HIGHEST) + b_ref[0]


def _modulation(cvec, mod_w, mod_b):
    n_layers, d, n6 = mod_w.shape
    tn = 1024 if n6 % 1024 == 0 else n6
    return pl.pallas_call(
        _mod_kernel,
        out_shape=jax.ShapeDtypeStruct((n_layers, cvec.shape[0], n6), F32),
        grid=(n_layers, n6 // tn),
        in_specs=[pl.BlockSpec((cvec.shape[0], d), lambda l, j: (0, 0)),
                  pl.BlockSpec((1, d, tn), lambda l, j: (l, 0, j)),
                  pl.BlockSpec((1, 1, tn), lambda l, j: (l, 0, j))],
        out_specs=pl.BlockSpec((1, cvec.shape[0], tn), lambda l, j: (l, 0, j)),
        compiler_params=_cparams("parallel", "parallel"),
        name="modulation",
    )(cvec, mod_w, mod_b.reshape(n_layers, 1, n6))


def _inproj_kernel(splits, x_ref, sh_ref, sc_ref, w_ref, *o_refs):
    xm = _rows_add(_rows_mod(x_ref[...], 1.0 + sc_ref[0]), sh_ref[0]).astype(BF16)
    for (a, b, slabbed), o_ref in zip(splits, o_refs):
        y = jnp.dot(xm, w_ref[:, a:b], preferred_element_type=F32)
        if slabbed:
            for j in range((b - a) // LANES):
                o_ref[j] = y[:, j * LANES:(j + 1) * LANES]
        else:
            o_ref[...] = y


def _in_projection(x_all, modtab, w_bf16, splits, n_ctx_rows):
    rows, d = x_all.shape
    tm = ROW_TILE
    nctx = n_ctx_rows // tm
    seg = lambda i: (i >= nctx).astype(jnp.int32)
    shapes, specs = [], []
    for a, b, slabbed in splits:
        if slabbed:
            shapes.append(jax.ShapeDtypeStruct(((b - a) // LANES, rows, LANES), F32))
            specs.append(pl.BlockSpec(((b - a) // LANES, tm, LANES), lambda i: (0, i, 0)))
        else:
            shapes.append(jax.ShapeDtypeStruct((rows, b - a), F32))
            specs.append(pl.BlockSpec((tm, b - a), lambda i: (i, 0)))
    return pl.pallas_call(
        functools.partial(_inproj_kernel, splits),
        out_shape=shapes,
        grid=(rows // tm,),
        in_specs=[pl.BlockSpec((tm, d), lambda i: (i, 0)),
                  pl.BlockSpec((1, SUBLANES, d), lambda i: (seg(i), 0, 0)),
                  pl.BlockSpec((1, SUBLANES, d), lambda i: (seg(i), 0, 1)),
                  pl.BlockSpec(w_bf16.shape, lambda i: (0, 0), pipeline_mode=pl.Buffered(1))],
        out_specs=specs,
        compiler_params=_cparams("parallel"),
        name="in_projection",
    )(x_all, modtab, modtab, w_bf16)


def _rwkv_prep_kernel(width, n_ctx_rows, n_rows,
                      p_ref, prev_ref, next_ref, mu_ref, w0_ref, wup_ref, a0_ref, aup_ref, gup_ref,
                      kk_ref, ka_ref, rk_ref, ones_ref,
                      r_out, v_out, kkn_out, g_out, bonus_out, lw_out, kd_out, bv_out):
    i = pl.program_id(0)
    tm = p_ref.shape[0]
    r0 = i * tm
    seg_start = jnp.logical_or(r0 == 0, r0 == n_ctx_rows)
    seg_end = jnp.logical_or(r0 + tm == n_ctx_rows, r0 + tm == n_rows)
    p = p_ref[...]
    halo_prev = jnp.where(seg_start, 0.0, prev_ref[...])
    halo_next = jnp.where(seg_end, 0.0, next_ref[...])
    prev = jnp.concatenate([halo_prev, p[:tm - SUBLANES]], axis=0)
    nxt = jnp.concatenate([p[SUBLANES:], halo_next], axis=0)
    p = p + mu_ref[...] * (0.5 * (prev + nxt) - p)

    w = width
    r, k, v = p[:, 0:w], p[:, w:2 * w], p[:, 2 * w:3 * w]
    wa = p[:, 3 * w:3 * w + LANES]
    gd = p[:, 3 * w + LANES:3 * w + 2 * LANES]
    ones_bd = ones_ref[...]

    g_out[...] = _mm(_sigmoid(gd), gup_ref[...])
    kk = k * kk_ref[...]
    kk = kk * lax.rsqrt(jnp.maximum(_segsum(kk * kk, ones_bd), KK_EPS))
    r_out[...] = r
    v_out[...] = v
    kkn_out[...] = kk
    tanh_wa = jnp.tanh(wa)
    rk_sum = jnp.zeros_like(r)
    for d in range(2):
        w_pre = w0_ref[d] + _mm(tanh_wa, wup_ref[d])
        lw_out[d] = -jnp.exp(-_softplus(-w_pre) - 0.5)
        a = _sigmoid(a0_ref[d] + _mm(wa, aup_ref[d]))
        k_dir = k * (1.0 + (a - 1.0) * ka_ref[...])
        kd_out[d] = k_dir
        bv_out[d] = kk * a
        rk_sum = rk_sum + r * k_dir
    bonus_out[...] = _segsum(rk_sum * rk_ref[...], ones_bd) * v


def _rwkv_prepare(pa, prm, n_ctx_rows):
    rows, cols = pa.shape
    width = prm["k_k"].shape[-1]
    tm = ROW_TILE
    hb = tm // SUBLANES
    last = rows // SUBLANES - 1
    full = lambda a: pl.BlockSpec(a.shape, lambda i: (0,) * a.ndim)
    row = lambda n: pl.BlockSpec((tm, n), lambda i: (i, 0))
    row2 = lambda n: pl.BlockSpec((2, tm, n), lambda i: (0, i, 0))
    consts = [prm["mu"], prm["w0"], prm["w_up"], prm["a0"], prm["a_up"], prm["g_up"],
              prm["k_k"], prm["k_a"], prm["r_k"], prm["ones_bd"]]
    one = jax.ShapeDtypeStruct((rows, width), F32)
    two = jax.ShapeDtypeStruct((2, rows, width), F32)
    return pl.pallas_call(
        functools.partial(_rwkv_prep_kernel, width, n_ctx_rows, rows),
        out_shape=[one, one, one, one, one, two, two, two],
        grid=(rows // tm,),
        in_specs=[row(cols),
                  pl.BlockSpec((SUBLANES, cols), lambda i: (jnp.maximum(i * hb - 1, 0), 0)),
                  pl.BlockSpec((SUBLANES, cols), lambda i: (jnp.minimum((i + 1) * hb, last), 0))]
                 + [full(a) for a in consts],
        out_specs=[row(width)] * 5 + [row2(width)] * 3,
        compiler_params=_cparams("parallel"),
        name="rwkv_prepare",
    )(pa, pa, pa, *consts)


def _scan_block(d, c, n_ctx_blk, n_blk):
    bwd = jnp.where(c < n_ctx_blk, n_ctx_blk - 1 - c, n_ctx_blk + (n_blk - 1 - c))
    return jnp.where(d == 0, c, bwd)


def _tri_inverse(a_list, ri, ci, n):
    eye = (ri == ci).astype(F32)
    same = (ri // SUBLANES) == (ci // SUBLANES)
    xs = [-jnp.where(same, a, 0.0) for a in a_list]
    x2 = [_mm(x, x) for x in xs]
    x4 = [_mm(x, x) for x in x2]
    ts = [eye + x for x in xs]
    ts = [t + _mm(t, x) for t, x in zip(ts, x2)]
    ts = [t + _mm(t, x) for t, x in zip(ts, x4)]
    size = SUBLANES
    while size < n:
        pick = jnp.logical_and((ri // (2 * size)) == (ci // (2 * size)), (ri // size) != (ci // size))
        lows = [_mm(jnp.where(pick, a, 0.0), t) for a, t in zip(a_list, ts)]
        ts = [t - _mm(t, low) for t, low in zip(ts, lows)]
        size *= 2
    return ts


def _rwkv_scan_kernel(n_batch, r_ref, v_ref, kk_ref, lw_ref, kd_ref, bv_ref, y_ref, state_ref):
    d = pl.program_id(0)
    c = pl.program_id(2)
    n = r_ref.shape[0] // n_batch
    m = 2 * n
    batches = range(n_batch)

    @pl.when(c == 0)
    def _():
        state_ref[...] = jnp.zeros_like(state_ref)

    sign = 1 - 2 * d
    ri = lax.broadcasted_iota(jnp.int32, (m, m), 0)
    ci = lax.broadcasted_iota(jnp.int32, (m, m), 1)
    same_head = (ri // n) == (ci // n)
    order = (ri - ci) * sign
    strict = jnp.logical_and(same_head, order > 0)
    incl = jnp.logical_and(same_head, order >= 0)
    ti = lax.broadcasted_iota(jnp.int32, (n, n), 0)
    si = lax.broadcasted_iota(jnp.int32, (n, n), 1)
    tri = ((ti - si) * sign >= 0).astype(BF16)
    head0 = lax.broadcasted_iota(jnp.int32, (n, LANES), 1) < HEAD_DIM
    blockdiag = (lax.broadcasted_iota(jnp.int32, (LANES, LANES), 0) // HEAD_DIM
                 == lax.broadcasted_iota(jnp.int32, (LANES, LANES), 1) // HEAD_DIM)

    def stack(x):
        return jnp.concatenate([jnp.where(head0, x, 0.0), jnp.where(head0, 0.0, x)], axis=0)

    def rows(b):
        return pl.ds(b, n, stride=n_batch)

    lw = [lw_ref[0, rows(b), :] for b in batches]
    hi = [x.astype(BF16) for x in lw]
    lo = [(x - h.astype(F32)).astype(BF16) for x, h in zip(lw, hi)]
    cum = [jnp.dot(tri, h, preferred_element_type=F32) + jnp.dot(tri, l, preferred_element_type=F32)
           for h, l in zip(hi, lo)]
    e_cum = [jnp.exp(x) for x in cum]
    e_neg = [jnp.exp(-x) for x in cum]
    kap = [kk_ref[rows(b), :] * jnp.exp(cum[b] - lw[b]) for b in batches]
    rt = [r_ref[rows(b), :] * e_cum[b] for b in batches]
    bt = [bv_ref[0, rows(b), :] * e_neg[b] for b in batches]
    kt = [kd_ref[0, rows(b), :] * e_neg[b] for b in batches]
    v = [v_ref[rows(b), :] for b in batches]
    total = [jnp.where(d == 0, e[n - 1:n], e[0:1]) for e in e_cum]
    state = [state_ref[b] for b in batches]

    scores = [_mm_nt(jnp.concatenate([stack(kap[b]), stack(rt[b])], axis=0),
                     jnp.concatenate([stack(bt[b]), stack(kt[b])], axis=0)) for b in batches]
    a_b = [jnp.where(strict, s[0:m, 0:m], 0.0) for s in scores]
    a_k = [jnp.where(strict, s[0:m, m:2 * m], 0.0) for s in scores]
    a_r = [jnp.concatenate([jnp.where(incl, s[m:2 * m, m:2 * m], 0.0),
                            jnp.where(incl, -s[m:2 * m, 0:m], 0.0)], axis=1) for s in scores]
    g0 = [_mm_nt(jnp.concatenate([kap[b], rt[b]], axis=0), state[b]) for b in batches]
    t_inv = _tri_inverse(a_b, ri, ci, n)
    vs = [stack(x) for x in v]
    rhs = [stack(g0[b][0:n]) + _mm(a_k[b], vs[b]) for b in batches]
    us = [_mm(t_inv[b], rhs[b]) for b in batches]
    ys = [stack(g0[b][n:m]) + _mm(a_r[b], jnp.concatenate([vs[b], us[b]], axis=0)) for b in batches]
    for b in batches:
        y_ref[0, rows(b), :] = ys[b][0:n] + ys[b][n:m]
    ds = [_mm_tn(jnp.concatenate([-(us[b][0:n] + us[b][n:m]), v[b]], axis=0),
                 jnp.concatenate([bt[b], kt[b]], axis=0)) for b in batches]
    for b in batches:
        state_ref[b] = (state[b] + jnp.where(blockdiag, ds[b], 0.0)) * total[b]


def _rwkv_scan(r, v, kk, lw, kd, bv, n_batch, n_ctx):
    rows, width = r.shape
    tm = SCAN_CHUNK * n_batch
    n_blk, n_ctx_blk = rows // tm, n_ctx * n_batch // tm
    blk = lambda d, c: _scan_block(d, c, n_ctx_blk, n_blk)
    shared = pl.BlockSpec((tm, LANES), lambda d, p, c: (blk(d, c), p))
    split = pl.BlockSpec((1, tm, LANES), lambda d, p, c: (d, blk(d, c), p))
    return pl.pallas_call(
        functools.partial(_rwkv_scan_kernel, n_batch),
        out_shape=jax.ShapeDtypeStruct((2, rows, width), F32),
        grid=(2, width // LANES, n_blk),
        in_specs=[shared, shared, shared, split, split, split],
        out_specs=split,
        scratch_shapes=[pltpu.VMEM((n_batch, LANES, LANES), F32)],
        compiler_params=_cparams("parallel", "parallel", "arbitrary"),
        name="rwkv_scan",
    )(r, v, kk, lw, kd, bv)


def _rwkv_readout_kernel(y_ref, g_ref, bonus_ref, gnw_ref, gnb_ref, ones_ref, o_ref):
    y = y_ref[0] + y_ref[1]
    ones_bd = ones_ref[...]
    inv = 1.0 / HEAD_DIM
    mean = _segsum(y, ones_bd) * inv
    yc = y - mean
    var = _segsum(yc * yc, ones_bd) * inv
    yn = yc * lax.rsqrt(var + GN_EPS) * gnw_ref[...] + gnb_ref[...]
    o_ref[...] = (yn + bonus_ref[...]) * g_ref[...]


def _rwkv_readout(ys, g, bonus, gn_w, gn_b, ones_bd):
    rows, width = g.shape
    tm = ROW_TILE
    row = pl.BlockSpec((tm, width), lambda i: (i, 0))
    vec = pl.BlockSpec((1, width), lambda i: (0, 0))
    return pl.pallas_call(
        _rwkv_readout_kernel,
        out_shape=jax.ShapeDtypeStruct((rows, width), F32),
        grid=(rows // tm,),
        in_specs=[pl.BlockSpec((2, tm, width), lambda i: (0, i, 0)), row, row, vec, vec,
                  pl.BlockSpec(ones_bd.shape, lambda i: (0, 0))],
        out_specs=row,
        compiler_params=_cparams("parallel"),
        name="rwkv_readout",
    )(ys, g, bonus, gn_w, gn_b, ones_bd)


def _norm_rope(x, gain, cos, sin_a, sin_b, ones_bd):
    ms = _segsum(x * x, ones_bd) * (1.0 / HEAD_DIM)
    xn = x * lax.rsqrt(ms + RMS_EPS) * gain
    return (xn * cos + pltpu.roll(xn, LANES - HEAD_DIM // 4, 1) * sin_a
            + pltpu.roll(xn, HEAD_DIM // 4, 1) * sin_b)


def _attn_prep_kernel(scale, n_batch, q_ref, k_ref, v_ref, cos_ref, sa_ref, sb_ref, qg_ref, kg_ref, ones_ref,
                      qo_ref, ko_ref, vo_ref):
    cos, sin_a, sin_b = cos_ref[...], sa_ref[...], sb_ref[...]
    ones_bd = ones_ref[...]
    tt = cos.shape[0]
    first = lax.broadcasted_iota(jnp.int32, (tt, LANES), 1) < HEAD_DIM
    for b in range(n_batch):
        rows = pl.ds(b, tt, stride=n_batch)
        for j in range(q_ref.shape[0]):
            y = _norm_rope(q_ref[j, rows, :], qg_ref[...], cos, sin_a, sin_b, ones_bd) * scale
            qo_ref[b, 2 * j] = y[:, :HEAD_DIM].astype(BF16)
            qo_ref[b, 2 * j + 1] = y[:, HEAD_DIM:].astype(BF16)
        for j in range(k_ref.shape[0]):
            y = _norm_rope(k_ref[j, rows, :], kg_ref[...], cos, sin_a, sin_b, ones_bd)
            ko_ref[b, 2 * j] = y[:, :HEAD_DIM].astype(BF16)
            ko_ref[b, 2 * j + 1] = y[:, HEAD_DIM:].astype(BF16)
            vv = v_ref[j, rows, :]
            vo_ref[b, 2 * j] = jnp.where(first, vv, 1.0).astype(BF16)
            vo_ref[b, 2 * j + 1] = jnp.where(first, pltpu.roll(vv, HEAD_DIM, 1), 1.0).astype(BF16)


def _attn_prepare(pq, pk, pv, cos, sin_a, sin_b, q_gain, k_gain, ones_pair, n_batch):
    q_slabs, rows, _ = pq.shape
    k_slabs = pk.shape[0]
    tt_all = rows // n_batch
    tt = math.gcd(PREP_TT, tt_all)
    tm = tt * n_batch
    qh, kh = 2 * q_slabs, 2 * k_slabs
    tab = pl.BlockSpec((tt, LANES), lambda i: (i, 0))
    vec = pl.BlockSpec((1, LANES), lambda i: (0, 0))
    return pl.pallas_call(
        functools.partial(_attn_prep_kernel, HEAD_DIM ** -0.5, n_batch),
        out_shape=[jax.ShapeDtypeStruct((n_batch, qh, tt_all, HEAD_DIM), BF16),
                   jax.ShapeDtypeStruct((n_batch, kh, tt_all, HEAD_DIM), BF16),
                   jax.ShapeDtypeStruct((n_batch, kh, tt_all, LANES), BF16)],
        grid=(tt_all // tt,),
        in_specs=[pl.BlockSpec((q_slabs, tm, LANES), lambda i: (0, i, 0)),
                  pl.BlockSpec((k_slabs, tm, LANES), lambda i: (0, i, 0)),
                  pl.BlockSpec((k_slabs, tm, LANES), lambda i: (0, i, 0)),
                  tab, tab, tab, vec, vec,
                  pl.BlockSpec(ones_pair.shape, lambda i: (0, 0))],
        out_specs=[pl.BlockSpec((n_batch, qh, tt, HEAD_DIM), lambda i: (0, 0, i, 0)),
                   pl.BlockSpec((n_batch, kh, tt, HEAD_DIM), lambda i: (0, 0, i, 0)),
                   pl.BlockSpec((n_batch, kh, tt, LANES), lambda i: (0, 0, i, 0))],
        compiler_params=_cparams("parallel"),
        name="attn_prepare",
    )(pq, pk, pv, cos, sin_a, sin_b, q_gain, k_gain, ones_pair)


def _flash_kernel(n_ctx, tk, q_ref, k_ref, v_ref, o_ref, m_ref, acc_ref):
    qi = pl.program_id(2)
    g, tq, dh = q_ref.shape[1], q_ref.shape[2], q_ref.shape[3]
    n_keys = k_ref.shape[2]
    q = q_ref[0].reshape(g * tq, dh)

    def scores(start, size):
        k = k_ref[0, 0, pl.ds(start, size), :]
        s = lax.dot_general(q, k, (((1,), (1,)), ((), ())), preferred_element_type=F32)
        blocks = [s[:, j * LANES:(j + 1) * LANES] for j in range(size // LANES)]
        m_blk = functools.reduce(jnp.maximum, blocks)
        m_row = jnp.broadcast_to(jnp.max(m_blk, axis=-1, keepdims=True), (g * tq, LANES))
        return blocks, m_row

    def probs(blocks, m_new):
        return jnp.concatenate([jnp.exp(blk - m_new).astype(BF16) for blk in blocks], axis=1)

    blocks, m_new = scores(0, n_ctx)
    m_ref[...] = m_new
    acc_ref[...] = jnp.dot(probs(blocks, m_new), v_ref[0, 0, pl.ds(0, n_ctx), :], preferred_element_type=F32)

    def body(j, carry):
        start = pl.multiple_of(n_ctx + j * tk, tk)
        blocks, m_row = scores(start, tk)
        m_prev = m_ref[...]
        m_new = jnp.maximum(m_prev, m_row)
        pv = jnp.dot(probs(blocks, m_new), v_ref[0, 0, pl.ds(start, tk), :], preferred_element_type=F32)
        acc_ref[...] = jnp.exp(m_prev - m_new) * acc_ref[...] + pv
        m_ref[...] = m_new
        return carry

    n_big = jnp.where(qi * tq >= n_ctx, (n_keys - n_ctx) // tk, 0)
    lax.fori_loop(0, n_big, body, 0)

    acc = acc_ref[...]
    o = acc / pltpu.roll(acc, HEAD_DIM, 1)
    first = lax.broadcasted_iota(jnp.int32, (tq, LANES), 1) < HEAD_DIM
    for h in range(0, g, 2):
        o_ref[0, h // 2] = jnp.where(first, o[h * tq:(h + 1) * tq],
                                     pltpu.roll(o[(h + 1) * tq:(h + 2) * tq], HEAD_DIM, 1))


def _flash_attention(qh, kh, vh, n_ctx):
    n_batch, n_qh, tt_all, dh = qh.shape
    n_kvh = kh.shape[1]
    g = n_qh // n_kvh
    tq = min(ATTN_TQ, n_ctx)
    tk = min(ATTN_TK, tt_all - n_ctx)
    assert n_ctx % tq == 0 and (tt_all - n_ctx) % tk == 0 and n_ctx % LANES == 0
    return pl.pallas_call(
        functools.partial(_flash_kernel, n_ctx, tk),
        out_shape=jax.ShapeDtypeStruct((n_batch, n_qh * dh // LANES, tt_all, LANES), F32),
        grid=(n_batch, n_kvh, tt_all // tq),
        in_specs=[pl.BlockSpec((1, g, tq, dh), lambda b, h, qi: (b, h, qi, 0)),
                  pl.BlockSpec((1, 1, tt_all, dh), lambda b, h, qi: (b, h, 0, 0)),
                  pl.BlockSpec((1, 1, tt_all, LANES), lambda b, h, qi: (b, h, 0, 0))],
        out_specs=pl.BlockSpec((1, g * dh // LANES, tq, LANES), lambda b, h, qi: (b, h, qi, 0)),
        scratch_shapes=[pltpu.VMEM((g * tq, LANES), F32), pltpu.VMEM((g * tq, LANES), F32)],
        compiler_params=_cparams("parallel", "parallel", "arbitrary"),
        name="flash_attention",
    )(qh, kh, vh)


def _s5_param_kernel(are_ref, aim_ref, ldt_ref, bre_ref, bim_ref, abar_re, abar_im, bf_re, bf_im):
    lam_re = jnp.minimum(are_ref[0], LAMBDA_RE_MAX)
    lam_im = aim_ref[0]
    dt = jnp.exp(ldt_ref[0])
    mag = jnp.exp(lam_re * dt)
    ar, ai = mag * jnp.cos(lam_im * dt), mag * jnp.sin(lam_im * dt)
    nr, ni = ar - 1.0, ai
    den = lam_re * lam_re + lam_im * lam_im
    cr = (nr * lam_re + ni * lam_im) / den
    cim = (ni * lam_re - nr * lam_im) / den
    abar_re[0], abar_im[0] = ar, ai
    bre, bim = bre_ref[...], bim_ref[...]
    bf_re[0] = cr * bre - cim * bim
    bf_im[0] = cr * bim + cim * bre


def _s5_params(a_re, a_im, log_dt, b_re, b_im):
    _, n_g, n_s = a_re.shape
    n_i = b_re.shape[-1]
    gn = n_g * n_s
    flat = lambda a: a.reshape(2, 1, gn)
    dt = jnp.broadcast_to(log_dt[:, :, None], (2, n_g, n_s))
    bt = lambda b: b.reshape(gn, n_i).T
    vec = pl.BlockSpec((1, 1, gn), lambda d: (d, 0, 0))
    mat = pl.BlockSpec((n_i, gn), lambda d: (0, 0))
    omat = pl.BlockSpec((1, n_i, gn), lambda d: (d, 0, 0))
    return pl.pallas_call(
        _s5_param_kernel,
        out_shape=[jax.ShapeDtypeStruct((2, 1, gn), F32)] * 2 + [jax.ShapeDtypeStruct((2, n_i, gn), F32)] * 2,
        grid=(2,),
        in_specs=[vec, vec, vec, mat, mat],
        out_specs=[vec, vec, omat, omat],
        compiler_params=_cparams("parallel"),
        name="s5_params",
    )(flat(a_re), flat(a_im), flat(dt), bt(b_re), bt(b_im))


def _s5_scan_kernel(u_ref, ar_ref, ai_ref, bre_ref, bim_ref, cre_ref, cim_ref, y_ref,
                    xr_ref, xi_ref, hr_ref, hi_ref):
    d = pl.program_id(0)
    c = pl.program_id(1)
    steps = u_ref.shape[0] // SUBLANES
    width = xr_ref.shape[1]

    @pl.when(c == 0)
    def _():
        hr_ref[...] = jnp.zeros_like(hr_ref)
        hi_ref[...] = jnp.zeros_like(hi_ref)

    u = u_ref[...].astype(BF16)
    xr_ref[...] = jnp.dot(u, bre_ref[0], preferred_element_type=F32)
    xi_ref[...] = jnp.dot(u, bim_ref[0], preferred_element_type=F32)

    for j in range(width // SSM_LANE_TILE):
        sl = pl.ds(j * SSM_LANE_TILE, SSM_LANE_TILE)
        ar = jnp.broadcast_to(ar_ref[0, :, sl], (SUBLANES, SSM_LANE_TILE))
        ai = jnp.broadcast_to(ai_ref[0, :, sl], (SUBLANES, SSM_LANE_TILE))

        def step(i, carry):
            hr, hi = carry
            t = jnp.where(d == 0, i, steps - 1 - i)
            rows = pl.ds(pl.multiple_of(t * SUBLANES, SUBLANES), SUBLANES)
            nr = ar * hr - ai * hi + xr_ref[rows, sl]
            ni = ar * hi + ai * hr + xi_ref[rows, sl]
            xr_ref[rows, sl] = nr
            xi_ref[rows, sl] = ni
            return nr, ni

        hr, hi = lax.fori_loop(0, steps, step, (hr_ref[:, sl], hi_ref[:, sl]), unroll=8)
        hr_ref[:, sl] = hr
        hi_ref[:, sl] = hi

    y_ref[0] = (jnp.dot(xr_ref[...].astype(BF16), cre_ref[...], preferred_element_type=F32)
                - jnp.dot(xi_ref[...].astype(BF16), cim_ref[...], preferred_element_type=F32))


def _s5_scan(u, abar_re, abar_im, b_re, b_im, c_re, c_im, n_batch, n_ctx):
    rows, width = u.shape
    gn = abar_re.shape[-1]
    tm = SSM_CHUNK * n_batch
    n_blk, n_ctx_blk = rows // tm, n_ctx * n_batch // tm
    blk = lambda d, c: _scan_block(d, c, n_ctx_blk, n_blk)
    vec = pl.BlockSpec((1, 1, gn), lambda d, c: (d, 0, 0))
    bmat = pl.BlockSpec((1, width, gn), lambda d, c: (d, 0, 0))
    cmat = pl.BlockSpec((gn, width), lambda d, c: (0, 0))
    return pl.pallas_call(
        _s5_scan_kernel,
        out_shape=jax.ShapeDtypeStruct((2, rows, width), F32),
        grid=(2, n_blk),
        in_specs=[pl.BlockSpec((tm, width), lambda d, c: (blk(d, c), 0)), vec, vec, bmat, bmat, cmat, cmat],
        out_specs=pl.BlockSpec((1, tm, width), lambda d, c: (d, blk(d, c), 0)),
        scratch_shapes=[pltpu.VMEM((tm, gn), F32), pltpu.VMEM((tm, gn), F32),
                        pltpu.VMEM((SUBLANES, gn), F32), pltpu.VMEM((SUBLANES, gn), F32)],
        compiler_params=_cparams("parallel", "arbitrary"),
        name="s5_scan",
    )(u, abar_re, abar_im, b_re, b_im, c_re, c_im)


def _gelu_tanh(x):
    return 0.5 * x * (1.0 + jnp.tanh(math.sqrt(2.0 / math.pi) * (x + 0.044715 * (x * x * x))))


def _s5_readout_kernel(y_ref, u_ref, d_ref, w_ref, b_ref, o_ref):
    y = y_ref[0] + y_ref[1] + d_ref[...] * u_ref[...]
    y = _gelu_tanh(y)
    o_ref[...] = y * _sigmoid(jnp.dot(y.astype(BF16), w_ref[...], preferred_element_type=F32) + b_ref[...])


def _s5_readout(ys, u, d_skip, glu_w, glu_b):
    rows, width = u.shape
    tm = ROW_TILE
    row = pl.BlockSpec((tm, width), lambda i: (i, 0))
    vec = pl.BlockSpec((1, width), lambda i: (0, 0))
    return pl.pallas_call(
        _s5_readout_kernel,
        out_shape=jax.ShapeDtypeStruct((rows, width), F32),
        grid=(rows // tm,),
        in_specs=[pl.BlockSpec((2, tm, width), lambda i: (0, i, 0)), row, vec,
                  pl.BlockSpec(glu_w.shape, lambda i: (0, 0)), vec],
        out_specs=row,
        compiler_params=_cparams("parallel"),
        name="s5_readout",
    )(ys, u, d_skip, glu_w, glu_b)


def _merge_kernel(alpha, ya_ref, yb_ref, yc_ref, pg_ref, x_ref, gt_ref, pa_ref, pb_ref, pc_ref, wo_ref,
                  g_ref, b_ref, o_ref, yb_rows):
    d = x_ref.shape[1]
    n_batch, n_slabs, tt = yb_ref.shape[0], yb_ref.shape[1], yb_ref.shape[2]
    for b in range(n_batch):
        for j in range(n_slabs):
            yb_rows[j, pl.ds(b, tt, stride=n_batch), :] = yb_ref[b, j]
    yb = jnp.concatenate([yb_rows[j].astype(BF16) for j in range(n_slabs)], axis=1)
    pg = pg_ref[...]
    merged = (_sigmoid(pg[:, 0:d]) * _mm(ya_ref[...], pa_ref[...])
              + _sigmoid(pg[:, d:2 * d]) * _mm(yb, pb_ref[...])
              + _sigmoid(pg[:, 2 * d:3 * d]) * _mm(yc_ref[...], pc_ref[...]))
    mix = _mm(merged, wo_ref[...])
    o_ref[...] = _layer_norm(alpha * x_ref[...] + _rows_mod(mix, gt_ref[0]), g_ref[...], b_ref[...])


def _merge(ya, yb, yc, pg, x_all, modtab, proj_a, proj_b, proj_c, w_out, ln_g, ln_b, alpha, n_ctx_rows):
    rows, d = x_all.shape
    n_batch, n_slabs = yb.shape[0], yb.shape[1]
    tm = ROW_TILE
    nctx = n_ctx_rows // tm
    seg = lambda i: (i >= nctx).astype(jnp.int32)
    row = lambda a: pl.BlockSpec((tm, a.shape[1]), lambda i: (i, 0))
    full = lambda a: pl.BlockSpec(a.shape, lambda i: (0, 0), pipeline_mode=pl.Buffered(1))
    vec = pl.BlockSpec((1, d), lambda i: (0, 0))
    return pl.pallas_call(
        functools.partial(_merge_kernel, alpha),
        out_shape=jax.ShapeDtypeStruct((rows, d), F32),
        grid=(rows // tm,),
        scratch_shapes=[pltpu.VMEM((n_slabs, tm, LANES), F32)],
        in_specs=[row(ya), pl.BlockSpec((n_batch, n_slabs, tm // n_batch, LANES), lambda i: (0, 0, i, 0)),
                  row(yc), row(pg), row(x_all),
                  pl.BlockSpec((1, SUBLANES, d), lambda i: (seg(i), 0, 2)),
                  full(proj_a), full(proj_b), full(proj_c), full(w_out), vec, vec],
        out_specs=pl.BlockSpec((tm, d), lambda i: (i, 0)),
        compiler_params=_cparams("parallel"),
        name="merge_ln",
    )(ya, yb, yc, pg, x_all, modtab, proj_a, proj_b, proj_c, w_out, ln_g, ln_b)


def _mlp_kernel(alpha, x_ref, sh_ref, sc_ref, gt_ref, w1_ref, w2_ref, g_ref, b_ref, o_ref, h_ref, acc_ref):
    j = pl.program_id(1)

    @pl.when(j == 0)
    def _():
        h_ref[...] = _rows_add(_rows_mod(x_ref[...], 1.0 + sc_ref[0]), sh_ref[0]).astype(BF16)
        acc_ref[...] = jnp.zeros_like(acc_ref)

    a = jnp.maximum(jnp.dot(h_ref[...], w1_ref[...], preferred_element_type=F32), 0.0)
    acc_ref[...] += jnp.dot((a * a).astype(BF16), w2_ref[...], preferred_element_type=F32)

    @pl.when(j == pl.num_programs(1) - 1)
    def _():
        o_ref[...] = _layer_norm(alpha * x_ref[...] + _rows_mod(acc_ref[...], gt_ref[0]),
                                 g_ref[...], b_ref[...])


def _mlp(x_mid, modtab, w1, w2, ln_g, ln_b, alpha, n_ctx_rows):
    rows, d = x_mid.shape
    ff = w1.shape[1]
    tm = min(MLP_ROW_TILE, n_ctx_rows)
    tf = min(MLP_FF_TILE, ff)
    nctx = n_ctx_rows // tm
    seg = lambda i: (i >= nctx).astype(jnp.int32)
    mod = lambda col: pl.BlockSpec((1, SUBLANES, d), lambda i, j: (seg(i), 0, col))
    vec = pl.BlockSpec((1, d), lambda i, j: (0, 0))
    return pl.pallas_call(
        functools.partial(_mlp_kernel, alpha),
        out_shape=jax.ShapeDtypeStruct((rows, d), F32),
        grid=(rows // tm, ff // tf),
        in_specs=[pl.BlockSpec((tm, d), lambda i, j: (i, 0)), mod(3), mod(4), mod(5),
                  pl.BlockSpec((d, tf), lambda i, j: (0, j)),
                  pl.BlockSpec((tf, d), lambda i, j: (j, 0)), vec, vec],
        out_specs=pl.BlockSpec((tm, d), lambda i, j: (i, 0)),
        scratch_shapes=[pltpu.VMEM((tm, d), BF16), pltpu.VMEM((tm, d), F32)],
        compiler_params=_cparams("parallel", "arbitrary"),
        name="mlp_ln",
    )(x_mid, modtab, modtab, modtab, w1, w2, ln_g, ln_b)


def _block_diag_ones(n, blk):
    i = jnp.arange(n) // blk
    return (i[:, None] == i[None, :]).astype(BF16)


def _rope_tables(n_ctx, n_lat):
    pairs = HEAD_DIM // 4
    rows = n_lat // GRID_W
    row = jnp.repeat(jnp.arange(rows, dtype=F32), GRID_W)
    col = jnp.tile(jnp.arange(GRID_W, dtype=F32), rows)
    inv = ROPE_THETA ** (-jnp.arange(pairs, dtype=F32) / pairs)
    ang = jnp.stack([row, col], axis=-1)[:, :, None] * inv
    ang = jnp.broadcast_to(ang[:, :, None, :], (n_lat, 2, 2, pairs)).reshape(n_lat, HEAD_DIM)
    cos = jnp.concatenate([jnp.ones((n_ctx, HEAD_DIM), F32), jnp.cos(ang)], axis=0)
    sin = jnp.concatenate([jnp.zeros((n_ctx, HEAD_DIM), F32), jnp.sin(ang)], axis=0)
    first = (jnp.arange(HEAD_DIM) % (2 * pairs)) < pairs
    sin_a = jnp.where(first, -sin, 0.0)
    sin_b = jnp.where(first, 0.0, sin)
    two = lambda a: jnp.concatenate([a, a], axis=1)
    return two(cos), two(sin_a), two(sin_b)


def _block_diag(blocks):
    g, a, b = blocks.shape
    eye = jnp.eye(g, dtype=blocks.dtype)
    return (eye[:, None, :, None] * blocks[:, :, None, :]).reshape(g * a, g * b)


def kernel(x, c, ctx, c_ctx, mod_w, mod_b, w_in, rwkv_mu, rwkv_w0, rwkv_w_up, rwkv_a0, rwkv_a_up, rwkv_g_up, rwkv_k_k, rwkv_k_a, rwkv_r_k, rwkv_gn_w, rwkv_gn_b, attn_q_gain, attn_k_gain, ssm_a_re, ssm_a_im, ssm_log_dt, ssm_b_re, ssm_b_im, ssm_c_re, ssm_c_im, ssm_d, ssm_glu_w, ssm_glu_b, proj_a, proj_b, proj_c, w_out, ln1_g, ln1_b, ln2_g, ln2_b, mlp_w1, mlp_w2):
    n_batch, n_lat, d_model = x.shape
    n_ctx = ctx.shape[1]
    depth = mod_w.shape[0]
    assert n_batch == SUBLANES, "row layout packs the batch into one sublane tile"
    alpha = (2 * depth) ** 0.25
    n_ctx_rows = n_ctx * n_batch

    rwkv_w = rwkv_k_k.shape[1]
    q_w = proj_b.shape[1]
    kv_w = q_w // GQA_GROUP
    ssm_w = ssm_d.shape[1]
    rwkv_cols = rwkv_mu.shape[1]
    edges = [0, rwkv_cols, rwkv_cols + q_w, rwkv_cols + q_w + kv_w, rwkv_cols + q_w + 2 * kv_w,
             rwkv_cols + q_w + 2 * kv_w + ssm_w, w_in.shape[2]]
    slabbed = (False, True, True, True, False, False)
    splits = tuple(zip(edges[:-1], edges[1:], slabbed))

    to_rows = lambda a: a.transpose(1, 0, 2).reshape(a.shape[1] * n_batch, d_model)
    x_all = jnp.concatenate([to_rows(ctx), to_rows(x)], axis=0)

    cvec = jnp.concatenate([jnp.broadcast_to(c_ctx[None], (n_batch, d_model)), c], axis=0)
    modtab = _modulation(cvec, mod_w, mod_b).reshape(depth, 2, n_batch, 6 * d_model)

    ones_rwkv = _block_diag_ones(rwkv_w, HEAD_DIM)
    ones_pair = _block_diag_ones(LANES, HEAD_DIM)
    cos, sin_a, sin_b = _rope_tables(n_ctx, n_lat)
    rank = rwkv_w_up.shape[2]
    zpad = jnp.zeros((depth, 2, rank, rwkv_w), F32)

    for l in range(depth):
        mt = modtab[l]
        pa, pq, pk, pv, pc, pg = _in_projection(x_all, mt, w_in[l].astype(BF16), splits, n_ctx_rows)

        prm = dict(mu=rwkv_mu[l][None], w0=rwkv_w0[l][:, None], a0=rwkv_a0[l][:, None],
                   w_up=jnp.concatenate([rwkv_w_up[l], zpad[l]], axis=1).astype(BF16),
                   a_up=jnp.concatenate([zpad[l], rwkv_a_up[l]], axis=1).astype(BF16),
                   g_up=rwkv_g_up[l].astype(BF16), k_k=rwkv_k_k[l][None], k_a=rwkv_k_a[l][None],
                   r_k=rwkv_r_k[l][None], ones_bd=ones_rwkv)
        r, v, kk, g, bonus, lw, kd, bv = _rwkv_prepare(pa, prm, n_ctx_rows)
        ys = _rwkv_scan(r, v, kk, lw, kd, bv, n_batch, n_ctx)
        ya = _rwkv_readout(ys, g, bonus, rwkv_gn_w[l][None], rwkv_gn_b[l][None], ones_rwkv)

        tile2 = lambda a: jnp.tile(a, 2)[None]
        qh, kh, vh = _attn_prepare(pq, pk, pv, cos, sin_a, sin_b, tile2(attn_q_gain[l]), tile2(attn_k_gain[l]),
                                   ones_pair, n_batch)
        yb = _flash_attention(qh, kh, vh, n_ctx)

        abar_re, abar_im, bf_re, bf_im = _s5_params(ssm_a_re[l], ssm_a_im[l], ssm_log_dt[l], ssm_b_re[l], ssm_b_im[l])
        n_g = ssm_a_re.shape[2]
        bmat = lambda bf: jnp.stack([_block_diag(bf[d].reshape(SSM_GROUP, n_g, SSM_STATE).transpose(1, 0, 2))
                                     for d in range(2)]).astype(BF16)
        cmat = lambda cc: _block_diag(cc.transpose(0, 2, 1)).astype(BF16)
        ysc = _s5_scan(pc, abar_re, abar_im, bmat(bf_re), bmat(bf_im), cmat(ssm_c_re[l]), cmat(ssm_c_im[l]),
                       n_batch, n_ctx)
        yc = _s5_readout(ysc, pc, ssm_d[l][None], ssm_glu_w[l].astype(BF16), ssm_glu_b[l][None])

        x_mid = _merge(ya, yb, yc, pg, x_all, mt, proj_a[l].astype(BF16), proj_b[l].astype(BF16),
                       proj_c[l].astype(BF16), w_out[l].astype(BF16), ln1_g[l][None], ln1_b[l][None],
                       alpha, n_ctx_rows)
        x_all = _mlp(x_mid, mt, mlp_w1[l].astype(BF16), mlp_w2[l].astype(BF16), ln2_g[l][None], ln2_b[l][None],
                     alpha, n_ctx_rows)

    out = x_all[n_ctx_rows:].reshape(n_lat, n_batch, d_model).transpose(1, 0, 2)
    return out
```

```python
import functools
import math

import jax
import jax.numpy as jnp
from jax import lax
from jax.experimental import pallas as pl
from jax.experimental.pallas import tpu as pltpu

HEAD_DIM = 64
GRID_W = 64
ROPE_THETA = 10000.0
LN_EPS = 1e-5
RMS_EPS = 1e-6
GN_EPS = 64e-5
KK_EPS = 1e-24
LAMBDA_RE_MAX = -1e-4
GQA_GROUP = 4
SSM_GROUP = 16
SSM_STATE = 64

LANES = 128
SUBLANES = 8
VMEM_LIMIT = 56 * 1024 * 1024

ROW_TILE = 256
MLP_ROW_TILE = 512
MLP_FF_TILE = 1024
SCAN_CHUNK = 64
SCAN_PAIRS = 2
SSM_CHUNK = 64
SSM_LANE_TILE = 512
ATTN_TQ = 256
ATTN_SUB = 256
PREP_TT = 128
VT_ROWS = 80

F32 = jnp.float32
BF16 = jnp.bfloat16


def _cparams(*sem):
    return pltpu.CompilerParams(dimension_semantics=sem, vmem_limit_bytes=VMEM_LIMIT)


def _mm(a, b):
    return jnp.dot(a.astype(BF16), b.astype(BF16), preferred_element_type=F32)


def _mm_nt(a, b):
    return lax.dot_general(a.astype(BF16), b.astype(BF16), (((1,), (1,)), ((), ())),
                           preferred_element_type=F32)


def _mm_tn(a, b):
    return lax.dot_general(a.astype(BF16), b.astype(BF16), (((0,), (0,)), ((), ())),
                           preferred_element_type=F32)


def _segsum(x, ones_bd):
    hi = x.astype(BF16)
    lo = (x - hi.astype(F32)).astype(BF16)
    return (jnp.dot(hi, ones_bd, preferred_element_type=F32)
            + jnp.dot(lo, ones_bd, preferred_element_type=F32))


def _sigmoid(x):
    return 1.0 / (1.0 + jnp.exp(-x))


def _softplus(x):
    return jnp.maximum(x, 0.0) + jnp.log(1.0 + jnp.exp(-jnp.abs(x)))


def _layer_norm(z, g, b):
    mu = jnp.mean(z, axis=-1, keepdims=True)
    zc = z - mu
    var = jnp.mean(zc * zc, axis=-1, keepdims=True)
    return zc * lax.rsqrt(var + LN_EPS) * g + b


def _rows_mod(x, m):
    tm, n = x.shape
    return (x.reshape(tm // SUBLANES, SUBLANES, n) * m[None]).reshape(tm, n)


def _rows_add(x, m):
    tm, n = x.shape
    return (x.reshape(tm // SUBLANES, SUBLANES, n) + m[None]).reshape(tm, n)


def _mod_kernel(c_ref, w_ref, b_ref, o_ref):
    c = c_ref[...]
    s = c * _sigmoid(c)
    o_ref[0] = jnp.dot(s, w_ref[0], preferred_element_type=F32,
                       precision=lax.Precision.HIGHEST) + b_ref[0]


def _modulation(cvec, mod_w, mod_b):
    n_layers, d, n6 = mod_w.shape
    tn = 1024 if n6 % 1024 == 0 else n6
    return pl.pallas_call(
        _mod_kernel,
        out_shape=jax.ShapeDtypeStruct((n_layers, cvec.shape[0], n6), F32),
        grid=(n_layers, n6 // tn),
        in_specs=[pl.BlockSpec((cvec.shape[0], d), lambda l, j: (0, 0)),
                  pl.BlockSpec((1, d, tn), lambda l, j: (l, 0, j)),
                  pl.BlockSpec((1, 1, tn), lambda l, j: (l, 0, j))],
        out_specs=pl.BlockSpec((1, cvec.shape[0], tn), lambda l, j: (l, 0, j)),
        compiler_params=_cparams("parallel", "parallel"),
        name="modulation",
    )(cvec, mod_w, mod_b.reshape(n_layers, 1, n6))


def _inproj_kernel(splits, x_ref, sh_ref, sc_ref, w_ref, *o_refs):
    xm = _rows_add(_rows_mod(x_ref[...], 1.0 + sc_ref[0]), sh_ref[0]).astype(BF16)
    for (a, b, slabbed), o_ref in zip(splits, o_refs):
        y = jnp.dot(xm, w_ref[:, a:b], preferred_element_type=F32)
        if slabbed:
            for j in range((b - a) // LANES):
                o_ref[j] = y[:, j * LANES:(j + 1) * LANES]
        else:
            o_ref[...] = y


def _in_projection(x_all, modtab, w_bf16, splits, n_ctx_rows):
    rows, d = x_all.shape
    tm = ROW_TILE
    nctx = n_ctx_rows // tm
    seg = lambda i: (i >= nctx).astype(jnp.int32)
    shapes, specs = [], []
    for a, b, slabbed in splits:
        if slabbed:
            shapes.append(jax.ShapeDtypeStruct(((b - a) // LANES, rows, LANES), F32))
            specs.append(pl.BlockSpec(((b - a) // LANES, tm, LANES), lambda i: (0, i, 0)))
        else:
            shapes.append(jax.ShapeDtypeStruct((rows, b - a), F32))
            specs.append(pl.BlockSpec((tm, b - a), lambda i: (i, 0)))
    return pl.pallas_call(
        functools.partial(_inproj_kernel, splits),
        out_shape=shapes,
        grid=(rows // tm,),
        in_specs=[pl.BlockSpec((tm, d), lambda i: (i, 0)),
                  pl.BlockSpec((1, SUBLANES, d), lambda i: (seg(i), 0, 0)),
                  pl.BlockSpec((1, SUBLANES, d), lambda i: (seg(i), 0, 1)),
                  pl.BlockSpec(w_bf16.shape, lambda i: (0, 0), pipeline_mode=pl.Buffered(1))],
        out_specs=specs,
        compiler_params=_cparams("parallel"),
        name="in_projection",
    )(x_all, modtab, modtab, w_bf16)


def _rwkv_prep_kernel(width, n_ctx_rows, n_rows,
                      p_ref, prev_ref, next_ref, mu_ref, w0_ref, wup_ref, a0_ref, aup_ref, gup_ref,
                      kk_ref, ka_ref, rk_ref, ones_ref,
                      r_out, v_out, kkn_out, g_out, bonus_out, lw_out, kd_out, bv_out):
    i = pl.program_id(0)
    tm = p_ref.shape[0]
    r0 = i * tm
    seg_start = jnp.logical_or(r0 == 0, r0 == n_ctx_rows)
    seg_end = jnp.logical_or(r0 + tm == n_ctx_rows, r0 + tm == n_rows)
    p = p_ref[...]
    halo_prev = jnp.where(seg_start, 0.0, prev_ref[...])
    halo_next = jnp.where(seg_end, 0.0, next_ref[...])
    prev = jnp.concatenate([halo_prev, p[:tm - SUBLANES]], axis=0)
    nxt = jnp.concatenate([p[SUBLANES:], halo_next], axis=0)
    p = p + mu_ref[...] * (0.5 * (prev + nxt) - p)

    w = width
    r, k, v = p[:, 0:w], p[:, w:2 * w], p[:, 2 * w:3 * w]
    wa = p[:, 3 * w:3 * w + LANES]
    gd = p[:, 3 * w + LANES:3 * w + 2 * LANES]
    ones_bd = ones_ref[...]

    def put(o_ref, y, *lead):
        for j in range(w // LANES):
            o_ref[lead + (j,)] = y[:, j * LANES:(j + 1) * LANES]

    g_out[...] = _mm(_sigmoid(gd), gup_ref[...])
    kk = k * kk_ref[...]
    kk = kk * lax.rsqrt(jnp.maximum(_segsum(kk * kk, ones_bd), KK_EPS))
    put(r_out, r)
    put(v_out, v)
    put(kkn_out, kk)
    tanh_wa = jnp.tanh(wa)
    rk_sum = jnp.zeros_like(r)
    for d in range(2):
        w_pre = w0_ref[d] + _mm(tanh_wa, wup_ref[d])
        put(lw_out, -jnp.exp(-_softplus(-w_pre) - 0.5), d)
        a = _sigmoid(a0_ref[d] + _mm(wa, aup_ref[d]))
        k_dir = k * (1.0 + (a - 1.0) * ka_ref[...])
        put(kd_out, k_dir, d)
        put(bv_out, kk * a, d)
        rk_sum = rk_sum + r * k_dir
    bonus_out[...] = _segsum(rk_sum * rk_ref[...], ones_bd) * v


def _rwkv_prepare(pa, prm, n_ctx_rows):
    rows, cols = pa.shape
    width = prm["k_k"].shape[-1]
    tm = ROW_TILE
    hb = tm // SUBLANES
    last = rows // SUBLANES - 1
    full = lambda a: pl.BlockSpec(a.shape, lambda i: (0,) * a.ndim)
    row = lambda n: pl.BlockSpec((tm, n), lambda i: (i, 0))
    n_slabs = width // LANES
    slab = pl.BlockSpec((n_slabs, tm, LANES), lambda i: (0, i, 0))
    slab2 = pl.BlockSpec((2, n_slabs, tm, LANES), lambda i: (0, 0, i, 0))
    consts = [prm["mu"], prm["w0"], prm["w_up"], prm["a0"], prm["a_up"], prm["g_up"],
              prm["k_k"], prm["k_a"], prm["r_k"], prm["ones_bd"]]
    flat = jax.ShapeDtypeStruct((rows, width), F32)
    one = jax.ShapeDtypeStruct((n_slabs, rows, LANES), F32)
    two = jax.ShapeDtypeStruct((2, n_slabs, rows, LANES), F32)
    return pl.pallas_call(
        functools.partial(_rwkv_prep_kernel, width, n_ctx_rows, rows),
        out_shape=[one, one, one, flat, flat, two, two, two],
        grid=(rows // tm,),
        in_specs=[row(cols),
                  pl.BlockSpec((SUBLANES, cols), lambda i: (jnp.maximum(i * hb - 1, 0), 0)),
                  pl.BlockSpec((SUBLANES, cols), lambda i: (jnp.minimum((i + 1) * hb, last), 0))]
                 + [full(a) for a in consts],
        out_specs=[slab] * 3 + [row(width)] * 2 + [slab2] * 3,
        compiler_params=_cparams("parallel"),
        name="rwkv_prepare",
    )(pa, pa, pa, *consts)


def _scan_block(d, c, n_ctx_blk, n_blk):
    bwd = jnp.where(c < n_ctx_blk, n_ctx_blk - 1 - c, n_ctx_blk + (n_blk - 1 - c))
    return jnp.where(d == 0, c, bwd)


def _tri_inverse(a_list, ri, ci, n):
    eye = (ri == ci).astype(F32)
    same = (ri // SUBLANES) == (ci // SUBLANES)
    xs = [-jnp.where(same, a, 0.0) for a in a_list]
    x2 = [_mm(x, x) for x in xs]
    x4 = [_mm(x, x) for x in x2]
    ts = [eye + x for x in xs]
    ts = [t + _mm(t, x) for t, x in zip(ts, x2)]
    ts = [t + _mm(t, x) for t, x in zip(ts, x4)]
    size = SUBLANES
    while size < n:
        pick = jnp.logical_and((ri // (2 * size)) == (ci // (2 * size)), (ri // size) != (ci // size))
        lows = [_mm(jnp.where(pick, a, 0.0), t) for a, t in zip(a_list, ts)]
        ts = [t - _mm(t, low) for t, low in zip(ts, lows)]
        size *= 2
    return ts


def _rwkv_scan_kernel(n_batch, r_ref, v_ref, kk_ref, lw_ref, kd_ref, bv_ref, y_ref, state_ref):
    d = pl.program_id(0)
    c = pl.program_id(2)
    n = r_ref.shape[1] // n_batch
    m = 2 * n
    chains = [(p, b) for p in range(r_ref.shape[0]) for b in range(n_batch)]
    batches = range(len(chains))

    @pl.when(c == 0)
    def _():
        state_ref[...] = jnp.zeros_like(state_ref)

    sign = 1 - 2 * d
    ri = lax.broadcasted_iota(jnp.int32, (m, m), 0)
    ci = lax.broadcasted_iota(jnp.int32, (m, m), 1)
    same_head = (ri // n) == (ci // n)
    order = (ri - ci) * sign
    strict = jnp.logical_and(same_head, order > 0)
    incl = jnp.logical_and(same_head, order >= 0)
    ti = lax.broadcasted_iota(jnp.int32, (n, n), 0)
    si = lax.broadcasted_iota(jnp.int32, (n, n), 1)
    tri = ((ti - si) * sign >= 0).astype(BF16)
    head0 = lax.broadcasted_iota(jnp.int32, (n, LANES), 1) < HEAD_DIM
    blockdiag = (lax.broadcasted_iota(jnp.int32, (LANES, LANES), 0) // HEAD_DIM
                 == lax.broadcasted_iota(jnp.int32, (LANES, LANES), 1) // HEAD_DIM)

    def stack(x):
        return jnp.concatenate([jnp.where(head0, x, 0.0), jnp.where(head0, 0.0, x)], axis=0)

    def shared(ref, i):
        p, b = chains[i]
        return ref[p, pl.ds(b, n, stride=n_batch), :]

    def split(ref, i):
        p, b = chains[i]
        return ref[0, p, pl.ds(b, n, stride=n_batch), :]

    lw = [split(lw_ref, b) for b in batches]
    hi = [x.astype(BF16) for x in lw]
    lo = [(x - h.astype(F32)).astype(BF16) for x, h in zip(lw, hi)]
    cum = [jnp.dot(tri, h, preferred_element_type=F32) + jnp.dot(tri, l, preferred_element_type=F32)
           for h, l in zip(hi, lo)]
    e_cum = [jnp.exp(x) for x in cum]
    e_neg = [jnp.exp(-x) for x in cum]
    kap = [shared(kk_ref, b) * jnp.exp(cum[b] - lw[b]) for b in batches]
    rt = [shared(r_ref, b) * e_cum[b] for b in batches]
    bt = [split(bv_ref, b) * e_neg[b] for b in batches]
    kt = [split(kd_ref, b) * e_neg[b] for b in batches]
    v = [shared(v_ref, b) for b in batches]
    total = [jnp.where(d == 0, e[n - 1:n], e[0:1]) for e in e_cum]
    state = [state_ref[b] for b in batches]

    scores = [_mm_nt(jnp.concatenate([stack(kap[b]), stack(rt[b])], axis=0),
                     jnp.concatenate([stack(bt[b]), stack(kt[b])], axis=0)) for b in batches]
    a_b = [jnp.where(strict, s[0:m, 0:m], 0.0) for s in scores]
    a_k = [jnp.where(strict, s[0:m, m:2 * m], 0.0) for s in scores]
    a_r = [jnp.concatenate([jnp.where(incl, s[m:2 * m, m:2 * m], 0.0),
                            jnp.where(incl, -s[m:2 * m, 0:m], 0.0)], axis=1) for s in scores]
    g0 = [_mm_nt(jnp.concatenate([kap[b], rt[b]], axis=0), state[b]) for b in batches]
    t_inv = _tri_inverse(a_b, ri, ci, n)
    vs = [stack(x) for x in v]
    rhs = [stack(g0[b][0:n]) + _mm(a_k[b], vs[b]) for b in batches]
    us = [_mm(t_inv[b], rhs[b]) for b in batches]
    ys = [stack(g0[b][n:m]) + _mm(a_r[b], jnp.concatenate([vs[b], us[b]], axis=0)) for b in batches]
    for i in batches:
        p, b = chains[i]
        y_ref[0, p, pl.ds(b, n, stride=n_batch), :] = ys[i][0:n] + ys[i][n:m]
    ds = [_mm_tn(jnp.concatenate([-(us[b][0:n] + us[b][n:m]), v[b]], axis=0),
                 jnp.concatenate([bt[b], kt[b]], axis=0)) for b in batches]
    for b in batches:
        state_ref[b] = (state[b] + jnp.where(blockdiag, ds[b], 0.0)) * total[b]


def _rwkv_scan(r, v, kk, lw, kd, bv, n_batch, n_ctx):
    n_slabs, rows, _ = r.shape
    tm = SCAN_CHUNK * n_batch
    pp = min(SCAN_PAIRS, n_slabs)
    n_blk, n_ctx_blk = rows // tm, n_ctx * n_batch // tm
    blk = lambda d, c: _scan_block(d, c, n_ctx_blk, n_blk)
    shared = pl.BlockSpec((pp, tm, LANES), lambda d, p, c: (p, blk(d, c), 0))
    split = pl.BlockSpec((1, pp, tm, LANES), lambda d, p, c: (d, p, blk(d, c), 0))
    return pl.pallas_call(
        functools.partial(_rwkv_scan_kernel, n_batch),
        out_shape=jax.ShapeDtypeStruct((2, n_slabs, rows, LANES), F32),
        grid=(2, n_slabs // pp, n_blk),
        in_specs=[shared, shared, shared, split, split, split],
        out_specs=split,
        scratch_shapes=[pltpu.VMEM((pp * n_batch, LANES, LANES), F32)],
        compiler_params=_cparams("parallel", "parallel", "arbitrary"),
        name="rwkv_scan",
    )(r, v, kk, lw, kd, bv)


def _rwkv_readout_kernel(y_ref, g_ref, bonus_ref, gnw_ref, gnb_ref, ones_ref, o_ref):
    y = jnp.concatenate([y_ref[0, j] + y_ref[1, j] for j in range(y_ref.shape[1])], axis=1)
    ones_bd = ones_ref[...]
    inv = 1.0 / HEAD_DIM
    mean = _segsum(y, ones_bd) * inv
    yc = y - mean
    var = _segsum(yc * yc, ones_bd) * inv
    yn = yc * lax.rsqrt(var + GN_EPS) * gnw_ref[...] + gnb_ref[...]
    o_ref[...] = (yn + bonus_ref[...]) * g_ref[...]


def _rwkv_readout(ys, g, bonus, gn_w, gn_b, ones_bd):
    rows, width = g.shape
    tm = ROW_TILE
    row = pl.BlockSpec((tm, width), lambda i: (i, 0))
    vec = pl.BlockSpec((1, width), lambda i: (0, 0))
    return pl.pallas_call(
        _rwkv_readout_kernel,
        out_shape=jax.ShapeDtypeStruct((rows, width), F32),
        grid=(rows // tm,),
        in_specs=[pl.BlockSpec((2, width // LANES, tm, LANES), lambda i: (0, 0, i, 0)), row, row, vec, vec,
                  pl.BlockSpec(ones_bd.shape, lambda i: (0, 0))],
        out_specs=row,
        compiler_params=_cparams("parallel"),
        name="rwkv_readout",
    )(ys, g, bonus, gn_w, gn_b, ones_bd)


def _norm_rope(x, gain, cos, sin_a, sin_b, ones_bd):
    ms = _segsum(x * x, ones_bd) * (1.0 / HEAD_DIM)
    xn = x * lax.rsqrt(ms + RMS_EPS) * gain
    return (xn * cos + pltpu.roll(xn, LANES - HEAD_DIM // 4, 1) * sin_a
            + pltpu.roll(xn, HEAD_DIM // 4, 1) * sin_b)


def _attn_prep_kernel(scale, n_batch, q_ref, k_ref, v_ref, cos_ref, sa_ref, sb_ref, qg_ref, kg_ref, ones_ref,
                      qo_ref, ko_ref, vo_ref, scr):
    cos, sin_a, sin_b = cos_ref[...], sa_ref[...], sb_ref[...]
    ones_bd = ones_ref[...]
    tm = cos.shape[0]
    tt = tm // n_batch
    first = lax.broadcasted_iota(jnp.int32, (tm, LANES), 1) < HEAD_DIM
    n_q, n_k = q_ref.shape[0], k_ref.shape[0]
    for j in range(n_q):
        scr[j] = _norm_rope(q_ref[j], qg_ref[...], cos, sin_a, sin_b, ones_bd) * scale
    for j in range(n_k):
        y = _norm_rope(k_ref[j], kg_ref[...], cos, sin_a, sin_b, ones_bd)
        swapped = pltpu.roll(y, HEAD_DIM, 1)
        scr[n_q + 2 * j] = jnp.where(first, y, swapped)
        scr[n_q + 2 * j + 1] = jnp.where(first, swapped, y)
    ones_rows = jnp.ones((vo_ref.shape[3] - HEAD_DIM, LANES), BF16)
    for b in range(n_batch):
        rows = pl.ds(b, tt, stride=n_batch)
        for j in range(n_q):
            qo_ref[b, j] = scr[j, rows, :].astype(BF16)
        for h in range(2 * n_k):
            ko_ref[b, h] = scr[n_q + h, rows, :].astype(BF16)
        for j in range(n_k):
            vb = v_ref[j, rows, :]
            for g in range(tt // LANES):
                vt = vb[g * LANES:(g + 1) * LANES].T.astype(BF16)
                for h in range(2):
                    vo_ref[b, 2 * j + h, g, 0:HEAD_DIM] = vt[h * HEAD_DIM:(h + 1) * HEAD_DIM]
                    vo_ref[b, 2 * j + h, g, HEAD_DIM:] = ones_rows


def _attn_prepare(pq, pk, pv, cos, sin_a, sin_b, q_gain, k_gain, ones_pair, n_batch):
    q_slabs, rows, _ = pq.shape
    k_slabs = pk.shape[0]
    tt_all = rows // n_batch
    tt = PREP_TT
    tm = tt * n_batch
    kh = 2 * k_slabs
    tab = pl.BlockSpec((tm, LANES), lambda i: (i, 0))
    vec = pl.BlockSpec((1, LANES), lambda i: (0, 0))
    return pl.pallas_call(
        functools.partial(_attn_prep_kernel, HEAD_DIM ** -0.5 * math.log2(math.e), n_batch),
        out_shape=[jax.ShapeDtypeStruct((n_batch, q_slabs, tt_all, LANES), BF16),
                   jax.ShapeDtypeStruct((n_batch, kh, tt_all, LANES), BF16),
                   jax.ShapeDtypeStruct((n_batch, kh, tt_all // LANES, VT_ROWS, LANES), BF16)],
        grid=(tt_all // tt,),
        in_specs=[pl.BlockSpec((q_slabs, tm, LANES), lambda i: (0, i, 0)),
                  pl.BlockSpec((k_slabs, tm, LANES), lambda i: (0, i, 0)),
                  pl.BlockSpec((k_slabs, tm, LANES), lambda i: (0, i, 0)),
                  tab, tab, tab, vec, vec,
                  pl.BlockSpec(ones_pair.shape, lambda i: (0, 0))],
        out_specs=[pl.BlockSpec((n_batch, q_slabs, tt, LANES), lambda i: (0, 0, i, 0)),
                   pl.BlockSpec((n_batch, kh, tt, LANES), lambda i: (0, 0, i, 0)),
                   pl.BlockSpec((n_batch, kh, tt // LANES, VT_ROWS, LANES), lambda i: (0, 0, i, 0, 0))],
        scratch_shapes=[pltpu.VMEM((q_slabs + kh, tm, LANES), F32)],
        compiler_params=_cparams("parallel"),
        name="attn_prepare",
    )(pq, pk, pv, cos, sin_a, sin_b, q_gain, k_gain, ones_pair)


def _flash_kernel(n_ctx, sub, q_ref, k_ref, vt_ref, o_ref):
    qi = pl.program_id(2)
    n_pairs, tq = q_ref.shape[1], q_ref.shape[2]
    n_keys = k_ref.shape[2]
    first = lax.broadcasted_iota(jnp.int32, (tq, LANES), 1) < HEAD_DIM
    zero = jnp.zeros((tq, LANES), BF16)
    q = jnp.concatenate([jnp.where(first if h == 0 else jnp.logical_not(first), q_ref[0, p], zero)
                         for p in range(n_pairs) for h in range(2)], axis=0)

    def scores(i):
        k = k_ref[0, 0, i * sub:(i + 1) * sub, :]
        return lax.dot_general(k, q, (((1,), (1,)), ((), ())), preferred_element_type=F32)

    def weighted_values(s, m_new, i):
        p = jnp.exp2(s - m_new).astype(BF16)
        g0 = i * sub // LANES
        vt = jnp.concatenate([vt_ref[0, 0, g0 + g] for g in range(sub // LANES)], axis=1)
        return jnp.dot(vt, p, preferred_element_type=F32)

    def attend(n_sub):
        s = scores(0)
        m_prev = acc = None
        for i in range(n_sub):
            s_next = scores(i + 1) if i + 1 < n_sub else None
            m_new = jnp.max(s, axis=0, keepdims=True)
            if m_prev is None:
                acc = weighted_values(s, m_new, i)
            else:
                m_new = jnp.maximum(m_prev, m_new)
                acc = jnp.exp2(m_prev - m_new) * acc + weighted_values(s, m_new, i)
            m_prev, s = m_new, s_next
        out_t = acc[0:HEAD_DIM] / acc[HEAD_DIM:HEAD_DIM + 1]
        for p in range(n_pairs):
            pair_t = jnp.concatenate([out_t[:, (2 * p) * tq:(2 * p + 1) * tq],
                                      out_t[:, (2 * p + 1) * tq:(2 * p + 2) * tq]], axis=0)
            o_ref[0, p] = pair_t.T

    is_ctx = qi * tq < n_ctx

    @pl.when(is_ctx)
    def _():
        attend(n_ctx // sub)

    @pl.when(jnp.logical_not(is_ctx))
    def _():
        attend(n_keys // sub)


def _flash_attention(qh, kh, vt, n_ctx):
    n_batch, q_slabs, tt_all, _ = qh.shape
    n_kvh = kh.shape[1]
    n_pairs = q_slabs // n_kvh
    tq = min(ATTN_TQ, n_ctx)
    sub = min(ATTN_SUB, n_ctx)
    assert n_ctx % tq == 0 and n_ctx % sub == 0 and tt_all % sub == 0 and sub % LANES == 0 and tq % LANES == 0
    return pl.pallas_call(
        functools.partial(_flash_kernel, n_ctx, sub),
        out_shape=jax.ShapeDtypeStruct((n_batch, q_slabs, tt_all, LANES), F32),
        grid=(n_batch, n_kvh, tt_all // tq),
        in_specs=[pl.BlockSpec((1, n_pairs, tq, LANES), lambda b, h, qi: (b, h, qi, 0)),
                  pl.BlockSpec((1, 1, tt_all, LANES), lambda b, h, qi: (b, h, 0, 0)),
                  pl.BlockSpec((1, 1, tt_all // LANES, VT_ROWS, LANES), lambda b, h, qi: (b, h, 0, 0, 0))],
        out_specs=pl.BlockSpec((1, n_pairs, tq, LANES), lambda b, h, qi: (b, h, qi, 0)),
        compiler_params=_cparams("parallel", "parallel", "parallel"),
        name="flash_attention",
    )(qh, kh, vt)


def _s5_param_kernel(are_ref, aim_ref, ldt_ref, bre_ref, bim_ref, abar_re, abar_im, bf_re, bf_im):
    lam_re = jnp.minimum(are_ref[0], LAMBDA_RE_MAX)
    lam_im = aim_ref[0]
    dt = jnp.exp(ldt_ref[0])
    mag = jnp.exp(lam_re * dt)
    ar, ai = mag * jnp.cos(lam_im * dt), mag * jnp.sin(lam_im * dt)
    nr, ni = ar - 1.0, ai
    den = lam_re * lam_re + lam_im * lam_im
    cr = (nr * lam_re + ni * lam_im) / den
    cim = (ni * lam_re - nr * lam_im) / den
    abar_re[0], abar_im[0] = ar, ai
    bre, bim = bre_ref[...], bim_ref[...]
    bf_re[0] = cr * bre - cim * bim
    bf_im[0] = cr * bim + cim * bre


def _s5_params(a_re, a_im, log_dt, b_re, b_im):
    _, n_g, n_s = a_re.shape
    n_i = b_re.shape[-1]
    gn = n_g * n_s
    flat = lambda a: a.reshape(2, 1, gn)
    dt = jnp.broadcast_to(log_dt[:, :, None], (2, n_g, n_s))
    bt = lambda b: b.reshape(gn, n_i).T
    vec = pl.BlockSpec((1, 1, gn), lambda d: (d, 0, 0))
    mat = pl.BlockSpec((n_i, gn), lambda d: (0, 0))
    omat = pl.BlockSpec((1, n_i, gn), lambda d: (d, 0, 0))
    return pl.pallas_call(
        _s5_param_kernel,
        out_shape=[jax.ShapeDtypeStruct((2, 1, gn), F32)] * 2 + [jax.ShapeDtypeStruct((2, n_i, gn), F32)] * 2,
        grid=(2,),
        in_specs=[vec, vec, vec, mat, mat],
        out_specs=[vec, vec, omat, omat],
        compiler_params=_cparams("parallel"),
        name="s5_params",
    )(flat(a_re), flat(a_im), flat(dt), bt(b_re), bt(b_im))


def _s5_scan_kernel(u_ref, ar_ref, ai_ref, bre_ref, bim_ref, cre_ref, cim_ref, y_ref,
                    xr_ref, xi_ref, hr_ref, hi_ref):
    d = pl.program_id(0)
    c = pl.program_id(1)
    steps = u_ref.shape[0] // SUBLANES
    width = xr_ref.shape[1]

    @pl.when(c == 0)
    def _():
        hr_ref[...] = jnp.zeros_like(hr_ref)
        hi_ref[...] = jnp.zeros_like(hi_ref)

    u = u_ref[...].astype(BF16)
    xr_ref[...] = jnp.dot(u, bre_ref[0], preferred_element_type=F32)
    xi_ref[...] = jnp.dot(u, bim_ref[0], preferred_element_type=F32)

    for j in range(width // SSM_LANE_TILE):
        sl = pl.ds(j * SSM_LANE_TILE, SSM_LANE_TILE)
        ar = jnp.broadcast_to(ar_ref[0, :, sl], (SUBLANES, SSM_LANE_TILE))
        ai = jnp.broadcast_to(ai_ref[0, :, sl], (SUBLANES, SSM_LANE_TILE))

        def step(i, carry):
            hr, hi = carry
            t = jnp.where(d == 0, i, steps - 1 - i)
            rows = pl.ds(pl.multiple_of(t * SUBLANES, SUBLANES), SUBLANES)
            nr = ar * hr - ai * hi + xr_ref[rows, sl]
            ni = ar * hi + ai * hr + xi_ref[rows, sl]
            xr_ref[rows, sl] = nr
            xi_ref[rows, sl] = ni
            return nr, ni

        hr, hi = lax.fori_loop(0, steps, step, (hr_ref[:, sl], hi_ref[:, sl]), unroll=8)
        hr_ref[:, sl] = hr
        hi_ref[:, sl] = hi

    y_ref[0] = (jnp.dot(xr_ref[...].astype(BF16), cre_ref[...], preferred_element_type=F32)
                - jnp.dot(xi_ref[...].astype(BF16), cim_ref[...], preferred_element_type=F32))


def _s5_scan(u, abar_re, abar_im, b_re, b_im, c_re, c_im, n_batch, n_ctx):
    rows, width = u.shape
    gn = abar_re.shape[-1]
    tm = SSM_CHUNK * n_batch
    n_blk, n_ctx_blk = rows // tm, n_ctx * n_batch // tm
    blk = lambda d, c: _scan_block(d, c, n_ctx_blk, n_blk)
    vec = pl.BlockSpec((1, 1, gn), lambda d, c: (d, 0, 0))
    bmat = pl.BlockSpec((1, width, gn), lambda d, c: (d, 0, 0))
    cmat = pl.BlockSpec((gn, width), lambda d, c: (0, 0))
    return pl.pallas_call(
        _s5_scan_kernel,
        out_shape=jax.ShapeDtypeStruct((2, rows, width), F32),
        grid=(2, n_blk),
        in_specs=[pl.BlockSpec((tm, width), lambda d, c: (blk(d, c), 0)), vec, vec, bmat, bmat, cmat, cmat],
        out_specs=pl.BlockSpec((1, tm, width), lambda d, c: (d, blk(d, c), 0)),
        scratch_shapes=[pltpu.VMEM((tm, gn), F32), pltpu.VMEM((tm, gn), F32),
                        pltpu.VMEM((SUBLANES, gn), F32), pltpu.VMEM((SUBLANES, gn), F32)],
        compiler_params=_cparams("parallel", "arbitrary"),
        name="s5_scan",
    )(u, abar_re, abar_im, b_re, b_im, c_re, c_im)


def _gelu_tanh(x):
    return 0.5 * x * (1.0 + jnp.tanh(math.sqrt(2.0 / math.pi) * (x + 0.044715 * (x * x * x))))


def _s5_readout_kernel(y_ref, u_ref, d_ref, w_ref, b_ref, o_ref):
    y = y_ref[0] + y_ref[1] + d_ref[...] * u_ref[...]
    y = _gelu_tanh(y)
    o_ref[...] = y * _sigmoid(jnp.dot(y.astype(BF16), w_ref[...], preferred_element_type=F32) + b_ref[...])


def _s5_readout(ys, u, d_skip, glu_w, glu_b):
    rows, width = u.shape
    tm = ROW_TILE
    row = pl.BlockSpec((tm, width), lambda i: (i, 0))
    vec = pl.BlockSpec((1, width), lambda i: (0, 0))
    return pl.pallas_call(
        _s5_readout_kernel,
        out_shape=jax.ShapeDtypeStruct((rows, width), F32),
        grid=(rows // tm,),
        in_specs=[pl.BlockSpec((2, tm, width), lambda i: (0, i, 0)), row, vec,
                  pl.BlockSpec(glu_w.shape, lambda i: (0, 0)), vec],
        out_specs=row,
        compiler_params=_cparams("parallel"),
        name="s5_readout",
    )(ys, u, d_skip, glu_w, glu_b)


def _merge_kernel(alpha, ya_ref, yb_ref, yc_ref, pg_ref, x_ref, gt_ref, pa_ref, pb_ref, pc_ref, wo_ref,
                  g_ref, b_ref, o_ref, yb_rows):
    d = x_ref.shape[1]
    n_batch, n_slabs, tt = yb_ref.shape[0], yb_ref.shape[1], yb_ref.shape[2]
    for b in range(n_batch):
        for j in range(n_slabs):
            yb_rows[j, pl.ds(b, tt, stride=n_batch), :] = yb_ref[b, j]
    yb = jnp.concatenate([yb_rows[j].astype(BF16) for j in range(n_slabs)], axis=1)
    pg = pg_ref[...]
    merged = (_sigmoid(pg[:, 0:d]) * _mm(ya_ref[...], pa_ref[...])
              + _sigmoid(pg[:, d:2 * d]) * _mm(yb, pb_ref[...])
              + _sigmoid(pg[:, 2 * d:3 * d]) * _mm(yc_ref[...], pc_ref[...]))
    mix = _mm(merged, wo_ref[...])
    o_ref[...] = _layer_norm(alpha * x_ref[...] + _rows_mod(mix, gt_ref[0]), g_ref[...], b_ref[...])


def _merge(ya, yb, yc, pg, x_all, modtab, proj_a, proj_b, proj_c, w_out, ln_g, ln_b, alpha, n_ctx_rows):
    rows, d = x_all.shape
    n_batch, n_slabs = yb.shape[0], yb.shape[1]
    tm = ROW_TILE
    nctx = n_ctx_rows // tm
    seg = lambda i: (i >= nctx).astype(jnp.int32)
    row = lambda a: pl.BlockSpec((tm, a.shape[1]), lambda i: (i, 0))
    full = lambda a: pl.BlockSpec(a.shape, lambda i: (0, 0), pipeline_mode=pl.Buffered(1))
    vec = pl.BlockSpec((1, d), lambda i: (0, 0))
    return pl.pallas_call(
        functools.partial(_merge_kernel, alpha),
        out_shape=jax.ShapeDtypeStruct((rows, d), F32),
        grid=(rows // tm,),
        scratch_shapes=[pltpu.VMEM((n_slabs, tm, LANES), F32)],
        in_specs=[row(ya), pl.BlockSpec((n_batch, n_slabs, tm // n_batch, LANES), lambda i: (0, 0, i, 0)),
                  row(yc), row(pg), row(x_all),
                  pl.BlockSpec((1, SUBLANES, d), lambda i: (seg(i), 0, 2)),
                  full(proj_a), full(proj_b), full(proj_c), full(w_out), vec, vec],
        out_specs=pl.BlockSpec((tm, d), lambda i: (i, 0)),
        compiler_params=_cparams("parallel"),
        name="merge_ln",
    )(ya, yb, yc, pg, x_all, modtab, proj_a, proj_b, proj_c, w_out, ln_g, ln_b)


def _mlp_kernel(alpha, x_ref, sh_ref, sc_ref, gt_ref, w1_ref, w2_ref, g_ref, b_ref, o_ref, h_ref, acc_ref):
    j = pl.program_id(1)

    @pl.when(j == 0)
    def _():
        h_ref[...] = _rows_add(_rows_mod(x_ref[...], 1.0 + sc_ref[0]), sh_ref[0]).astype(BF16)
        acc_ref[...] = jnp.zeros_like(acc_ref)

    a = jnp.maximum(jnp.dot(h_ref[...], w1_ref[...], preferred_element_type=F32), 0.0)
    acc_ref[...] += jnp.dot((a * a).astype(BF16), w2_ref[...], preferred_element_type=F32)

    @pl.when(j == pl.num_programs(1) - 1)
    def _():
        o_ref[...] = _layer_norm(alpha * x_ref[...] + _rows_mod(acc_ref[...], gt_ref[0]),
                                 g_ref[...], b_ref[...])


def _mlp(x_mid, modtab, w1, w2, ln_g, ln_b, alpha, n_ctx_rows):
    rows, d = x_mid.shape
    ff = w1.shape[1]
    tm = min(MLP_ROW_TILE, n_ctx_rows)
    tf = min(MLP_FF_TILE, ff)
    nctx = n_ctx_rows // tm
    seg = lambda i: (i >= nctx).astype(jnp.int32)
    mod = lambda col: pl.BlockSpec((1, SUBLANES, d), lambda i, j: (seg(i), 0, col))
    vec = pl.BlockSpec((1, d), lambda i, j: (0, 0))
    return pl.pallas_call(
        functools.partial(_mlp_kernel, alpha),
        out_shape=jax.ShapeDtypeStruct((rows, d), F32),
        grid=(rows // tm, ff // tf),
        in_specs=[pl.BlockSpec((tm, d), lambda i, j: (i, 0)), mod(3), mod(4), mod(5),
                  pl.BlockSpec((d, tf), lambda i, j: (0, j)),
                  pl.BlockSpec((tf, d), lambda i, j: (j, 0)), vec, vec],
        out_specs=pl.BlockSpec((tm, d), lambda i, j: (i, 0)),
        scratch_shapes=[pltpu.VMEM((tm, d), BF16), pltpu.VMEM((tm, d), F32)],
        compiler_params=_cparams("parallel", "arbitrary"),
        name="mlp_ln",
    )(x_mid, modtab, modtab, modtab, w1, w2, ln_g, ln_b)


def _block_diag_ones(n, blk):
    i = jnp.arange(n) // blk
    return (i[:, None] == i[None, :]).astype(BF16)


def _rope_tables(n_ctx, n_lat):
    pairs = HEAD_DIM // 4
    rows = n_lat // GRID_W
    row = jnp.repeat(jnp.arange(rows, dtype=F32), GRID_W)
    col = jnp.tile(jnp.arange(GRID_W, dtype=F32), rows)
    inv = ROPE_THETA ** (-jnp.arange(pairs, dtype=F32) / pairs)
    ang = jnp.stack([row, col], axis=-1)[:, :, None] * inv
    ang = jnp.broadcast_to(ang[:, :, None, :], (n_lat, 2, 2, pairs)).reshape(n_lat, HEAD_DIM)
    cos = jnp.concatenate([jnp.ones((n_ctx, HEAD_DIM), F32), jnp.cos(ang)], axis=0)
    sin = jnp.concatenate([jnp.zeros((n_ctx, HEAD_DIM), F32), jnp.sin(ang)], axis=0)
    first = (jnp.arange(HEAD_DIM) % (2 * pairs)) < pairs
    sin_a = jnp.where(first, -sin, 0.0)
    sin_b = jnp.where(first, 0.0, sin)
    two = lambda a: jnp.concatenate([a, a], axis=1)
    return two(cos), two(sin_a), two(sin_b)


def _block_diag(blocks):
    g, a, b = blocks.shape
    eye = jnp.eye(g, dtype=blocks.dtype)
    return (eye[:, None, :, None] * blocks[:, :, None, :]).reshape(g * a, g * b)


def kernel(x, c, ctx, c_ctx, mod_w, mod_b, w_in, rwkv_mu, rwkv_w0, rwkv_w_up, rwkv_a0, rwkv_a_up, rwkv_g_up, rwkv_k_k, rwkv_k_a, rwkv_r_k, rwkv_gn_w, rwkv_gn_b, attn_q_gain, attn_k_gain, ssm_a_re, ssm_a_im, ssm_log_dt, ssm_b_re, ssm_b_im, ssm_c_re, ssm_c_im, ssm_d, ssm_glu_w, ssm_glu_b, proj_a, proj_b, proj_c, w_out, ln1_g, ln1_b, ln2_g, ln2_b, mlp_w1, mlp_w2):
    n_batch, n_lat, d_model = x.shape
    n_ctx = ctx.shape[1]
    depth = mod_w.shape[0]
    assert n_batch == SUBLANES, "row layout packs the batch into one sublane tile"
    alpha = (2 * depth) ** 0.25
    n_ctx_rows = n_ctx * n_batch

    rwkv_w = rwkv_k_k.shape[1]
    q_w = proj_b.shape[1]
    kv_w = q_w // GQA_GROUP
    ssm_w = ssm_d.shape[1]
    rwkv_cols = rwkv_mu.shape[1]
    edges = [0, rwkv_cols, rwkv_cols + q_w, rwkv_cols + q_w + kv_w, rwkv_cols + q_w + 2 * kv_w,
             rwkv_cols + q_w + 2 * kv_w + ssm_w, w_in.shape[2]]
    slabbed = (False, True, True, True, False, False)
    splits = tuple(zip(edges[:-1], edges[1:], slabbed))

    to_rows = lambda a: a.transpose(1, 0, 2).reshape(a.shape[1] * n_batch, d_model)
    x_all = jnp.concatenate([to_rows(ctx), to_rows(x)], axis=0)

    cvec = jnp.concatenate([jnp.broadcast_to(c_ctx[None], (n_batch, d_model)), c], axis=0)
    modtab = _modulation(cvec, mod_w, mod_b).reshape(depth, 2, n_batch, 6 * d_model)

    ones_rwkv = _block_diag_ones(rwkv_w, HEAD_DIM)
    ones_pair = _block_diag_ones(LANES, HEAD_DIM)
    cos, sin_a, sin_b = (jnp.repeat(t, n_batch, axis=0) for t in _rope_tables(n_ctx, n_lat))
    rank = rwkv_w_up.shape[2]
    zpad = jnp.zeros((depth, 2, rank, rwkv_w), F32)

    for l in range(depth):
        mt = modtab[l]
        pa, pq, pk, pv, pc, pg = _in_projection(x_all, mt, w_in[l].astype(BF16), splits, n_ctx_rows)

        prm = dict(mu=rwkv_mu[l][None], w0=rwkv_w0[l][:, None], a0=rwkv_a0[l][:, None],
                   w_up=jnp.concatenate([rwkv_w_up[l], zpad[l]], axis=1).astype(BF16),
                   a_up=jnp.concatenate([zpad[l], rwkv_a_up[l]], axis=1).astype(BF16),
                   g_up=rwkv_g_up[l].astype(BF16), k_k=rwkv_k_k[l][None], k_a=rwkv_k_a[l][None],
                   r_k=rwkv_r_k[l][None], ones_bd=ones_rwkv)
        r, v, kk, g, bonus, lw, kd, bv = _rwkv_prepare(pa, prm, n_ctx_rows)
        ys = _rwkv_scan(r, v, kk, lw, kd, bv, n_batch, n_ctx)
        ya = _rwkv_readout(ys, g, bonus, rwkv_gn_w[l][None], rwkv_gn_b[l][None], ones_rwkv)

        tile2 = lambda a: jnp.tile(a, 2)[None]
        qh, kh, vh = _attn_prepare(pq, pk, pv, cos, sin_a, sin_b, tile2(attn_q_gain[l]), tile2(attn_k_gain[l]),
                                   ones_pair, n_batch)
        yb = _flash_attention(qh, kh, vh, n_ctx)

        abar_re, abar_im, bf_re, bf_im = _s5_params(ssm_a_re[l], ssm_a_im[l], ssm_log_dt[l], ssm_b_re[l], ssm_b_im[l])
        n_g = ssm_a_re.shape[2]
        bmat = lambda bf: jnp.stack([_block_diag(bf[d].reshape(SSM_GROUP, n_g, SSM_STATE).transpose(1, 0, 2))
                                     for d in range(2)]).astype(BF16)
        cmat = lambda cc: _block_diag(cc.transpose(0, 2, 1)).astype(BF16)
        ysc = _s5_scan(pc, abar_re, abar_im, bmat(bf_re), bmat(bf_im), cmat(ssm_c_re[l]), cmat(ssm_c_im[l]),
                       n_batch, n_ctx)
        yc = _s5_readout(ysc, pc, ssm_d[l][None], ssm_glu_w[l].astype(BF16), ssm_glu_b[l][None])

        x_mid = _merge(ya, yb, yc, pg, x_all, mt, proj_a[l].astype(BF16), proj_b[l].astype(BF16),
                       proj_c[l].astype(BF16), w_out[l].astype(BF16), ln1_g[l][None], ln1_b[l][None],
                       alpha, n_ctx_rows)
        x_all = _mlp(x_mid, mt, mlp_w1[l].astype(BF16), mlp_w2[l].astype(BF16), ln2_g[l][None], ln2_b[l][None],
                     alpha, n_ctx_rows)

    out = x_all[n_ctx_rows:].reshape(n_lat, n_batch, d_model).transpose(1, 0, 2)
    return out
```

```python
import functools
import math

import jax
import jax.numpy as jnp
from jax import lax
from jax.experimental import pallas as pl
from jax.experimental.pallas import tpu as pltpu

HEAD_DIM = 64
GRID_W = 64
ROPE_THETA = 10000.0
LN_EPS = 1e-5
RMS_EPS = 1e-6
GN_EPS = 64e-5
KK_EPS = 1e-24
LAMBDA_RE_MAX = -1e-4
GQA_GROUP = 4
SSM_GROUP = 16
SSM_STATE = 64

LANES = 128
SUBLANES = 8
MXU_DIM = 256
VMEM_LIMIT = 56 * 1024 * 1024

ROW_TILE = 256
MLP_ROW_TILE = 512
MLP_FF_TILE = 1024
SCAN_CHUNK = 64
SCAN_PAIRS = 2
SSM_CHUNK = 64
SSM_LANE_TILE = 512
ATTN_TQ = 256
ATTN_SUB = 256
PREP_TT = 128
VT_ROWS = 80

F32 = jnp.float32
BF16 = jnp.bfloat16


def _cparams(*sem):
    return pltpu.CompilerParams(dimension_semantics=sem, vmem_limit_bytes=VMEM_LIMIT)


def _mm(a, b):
    return jnp.dot(a.astype(BF16), b.astype(BF16), preferred_element_type=F32)


def _mm_nt(a, b):
    return lax.dot_general(a.astype(BF16), b.astype(BF16), (((1,), (1,)), ((), ())),
                           preferred_element_type=F32)


def _mm_tn(a, b):
    return lax.dot_general(a.astype(BF16), b.astype(BF16), (((0,), (0,)), ((), ())),
                           preferred_element_type=F32)


def _segsum(x, ones_bd):
    hi = x.astype(BF16)
    lo = (x - hi.astype(F32)).astype(BF16)
    return (jnp.dot(hi, ones_bd, preferred_element_type=F32)
            + jnp.dot(lo, ones_bd, preferred_element_type=F32))


def _sigmoid(x):
    return 1.0 / (1.0 + jnp.exp(-x))


def _softplus(x):
    return jnp.maximum(x, 0.0) + jnp.log(1.0 + jnp.exp(-jnp.abs(x)))


def _layer_norm(z, g, b):
    mu = jnp.mean(z, axis=-1, keepdims=True)
    zc = z - mu
    var = jnp.mean(zc * zc, axis=-1, keepdims=True)
    return zc * lax.rsqrt(var + LN_EPS) * g + b


def _rows_mod(x, m):
    tm, n = x.shape
    return (x.reshape(tm // SUBLANES, SUBLANES, n) * m[None]).reshape(tm, n)


def _rows_add(x, m):
    tm, n = x.shape
    return (x.reshape(tm // SUBLANES, SUBLANES, n) + m[None]).reshape(tm, n)


def _mod_kernel(c_ref, w_ref, b_ref, o_ref):
    c = c_ref[...]
    s = c * _sigmoid(c)
    o_ref[0] = jnp.dot(s, w_ref[0], preferred_element_type=F32,
                       precision=lax.Precision.HIGHEST) + b_ref[0]


def _modulation(cvec, mod_w, mod_b):
    n_layers, d, n6 = mod_w.shape
    tn = 1024 if n6 % 1024 == 0 else n6
    return pl.pallas_call(
        _mod_kernel,
        out_shape=jax.ShapeDtypeStruct((n_layers, cvec.shape[0], n6), F32),
        grid=(n_layers, n6 // tn),
        in_specs=[pl.BlockSpec((cvec.shape[0], d), lambda l, j: (0, 0)),
                  pl.BlockSpec((1, d, tn), lambda l, j: (l, 0, j)),
                  pl.BlockSpec((1, 1, tn), lambda l, j: (l, 0, j))],
        out_specs=pl.BlockSpec((1, cvec.shape[0], tn), lambda l, j: (l, 0, j)),
        compiler_params=_cparams("parallel", "parallel"),
        name="modulation",
    )(cvec, mod_w, mod_b.reshape(n_layers, 1, n6))


def _inproj_kernel(splits, n_ctx_rows, n_rows, x_ref, xp_ref, xn_ref, sh_ref, sc_ref, w_ref,
                   mu_ref, w0_ref, wup_ref, a0_ref, aup_ref, gup_ref, kk_ref, ka_ref, rk_ref, ones_ref, *o_refs):
    i = pl.program_id(0)
    tm = x_ref.shape[0]
    r0 = i * tm
    seg_start = jnp.logical_or(r0 == 0, r0 == n_ctx_rows)
    seg_end = jnp.logical_or(r0 + tm == n_ctx_rows, r0 + tm == n_rows)
    scale, shift = 1.0 + sc_ref[0], sh_ref[0]
    mod = lambda x: _rows_add(_rows_mod(x, scale), shift)
    xm = mod(x_ref[...])
    x_ext = jnp.concatenate([jnp.where(seg_start, 0.0, mod(xp_ref[...])), xm,
                             jnp.where(seg_end, 0.0, mod(xn_ref[...]))], axis=0).astype(BF16)
    a, b, _ = splits[0]
    n_rwkv = 8
    _rwkv_prep(jnp.dot(x_ext, w_ref[:, a:b], preferred_element_type=F32), mu_ref[...], w0_ref, wup_ref,
               a0_ref, aup_ref, gup_ref[...], kk_ref[...], ka_ref[...], rk_ref[...], ones_ref[...],
               o_refs[:n_rwkv])
    xm = xm.astype(BF16)
    for (a, b, slabbed), o_ref in zip(splits[1:], o_refs[n_rwkv:]):
        y = jnp.dot(xm, w_ref[:, a:b], preferred_element_type=F32)
        if slabbed:
            for j in range((b - a) // LANES):
                o_ref[j] = y[:, j * LANES:(j + 1) * LANES]
        else:
            o_ref[...] = y


def _in_projection(x_all, modtab, w_bf16, splits, prm, n_ctx_rows):
    rows, d = x_all.shape
    width = prm["k_k"].shape[-1]
    tm = ROW_TILE
    nctx = n_ctx_rows // tm
    hb = tm // SUBLANES
    last = rows // SUBLANES - 1
    seg = lambda i: (i >= nctx).astype(jnp.int32)
    n_slabs = width // LANES
    slab = pl.BlockSpec((n_slabs, tm, LANES), lambda i: (0, i, 0))
    slab2 = pl.BlockSpec((2, n_slabs, tm, LANES), lambda i: (0, 0, i, 0))
    row = pl.BlockSpec((tm, width), lambda i: (i, 0))
    flat = jax.ShapeDtypeStruct((rows, width), F32)
    one = jax.ShapeDtypeStruct((n_slabs, rows, LANES), F32)
    two = jax.ShapeDtypeStruct((2, n_slabs, rows, LANES), F32)
    shapes = [one, one, one, flat, flat, two, two, two]
    specs = [slab] * 3 + [row] * 2 + [slab2] * 3
    for a, b, slabbed in splits[1:]:
        if slabbed:
            shapes.append(jax.ShapeDtypeStruct(((b - a) // LANES, rows, LANES), F32))
            specs.append(pl.BlockSpec(((b - a) // LANES, tm, LANES), lambda i: (0, i, 0)))
        else:
            shapes.append(jax.ShapeDtypeStruct((rows, b - a), F32))
            specs.append(pl.BlockSpec((tm, b - a), lambda i: (i, 0)))
    consts = [prm["mu"], prm["w0"], prm["w_up"], prm["a0"], prm["a_up"], prm["g_up"],
              prm["k_k"], prm["k_a"], prm["r_k"], prm["ones_bd"]]
    full = lambda a: pl.BlockSpec(a.shape, lambda i: (0,) * a.ndim)
    return pl.pallas_call(
        functools.partial(_inproj_kernel, splits, n_ctx_rows, rows),
        out_shape=shapes,
        grid=(rows // tm,),
        in_specs=[pl.BlockSpec((tm, d), lambda i: (i, 0)),
                  pl.BlockSpec((SUBLANES, d), lambda i: (jnp.maximum(i * hb - 1, 0), 0)),
                  pl.BlockSpec((SUBLANES, d), lambda i: (jnp.minimum((i + 1) * hb, last), 0)),
                  pl.BlockSpec((1, SUBLANES, d), lambda i: (seg(i), 0, 0)),
                  pl.BlockSpec((1, SUBLANES, d), lambda i: (seg(i), 0, 1)),
                  pl.BlockSpec(w_bf16.shape, lambda i: (0, 0), pipeline_mode=pl.Buffered(1))]
                 + [full(a) for a in consts],
        out_specs=specs,
        compiler_params=_cparams("parallel"),
        name="in_projection",
    )(x_all, x_all, x_all, modtab, modtab, w_bf16, *consts)


def _rwkv_prep(p_ext, mu, w0_ref, wup_ref, a0_ref, aup_ref, g_up, k_k, k_a, r_k, ones_bd, outs):
    r_out, v_out, kkn_out, g_out, bonus_out, lw_out, kd_out, bv_out = outs
    tm = p_ext.shape[0] - 2 * SUBLANES
    p = p_ext[SUBLANES:SUBLANES + tm]
    p = p + mu * (0.5 * (p_ext[0:tm] + p_ext[2 * SUBLANES:]) - p)

    w = k_k.shape[-1]
    r, k, v = p[:, 0:w], p[:, w:2 * w], p[:, 2 * w:3 * w]
    wa = p[:, 3 * w:3 * w + LANES]
    gd = p[:, 3 * w + LANES:3 * w + 2 * LANES]

    def put(o_ref, y, *lead):
        for j in range(w // LANES):
            o_ref[lead + (j,)] = y[:, j * LANES:(j + 1) * LANES]

    g_out[...] = _mm(_sigmoid(gd), g_up)
    kk = k * k_k
    kk = kk * lax.rsqrt(jnp.maximum(_segsum(kk * kk, ones_bd), KK_EPS))
    put(r_out, r)
    put(v_out, v)
    put(kkn_out, kk)
    tanh_wa = jnp.tanh(wa)
    rk_sum = jnp.zeros_like(r)
    for d in range(2):
        w_pre = w0_ref[d] + _mm(tanh_wa, wup_ref[d])
        put(lw_out, -jnp.exp(-_softplus(-w_pre) - 0.5), d)
        a = _sigmoid(a0_ref[d] + _mm(wa, aup_ref[d]))
        k_dir = k * (1.0 + (a - 1.0) * k_a)
        put(kd_out, k_dir, d)
        put(bv_out, kk * a, d)
        rk_sum = rk_sum + r * k_dir
    bonus_out[...] = _segsum(rk_sum * r_k, ones_bd) * v


def _scan_block(d, c, n_ctx_blk, n_blk):
    bwd = jnp.where(c < n_ctx_blk, n_ctx_blk - 1 - c, n_ctx_blk + (n_blk - 1 - c))
    return jnp.where(d == 0, c, bwd)


def _tri_inverse(a_list, ri, ci, n):
    eye = (ri == ci).astype(F32)
    same = (ri // SUBLANES) == (ci // SUBLANES)
    xs = [-jnp.where(same, a, 0.0) for a in a_list]
    x2 = [_mm(x, x) for x in xs]
    x4 = [_mm(x, x) for x in x2]
    ts = [eye + x for x in xs]
    ts = [t + _mm(t, x) for t, x in zip(ts, x2)]
    ts = [t + _mm(t, x) for t, x in zip(ts, x4)]
    size = SUBLANES
    while size < n:
        pick = jnp.logical_and((ri // (2 * size)) == (ci // (2 * size)), (ri // size) != (ci // size))
        lows = [_mm(jnp.where(pick, a, 0.0), t) for a, t in zip(a_list, ts)]
        ts = [t - _mm(t, low) for t, low in zip(ts, lows)]
        size *= 2
    return ts


def _rwkv_scan_kernel(n_batch, r_ref, v_ref, kk_ref, lw_ref, kd_ref, bv_ref, y_ref, state_ref):
    d = pl.program_id(0)
    c = pl.program_id(2)
    n = r_ref.shape[1] // n_batch
    m = 2 * n
    chains = [(p, b) for p in range(r_ref.shape[0]) for b in range(n_batch)]
    batches = range(len(chains))

    @pl.when(c == 0)
    def _():
        state_ref[...] = jnp.zeros_like(state_ref)

    sign = 1 - 2 * d
    ri = lax.broadcasted_iota(jnp.int32, (m, m), 0)
    ci = lax.broadcasted_iota(jnp.int32, (m, m), 1)
    same_head = (ri // n) == (ci // n)
    order = (ri - ci) * sign
    strict = jnp.logical_and(same_head, order > 0)
    incl = jnp.logical_and(same_head, order >= 0)
    ti = lax.broadcasted_iota(jnp.int32, (n, n), 0)
    si = lax.broadcasted_iota(jnp.int32, (n, n), 1)
    tri = ((ti - si) * sign >= 0).astype(BF16)
    head0 = lax.broadcasted_iota(jnp.int32, (n, LANES), 1) < HEAD_DIM
    blockdiag = (lax.broadcasted_iota(jnp.int32, (LANES, LANES), 0) // HEAD_DIM
                 == lax.broadcasted_iota(jnp.int32, (LANES, LANES), 1) // HEAD_DIM)

    def stack(x):
        return jnp.concatenate([jnp.where(head0, x, 0.0), jnp.where(head0, 0.0, x)], axis=0)

    def shared(ref, i):
        p, b = chains[i]
        return ref[p, pl.ds(b, n, stride=n_batch), :]

    def split(ref, i):
        p, b = chains[i]
        return ref[0, p, pl.ds(b, n, stride=n_batch), :]

    lw = [split(lw_ref, b) for b in batches]
    hi = [x.astype(BF16) for x in lw]
    lo = [(x - h.astype(F32)).astype(BF16) for x, h in zip(lw, hi)]
    cum = [jnp.dot(tri, h, preferred_element_type=F32) + jnp.dot(tri, l, preferred_element_type=F32)
           for h, l in zip(hi, lo)]
    e_cum = [jnp.exp(x) for x in cum]
    e_neg = [jnp.exp(-x) for x in cum]
    kap = [shared(kk_ref, b) * jnp.exp(cum[b] - lw[b]) for b in batches]
    rt = [shared(r_ref, b) * e_cum[b] for b in batches]
    bt = [split(bv_ref, b) * e_neg[b] for b in batches]
    kt = [split(kd_ref, b) * e_neg[b] for b in batches]
    v = [shared(v_ref, b) for b in batches]
    total = [jnp.where(d == 0, e[n - 1:n], e[0:1]) for e in e_cum]
    state = [state_ref[b] for b in batches]

    scores = [_mm_nt(jnp.concatenate([stack(kap[b]), stack(rt[b])], axis=0),
                     jnp.concatenate([stack(bt[b]), stack(kt[b])], axis=0)) for b in batches]
    a_b = [jnp.where(strict, s[0:m, 0:m], 0.0) for s in scores]
    a_k = [jnp.where(strict, s[0:m, m:2 * m], 0.0) for s in scores]
    a_r = [jnp.concatenate([jnp.where(incl, s[m:2 * m, m:2 * m], 0.0),
                            jnp.where(incl, -s[m:2 * m, 0:m], 0.0)], axis=1) for s in scores]
    g0 = [_mm_nt(jnp.concatenate([kap[b], rt[b]], axis=0), state[b]) for b in batches]
    t_inv = _tri_inverse(a_b, ri, ci, n)
    vs = [stack(x) for x in v]
    rhs = [stack(g0[b][0:n]) + _mm(a_k[b], vs[b]) for b in batches]
    us = [_mm(t_inv[b], rhs[b]) for b in batches]
    ys = [stack(g0[b][n:m]) + _mm(a_r[b], jnp.concatenate([vs[b], us[b]], axis=0)) for b in batches]
    for i in batches:
        p, b = chains[i]
        y_ref[0, p, pl.ds(b, n, stride=n_batch), :] = ys[i][0:n] + ys[i][n:m]
    ds = [_mm_tn(jnp.concatenate([-(us[b][0:n] + us[b][n:m]), v[b]], axis=0),
                 jnp.concatenate([bt[b], kt[b]], axis=0)) for b in batches]
    for b in batches:
        state_ref[b] = (state[b] + jnp.where(blockdiag, ds[b], 0.0)) * total[b]


def _rwkv_scan(r, v, kk, lw, kd, bv, n_batch, n_ctx):
    n_slabs, rows, _ = r.shape
    tm = SCAN_CHUNK * n_batch
    pp = min(SCAN_PAIRS, n_slabs)
    n_blk, n_ctx_blk = rows // tm, n_ctx * n_batch // tm
    blk = lambda d, c: _scan_block(d, c, n_ctx_blk, n_blk)
    shared = pl.BlockSpec((pp, tm, LANES), lambda d, p, c: (p, blk(d, c), 0))
    split = pl.BlockSpec((1, pp, tm, LANES), lambda d, p, c: (d, p, blk(d, c), 0))
    return pl.pallas_call(
        functools.partial(_rwkv_scan_kernel, n_batch),
        out_shape=jax.ShapeDtypeStruct((2, n_slabs, rows, LANES), F32),
        grid=(2, n_slabs // pp, n_blk),
        in_specs=[shared, shared, shared, split, split, split],
        out_specs=split,
        scratch_shapes=[pltpu.VMEM((pp * n_batch, LANES, LANES), F32)],
        compiler_params=_cparams("parallel", "parallel", "arbitrary"),
        name="rwkv_scan",
    )(r, v, kk, lw, kd, bv)


def _rwkv_readout(y_ref, g, bonus, gn_w, gn_b, ones_bd):
    y = jnp.concatenate([y_ref[0, j] + y_ref[1, j] for j in range(y_ref.shape[1])], axis=1)
    inv = 1.0 / HEAD_DIM
    mean = _segsum(y, ones_bd) * inv
    yc = y - mean
    var = _segsum(yc * yc, ones_bd) * inv
    yn = yc * lax.rsqrt(var + GN_EPS) * gn_w + gn_b
    return (yn + bonus) * g


def _norm_rope(x, gain, cos, sin_a, sin_b, ones_bd):
    ms = _segsum(x * x, ones_bd) * (1.0 / HEAD_DIM)
    xn = x * lax.rsqrt(ms + RMS_EPS) * gain
    return (xn * cos + pltpu.roll(xn, LANES - HEAD_DIM // 4, 1) * sin_a
            + pltpu.roll(xn, HEAD_DIM // 4, 1) * sin_b)


def _attn_prep_kernel(scale, n_batch, q_ref, k_ref, v_ref, cos_ref, sa_ref, sb_ref, qg_ref, kg_ref, ones_ref,
                      qo_ref, ko_ref, vo_ref, scr):
    cos, sin_a, sin_b = cos_ref[...], sa_ref[...], sb_ref[...]
    ones_bd = ones_ref[...]
    tm = cos.shape[0]
    tt = tm // n_batch
    first = lax.broadcasted_iota(jnp.int32, (tm, LANES), 1) < HEAD_DIM
    n_q, n_k = q_ref.shape[0], k_ref.shape[0]
    for j in range(n_q):
        scr[j] = _norm_rope(q_ref[j], qg_ref[...], cos, sin_a, sin_b, ones_bd) * scale
    for j in range(n_k):
        y = _norm_rope(k_ref[j], kg_ref[...], cos, sin_a, sin_b, ones_bd)
        swapped = pltpu.roll(y, HEAD_DIM, 1)
        scr[n_q + 2 * j] = jnp.where(first, y, swapped)
        scr[n_q + 2 * j + 1] = jnp.where(first, swapped, y)
    ones_rows = jnp.ones((vo_ref.shape[3] - HEAD_DIM, LANES), BF16)
    for b in range(n_batch):
        rows = pl.ds(b, tt, stride=n_batch)
        for j in range(n_q):
            qo_ref[b, j] = scr[j, rows, :].astype(BF16)
        for h in range(2 * n_k):
            ko_ref[b, h] = scr[n_q + h, rows, :].astype(BF16)
        for j in range(n_k):
            vb = v_ref[j, rows, :]
            for g in range(tt // LANES):
                vt = vb[g * LANES:(g + 1) * LANES].T.astype(BF16)
                for h in range(2):
                    vo_ref[b, 2 * j + h, g, 0:HEAD_DIM] = vt[h * HEAD_DIM:(h + 1) * HEAD_DIM]
                    vo_ref[b, 2 * j + h, g, HEAD_DIM:] = ones_rows


def _attn_prepare(pq, pk, pv, cos, sin_a, sin_b, q_gain, k_gain, ones_pair, n_batch):
    q_slabs, rows, _ = pq.shape
    k_slabs = pk.shape[0]
    tt_all = rows // n_batch
    tt = PREP_TT
    tm = tt * n_batch
    kh = 2 * k_slabs
    tab = pl.BlockSpec((tm, LANES), lambda i: (i, 0))
    vec = pl.BlockSpec((1, LANES), lambda i: (0, 0))
    return pl.pallas_call(
        functools.partial(_attn_prep_kernel, HEAD_DIM ** -0.5 * math.log2(math.e), n_batch),
        out_shape=[jax.ShapeDtypeStruct((n_batch, q_slabs, tt_all, LANES), BF16),
                   jax.ShapeDtypeStruct((n_batch, kh, tt_all, LANES), BF16),
                   jax.ShapeDtypeStruct((n_batch, kh, tt_all // LANES, VT_ROWS, LANES), BF16)],
        grid=(tt_all // tt,),
        in_specs=[pl.BlockSpec((q_slabs, tm, LANES), lambda i: (0, i, 0)),
                  pl.BlockSpec((k_slabs, tm, LANES), lambda i: (0, i, 0)),
                  pl.BlockSpec((k_slabs, tm, LANES), lambda i: (0, i, 0)),
                  tab, tab, tab, vec, vec,
                  pl.BlockSpec(ones_pair.shape, lambda i: (0, 0))],
        out_specs=[pl.BlockSpec((n_batch, q_slabs, tt, LANES), lambda i: (0, 0, i, 0)),
                   pl.BlockSpec((n_batch, kh, tt, LANES), lambda i: (0, 0, i, 0)),
                   pl.BlockSpec((n_batch, kh, tt // LANES, VT_ROWS, LANES), lambda i: (0, 0, i, 0, 0))],
        scratch_shapes=[pltpu.VMEM((q_slabs + kh, tm, LANES), F32)],
        compiler_params=_cparams("parallel"),
        name="attn_prepare",
    )(pq, pk, pv, cos, sin_a, sin_b, q_gain, k_gain, ones_pair)


def _flash_kernel(n_ctx, sub, q_ref, k_ref, vt_ref, o_ref):
    qi = pl.program_id(2)
    n_pairs, tq = q_ref.shape[1], q_ref.shape[2]
    n_keys = k_ref.shape[2]
    first = lax.broadcasted_iota(jnp.int32, (tq, LANES), 1) < HEAD_DIM
    zero = jnp.zeros((tq, LANES), BF16)
    q = jnp.concatenate([jnp.where(first if h == 0 else jnp.logical_not(first), q_ref[0, p], zero)
                         for p in range(n_pairs) for h in range(2)], axis=0)

    def scores(i):
        k = k_ref[0, 0, i * sub:(i + 1) * sub, :]
        return lax.dot_general(k, q, (((1,), (1,)), ((), ())), preferred_element_type=F32)

    def weighted_values(s, m_new, i):
        p = jnp.exp2(s - m_new).astype(BF16)
        g0 = i * sub // LANES
        vt = jnp.concatenate([vt_ref[0, 0, g0 + g] for g in range(sub // LANES)], axis=1)
        return jnp.dot(vt, p, preferred_element_type=F32)

    def attend(n_sub):
        s = scores(0)
        m_prev = acc = None
        for i in range(n_sub):
            s_next = scores(i + 1) if i + 1 < n_sub else None
            m_new = jnp.max(s, axis=0, keepdims=True)
            if m_prev is None:
                acc = weighted_values(s, m_new, i)
            else:
                m_new = jnp.maximum(m_prev, m_new)
                acc = jnp.exp2(m_prev - m_new) * acc + weighted_values(s, m_new, i)
            m_prev, s = m_new, s_next
        out_t = acc[0:HEAD_DIM] / acc[HEAD_DIM:HEAD_DIM + 1]
        for p in range(n_pairs):
            pair_t = jnp.concatenate([out_t[:, (2 * p) * tq:(2 * p + 1) * tq],
                                      out_t[:, (2 * p + 1) * tq:(2 * p + 2) * tq]], axis=0)
            o_ref[0, p] = pair_t.T

    is_ctx = qi * tq < n_ctx

    @pl.when(is_ctx)
    def _():
        attend(n_ctx // sub)

    @pl.when(jnp.logical_not(is_ctx))
    def _():
        attend(n_keys // sub)


def _flash_attention(qh, kh, vt, n_ctx):
    n_batch, q_slabs, tt_all, _ = qh.shape
    n_kvh = kh.shape[1]
    n_pairs = q_slabs // n_kvh
    tq = min(ATTN_TQ, n_ctx)
    sub = min(ATTN_SUB, n_ctx)
    assert n_ctx % tq == 0 and n_ctx % sub == 0 and tt_all % sub == 0 and sub % LANES == 0 and tq % LANES == 0
    return pl.pallas_call(
        functools.partial(_flash_kernel, n_ctx, sub),
        out_shape=jax.ShapeDtypeStruct((n_batch, q_slabs, tt_all, LANES), F32),
        grid=(n_batch, n_kvh, tt_all // tq),
        in_specs=[pl.BlockSpec((1, n_pairs, tq, LANES), lambda b, h, qi: (b, h, qi, 0)),
                  pl.BlockSpec((1, 1, tt_all, LANES), lambda b, h, qi: (b, h, 0, 0)),
                  pl.BlockSpec((1, 1, tt_all // LANES, VT_ROWS, LANES), lambda b, h, qi: (b, h, 0, 0, 0))],
        out_specs=pl.BlockSpec((1, n_pairs, tq, LANES), lambda b, h, qi: (b, h, qi, 0)),
        compiler_params=_cparams("parallel", "parallel", "parallel"),
        name="flash_attention",
    )(qh, kh, vt)


def _s5_param_kernel(are_ref, aim_ref, ldt_ref, bre_ref, bim_ref, abar_re, abar_im, bf_re, bf_im):
    lam_re = jnp.minimum(are_ref[0], LAMBDA_RE_MAX)
    lam_im = aim_ref[0]
    dt = jnp.exp(ldt_ref[0])
    mag = jnp.exp(lam_re * dt)
    ar, ai = mag * jnp.cos(lam_im * dt), mag * jnp.sin(lam_im * dt)
    nr, ni = ar - 1.0, ai
    den = lam_re * lam_re + lam_im * lam_im
    cr = (nr * lam_re + ni * lam_im) / den
    cim = (ni * lam_re - nr * lam_im) / den
    abar_re[0], abar_im[0] = ar, ai
    bre, bim = bre_ref[...], bim_ref[...]
    bf_re[0] = cr * bre - cim * bim
    bf_im[0] = cr * bim + cim * bre


def _s5_params(a_re, a_im, log_dt, b_re, b_im):
    _, n_g, n_s = a_re.shape
    n_i = b_re.shape[-1]
    gn = n_g * n_s
    flat = lambda a: a.reshape(2, 1, gn)
    dt = jnp.broadcast_to(log_dt[:, :, None], (2, n_g, n_s))
    bt = lambda b: b.reshape(gn, n_i).T
    vec = pl.BlockSpec((1, 1, gn), lambda d: (d, 0, 0))
    mat = pl.BlockSpec((n_i, gn), lambda d: (0, 0))
    omat = pl.BlockSpec((1, n_i, gn), lambda d: (d, 0, 0))
    return pl.pallas_call(
        _s5_param_kernel,
        out_shape=[jax.ShapeDtypeStruct((2, 1, gn), F32)] * 2 + [jax.ShapeDtypeStruct((2, n_i, gn), F32)] * 2,
        grid=(2,),
        in_specs=[vec, vec, vec, mat, mat],
        out_specs=[vec, vec, omat, omat],
        compiler_params=_cparams("parallel"),
        name="s5_params",
    )(flat(a_re), flat(a_im), flat(dt), bt(b_re), bt(b_im))


def _s5_scan_kernel(u_ref, ar_ref, ai_ref, bre_ref, bim_ref, cre_ref, cim_ref, y_ref,
                    xr_ref, xi_ref, hr_ref, hi_ref):
    d = pl.program_id(0)
    c = pl.program_id(1)
    steps = u_ref.shape[0] // SUBLANES
    width = xr_ref.shape[1]

    @pl.when(c == 0)
    def _():
        hr_ref[...] = jnp.zeros_like(hr_ref)
        hi_ref[...] = jnp.zeros_like(hi_ref)

    n_split, kw, sw = bre_ref.shape[1], bre_ref.shape[2], bre_ref.shape[3]
    u = u_ref[...].astype(BF16)
    for h in range(n_split):
        uh = u[:, h * kw:(h + 1) * kw]
        xr_ref[:, h * sw:(h + 1) * sw] = jnp.dot(uh, bre_ref[0, h], preferred_element_type=F32)
        xi_ref[:, h * sw:(h + 1) * sw] = jnp.dot(uh, bim_ref[0, h], preferred_element_type=F32)

    for j in range(width // SSM_LANE_TILE):
        sl = pl.ds(j * SSM_LANE_TILE, SSM_LANE_TILE)
        ar = jnp.broadcast_to(ar_ref[0, :, sl], (SUBLANES, SSM_LANE_TILE))
        ai = jnp.broadcast_to(ai_ref[0, :, sl], (SUBLANES, SSM_LANE_TILE))

        def step(i, carry):
            hr, hi = carry
            t = jnp.where(d == 0, i, steps - 1 - i)
            rows = pl.ds(pl.multiple_of(t * SUBLANES, SUBLANES), SUBLANES)
            nr = ar * hr - ai * hi + xr_ref[rows, sl]
            ni = ar * hi + ai * hr + xi_ref[rows, sl]
            xr_ref[rows, sl] = nr
            xi_ref[rows, sl] = ni
            return nr, ni

        hr, hi = lax.fori_loop(0, steps, step, (hr_ref[:, sl], hi_ref[:, sl]), unroll=8)
        hr_ref[:, sl] = hr
        hi_ref[:, sl] = hi

    for h in range(n_split):
        xs = pl.ds(h * sw, sw)
        y_ref[0, :, h * kw:(h + 1) * kw] = (
            jnp.dot(xr_ref[:, xs].astype(BF16), cre_ref[h], preferred_element_type=F32)
            - jnp.dot(xi_ref[:, xs].astype(BF16), cim_ref[h], preferred_element_type=F32))


def _s5_scan(u, abar_re, abar_im, b_re, b_im, c_re, c_im, n_batch, n_ctx):
    rows, width = u.shape
    gn = abar_re.shape[-1]
    tm = SSM_CHUNK * n_batch
    n_blk, n_ctx_blk = rows // tm, n_ctx * n_batch // tm
    blk = lambda d, c: _scan_block(d, c, n_ctx_blk, n_blk)
    vec = pl.BlockSpec((1, 1, gn), lambda d, c: (d, 0, 0))
    bmat = pl.BlockSpec((1,) + b_re.shape[1:], lambda d, c: (d, 0, 0, 0))
    cmat = pl.BlockSpec(c_re.shape, lambda d, c: (0, 0, 0))
    return pl.pallas_call(
        _s5_scan_kernel,
        out_shape=jax.ShapeDtypeStruct((2, rows, width), F32),
        grid=(2, n_blk),
        in_specs=[pl.BlockSpec((tm, width), lambda d, c: (blk(d, c), 0)), vec, vec, bmat, bmat, cmat, cmat],
        out_specs=pl.BlockSpec((1, tm, width), lambda d, c: (d, blk(d, c), 0)),
        scratch_shapes=[pltpu.VMEM((tm, gn), F32), pltpu.VMEM((tm, gn), F32),
                        pltpu.VMEM((SUBLANES, gn), F32), pltpu.VMEM((SUBLANES, gn), F32)],
        compiler_params=_cparams("parallel", "arbitrary"),
        name="s5_scan",
    )(u, abar_re, abar_im, b_re, b_im, c_re, c_im)


def _gelu_tanh(x):
    return 0.5 * x * (1.0 + jnp.tanh(math.sqrt(2.0 / math.pi) * (x + 0.044715 * (x * x * x))))


def _s5_readout(y_ref, u, d_skip, glu_w, glu_b):
    y = _gelu_tanh(y_ref[0] + y_ref[1] + d_skip * u)
    return y * _sigmoid(jnp.dot(y.astype(BF16), glu_w, preferred_element_type=F32) + glu_b)


def _merge_kernel(alpha, ysa_ref, g_ref, bonus_ref, gnw_ref, gnb_ref, ones_ref,
                  yb_ref, ysc_ref, u_ref, dskip_ref, gluw_ref, glub_ref,
                  pg_ref, x_ref, gt_ref, pa_ref, pb_ref, pc_ref, wo_ref, lng_ref, lnb_ref, o_ref, yb_rows):
    d = x_ref.shape[1]
    ya = _rwkv_readout(ysa_ref, g_ref[...], bonus_ref[...], gnw_ref[...], gnb_ref[...], ones_ref[...])
    yc = _s5_readout(ysc_ref, u_ref[...], dskip_ref[...], gluw_ref[...], glub_ref[...])
    n_batch, n_slabs, tt = yb_ref.shape[0], yb_ref.shape[1], yb_ref.shape[2]
    for b in range(n_batch):
        for j in range(n_slabs):
            yb_rows[j, pl.ds(b, tt, stride=n_batch), :] = yb_ref[b, j]
    yb = jnp.concatenate([yb_rows[j].astype(BF16) for j in range(n_slabs)], axis=1)
    pg = pg_ref[...]
    merged = (_sigmoid(pg[:, 0:d]) * _mm(ya, pa_ref[...])
              + _sigmoid(pg[:, d:2 * d]) * _mm(yb, pb_ref[...])
              + _sigmoid(pg[:, 2 * d:3 * d]) * _mm(yc, pc_ref[...]))
    mix = _mm(merged, wo_ref[...])
    o_ref[...] = _layer_norm(alpha * x_ref[...] + _rows_mod(mix, gt_ref[0]), lng_ref[...], lnb_ref[...])


def _merge(rwkv, yb, s5, pg, x_all, modtab, proj_a, proj_b, proj_c, w_out, ln_g, ln_b, alpha, n_ctx_rows):
    rows, d = x_all.shape
    n_batch, n_slabs = yb.shape[0], yb.shape[1]
    ysa, g, bonus, gn_w, gn_b, ones_bd = rwkv
    ysc, u, d_skip, glu_w, glu_b = s5
    tm = ROW_TILE
    nctx = n_ctx_rows // tm
    seg = lambda i: (i >= nctx).astype(jnp.int32)
    row = lambda a: pl.BlockSpec((tm, a.shape[1]), lambda i: (i, 0))
    full = lambda a: pl.BlockSpec(a.shape, lambda i: (0, 0), pipeline_mode=pl.Buffered(1))
    vec = lambda a: pl.BlockSpec((1, a.shape[1]), lambda i: (0, 0))
    return pl.pallas_call(
        functools.partial(_merge_kernel, alpha),
        out_shape=jax.ShapeDtypeStruct((rows, d), F32),
        grid=(rows // tm,),
        scratch_shapes=[pltpu.VMEM((n_slabs, tm, LANES), F32)],
        in_specs=[pl.BlockSpec((2, ysa.shape[1], tm, LANES), lambda i: (0, 0, i, 0)),
                  row(g), row(bonus), vec(gn_w), vec(gn_b), full(ones_bd),
                  pl.BlockSpec((n_batch, n_slabs, tm // n_batch, LANES), lambda i: (0, 0, i, 0)),
                  pl.BlockSpec((2, tm, u.shape[1]), lambda i: (0, i, 0)), row(u), vec(d_skip), full(glu_w), vec(glu_b),
                  row(pg), row(x_all),
                  pl.BlockSpec((1, SUBLANES, d), lambda i: (seg(i), 0, 2)),
                  full(proj_a), full(proj_b), full(proj_c), full(w_out), vec(ln_g), vec(ln_b)],
        out_specs=pl.BlockSpec((tm, d), lambda i: (i, 0)),
        compiler_params=_cparams("parallel"),
        name="merge_ln",
    )(ysa, g, bonus, gn_w, gn_b, ones_bd, yb, ysc, u, d_skip, glu_w, glu_b,
      pg, x_all, modtab, proj_a, proj_b, proj_c, w_out, ln_g, ln_b)


def _mlp_kernel(alpha, x_ref, sh_ref, sc_ref, gt_ref, w1_ref, w2_ref, g_ref, b_ref, o_ref, h_ref, acc_ref):
    j = pl.program_id(1)

    @pl.when(j == 0)
    def _():
        h_ref[...] = _rows_add(_rows_mod(x_ref[...], 1.0 + sc_ref[0]), sh_ref[0]).astype(BF16)
        acc_ref[...] = jnp.zeros_like(acc_ref)

    a = jnp.maximum(jnp.dot(h_ref[...], w1_ref[...], preferred_element_type=F32), 0.0)
    acc_ref[...] += jnp.dot((a * a).astype(BF16), w2_ref[...], preferred_element_type=F32)

    @pl.when(j == pl.num_programs(1) - 1)
    def _():
        o_ref[...] = _layer_norm(alpha * x_ref[...] + _rows_mod(acc_ref[...], gt_ref[0]),
                                 g_ref[...], b_ref[...])


def _mlp(x_mid, modtab, w1, w2, ln_g, ln_b, alpha, n_ctx_rows):
    rows, d = x_mid.shape
    ff = w1.shape[1]
    tm = min(MLP_ROW_TILE, n_ctx_rows)
    tf = min(MLP_FF_TILE, ff)
    nctx = n_ctx_rows // tm
    seg = lambda i: (i >= nctx).astype(jnp.int32)
    mod = lambda col: pl.BlockSpec((1, SUBLANES, d), lambda i, j: (seg(i), 0, col))
    vec = pl.BlockSpec((1, d), lambda i, j: (0, 0))
    return pl.pallas_call(
        functools.partial(_mlp_kernel, alpha),
        out_shape=jax.ShapeDtypeStruct((rows, d), F32),
        grid=(rows // tm, ff // tf),
        in_specs=[pl.BlockSpec((tm, d), lambda i, j: (i, 0)), mod(3), mod(4), mod(5),
                  pl.BlockSpec((d, tf), lambda i, j: (0, j)),
                  pl.BlockSpec((tf, d), lambda i, j: (j, 0)), vec, vec],
        out_specs=pl.BlockSpec((tm, d), lambda i, j: (i, 0)),
        scratch_shapes=[pltpu.VMEM((tm, d), BF16), pltpu.VMEM((tm, d), F32)],
        compiler_params=_cparams("parallel", "arbitrary"),
        name="mlp_ln",
    )(x_mid, modtab, modtab, modtab, w1, w2, ln_g, ln_b)


def _block_diag_ones(n, blk):
    i = jnp.arange(n) // blk
    return (i[:, None] == i[None, :]).astype(BF16)


def _rope_tables(n_ctx, n_lat):
    pairs = HEAD_DIM // 4
    rows = n_lat // GRID_W
    row = jnp.repeat(jnp.arange(rows, dtype=F32), GRID_W)
    col = jnp.tile(jnp.arange(GRID_W, dtype=F32), rows)
    inv = ROPE_THETA ** (-jnp.arange(pairs, dtype=F32) / pairs)
    ang = jnp.stack([row, col], axis=-1)[:, :, None] * inv
    ang = jnp.broadcast_to(ang[:, :, None, :], (n_lat, 2, 2, pairs)).reshape(n_lat, HEAD_DIM)
    cos = jnp.concatenate([jnp.ones((n_ctx, HEAD_DIM), F32), jnp.cos(ang)], axis=0)
    sin = jnp.concatenate([jnp.zeros((n_ctx, HEAD_DIM), F32), jnp.sin(ang)], axis=0)
    first = (jnp.arange(HEAD_DIM) % (2 * pairs)) < pairs
    sin_a = jnp.where(first, -sin, 0.0)
    sin_b = jnp.where(first, 0.0, sin)
    two = lambda a: jnp.concatenate([a, a], axis=1)
    return two(cos), two(sin_a), two(sin_b)


def _block_diag(blocks):
    g, a, b = blocks.shape
    eye = jnp.eye(g, dtype=blocks.dtype)
    return (eye[:, None, :, None] * blocks[:, :, None, :]).reshape(g * a, g * b)


def kernel(x, c, ctx, c_ctx, mod_w, mod_b, w_in, rwkv_mu, rwkv_w0, rwkv_w_up, rwkv_a0, rwkv_a_up, rwkv_g_up, rwkv_k_k, rwkv_k_a, rwkv_r_k, rwkv_gn_w, rwkv_gn_b, attn_q_gain, attn_k_gain, ssm_a_re, ssm_a_im, ssm_log_dt, ssm_b_re, ssm_b_im, ssm_c_re, ssm_c_im, ssm_d, ssm_glu_w, ssm_glu_b, proj_a, proj_b, proj_c, w_out, ln1_g, ln1_b, ln2_g, ln2_b, mlp_w1, mlp_w2):
    n_batch, n_lat, d_model = x.shape
    n_ctx = ctx.shape[1]
    depth = mod_w.shape[0]
    assert n_batch == SUBLANES, "row layout packs the batch into one sublane tile"
    alpha = (2 * depth) ** 0.25
    n_ctx_rows = n_ctx * n_batch

    rwkv_w = rwkv_k_k.shape[1]
    q_w = proj_b.shape[1]
    kv_w = q_w // GQA_GROUP
    ssm_w = ssm_d.shape[1]
    rwkv_cols = rwkv_mu.shape[1]
    edges = [0, rwkv_cols, rwkv_cols + q_w, rwkv_cols + q_w + kv_w, rwkv_cols + q_w + 2 * kv_w,
             rwkv_cols + q_w + 2 * kv_w + ssm_w, w_in.shape[2]]
    slabbed = (False, True, True, True, False, False)
    splits = tuple(zip(edges[:-1], edges[1:], slabbed))

    to_rows = lambda a: a.transpose(1, 0, 2).reshape(a.shape[1] * n_batch, d_model)
    x_all = jnp.concatenate([to_rows(ctx), to_rows(x)], axis=0)

    cvec = jnp.concatenate([jnp.broadcast_to(c_ctx[None], (n_batch, d_model)), c], axis=0)
    modtab = _modulation(cvec, mod_w, mod_b).reshape(depth, 2, n_batch, 6 * d_model)

    ones_rwkv = _block_diag_ones(rwkv_w, HEAD_DIM)
    ones_pair = _block_diag_ones(LANES, HEAD_DIM)
    cos, sin_a, sin_b = (jnp.repeat(t, n_batch, axis=0) for t in _rope_tables(n_ctx, n_lat))
    rank = rwkv_w_up.shape[2]
    zpad = jnp.zeros((depth, 2, rank, rwkv_w), F32)

    for l in range(depth):
        mt = modtab[l]
        prm = dict(mu=rwkv_mu[l][None], w0=rwkv_w0[l][:, None], a0=rwkv_a0[l][:, None],
                   w_up=jnp.concatenate([rwkv_w_up[l], zpad[l]], axis=1).astype(BF16),
                   a_up=jnp.concatenate([zpad[l], rwkv_a_up[l]], axis=1).astype(BF16),
                   g_up=rwkv_g_up[l].astype(BF16), k_k=rwkv_k_k[l][None], k_a=rwkv_k_a[l][None],
                   r_k=rwkv_r_k[l][None], ones_bd=ones_rwkv)
        r, v, kk, g, bonus, lw, kd, bv, pq, pk, pv, pc, pg = _in_projection(
            x_all, mt, w_in[l].astype(BF16), splits, prm, n_ctx_rows)
        ysa = _rwkv_scan(r, v, kk, lw, kd, bv, n_batch, n_ctx)
        rwkv = (ysa, g, bonus, rwkv_gn_w[l][None], rwkv_gn_b[l][None], ones_rwkv)

        tile2 = lambda a: jnp.tile(a, 2)[None]
        qh, kh, vh = _attn_prepare(pq, pk, pv, cos, sin_a, sin_b, tile2(attn_q_gain[l]), tile2(attn_k_gain[l]),
                                   ones_pair, n_batch)
        yb = _flash_attention(qh, kh, vh, n_ctx)

        abar_re, abar_im, bf_re, bf_im = _s5_params(ssm_a_re[l], ssm_a_im[l], ssm_log_dt[l], ssm_b_re[l], ssm_b_im[l])
        n_g = ssm_a_re.shape[2]
        n_split = max(1, ssm_w // MXU_DIM)
        gs = n_g // n_split
        diag = lambda blocks: jnp.stack([_block_diag(blocks[h * gs:(h + 1) * gs]) for h in range(n_split)])
        bmat = lambda bf: jnp.stack([diag(bf[d].reshape(SSM_GROUP, n_g, SSM_STATE).transpose(1, 0, 2))
                                     for d in range(2)]).astype(BF16)
        cmat = lambda cc: diag(cc.transpose(0, 2, 1)).astype(BF16)
        ysc = _s5_scan(pc, abar_re, abar_im, bmat(bf_re), bmat(bf_im), cmat(ssm_c_re[l]), cmat(ssm_c_im[l]),
                       n_batch, n_ctx)
        s5 = (ysc, pc, ssm_d[l][None], ssm_glu_w[l].astype(BF16), ssm_glu_b[l][None])

        x_mid = _merge(rwkv, yb, s5, pg, x_all, mt, proj_a[l].astype(BF16), proj_b[l].astype(BF16),
                       proj_c[l].astype(BF16), w_out[l].astype(BF16), ln1_g[l][None], ln1_b[l][None],
                       alpha, n_ctx_rows)
        x_all = _mlp(x_mid, mt, mlp_w1[l].astype(BF16), mlp_w2[l].astype(BF16), ln2_g[l][None], ln2_b[l][None],
                     alpha, n_ctx_rows)

    out = x_all[n_ctx_rows:].reshape(n_lat, n_batch, d_model).transpose(1, 0, 2)
    return out
```

```python
import functools
import math

import jax
import jax.numpy as jnp
from jax import lax
from jax.experimental import pallas as pl
from jax.experimental.pallas import tpu as pltpu

HEAD_DIM = 64
GRID_W = 64
ROPE_THETA = 10000.0
LN_EPS = 1e-5
RMS_EPS = 1e-6
GN_EPS = 64e-5
KK_EPS = 1e-24
LAMBDA_RE_MAX = -1e-4
GQA_GROUP = 4
SSM_GROUP = 16
SSM_STATE = 64

LANES = 128
SUBLANES = 8
MXU_DIM = 256
VMEM_LIMIT = 56 * 1024 * 1024

ROW_TILE = 256
MLP_ROW_TILE = 512
MLP_FF_TILE = 1024
SCAN_CHUNK = 64
SCAN_PAIRS = 2
SSM_CHUNK = 64
SSM_LANE_TILE = 512
ATTN_TQ = 256
ATTN_SUB = 256
ATTN_AHEAD = 2
PREP_TT = 128
VT_ROWS = 80

F32 = jnp.float32
BF16 = jnp.bfloat16


def _cparams(*sem):
    return pltpu.CompilerParams(dimension_semantics=sem, vmem_limit_bytes=VMEM_LIMIT)


def _mm(a, b):
    return jnp.dot(a.astype(BF16), b.astype(BF16), preferred_element_type=F32)


def _mm_nt(a, b):
    return lax.dot_general(a.astype(BF16), b.astype(BF16), (((1,), (1,)), ((), ())),
                           preferred_element_type=F32)


def _mm_tn(a, b):
    return lax.dot_general(a.astype(BF16), b.astype(BF16), (((0,), (0,)), ((), ())),
                           preferred_element_type=F32)


def _segsum(x, ones_bd):
    hi = x.astype(BF16)
    lo = (x - hi.astype(F32)).astype(BF16)
    return (jnp.dot(hi, ones_bd, preferred_element_type=F32)
            + jnp.dot(lo, ones_bd, preferred_element_type=F32))


def _sigmoid(x):
    return 1.0 / (1.0 + jnp.exp(-x))


def _softplus(x):
    return jnp.maximum(x, 0.0) + jnp.log(1.0 + jnp.exp(-jnp.abs(x)))


def _layer_norm(z, g, b):
    mu = jnp.mean(z, axis=-1, keepdims=True)
    zc = z - mu
    var = jnp.mean(zc * zc, axis=-1, keepdims=True)
    return zc * lax.rsqrt(var + LN_EPS) * g + b


def _rows_mod(x, m):
    tm, n = x.shape
    return (x.reshape(tm // SUBLANES, SUBLANES, n) * m[None]).reshape(tm, n)


def _rows_add(x, m):
    tm, n = x.shape
    return (x.reshape(tm // SUBLANES, SUBLANES, n) + m[None]).reshape(tm, n)


def _mod_kernel(c_ref, w_ref, b_ref, o_ref):
    c = c_ref[...]
    s = c * _sigmoid(c)
    o_ref[0] = jnp.dot(s, w_ref[0], preferred_element_type=F32,
                       precision=lax.Precision.HIGHEST) + b_ref[0]


def _modulation(cvec, mod_w, mod_b):
    n_layers, d, n6 = mod_w.shape
    tn = 1024 if n6 % 1024 == 0 else n6
    return pl.pallas_call(
        _mod_kernel,
        out_shape=jax.ShapeDtypeStruct((n_layers, cvec.shape[0], n6), F32),
        grid=(n_layers, n6 // tn),
        in_specs=[pl.BlockSpec((cvec.shape[0], d), lambda l, j: (0, 0)),
                  pl.BlockSpec((1, d, tn), lambda l, j: (l, 0, j)),
                  pl.BlockSpec((1, 1, tn), lambda l, j: (l, 0, j))],
        out_specs=pl.BlockSpec((1, cvec.shape[0], tn), lambda l, j: (l, 0, j)),
        compiler_params=_cparams("parallel", "parallel"),
        name="modulation",
    )(cvec, mod_w, mod_b.reshape(n_layers, 1, n6))


def _inproj_kernel(splits, n_ctx_rows, n_rows, x_ref, xp_ref, xn_ref, sh_ref, sc_ref, w_ref,
                   mu_ref, w0_ref, wup_ref, a0_ref, aup_ref, gup_ref, kk_ref, ka_ref, rk_ref, ones_ref, *o_refs):
    i = pl.program_id(0)
    tm = x_ref.shape[0]
    r0 = i * tm
    seg_start = jnp.logical_or(r0 == 0, r0 == n_ctx_rows)
    seg_end = jnp.logical_or(r0 + tm == n_ctx_rows, r0 + tm == n_rows)
    scale, shift = 1.0 + sc_ref[0], sh_ref[0]
    mod = lambda x: _rows_add(_rows_mod(x, scale), shift)
    xm = mod(x_ref[...])
    x_ext = jnp.concatenate([jnp.where(seg_start, 0.0, mod(xp_ref[...])), xm,
                             jnp.where(seg_end, 0.0, mod(xn_ref[...]))], axis=0).astype(BF16)
    a, b, _ = splits[0]
    n_rwkv = 8
    _rwkv_prep(jnp.dot(x_ext, w_ref[:, a:b], preferred_element_type=F32), mu_ref[...], w0_ref, wup_ref,
               a0_ref, aup_ref, gup_ref[...], kk_ref[...], ka_ref[...], rk_ref[...], ones_ref[...],
               o_refs[:n_rwkv])
    xm = xm.astype(BF16)
    for (a, b, slabbed), o_ref in zip(splits[1:], o_refs[n_rwkv:]):
        y = jnp.dot(xm, w_ref[:, a:b], preferred_element_type=F32)
        if slabbed:
            for j in range((b - a) // LANES):
                o_ref[j] = y[:, j * LANES:(j + 1) * LANES]
        else:
            o_ref[...] = y


def _in_projection(x_all, modtab, w_bf16, splits, prm, n_ctx_rows):
    rows, d = x_all.shape
    width = prm["k_k"].shape[-1]
    tm = ROW_TILE
    nctx = n_ctx_rows // tm
    hb = tm // SUBLANES
    last = rows // SUBLANES - 1
    seg = lambda i: (i >= nctx).astype(jnp.int32)
    n_slabs = width // LANES
    slab = pl.BlockSpec((n_slabs, tm, LANES), lambda i: (0, i, 0))
    slab2 = pl.BlockSpec((2, n_slabs, tm, LANES), lambda i: (0, 0, i, 0))
    row = pl.BlockSpec((tm, width), lambda i: (i, 0))
    flat = jax.ShapeDtypeStruct((rows, width), F32)
    one = jax.ShapeDtypeStruct((n_slabs, rows, LANES), F32)
    two = jax.ShapeDtypeStruct((2, n_slabs, rows, LANES), F32)
    shapes = [one, one, one, flat, flat, two, two, two]
    specs = [slab] * 3 + [row] * 2 + [slab2] * 3
    for a, b, slabbed in splits[1:]:
        if slabbed:
            shapes.append(jax.ShapeDtypeStruct(((b - a) // LANES, rows, LANES), F32))
            specs.append(pl.BlockSpec(((b - a) // LANES, tm, LANES), lambda i: (0, i, 0)))
        else:
            shapes.append(jax.ShapeDtypeStruct((rows, b - a), F32))
            specs.append(pl.BlockSpec((tm, b - a), lambda i: (i, 0)))
    consts = [prm["mu"], prm["w0"], prm["w_up"], prm["a0"], prm["a_up"], prm["g_up"],
              prm["k_k"], prm["k_a"], prm["r_k"], prm["ones_bd"]]
    full = lambda a: pl.BlockSpec(a.shape, lambda i: (0,) * a.ndim)
    return pl.pallas_call(
        functools.partial(_inproj_kernel, splits, n_ctx_rows, rows),
        out_shape=shapes,
        grid=(rows // tm,),
        in_specs=[pl.BlockSpec((tm, d), lambda i: (i, 0)),
                  pl.BlockSpec((SUBLANES, d), lambda i: (jnp.maximum(i * hb - 1, 0), 0)),
                  pl.BlockSpec((SUBLANES, d), lambda i: (jnp.minimum((i + 1) * hb, last), 0)),
                  pl.BlockSpec((1, SUBLANES, d), lambda i: (seg(i), 0, 0)),
                  pl.BlockSpec((1, SUBLANES, d), lambda i: (seg(i), 0, 1)),
                  pl.BlockSpec(w_bf16.shape, lambda i: (0, 0), pipeline_mode=pl.Buffered(1))]
                 + [full(a) for a in consts],
        out_specs=specs,
        compiler_params=_cparams("parallel"),
        name="in_projection",
    )(x_all, x_all, x_all, modtab, modtab, w_bf16, *consts)


def _rwkv_prep(p_ext, mu, w0_ref, wup_ref, a0_ref, aup_ref, g_up, k_k, k_a, r_k, ones_bd, outs):
    r_out, v_out, kkn_out, g_out, bonus_out, lw_out, kd_out, bv_out = outs
    tm = p_ext.shape[0] - 2 * SUBLANES
    p = p_ext[SUBLANES:SUBLANES + tm]
    p = p + mu * (0.5 * (p_ext[0:tm] + p_ext[2 * SUBLANES:]) - p)

    w = k_k.shape[-1]
    r, k, v = p[:, 0:w], p[:, w:2 * w], p[:, 2 * w:3 * w]
    wa = p[:, 3 * w:3 * w + LANES]
    gd = p[:, 3 * w + LANES:3 * w + 2 * LANES]

    def put(o_ref, y, *lead):
        for j in range(w // LANES):
            o_ref[lead + (j,)] = y[:, j * LANES:(j + 1) * LANES]

    g_out[...] = _mm(_sigmoid(gd), g_up)
    kk = k * k_k
    kk = kk * lax.rsqrt(jnp.maximum(_segsum(kk * kk, ones_bd), KK_EPS))
    put(r_out, r)
    put(v_out, v)
    put(kkn_out, kk)
    tanh_wa = jnp.tanh(wa)
    rk_sum = jnp.zeros_like(r)
    for d in range(2):
        w_pre = w0_ref[d] + _mm(tanh_wa, wup_ref[d])
        put(lw_out, -jnp.exp(-_softplus(-w_pre) - 0.5), d)
        a = _sigmoid(a0_ref[d] + _mm(wa, aup_ref[d]))
        k_dir = k * (1.0 + (a - 1.0) * k_a)
        put(kd_out, k_dir, d)
        put(bv_out, kk * a, d)
        rk_sum = rk_sum + r * k_dir
    bonus_out[...] = _segsum(rk_sum * r_k, ones_bd) * v


def _scan_block(d, c, n_ctx_blk, n_blk):
    bwd = jnp.where(c < n_ctx_blk, n_ctx_blk - 1 - c, n_ctx_blk + (n_blk - 1 - c))
    return jnp.where(d == 0, c, bwd)


def _tri_inverse(a_list, ri, ci, n):
    eye = (ri == ci).astype(F32)
    same = (ri // SUBLANES) == (ci // SUBLANES)
    xs = [-jnp.where(same, a, 0.0) for a in a_list]
    x2 = [_mm(x, x) for x in xs]
    x4 = [_mm(x, x) for x in x2]
    ts = [eye + x for x in xs]
    ts = [t + _mm(t, x) for t, x in zip(ts, x2)]
    ts = [t + _mm(t, x) for t, x in zip(ts, x4)]
    size = SUBLANES
    while size < n:
        pick = jnp.logical_and((ri // (2 * size)) == (ci // (2 * size)), (ri // size) != (ci // size))
        lows = [_mm(jnp.where(pick, a, 0.0), t) for a, t in zip(a_list, ts)]
        ts = [t - _mm(t, low) for t, low in zip(ts, lows)]
        size *= 2
    return ts


def _rwkv_scan_kernel(n_batch, r_ref, v_ref, kk_ref, lw_ref, kd_ref, bv_ref, y_ref, state_ref):
    d = pl.program_id(0)
    c = pl.program_id(2)
    n = r_ref.shape[1] // n_batch
    m = 2 * n
    chains = [(p, b) for p in range(r_ref.shape[0]) for b in range(n_batch)]
    batches = range(len(chains))

    @pl.when(c == 0)
    def _():
        state_ref[...] = jnp.zeros_like(state_ref)

    sign = 1 - 2 * d
    ri = lax.broadcasted_iota(jnp.int32, (m, m), 0)
    ci = lax.broadcasted_iota(jnp.int32, (m, m), 1)
    same_head = (ri // n) == (ci // n)
    order = (ri - ci) * sign
    strict = jnp.logical_and(same_head, order > 0)
    incl = jnp.logical_and(same_head, order >= 0)
    ti = lax.broadcasted_iota(jnp.int32, (n, n), 0)
    si = lax.broadcasted_iota(jnp.int32, (n, n), 1)
    tri = ((ti - si) * sign >= 0).astype(BF16)
    head0 = lax.broadcasted_iota(jnp.int32, (n, LANES), 1) < HEAD_DIM
    blockdiag = (lax.broadcasted_iota(jnp.int32, (LANES, LANES), 0) // HEAD_DIM
                 == lax.broadcasted_iota(jnp.int32, (LANES, LANES), 1) // HEAD_DIM)

    def stack(x):
        return jnp.concatenate([jnp.where(head0, x, 0.0), jnp.where(head0, 0.0, x)], axis=0)

    def shared(ref, i):
        p, b = chains[i]
        return ref[p, pl.ds(b, n, stride=n_batch), :]

    def split(ref, i):
        p, b = chains[i]
        return ref[0, p, pl.ds(b, n, stride=n_batch), :]

    duos = range(0, len(chains), 2)
    cat = lambda xs, i: jnp.concatenate([xs[i], xs[i + 1]], axis=1)
    uncat = lambda xs: [x[:, o:o + LANES] for x in xs for o in (0, LANES)]

    lw = [split(lw_ref, b) for b in batches]
    hi = [x.astype(BF16) for x in lw]
    lo = [(x - h.astype(F32)).astype(BF16) for x, h in zip(lw, hi)]
    cum = uncat([jnp.dot(tri, cat(hi, i), preferred_element_type=F32)
                 + jnp.dot(tri, cat(lo, i), preferred_element_type=F32) for i in duos])
    e_cum = [jnp.exp(x) for x in cum]
    e_neg = [jnp.exp(-x) for x in cum]
    kap = [shared(kk_ref, b) * jnp.exp(cum[b] - lw[b]) for b in batches]
    rt = [shared(r_ref, b) * e_cum[b] for b in batches]
    bt = [split(bv_ref, b) * e_neg[b] for b in batches]
    kt = [split(kd_ref, b) * e_neg[b] for b in batches]
    v = [shared(v_ref, b) for b in batches]
    total = [jnp.where(d == 0, e[n - 1:n], e[0:1]) for e in e_cum]
    state = [state_ref[b] for b in batches]

    scores = [_mm_nt(jnp.concatenate([stack(kap[b]), stack(rt[b])], axis=0),
                     jnp.concatenate([stack(bt[b]), stack(kt[b])], axis=0)) for b in batches]
    a_b = [jnp.where(strict, s[0:m, 0:m], 0.0) for s in scores]
    a_k = [jnp.where(strict, s[0:m, m:2 * m], 0.0) for s in scores]
    a_r = [jnp.concatenate([jnp.where(incl, s[m:2 * m, m:2 * m], 0.0),
                            jnp.where(incl, -s[m:2 * m, 0:m], 0.0)], axis=1) for s in scores]
    g0 = [_mm_nt(jnp.concatenate([kap[b], rt[b]], axis=0), state[b]) for b in batches]
    t_inv = _tri_inverse(a_b, ri, ci, n)
    vs = [stack(x) for x in v]
    rhs = [stack(g0[b][0:n]) + _mm(a_k[b], vs[b]) for b in batches]
    us = [_mm(t_inv[b], rhs[b]) for b in batches]
    ys = [stack(g0[b][n:m]) + _mm(a_r[b], jnp.concatenate([vs[b], us[b]], axis=0)) for b in batches]
    for i in batches:
        p, b = chains[i]
        y_ref[0, p, pl.ds(b, n, stride=n_batch), :] = ys[i][0:n] + ys[i][n:m]
    ds = [_mm_tn(jnp.concatenate([-(us[b][0:n] + us[b][n:m]), v[b]], axis=0),
                 jnp.concatenate([bt[b], kt[b]], axis=0)) for b in batches]
    for b in batches:
        state_ref[b] = (state[b] + jnp.where(blockdiag, ds[b], 0.0)) * total[b]


def _rwkv_scan(r, v, kk, lw, kd, bv, n_batch, n_ctx):
    n_slabs, rows, _ = r.shape
    tm = SCAN_CHUNK * n_batch
    pp = min(SCAN_PAIRS, n_slabs)
    n_blk, n_ctx_blk = rows // tm, n_ctx * n_batch // tm
    blk = lambda d, c: _scan_block(d, c, n_ctx_blk, n_blk)
    shared = pl.BlockSpec((pp, tm, LANES), lambda d, p, c: (p, blk(d, c), 0))
    split = pl.BlockSpec((1, pp, tm, LANES), lambda d, p, c: (d, p, blk(d, c), 0))
    return pl.pallas_call(
        functools.partial(_rwkv_scan_kernel, n_batch),
        out_shape=jax.ShapeDtypeStruct((2, n_slabs, rows, LANES), F32),
        grid=(2, n_slabs // pp, n_blk),
        in_specs=[shared, shared, shared, split, split, split],
        out_specs=split,
        scratch_shapes=[pltpu.VMEM((pp * n_batch, LANES, LANES), F32)],
        compiler_params=_cparams("parallel", "parallel", "arbitrary"),
        name="rwkv_scan",
    )(r, v, kk, lw, kd, bv)


def _rwkv_readout(y_ref, g, bonus, gn_w, gn_b, ones_bd):
    y = jnp.concatenate([y_ref[0, j] + y_ref[1, j] for j in range(y_ref.shape[1])], axis=1)
    inv = 1.0 / HEAD_DIM
    mean = _segsum(y, ones_bd) * inv
    yc = y - mean
    var = _segsum(yc * yc, ones_bd) * inv
    yn = yc * lax.rsqrt(var + GN_EPS) * gn_w + gn_b
    return (yn + bonus) * g


def _norm_rope(x, gain, cos, sin_a, sin_b, ones_bd):
    ms = _segsum(x * x, ones_bd) * (1.0 / HEAD_DIM)
    xn = x * lax.rsqrt(ms + RMS_EPS) * gain
    return (xn * cos + pltpu.roll(xn, LANES - HEAD_DIM // 4, 1) * sin_a
            + pltpu.roll(xn, HEAD_DIM // 4, 1) * sin_b)


def _attn_prep_kernel(scale, n_batch, q_ref, k_ref, v_ref, cos_ref, sa_ref, sb_ref, qg_ref, kg_ref, ones_ref,
                      qo_ref, ko_ref, vo_ref, scr):
    cos, sin_a, sin_b = cos_ref[...], sa_ref[...], sb_ref[...]
    ones_bd = ones_ref[...]
    tm = cos.shape[0]
    tt = tm // n_batch
    first = lax.broadcasted_iota(jnp.int32, (tm, LANES), 1) < HEAD_DIM
    n_q, n_k = q_ref.shape[0], k_ref.shape[0]
    for j in range(n_q):
        scr[j] = _norm_rope(q_ref[j], qg_ref[...], cos, sin_a, sin_b, ones_bd) * scale
    for j in range(n_k):
        y = _norm_rope(k_ref[j], kg_ref[...], cos, sin_a, sin_b, ones_bd)
        swapped = pltpu.roll(y, HEAD_DIM, 1)
        scr[n_q + 2 * j] = jnp.where(first, y, swapped)
        scr[n_q + 2 * j + 1] = jnp.where(first, swapped, y)
    ones_rows = jnp.ones((vo_ref.shape[3] - HEAD_DIM, LANES), BF16)
    for b in range(n_batch):
        rows = pl.ds(b, tt, stride=n_batch)
        for j in range(n_q):
            qo_ref[b, j] = scr[j, rows, :].astype(BF16)
        for h in range(2 * n_k):
            ko_ref[b, h] = scr[n_q + h, rows, :].astype(BF16)
        for j in range(n_k):
            vb = v_ref[j, rows, :]
            for g in range(tt // LANES):
                vt = vb[g * LANES:(g + 1) * LANES].T.astype(BF16)
                for h in range(2):
                    vo_ref[b, 2 * j + h, g, 0:HEAD_DIM] = vt[h * HEAD_DIM:(h + 1) * HEAD_DIM]
                    vo_ref[b, 2 * j + h, g, HEAD_DIM:] = ones_rows


def _attn_prepare(pq, pk, pv, cos, sin_a, sin_b, q_gain, k_gain, ones_pair, n_batch):
    q_slabs, rows, _ = pq.shape
    k_slabs = pk.shape[0]
    tt_all = rows // n_batch
    tt = PREP_TT
    tm = tt * n_batch
    kh = 2 * k_slabs
    tab = pl.BlockSpec((tm, LANES), lambda i: (i, 0))
    vec = pl.BlockSpec((1, LANES), lambda i: (0, 0))
    return pl.pallas_call(
        functools.partial(_attn_prep_kernel, HEAD_DIM ** -0.5 * math.log2(math.e), n_batch),
        out_shape=[jax.ShapeDtypeStruct((n_batch, q_slabs, tt_all, LANES), BF16),
                   jax.ShapeDtypeStruct((n_batch, kh, tt_all, LANES), BF16),
                   jax.ShapeDtypeStruct((n_batch, kh, tt_all // LANES, VT_ROWS, LANES), BF16)],
        grid=(tt_all // tt,),
        in_specs=[pl.BlockSpec((q_slabs, tm, LANES), lambda i: (0, i, 0)),
                  pl.BlockSpec((k_slabs, tm, LANES), lambda i: (0, i, 0)),
                  pl.BlockSpec((k_slabs, tm, LANES), lambda i: (0, i, 0)),
                  tab, tab, tab, vec, vec,
                  pl.BlockSpec(ones_pair.shape, lambda i: (0, 0))],
        out_specs=[pl.BlockSpec((n_batch, q_slabs, tt, LANES), lambda i: (0, 0, i, 0)),
                   pl.BlockSpec((n_batch, kh, tt, LANES), lambda i: (0, 0, i, 0)),
                   pl.BlockSpec((n_batch, kh, tt // LANES, VT_ROWS, LANES), lambda i: (0, 0, i, 0, 0))],
        scratch_shapes=[pltpu.VMEM((q_slabs + kh, tm, LANES), F32)],
        compiler_params=_cparams("parallel"),
        name="attn_prepare",
    )(pq, pk, pv, cos, sin_a, sin_b, q_gain, k_gain, ones_pair)


def _flash_kernel(n_ctx, sub, q_ref, k_ref, vt_ref, o_ref):
    qi = pl.program_id(2)
    n_pairs, tq = q_ref.shape[1], q_ref.shape[2]
    n_keys = k_ref.shape[2]
    first = lax.broadcasted_iota(jnp.int32, (tq, LANES), 1) < HEAD_DIM
    zero = jnp.zeros((tq, LANES), BF16)
    q = jnp.concatenate([jnp.where(first if h == 0 else jnp.logical_not(first), q_ref[0, p], zero)
                         for p in range(n_pairs) for h in range(2)], axis=0)

    def scores(i):
        k = k_ref[0, 0, i * sub:(i + 1) * sub, :]
        return lax.dot_general(k, q, (((1,), (1,)), ((), ())), preferred_element_type=F32)

    def weighted_values(s, m_new, i):
        p = jnp.exp2(s - m_new).astype(BF16)
        g0 = i * sub // LANES
        vt = jnp.concatenate([vt_ref[0, 0, g0 + g] for g in range(sub // LANES)], axis=1)
        return jnp.dot(vt, p, preferred_element_type=F32)

    def attend(n_sub):
        ahead = [scores(i) for i in range(min(ATTN_AHEAD, n_sub))]
        m_prev = acc = None
        for i in range(n_sub):
            s = ahead.pop(0)
            if i + ATTN_AHEAD < n_sub:
                ahead.append(scores(i + ATTN_AHEAD))
            m_new = jnp.max(s, axis=0, keepdims=True)
            if m_prev is None:
                acc = weighted_values(s, m_new, i)
            else:
                m_new = jnp.maximum(m_prev, m_new)
                acc = jnp.exp2(m_prev - m_new) * acc + weighted_values(s, m_new, i)
            m_prev = m_new
        out_t = acc[0:HEAD_DIM] / acc[HEAD_DIM:HEAD_DIM + 1]
        for p in range(n_pairs):
            pair_t = jnp.concatenate([out_t[:, (2 * p) * tq:(2 * p + 1) * tq],
                                      out_t[:, (2 * p + 1) * tq:(2 * p + 2) * tq]], axis=0)
            o_ref[0, p] = pair_t.T

    is_ctx = qi * tq < n_ctx

    @pl.when(is_ctx)
    def _():
        attend(n_ctx // sub)

    @pl.when(jnp.logical_not(is_ctx))
    def _():
        attend(n_keys // sub)


def _flash_attention(qh, kh, vt, n_ctx):
    n_batch, q_slabs, tt_all, _ = qh.shape
    n_kvh = kh.shape[1]
    n_pairs = q_slabs // n_kvh
    tq = min(ATTN_TQ, n_ctx)
    sub = min(ATTN_SUB, n_ctx)
    assert n_ctx % tq == 0 and n_ctx % sub == 0 and tt_all % sub == 0 and sub % LANES == 0 and tq % LANES == 0
    return pl.pallas_call(
        functools.partial(_flash_kernel, n_ctx, sub),
        out_shape=jax.ShapeDtypeStruct((n_batch, q_slabs, tt_all, LANES), F32),
        grid=(n_batch, n_kvh, tt_all // tq),
        in_specs=[pl.BlockSpec((1, n_pairs, tq, LANES), lambda b, h, qi: (b, h, qi, 0)),
                  pl.BlockSpec((1, 1, tt_all, LANES), lambda b, h, qi: (b, h, 0, 0)),
                  pl.BlockSpec((1, 1, tt_all // LANES, VT_ROWS, LANES), lambda b, h, qi: (b, h, 0, 0, 0))],
        out_specs=pl.BlockSpec((1, n_pairs, tq, LANES), lambda b, h, qi: (b, h, qi, 0)),
        compiler_params=_cparams("parallel", "parallel", "parallel"),
        name="flash_attention",
    )(qh, kh, vt)


def _s5_param_kernel(are_ref, aim_ref, ldt_ref, bre_ref, bim_ref, abar_re, abar_im, bf_re, bf_im):
    lam_re = jnp.minimum(are_ref[0], LAMBDA_RE_MAX)
    lam_im = aim_ref[0]
    dt = jnp.exp(ldt_ref[0])
    mag = jnp.exp(lam_re * dt)
    ar, ai = mag * jnp.cos(lam_im * dt), mag * jnp.sin(lam_im * dt)
    nr, ni = ar - 1.0, ai
    den = lam_re * lam_re + lam_im * lam_im
    cr = (nr * lam_re + ni * lam_im) / den
    cim = (ni * lam_re - nr * lam_im) / den
    abar_re[0], abar_im[0] = ar, ai
    bre, bim = bre_ref[...], bim_ref[...]
    bf_re[0] = cr * bre - cim * bim
    bf_im[0] = cr * bim + cim * bre


def _s5_params(a_re, a_im, log_dt, b_re, b_im):
    _, n_g, n_s = a_re.shape
    n_i = b_re.shape[-1]
    gn = n_g * n_s
    flat = lambda a: a.reshape(2, 1, gn)
    dt = jnp.broadcast_to(log_dt[:, :, None], (2, n_g, n_s))
    bt = lambda b: b.reshape(gn, n_i).T
    vec = pl.BlockSpec((1, 1, gn), lambda d: (d, 0, 0))
    mat = pl.BlockSpec((n_i, gn), lambda d: (0, 0))
    omat = pl.BlockSpec((1, n_i, gn), lambda d: (d, 0, 0))
    return pl.pallas_call(
        _s5_param_kernel,
        out_shape=[jax.ShapeDtypeStruct((2, 1, gn), F32)] * 2 + [jax.ShapeDtypeStruct((2, n_i, gn), F32)] * 2,
        grid=(2,),
        in_specs=[vec, vec, vec, mat, mat],
        out_specs=[vec, vec, omat, omat],
        compiler_params=_cparams("parallel"),
        name="s5_params",
    )(flat(a_re), flat(a_im), flat(dt), bt(b_re), bt(b_im))


def _s5_scan_kernel(u_ref, ar_ref, ai_ref, bre_ref, bim_ref, cre_ref, cim_ref, y_ref,
                    xr_ref, xi_ref, hr_ref, hi_ref):
    d = pl.program_id(0)
    c = pl.program_id(1)
    steps = u_ref.shape[0] // SUBLANES
    width = xr_ref.shape[1]

    @pl.when(c == 0)
    def _():
        hr_ref[...] = jnp.zeros_like(hr_ref)
        hi_ref[...] = jnp.zeros_like(hi_ref)

    n_split, kw, sw = bre_ref.shape[1], bre_ref.shape[2], bre_ref.shape[3]
    lt = min(SSM_LANE_TILE, sw)
    tiles = [(h, o) for h in range(n_split) for o in range(0, sw, lt)]
    u = u_ref[...].astype(BF16)

    def drive(tile):
        h, o = tile
        uh = u[:, h * kw:(h + 1) * kw]
        sl = slice(h * sw + o, h * sw + o + lt)
        xr_ref[:, sl] = jnp.dot(uh, bre_ref[0, h, :, o:o + lt], preferred_element_type=F32)
        xi_ref[:, sl] = jnp.dot(uh, bim_ref[0, h, :, o:o + lt], preferred_element_type=F32)

    def scan(tile, reverse):
        h, o = tile
        sl = slice(h * sw + o, h * sw + o + lt)
        ar = jnp.broadcast_to(ar_ref[0, :, sl], (SUBLANES, lt))
        ai = jnp.broadcast_to(ai_ref[0, :, sl], (SUBLANES, lt))
        hr, hi = hr_ref[:, sl], hi_ref[:, sl]
        for i in range(steps):
            t = steps - 1 - i if reverse else i
            rows = slice(t * SUBLANES, (t + 1) * SUBLANES)
            hr, hi = ar * hr - ai * hi + xr_ref[rows, sl], ar * hi + ai * hr + xi_ref[rows, sl]
            xr_ref[rows, sl] = hr
            xi_ref[rows, sl] = hi
        hr_ref[:, sl] = hr
        hi_ref[:, sl] = hi

    def readout(tile):
        h, o = tile
        sl = slice(h * sw + o, h * sw + o + lt)
        return (jnp.dot(xr_ref[:, sl].astype(BF16), cre_ref[h, o:o + lt, :], preferred_element_type=F32)
                - jnp.dot(xi_ref[:, sl].astype(BF16), cim_ref[h, o:o + lt, :], preferred_element_type=F32))

    def run(reverse):
        y = [None] * n_split
        drive(tiles[0])
        for n, tile in enumerate(tiles):
            if n + 1 < len(tiles):
                drive(tiles[n + 1])
            if n > 0:
                ph = tiles[n - 1][0]
                part = readout(tiles[n - 1])
                y[ph] = part if y[ph] is None else y[ph] + part
            scan(tile, reverse)
        ph = tiles[-1][0]
        part = readout(tiles[-1])
        y[ph] = part if y[ph] is None else y[ph] + part
        for h in range(n_split):
            y_ref[0, :, h * kw:(h + 1) * kw] = y[h]

    @pl.when(d == 0)
    def _():
        run(False)

    @pl.when(d == 1)
    def _():
        run(True)


def _s5_scan(u, abar_re, abar_im, b_re, b_im, c_re, c_im, n_batch, n_ctx):
    rows, width = u.shape
    gn = abar_re.shape[-1]
    tm = SSM_CHUNK * n_batch
    n_blk, n_ctx_blk = rows // tm, n_ctx * n_batch // tm
    blk = lambda d, c: _scan_block(d, c, n_ctx_blk, n_blk)
    vec = pl.BlockSpec((1, 1, gn), lambda d, c: (d, 0, 0))
    bmat = pl.BlockSpec((1,) + b_re.shape[1:], lambda d, c: (d, 0, 0, 0))
    cmat = pl.BlockSpec(c_re.shape, lambda d, c: (0, 0, 0))
    return pl.pallas_call(
        _s5_scan_kernel,
        out_shape=jax.ShapeDtypeStruct((2, rows, width), F32),
        grid=(2, n_blk),
        in_specs=[pl.BlockSpec((tm, width), lambda d, c: (blk(d, c), 0)), vec, vec, bmat, bmat, cmat, cmat],
        out_specs=pl.BlockSpec((1, tm, width), lambda d, c: (d, blk(d, c), 0)),
        scratch_shapes=[pltpu.VMEM((tm, gn), F32), pltpu.VMEM((tm, gn), F32),
                        pltpu.VMEM((SUBLANES, gn), F32), pltpu.VMEM((SUBLANES, gn), F32)],
        compiler_params=_cparams("parallel", "arbitrary"),
        name="s5_scan",
    )(u, abar_re, abar_im, b_re, b_im, c_re, c_im)


def _gelu_tanh(x):
    return 0.5 * x * (1.0 + jnp.tanh(math.sqrt(2.0 / math.pi) * (x + 0.044715 * (x * x * x))))


def _s5_readout(y_ref, u, d_skip, glu_w, glu_b):
    y = _gelu_tanh(y_ref[0] + y_ref[1] + d_skip * u)
    return y * _sigmoid(jnp.dot(y.astype(BF16), glu_w, preferred_element_type=F32) + glu_b)


def _merge_kernel(alpha, ysa_ref, g_ref, bonus_ref, gnw_ref, gnb_ref, ones_ref,
                  yb_ref, ysc_ref, u_ref, dskip_ref, gluw_ref, glub_ref,
                  pg_ref, x_ref, gt_ref, pa_ref, pb_ref, pc_ref, wo_ref, lng_ref, lnb_ref, o_ref, yb_rows):
    d = x_ref.shape[1]
    ya = _rwkv_readout(ysa_ref, g_ref[...], bonus_ref[...], gnw_ref[...], gnb_ref[...], ones_ref[...])
    yc = _s5_readout(ysc_ref, u_ref[...], dskip_ref[...], gluw_ref[...], glub_ref[...])
    n_batch, n_slabs, tt = yb_ref.shape[0], yb_ref.shape[1], yb_ref.shape[2]
    for b in range(n_batch):
        for j in range(n_slabs):
            yb_rows[j, pl.ds(b, tt, stride=n_batch), :] = yb_ref[b, j]
    yb = jnp.concatenate([yb_rows[j].astype(BF16) for j in range(n_slabs)], axis=1)
    pg = pg_ref[...]
    merged = (_sigmoid(pg[:, 0:d]) * _mm(ya, pa_ref[...])
              + _sigmoid(pg[:, d:2 * d]) * _mm(yb, pb_ref[...])
              + _sigmoid(pg[:, 2 * d:3 * d]) * _mm(yc, pc_ref[...]))
    mix = _mm(merged, wo_ref[...])
    o_ref[...] = _layer_norm(alpha * x_ref[...] + _rows_mod(mix, gt_ref[0]), lng_ref[...], lnb_ref[...])


def _merge(rwkv, yb, s5, pg, x_all, modtab, proj_a, proj_b, proj_c, w_out, ln_g, ln_b, alpha, n_ctx_rows):
    rows, d = x_all.shape
    n_batch, n_slabs = yb.shape[0], yb.shape[1]
    ysa, g, bonus, gn_w, gn_b, ones_bd = rwkv
    ysc, u, d_skip, glu_w, glu_b = s5
    tm = ROW_TILE
    nctx = n_ctx_rows // tm
    seg = lambda i: (i >= nctx).astype(jnp.int32)
    row = lambda a: pl.BlockSpec((tm, a.shape[1]), lambda i: (i, 0))
    full = lambda a: pl.BlockSpec(a.shape, lambda i: (0, 0), pipeline_mode=pl.Buffered(1))
    vec = lambda a: pl.BlockSpec((1, a.shape[1]), lambda i: (0, 0))
    return pl.pallas_call(
        functools.partial(_merge_kernel, alpha),
        out_shape=jax.ShapeDtypeStruct((rows, d), F32),
        grid=(rows // tm,),
        scratch_shapes=[pltpu.VMEM((n_slabs, tm, LANES), F32)],
        in_specs=[pl.BlockSpec((2, ysa.shape[1], tm, LANES), lambda i: (0, 0, i, 0)),
                  row(g), row(bonus), vec(gn_w), vec(gn_b), full(ones_bd),
                  pl.BlockSpec((n_batch, n_slabs, tm // n_batch, LANES), lambda i: (0, 0, i, 0)),
                  pl.BlockSpec((2, tm, u.shape[1]), lambda i: (0, i, 0)), row(u), vec(d_skip), full(glu_w), vec(glu_b),
                  row(pg), row(x_all),
                  pl.BlockSpec((1, SUBLANES, d), lambda i: (seg(i), 0, 2)),
                  full(proj_a), full(proj_b), full(proj_c), full(w_out), vec(ln_g), vec(ln_b)],
        out_specs=pl.BlockSpec((tm, d), lambda i: (i, 0)),
        compiler_params=_cparams("parallel"),
        name="merge_ln",
    )(ysa, g, bonus, gn_w, gn_b, ones_bd, yb, ysc, u, d_skip, glu_w, glu_b,
      pg, x_all, modtab, proj_a, proj_b, proj_c, w_out, ln_g, ln_b)


def _mlp_kernel(alpha, x_ref, sh_ref, sc_ref, gt_ref, w1_ref, w2_ref, g_ref, b_ref, o_ref, h_ref, acc_ref):
    j = pl.program_id(1)

    @pl.when(j == 0)
    def _():
        h_ref[...] = _rows_add(_rows_mod(x_ref[...], 1.0 + sc_ref[0]), sh_ref[0]).astype(BF16)
        acc_ref[...] = jnp.zeros_like(acc_ref)

    a = jnp.maximum(jnp.dot(h_ref[...], w1_ref[...], preferred_element_type=F32), 0.0)
    acc_ref[...] += jnp.dot((a * a).astype(BF16), w2_ref[...], preferred_element_type=F32)

    @pl.when(j == pl.num_programs(1) - 1)
    def _():
        o_ref[...] = _layer_norm(alpha * x_ref[...] + _rows_mod(acc_ref[...], gt_ref[0]),
                                 g_ref[...], b_ref[...])


def _mlp(x_mid, modtab, w1, w2, ln_g, ln_b, alpha, n_ctx_rows):
    rows, d = x_mid.shape
    ff = w1.shape[1]
    tm = min(MLP_ROW_TILE, n_ctx_rows)
    tf = min(MLP_FF_TILE, ff)
    nctx = n_ctx_rows // tm
    seg = lambda i: (i >= nctx).astype(jnp.int32)
    mod = lambda col: pl.BlockSpec((1, SUBLANES, d), lambda i, j: (seg(i), 0, col))
    vec = pl.BlockSpec((1, d), lambda i, j: (0, 0))
    return pl.pallas_call(
        functools.partial(_mlp_kernel, alpha),
        out_shape=jax.ShapeDtypeStruct((rows, d), F32),
        grid=(rows // tm, ff // tf),
        in_specs=[pl.BlockSpec((tm, d), lambda i, j: (i, 0)), mod(3), mod(4), mod(5),
                  pl.BlockSpec((d, tf), lambda i, j: (0, j)),
                  pl.BlockSpec((tf, d), lambda i, j: (j, 0)), vec, vec],
        out_specs=pl.BlockSpec((tm, d), lambda i, j: (i, 0)),
        scratch_shapes=[pltpu.VMEM((tm, d), BF16), pltpu.VMEM((tm, d), F32)],
        compiler_params=_cparams("parallel", "arbitrary"),
        name="mlp_ln",
    )(x_mid, modtab, modtab, modtab, w1, w2, ln_g, ln_b)


def _block_diag_ones(n, blk):
    i = jnp.arange(n) // blk
    return (i[:, None] == i[None, :]).astype(BF16)


def _rope_tables(n_ctx, n_lat):
    pairs = HEAD_DIM // 4
    rows = n_lat // GRID_W
    row = jnp.repeat(jnp.arange(rows, dtype=F32), GRID_W)
    col = jnp.tile(jnp.arange(GRID_W, dtype=F32), rows)
    inv = ROPE_THETA ** (-jnp.arange(pairs, dtype=F32) / pairs)
    ang = jnp.stack([row, col], axis=-1)[:, :, None] * inv
    ang = jnp.broadcast_to(ang[:, :, None, :], (n_lat, 2, 2, pairs)).reshape(n_lat, HEAD_DIM)
    cos = jnp.concatenate([jnp.ones((n_ctx, HEAD_DIM), F32), jnp.cos(ang)], axis=0)
    sin = jnp.concatenate([jnp.zeros((n_ctx, HEAD_DIM), F32), jnp.sin(ang)], axis=0)
    first = (jnp.arange(HEAD_DIM) % (2 * pairs)) < pairs
    sin_a = jnp.where(first, -sin, 0.0)
    sin_b = jnp.where(first, 0.0, sin)
    two = lambda a: jnp.concatenate([a, a], axis=1)
    return two(cos), two(sin_a), two(sin_b)


def _block_diag(blocks):
    g, a, b = blocks.shape
    eye = jnp.eye(g, dtype=blocks.dtype)
    return (eye[:, None, :, None] * blocks[:, :, None, :]).reshape(g * a, g * b)


def kernel(x, c, ctx, c_ctx, mod_w, mod_b, w_in, rwkv_mu, rwkv_w0, rwkv_w_up, rwkv_a0, rwkv_a_up, rwkv_g_up, rwkv_k_k, rwkv_k_a, rwkv_r_k, rwkv_gn_w, rwkv_gn_b, attn_q_gain, attn_k_gain, ssm_a_re, ssm_a_im, ssm_log_dt, ssm_b_re, ssm_b_im, ssm_c_re, ssm_c_im, ssm_d, ssm_glu_w, ssm_glu_b, proj_a, proj_b, proj_c, w_out, ln1_g, ln1_b, ln2_g, ln2_b, mlp_w1, mlp_w2):
    n_batch, n_lat, d_model = x.shape
    n_ctx = ctx.shape[1]
    depth = mod_w.shape[0]
    assert n_batch == SUBLANES, "row layout packs the batch into one sublane tile"
    alpha = (2 * depth) ** 0.25
    n_ctx_rows = n_ctx * n_batch

    rwkv_w = rwkv_k_k.shape[1]
    q_w = proj_b.shape[1]
    kv_w = q_w // GQA_GROUP
    ssm_w = ssm_d.shape[1]
    rwkv_cols = rwkv_mu.shape[1]
    edges = [0, rwkv_cols, rwkv_cols + q_w, rwkv_cols + q_w + kv_w, rwkv_cols + q_w + 2 * kv_w,
             rwkv_cols + q_w + 2 * kv_w + ssm_w, w_in.shape[2]]
    slabbed = (False, True, True, True, False, False)
    splits = tuple(zip(edges[:-1], edges[1:], slabbed))

    to_rows = lambda a: a.transpose(1, 0, 2).reshape(a.shape[1] * n_batch, d_model)
    x_all = jnp.concatenate([to_rows(ctx), to_rows(x)], axis=0)

    cvec = jnp.concatenate([jnp.broadcast_to(c_ctx[None], (n_batch, d_model)), c], axis=0)
    modtab = _modulation(cvec, mod_w, mod_b).reshape(depth, 2, n_batch, 6 * d_model)

    ones_rwkv = _block_diag_ones(rwkv_w, HEAD_DIM)
    ones_pair = _block_diag_ones(LANES, HEAD_DIM)
    cos, sin_a, sin_b = (jnp.repeat(t, n_batch, axis=0) for t in _rope_tables(n_ctx, n_lat))
    rank = rwkv_w_up.shape[2]
    zpad = jnp.zeros((depth, 2, rank, rwkv_w), F32)

    for l in range(depth):
        mt = modtab[l]
        prm = dict(mu=rwkv_mu[l][None], w0=rwkv_w0[l][:, None], a0=rwkv_a0[l][:, None],
                   w_up=jnp.concatenate([rwkv_w_up[l], zpad[l]], axis=1).astype(BF16),
                   a_up=jnp.concatenate([zpad[l], rwkv_a_up[l]], axis=1).astype(BF16),
                   g_up=rwkv_g_up[l].astype(BF16), k_k=rwkv_k_k[l][None], k_a=rwkv_k_a[l][None],
                   r_k=rwkv_r_k[l][None], ones_bd=ones_rwkv)
        r, v, kk, g, bonus, lw, kd, bv, pq, pk, pv, pc, pg = _in_projection(
            x_all, mt, w_in[l].astype(BF16), splits, prm, n_ctx_rows)
        ysa = _rwkv_scan(r, v, kk, lw, kd, bv, n_batch, n_ctx)
        rwkv = (ysa, g, bonus, rwkv_gn_w[l][None], rwkv_gn_b[l][None], ones_rwkv)

        tile2 = lambda a: jnp.tile(a, 2)[None]
        qh, kh, vh = _attn_prepare(pq, pk, pv, cos, sin_a, sin_b, tile2(attn_q_gain[l]), tile2(attn_k_gain[l]),
                                   ones_pair, n_batch)
        yb = _flash_attention(qh, kh, vh, n_ctx)

        abar_re, abar_im, bf_re, bf_im = _s5_params(ssm_a_re[l], ssm_a_im[l], ssm_log_dt[l], ssm_b_re[l], ssm_b_im[l])
        n_g = ssm_a_re.shape[2]
        n_split = max(1, ssm_w // MXU_DIM)
        gs = n_g // n_split
        diag = lambda blocks: jnp.stack([_block_diag(blocks[h * gs:(h + 1) * gs]) for h in range(n_split)])
        bmat = lambda bf: jnp.stack([diag(bf[d].reshape(SSM_GROUP, n_g, SSM_STATE).transpose(1, 0, 2))
                                     for d in range(2)]).astype(BF16)
        cmat = lambda cc: diag(cc.transpose(0, 2, 1)).astype(BF16)
        ysc = _s5_scan(pc, abar_re, abar_im, bmat(bf_re), bmat(bf_im), cmat(ssm_c_re[l]), cmat(ssm_c_im[l]),
                       n_batch, n_ctx)
        s5 = (ysc, pc, ssm_d[l][None], ssm_glu_w[l].astype(BF16), ssm_glu_b[l][None])

        x_mid = _merge(rwkv, yb, s5, pg, x_all, mt, proj_a[l].astype(BF16), proj_b[l].astype(BF16),
                       proj_c[l].astype(BF16), w_out[l].astype(BF16), ln1_g[l][None], ln1_b[l][None],
                       alpha, n_ctx_rows)
        x_all = _mlp(x_mid, mt, mlp_w1[l].astype(BF16), mlp_w2[l].astype(BF16), ln2_g[l][None], ln2_b[l][None],
                     alpha, n_ctx_rows)

    out = x_all[n_ctx_rows:].reshape(n_lat, n_batch, d_model).transpose(1, 0, 2)
    return out
```

```python
import functools
import math

import jax
import jax.numpy as jnp
from jax import lax
from jax.experimental import pallas as pl
from jax.experimental.pallas import tpu as pltpu

HEAD_DIM = 64
GRID_W = 64
ROPE_THETA = 10000.0
LN_EPS = 1e-5
RMS_EPS = 1e-6
GN_EPS = 64e-5
KK_EPS = 1e-24
LAMBDA_RE_MAX = -1e-4
GQA_GROUP = 4
SSM_GROUP = 16
SSM_STATE = 64

LANES = 128
SUBLANES = 8
MXU_DIM = 256
VMEM_LIMIT = 56 * 1024 * 1024

ROW_TILE = 256
MERGE_ROW_TILE = 512
MLP_ROW_TILE = 1024
MLP_FF_TILE = 1024
SCAN_CHUNK = 64
SCAN_PAIRS = 2
SSM_CHUNK = 64
SSM_LANE_TILE = 512
ATTN_TQ = 256
ATTN_SUB = 256
ATTN_AHEAD = 1
PREP_TT = 128
VT_ROWS = 80

F32 = jnp.float32
BF16 = jnp.bfloat16


def _cparams(*sem):
    return pltpu.CompilerParams(dimension_semantics=sem, vmem_limit_bytes=VMEM_LIMIT)


def _mm(a, b):
    return jnp.dot(a.astype(BF16), b.astype(BF16), preferred_element_type=F32)


def _mm_nt(a, b):
    return lax.dot_general(a.astype(BF16), b.astype(BF16), (((1,), (1,)), ((), ())),
                           preferred_element_type=F32)


def _mm_tn(a, b):
    return lax.dot_general(a.astype(BF16), b.astype(BF16), (((0,), (0,)), ((), ())),
                           preferred_element_type=F32)


def _segsum(x, ones_bd):
    hi = x.astype(BF16)
    lo = (x - hi.astype(F32)).astype(BF16)
    return (jnp.dot(hi, ones_bd, preferred_element_type=F32)
            + jnp.dot(lo, ones_bd, preferred_element_type=F32))


def _sigmoid(x):
    return 1.0 / (1.0 + jnp.exp(-x))


def _softplus(x):
    return jnp.maximum(x, 0.0) + jnp.log(1.0 + jnp.exp(-jnp.abs(x)))


def _layer_norm(z, g, b):
    mu = jnp.mean(z, axis=-1, keepdims=True)
    zc = z - mu
    var = jnp.mean(zc * zc, axis=-1, keepdims=True)
    return zc * lax.rsqrt(var + LN_EPS) * g + b


def _rows_mod(x, m):
    tm, n = x.shape
    return (x.reshape(tm // SUBLANES, SUBLANES, n) * m[None]).reshape(tm, n)


def _rows_add(x, m):
    tm, n = x.shape
    return (x.reshape(tm // SUBLANES, SUBLANES, n) + m[None]).reshape(tm, n)


def _mod_kernel(c_ref, w_ref, b_ref, o_ref):
    c = c_ref[...]
    s = c * _sigmoid(c)
    o_ref[0] = jnp.dot(s, w_ref[0], preferred_element_type=F32,
                       precision=lax.Precision.HIGHEST) + b_ref[0]


def _modulation(cvec, mod_w, mod_b):
    n_layers, d, n6 = mod_w.shape
    tn = 1024 if n6 % 1024 == 0 else n6
    return pl.pallas_call(
        _mod_kernel,
        out_shape=jax.ShapeDtypeStruct((n_layers, cvec.shape[0], n6), F32),
        grid=(n_layers, n6 // tn),
        in_specs=[pl.BlockSpec((cvec.shape[0], d), lambda l, j: (0, 0)),
                  pl.BlockSpec((1, d, tn), lambda l, j: (l, 0, j)),
                  pl.BlockSpec((1, 1, tn), lambda l, j: (l, 0, j))],
        out_specs=pl.BlockSpec((1, cvec.shape[0], tn), lambda l, j: (l, 0, j)),
        compiler_params=_cparams("parallel", "parallel"),
        name="modulation",
    )(cvec, mod_w, mod_b.reshape(n_layers, 1, n6))


def _inproj_kernel(splits, n_ctx_rows, n_rows, x_ref, xp_ref, xn_ref, sh_ref, sc_ref, w_ref,
                   mu_ref, w0_ref, wup_ref, a0_ref, aup_ref, gup_ref, kk_ref, ka_ref, rk_ref, ones_ref, *o_refs):
    i = pl.program_id(0)
    tm = x_ref.shape[0]
    r0 = i * tm
    seg_start = jnp.logical_or(r0 == 0, r0 == n_ctx_rows)
    seg_end = jnp.logical_or(r0 + tm == n_ctx_rows, r0 + tm == n_rows)
    scale, shift = 1.0 + sc_ref[0], sh_ref[0]
    mod = lambda x: _rows_add(_rows_mod(x, scale), shift)
    xm = mod(x_ref[...])
    x_ext = jnp.concatenate([jnp.where(seg_start, 0.0, mod(xp_ref[...])), xm,
                             jnp.where(seg_end, 0.0, mod(xn_ref[...]))], axis=0).astype(BF16)
    a, b, _ = splits[0]
    n_rwkv = 8
    _rwkv_prep(jnp.dot(x_ext, w_ref[:, a:b], preferred_element_type=F32), mu_ref[...], w0_ref, wup_ref,
               a0_ref, aup_ref, gup_ref[...], kk_ref[...], ka_ref[...], rk_ref[...], ones_ref[...],
               o_refs[:n_rwkv])
    xm = xm.astype(BF16)
    for (a, b, slabbed), o_ref in zip(splits[1:], o_refs[n_rwkv:]):
        y = jnp.dot(xm, w_ref[:, a:b], preferred_element_type=F32)
        if slabbed:
            for j in range((b - a) // LANES):
                o_ref[j] = y[:, j * LANES:(j + 1) * LANES]
        else:
            o_ref[...] = y


def _in_projection(x_all, modtab, w_bf16, splits, prm, n_ctx_rows):
    rows, d = x_all.shape
    width = prm["k_k"].shape[-1]
    tm = ROW_TILE
    nctx = n_ctx_rows // tm
    hb = tm // SUBLANES
    last = rows // SUBLANES - 1
    seg = lambda i: (i >= nctx).astype(jnp.int32)
    n_slabs = width // LANES
    slab = pl.BlockSpec((n_slabs, tm, LANES), lambda i: (0, i, 0))
    slab2 = pl.BlockSpec((2, n_slabs, tm, LANES), lambda i: (0, 0, i, 0))
    row = pl.BlockSpec((tm, width), lambda i: (i, 0))
    flat = jax.ShapeDtypeStruct((rows, width), F32)
    one = jax.ShapeDtypeStruct((n_slabs, rows, LANES), F32)
    two = jax.ShapeDtypeStruct((2, n_slabs, rows, LANES), F32)
    shapes = [one, one, one, flat, flat, two, two, two]
    specs = [slab] * 3 + [row] * 2 + [slab2] * 3
    for a, b, slabbed in splits[1:]:
        if slabbed:
            shapes.append(jax.ShapeDtypeStruct(((b - a) // LANES, rows, LANES), F32))
            specs.append(pl.BlockSpec(((b - a) // LANES, tm, LANES), lambda i: (0, i, 0)))
        else:
            shapes.append(jax.ShapeDtypeStruct((rows, b - a), F32))
            specs.append(pl.BlockSpec((tm, b - a), lambda i: (i, 0)))
    consts = [prm["mu"], prm["w0"], prm["w_up"], prm["a0"], prm["a_up"], prm["g_up"],
              prm["k_k"], prm["k_a"], prm["r_k"], prm["ones_bd"]]
    full = lambda a: pl.BlockSpec(a.shape, lambda i: (0,) * a.ndim)
    return pl.pallas_call(
        functools.partial(_inproj_kernel, splits, n_ctx_rows, rows),
        out_shape=shapes,
        grid=(rows // tm,),
        in_specs=[pl.BlockSpec((tm, d), lambda i: (i, 0)),
                  pl.BlockSpec((SUBLANES, d), lambda i: (jnp.maximum(i * hb - 1, 0), 0)),
                  pl.BlockSpec((SUBLANES, d), lambda i: (jnp.minimum((i + 1) * hb, last), 0)),
                  pl.BlockSpec((1, SUBLANES, d), lambda i: (seg(i), 0, 0)),
                  pl.BlockSpec((1, SUBLANES, d), lambda i: (seg(i), 0, 1)),
                  pl.BlockSpec(w_bf16.shape, lambda i: (0, 0), pipeline_mode=pl.Buffered(1))]
                 + [full(a) for a in consts],
        out_specs=specs,
        compiler_params=_cparams("parallel"),
        name="in_projection",
    )(x_all, x_all, x_all, modtab, modtab, w_bf16, *consts)


def _rwkv_prep(p_ext, mu, w0_ref, wup_ref, a0_ref, aup_ref, g_up, k_k, k_a, r_k, ones_bd, outs):
    r_out, v_out, kkn_out, g_out, bonus_out, lw_out, kd_out, bv_out = outs
    tm = p_ext.shape[0] - 2 * SUBLANES
    p = p_ext[SUBLANES:SUBLANES + tm]
    p = p + mu * (0.5 * (p_ext[0:tm] + p_ext[2 * SUBLANES:]) - p)

    w = k_k.shape[-1]
    r, k, v = p[:, 0:w], p[:, w:2 * w], p[:, 2 * w:3 * w]
    wa = p[:, 3 * w:3 * w + LANES]
    gd = p[:, 3 * w + LANES:3 * w + 2 * LANES]

    def put(o_ref, y, *lead):
        for j in range(w // LANES):
            o_ref[lead + (j,)] = y[:, j * LANES:(j + 1) * LANES]

    g_out[...] = _mm(_sigmoid(gd), g_up)
    kk = k * k_k
    kk = kk * lax.rsqrt(jnp.maximum(_segsum(kk * kk, ones_bd), KK_EPS))
    put(r_out, r)
    put(v_out, v)
    put(kkn_out, kk)
    tanh_wa = jnp.tanh(wa)
    rk_sum = jnp.zeros_like(r)
    for d in range(2):
        w_pre = w0_ref[d] + _mm(tanh_wa, wup_ref[d])
        put(lw_out, -jnp.exp(-_softplus(-w_pre) - 0.5), d)
        a = _sigmoid(a0_ref[d] + _mm(wa, aup_ref[d]))
        k_dir = k * (1.0 + (a - 1.0) * k_a)
        put(kd_out, k_dir, d)
        put(bv_out, kk * a, d)
        rk_sum = rk_sum + r * k_dir
    bonus_out[...] = _segsum(rk_sum * r_k, ones_bd) * v


def _scan_block(d, c, n_ctx_blk, n_blk):
    bwd = jnp.where(c < n_ctx_blk, n_ctx_blk - 1 - c, n_ctx_blk + (n_blk - 1 - c))
    return jnp.where(d == 0, c, bwd)


def _tri_inverse(a_list, ri, ci, n):
    eye = (ri == ci).astype(F32)
    same = (ri // SUBLANES) == (ci // SUBLANES)
    xs = [-jnp.where(same, a, 0.0) for a in a_list]
    x2 = [_mm(x, x) for x in xs]
    x4 = [_mm(x, x) for x in x2]
    ts = [eye + x for x in xs]
    ts = [t + _mm(t, x) for t, x in zip(ts, x2)]
    ts = [t + _mm(t, x) for t, x in zip(ts, x4)]
    size = SUBLANES
    while size < n:
        pick = jnp.logical_and((ri // (2 * size)) == (ci // (2 * size)), (ri // size) != (ci // size))
        lows = [_mm(jnp.where(pick, a, 0.0), t) for a, t in zip(a_list, ts)]
        ts = [t - _mm(t, low) for t, low in zip(ts, lows)]
        size *= 2
    return ts


def _rwkv_scan_kernel(n_batch, r_ref, v_ref, kk_ref, lw_ref, kd_ref, bv_ref, y_ref, state_ref):
    d = pl.program_id(0)
    c = pl.program_id(2)
    n = r_ref.shape[1] // n_batch
    m = 2 * n
    chains = [(p, b) for p in range(r_ref.shape[0]) for b in range(n_batch)]
    batches = range(len(chains))

    @pl.when(c == 0)
    def _():
        state_ref[...] = jnp.zeros_like(state_ref)

    sign = 1 - 2 * d
    ri = lax.broadcasted_iota(jnp.int32, (m, m), 0)
    ci = lax.broadcasted_iota(jnp.int32, (m, m), 1)
    same_head = (ri // n) == (ci // n)
    order = (ri - ci) * sign
    strict = jnp.logical_and(same_head, order > 0)
    incl = jnp.logical_and(same_head, order >= 0)
    ti = lax.broadcasted_iota(jnp.int32, (n, n), 0)
    si = lax.broadcasted_iota(jnp.int32, (n, n), 1)
    tri = ((ti - si) * sign >= 0).astype(BF16)
    head0 = lax.broadcasted_iota(jnp.int32, (n, LANES), 1) < HEAD_DIM
    blockdiag = (lax.broadcasted_iota(jnp.int32, (LANES, LANES), 0) // HEAD_DIM
                 == lax.broadcasted_iota(jnp.int32, (LANES, LANES), 1) // HEAD_DIM)

    def stack(x):
        return jnp.concatenate([jnp.where(head0, x, 0.0), jnp.where(head0, 0.0, x)], axis=0)

    def shared(ref, i):
        p, b = chains[i]
        return ref[p, pl.ds(b, n, stride=n_batch), :]

    def split(ref, i):
        p, b = chains[i]
        return ref[0, p, pl.ds(b, n, stride=n_batch), :]

    duos = range(0, len(chains), 2)
    cat = lambda xs, i: jnp.concatenate([xs[i], xs[i + 1]], axis=1)
    uncat = lambda xs: [x[:, o:o + LANES] for x in xs for o in (0, LANES)]

    lw = [split(lw_ref, b) for b in batches]
    hi = [x.astype(BF16) for x in lw]
    lo = [(x - h.astype(F32)).astype(BF16) for x, h in zip(lw, hi)]
    cum = uncat([jnp.dot(tri, cat(hi, i), preferred_element_type=F32)
                 + jnp.dot(tri, cat(lo, i), preferred_element_type=F32) for i in duos])
    e_cum = [jnp.exp(x) for x in cum]
    e_neg = [jnp.exp(-x) for x in cum]
    kap = [shared(kk_ref, b) * jnp.exp(cum[b] - lw[b]) for b in batches]
    rt = [shared(r_ref, b) * e_cum[b] for b in batches]
    bt = [split(bv_ref, b) * e_neg[b] for b in batches]
    kt = [split(kd_ref, b) * e_neg[b] for b in batches]
    v = [shared(v_ref, b) for b in batches]
    total = [jnp.where(d == 0, e[n - 1:n], e[0:1]) for e in e_cum]
    state = [state_ref[b] for b in batches]

    scores = [_mm_nt(jnp.concatenate([stack(kap[b]), stack(rt[b])], axis=0),
                     jnp.concatenate([stack(bt[b]), stack(kt[b])], axis=0)) for b in batches]
    a_b = [jnp.where(strict, s[0:m, 0:m], 0.0) for s in scores]
    a_k = [jnp.where(strict, s[0:m, m:2 * m], 0.0) for s in scores]
    a_r = [jnp.concatenate([jnp.where(incl, s[m:2 * m, m:2 * m], 0.0),
                            jnp.where(incl, -s[m:2 * m, 0:m], 0.0)], axis=1) for s in scores]
    g0 = [_mm_nt(jnp.concatenate([kap[b], rt[b]], axis=0), state[b]) for b in batches]
    t_inv = _tri_inverse(a_b, ri, ci, n)
    vs = [stack(x) for x in v]
    rhs = [stack(g0[b][0:n]) + _mm(a_k[b], vs[b]) for b in batches]
    us = [_mm(t_inv[b], rhs[b]) for b in batches]
    ys = [stack(g0[b][n:m]) + _mm(a_r[b], jnp.concatenate([vs[b], us[b]], axis=0)) for b in batches]
    for i in batches:
        p, b = chains[i]
        y_ref[0, p, pl.ds(b, n, stride=n_batch), :] = ys[i][0:n] + ys[i][n:m]
    ds = [_mm_tn(jnp.concatenate([-(us[b][0:n] + us[b][n:m]), v[b]], axis=0),
                 jnp.concatenate([bt[b], kt[b]], axis=0)) for b in batches]
    for b in batches:
        state_ref[b] = (state[b] + jnp.where(blockdiag, ds[b], 0.0)) * total[b]


def _rwkv_scan(r, v, kk, lw, kd, bv, n_batch, n_ctx):
    n_slabs, rows, _ = r.shape
    tm = SCAN_CHUNK * n_batch
    pp = min(SCAN_PAIRS, n_slabs)
    n_blk, n_ctx_blk = rows // tm, n_ctx * n_batch // tm
    blk = lambda d, c: _scan_block(d, c, n_ctx_blk, n_blk)
    shared = pl.BlockSpec((pp, tm, LANES), lambda d, p, c: (p, blk(d, c), 0))
    split = pl.BlockSpec((1, pp, tm, LANES), lambda d, p, c: (d, p, blk(d, c), 0))
    return pl.pallas_call(
        functools.partial(_rwkv_scan_kernel, n_batch),
        out_shape=jax.ShapeDtypeStruct((2, n_slabs, rows, LANES), F32),
        grid=(2, n_slabs // pp, n_blk),
        in_specs=[shared, shared, shared, split, split, split],
        out_specs=split,
        scratch_shapes=[pltpu.VMEM((pp * n_batch, LANES, LANES), F32)],
        compiler_params=_cparams("parallel", "parallel", "arbitrary"),
        name="rwkv_scan",
    )(r, v, kk, lw, kd, bv)


def _rwkv_readout(y_ref, g, bonus, gn_w, gn_b, ones_bd):
    y = jnp.concatenate([y_ref[0, j] + y_ref[1, j] for j in range(y_ref.shape[1])], axis=1)
    inv = 1.0 / HEAD_DIM
    mean = _segsum(y, ones_bd) * inv
    yc = y - mean
    var = _segsum(yc * yc, ones_bd) * inv
    yn = yc * lax.rsqrt(var + GN_EPS) * gn_w + gn_b
    return (yn + bonus) * g


def _norm_rope(x, gain, cos, sin_a, sin_b, ones_bd):
    ms = _segsum(x * x, ones_bd) * (1.0 / HEAD_DIM)
    xn = x * lax.rsqrt(ms + RMS_EPS) * gain
    return (xn * cos + pltpu.roll(xn, LANES - HEAD_DIM // 4, 1) * sin_a
            + pltpu.roll(xn, HEAD_DIM // 4, 1) * sin_b)


def _attn_prep_kernel(scale, n_batch, q_ref, k_ref, v_ref, cos_ref, sa_ref, sb_ref, qg_ref, kg_ref, ones_ref,
                      qo_ref, ko_ref, vo_ref, scr):
    cos, sin_a, sin_b = cos_ref[...], sa_ref[...], sb_ref[...]
    ones_bd = ones_ref[...]
    tm = cos.shape[0]
    tt = tm // n_batch
    first = lax.broadcasted_iota(jnp.int32, (tm, LANES), 1) < HEAD_DIM
    n_q, n_k = q_ref.shape[0], k_ref.shape[0]
    for j in range(n_q):
        scr[j] = _norm_rope(q_ref[j], qg_ref[...], cos, sin_a, sin_b, ones_bd) * scale
    for j in range(n_k):
        y = _norm_rope(k_ref[j], kg_ref[...], cos, sin_a, sin_b, ones_bd)
        swapped = pltpu.roll(y, HEAD_DIM, 1)
        scr[n_q + 2 * j] = jnp.where(first, y, swapped)
        scr[n_q + 2 * j + 1] = jnp.where(first, swapped, y)
    ones_rows = jnp.ones((vo_ref.shape[3] - HEAD_DIM, LANES), BF16)
    for b in range(n_batch):
        rows = pl.ds(b, tt, stride=n_batch)
        for j in range(n_q):
            qo_ref[b, j] = scr[j, rows, :].astype(BF16)
        for h in range(2 * n_k):
            ko_ref[b, h] = scr[n_q + h, rows, :].astype(BF16)
        for j in range(n_k):
            vb = v_ref[j, rows, :]
            for g in range(tt // LANES):
                vt = vb[g * LANES:(g + 1) * LANES].T.astype(BF16)
                for h in range(2):
                    vo_ref[b, 2 * j + h, g, 0:HEAD_DIM] = vt[h * HEAD_DIM:(h + 1) * HEAD_DIM]
                    vo_ref[b, 2 * j + h, g, HEAD_DIM:] = ones_rows


def _attn_prepare(pq, pk, pv, cos, sin_a, sin_b, q_gain, k_gain, ones_pair, n_batch):
    q_slabs, rows, _ = pq.shape
    k_slabs = pk.shape[0]
    tt_all = rows // n_batch
    tt = PREP_TT
    tm = tt * n_batch
    kh = 2 * k_slabs
    tab = pl.BlockSpec((tm, LANES), lambda i: (i, 0))
    vec = pl.BlockSpec((1, LANES), lambda i: (0, 0))
    return pl.pallas_call(
        functools.partial(_attn_prep_kernel, HEAD_DIM ** -0.5 * math.log2(math.e), n_batch),
        out_shape=[jax.ShapeDtypeStruct((n_batch, q_slabs, tt_all, LANES), BF16),
                   jax.ShapeDtypeStruct((n_batch, kh, tt_all, LANES), BF16),
                   jax.ShapeDtypeStruct((n_batch, kh, tt_all // LANES, VT_ROWS, LANES), BF16)],
        grid=(tt_all // tt,),
        in_specs=[pl.BlockSpec((q_slabs, tm, LANES), lambda i: (0, i, 0)),
                  pl.BlockSpec((k_slabs, tm, LANES), lambda i: (0, i, 0)),
                  pl.BlockSpec((k_slabs, tm, LANES), lambda i: (0, i, 0)),
                  tab, tab, tab, vec, vec,
                  pl.BlockSpec(ones_pair.shape, lambda i: (0, 0))],
        out_specs=[pl.BlockSpec((n_batch, q_slabs, tt, LANES), lambda i: (0, 0, i, 0)),
                   pl.BlockSpec((n_batch, kh, tt, LANES), lambda i: (0, 0, i, 0)),
                   pl.BlockSpec((n_batch, kh, tt // LANES, VT_ROWS, LANES), lambda i: (0, 0, i, 0, 0))],
        scratch_shapes=[pltpu.VMEM((q_slabs + kh, tm, LANES), F32)],
        compiler_params=_cparams("parallel"),
        name="attn_prepare",
    )(pq, pk, pv, cos, sin_a, sin_b, q_gain, k_gain, ones_pair)


def _flash_kernel(n_ctx, sub, q_ref, k_ref, vt_ref, o_ref):
    qi = pl.program_id(2)
    n_pairs, tq = q_ref.shape[1], q_ref.shape[2]
    n_keys = k_ref.shape[2]
    first = lax.broadcasted_iota(jnp.int32, (tq, LANES), 1) < HEAD_DIM
    zero = jnp.zeros((tq, LANES), BF16)
    q = jnp.concatenate([jnp.where(first if h == 0 else jnp.logical_not(first), q_ref[0, p], zero)
                         for p in range(n_pairs) for h in range(2)], axis=0)
    q_t = q.astype(F32).T.astype(BF16)

    def scores(i):
        k = k_ref[0, 0, i * sub:(i + 1) * sub, :]
        return jnp.dot(k, q_t, preferred_element_type=F32)

    def weighted_values(s, m_new, i):
        p = jnp.exp2(s - m_new).astype(BF16)
        g0 = i * sub // LANES
        vt = jnp.concatenate([vt_ref[0, 0, g0 + g] for g in range(sub // LANES)], axis=1)
        return jnp.dot(vt, p, preferred_element_type=F32)

    def attend(n_sub):
        ahead = [scores(i) for i in range(min(ATTN_AHEAD, n_sub))]
        m_prev = acc = None
        for i in range(n_sub):
            s = ahead.pop(0)
            if i + ATTN_AHEAD < n_sub:
                ahead.append(scores(i + ATTN_AHEAD))
            m_new = jnp.max(s, axis=0, keepdims=True)
            if m_prev is None:
                acc = weighted_values(s, m_new, i)
            else:
                m_new = jnp.maximum(m_prev, m_new)
                acc = jnp.exp2(m_prev - m_new) * acc + weighted_values(s, m_new, i)
            m_prev = m_new
        out_t = acc[0:HEAD_DIM] / acc[HEAD_DIM:HEAD_DIM + 1]
        for p in range(n_pairs):
            pair_t = jnp.concatenate([out_t[:, (2 * p) * tq:(2 * p + 1) * tq],
                                      out_t[:, (2 * p + 1) * tq:(2 * p + 2) * tq]], axis=0)
            o_ref[0, p] = pair_t.T

    is_ctx = qi * tq < n_ctx

    @pl.when(is_ctx)
    def _():
        attend(n_ctx // sub)

    @pl.when(jnp.logical_not(is_ctx))
    def _():
        attend(n_keys // sub)


def _flash_attention(qh, kh, vt, n_ctx):
    n_batch, q_slabs, tt_all, _ = qh.shape
    n_kvh = kh.shape[1]
    n_pairs = q_slabs // n_kvh
    tq = min(ATTN_TQ, n_ctx)
    sub = min(ATTN_SUB, n_ctx)
    assert n_ctx % tq == 0 and n_ctx % sub == 0 and tt_all % sub == 0 and sub % LANES == 0 and tq % LANES == 0
    return pl.pallas_call(
        functools.partial(_flash_kernel, n_ctx, sub),
        out_shape=jax.ShapeDtypeStruct((n_batch, q_slabs, tt_all, LANES), F32),
        grid=(n_batch, n_kvh, tt_all // tq),
        in_specs=[pl.BlockSpec((1, n_pairs, tq, LANES), lambda b, h, qi: (b, h, qi, 0)),
                  pl.BlockSpec((1, 1, tt_all, LANES), lambda b, h, qi: (b, h, 0, 0)),
                  pl.BlockSpec((1, 1, tt_all // LANES, VT_ROWS, LANES), lambda b, h, qi: (b, h, 0, 0, 0))],
        out_specs=pl.BlockSpec((1, n_pairs, tq, LANES), lambda b, h, qi: (b, h, qi, 0)),
        compiler_params=_cparams("parallel", "parallel", "parallel"),
        name="flash_attention",
    )(qh, kh, vt)


def _s5_param_kernel(are_ref, aim_ref, ldt_ref, bre_ref, bim_ref, abar_re, abar_im, bf_re, bf_im):
    lam_re = jnp.minimum(are_ref[0], LAMBDA_RE_MAX)
    lam_im = aim_ref[0]
    dt = jnp.exp(ldt_ref[0])
    mag = jnp.exp(lam_re * dt)
    ar, ai = mag * jnp.cos(lam_im * dt), mag * jnp.sin(lam_im * dt)
    nr, ni = ar - 1.0, ai
    den = lam_re * lam_re + lam_im * lam_im
    cr = (nr * lam_re + ni * lam_im) / den
    cim = (ni * lam_re - nr * lam_im) / den
    abar_re[0], abar_im[0] = ar, ai
    bre, bim = bre_ref[...], bim_ref[...]
    bf_re[0] = cr * bre - cim * bim
    bf_im[0] = cr * bim + cim * bre


def _s5_params(a_re, a_im, log_dt, b_re, b_im):
    _, n_g, n_s = a_re.shape
    n_i = b_re.shape[-1]
    gn = n_g * n_s
    flat = lambda a: a.reshape(2, 1, gn)
    dt = jnp.broadcast_to(log_dt[:, :, None], (2, n_g, n_s))
    bt = lambda b: b.reshape(gn, n_i).T
    vec = pl.BlockSpec((1, 1, gn), lambda d: (d, 0, 0))
    mat = pl.BlockSpec((n_i, gn), lambda d: (0, 0))
    omat = pl.BlockSpec((1, n_i, gn), lambda d: (d, 0, 0))
    return pl.pallas_call(
        _s5_param_kernel,
        out_shape=[jax.ShapeDtypeStruct((2, 1, gn), F32)] * 2 + [jax.ShapeDtypeStruct((2, n_i, gn), F32)] * 2,
        grid=(2,),
        in_specs=[vec, vec, vec, mat, mat],
        out_specs=[vec, vec, omat, omat],
        compiler_params=_cparams("parallel"),
        name="s5_params",
    )(flat(a_re), flat(a_im), flat(dt), bt(b_re), bt(b_im))


def _s5_scan_kernel(u_ref, ar_ref, ai_ref, bre_ref, bim_ref, cre_ref, cim_ref, y_ref,
                    xr_ref, xi_ref, hr_ref, hi_ref):
    d = pl.program_id(0)
    c = pl.program_id(1)
    steps = u_ref.shape[0] // SUBLANES
    width = xr_ref.shape[1]

    @pl.when(c == 0)
    def _():
        hr_ref[...] = jnp.zeros_like(hr_ref)
        hi_ref[...] = jnp.zeros_like(hi_ref)

    n_split, kw, sw = bre_ref.shape[1], bre_ref.shape[2], bre_ref.shape[3]
    lt = min(SSM_LANE_TILE, sw)
    tiles = [(h, o) for h in range(n_split) for o in range(0, sw, lt)]
    u = u_ref[...].astype(BF16)

    def drive(tile):
        h, o = tile
        uh = u[:, h * kw:(h + 1) * kw]
        sl = slice(h * sw + o, h * sw + o + lt)
        xr_ref[:, sl] = jnp.dot(uh, bre_ref[0, h, :, o:o + lt], preferred_element_type=F32)
        xi_ref[:, sl] = jnp.dot(uh, bim_ref[0, h, :, o:o + lt], preferred_element_type=F32)

    def scan(tile, reverse):
        h, o = tile
        sl = slice(h * sw + o, h * sw + o + lt)
        ar = jnp.broadcast_to(ar_ref[0, :, sl], (SUBLANES, lt))
        ai = jnp.broadcast_to(ai_ref[0, :, sl], (SUBLANES, lt))
        hr, hi = hr_ref[:, sl], hi_ref[:, sl]
        for i in range(steps):
            t = steps - 1 - i if reverse else i
            rows = slice(t * SUBLANES, (t + 1) * SUBLANES)
            hr, hi = ar * hr - ai * hi + xr_ref[rows, sl], ar * hi + ai * hr + xi_ref[rows, sl]
            xr_ref[rows, sl] = hr
            xi_ref[rows, sl] = hi
        hr_ref[:, sl] = hr
        hi_ref[:, sl] = hi

    def readout(tile):
        h, o = tile
        sl = slice(h * sw + o, h * sw + o + lt)
        return (jnp.dot(xr_ref[:, sl].astype(BF16), cre_ref[h, o:o + lt, :], preferred_element_type=F32)
                - jnp.dot(xi_ref[:, sl].astype(BF16), cim_ref[h, o:o + lt, :], preferred_element_type=F32))

    def run(reverse):
        y = [None] * n_split
        drive(tiles[0])
        for n, tile in enumerate(tiles):
            if n + 1 < len(tiles):
                drive(tiles[n + 1])
            if n > 0:
                ph = tiles[n - 1][0]
                part = readout(tiles[n - 1])
                y[ph] = part if y[ph] is None else y[ph] + part
            scan(tile, reverse)
        ph = tiles[-1][0]
        part = readout(tiles[-1])
        y[ph] = part if y[ph] is None else y[ph] + part
        for h in range(n_split):
            y_ref[0, :, h * kw:(h + 1) * kw] = y[h]

    @pl.when(d == 0)
    def _():
        run(False)

    @pl.when(d == 1)
    def _():
        run(True)


def _s5_scan(u, abar_re, abar_im, b_re, b_im, c_re, c_im, n_batch, n_ctx):
    rows, width = u.shape
    gn = abar_re.shape[-1]
    tm = SSM_CHUNK * n_batch
    n_blk, n_ctx_blk = rows // tm, n_ctx * n_batch // tm
    blk = lambda d, c: _scan_block(d, c, n_ctx_blk, n_blk)
    vec = pl.BlockSpec((1, 1, gn), lambda d, c: (d, 0, 0))
    bmat = pl.BlockSpec((1,) + b_re.shape[1:], lambda d, c: (d, 0, 0, 0))
    cmat = pl.BlockSpec(c_re.shape, lambda d, c: (0, 0, 0))
    return pl.pallas_call(
        _s5_scan_kernel,
        out_shape=jax.ShapeDtypeStruct((2, rows, width), F32),
        grid=(2, n_blk),
        in_specs=[pl.BlockSpec((tm, width), lambda d, c: (blk(d, c), 0)), vec, vec, bmat, bmat, cmat, cmat],
        out_specs=pl.BlockSpec((1, tm, width), lambda d, c: (d, blk(d, c), 0)),
        scratch_shapes=[pltpu.VMEM((tm, gn), F32), pltpu.VMEM((tm, gn), F32),
                        pltpu.VMEM((SUBLANES, gn), F32), pltpu.VMEM((SUBLANES, gn), F32)],
        compiler_params=_cparams("parallel", "arbitrary"),
        name="s5_scan",
    )(u, abar_re, abar_im, b_re, b_im, c_re, c_im)


def _gelu_tanh(x):
    return 0.5 * x * (1.0 + jnp.tanh(math.sqrt(2.0 / math.pi) * (x + 0.044715 * (x * x * x))))


def _s5_readout(y_ref, u, d_skip, glu_w, glu_b):
    y = _gelu_tanh(y_ref[0] + y_ref[1] + d_skip * u)
    return y * _sigmoid(jnp.dot(y.astype(BF16), glu_w, preferred_element_type=F32) + glu_b)


def _merge_kernel(alpha, ysa_ref, g_ref, bonus_ref, gnw_ref, gnb_ref, ones_ref,
                  yb_ref, ysc_ref, u_ref, dskip_ref, gluw_ref, glub_ref,
                  pg_ref, x_ref, gt_ref, pa_ref, pb_ref, pc_ref, wo_ref, lng_ref, lnb_ref, o_ref, yb_rows):
    d = x_ref.shape[1]
    ya = _rwkv_readout(ysa_ref, g_ref[...], bonus_ref[...], gnw_ref[...], gnb_ref[...], ones_ref[...])
    yc = _s5_readout(ysc_ref, u_ref[...], dskip_ref[...], gluw_ref[...], glub_ref[...])
    n_batch, n_slabs, tt = yb_ref.shape[0], yb_ref.shape[1], yb_ref.shape[2]
    for b in range(n_batch):
        for j in range(n_slabs):
            yb_rows[j, pl.ds(b, tt, stride=n_batch), :] = yb_ref[b, j]
    yb = jnp.concatenate([yb_rows[j].astype(BF16) for j in range(n_slabs)], axis=1)
    pg = pg_ref[...]
    merged = (_sigmoid(pg[:, 0:d]) * _mm(ya, pa_ref[...])
              + _sigmoid(pg[:, d:2 * d]) * _mm(yb, pb_ref[...])
              + _sigmoid(pg[:, 2 * d:3 * d]) * _mm(yc, pc_ref[...]))
    mix = _mm(merged, wo_ref[...])
    o_ref[...] = _layer_norm(alpha * x_ref[...] + _rows_mod(mix, gt_ref[0]), lng_ref[...], lnb_ref[...])


def _merge(rwkv, yb, s5, pg, x_all, modtab, proj_a, proj_b, proj_c, w_out, ln_g, ln_b, alpha, n_ctx_rows):
    rows, d = x_all.shape
    n_batch, n_slabs = yb.shape[0], yb.shape[1]
    ysa, g, bonus, gn_w, gn_b, ones_bd = rwkv
    ysc, u, d_skip, glu_w, glu_b = s5
    tm = min(MERGE_ROW_TILE, n_ctx_rows)
    nctx = n_ctx_rows // tm
    seg = lambda i: (i >= nctx).astype(jnp.int32)
    row = lambda a: pl.BlockSpec((tm, a.shape[1]), lambda i: (i, 0))
    full = lambda a: pl.BlockSpec(a.shape, lambda i: (0, 0), pipeline_mode=pl.Buffered(1))
    vec = lambda a: pl.BlockSpec((1, a.shape[1]), lambda i: (0, 0))
    return pl.pallas_call(
        functools.partial(_merge_kernel, alpha),
        out_shape=jax.ShapeDtypeStruct((rows, d), F32),
        grid=(rows // tm,),
        scratch_shapes=[pltpu.VMEM((n_slabs, tm, LANES), F32)],
        in_specs=[pl.BlockSpec((2, ysa.shape[1], tm, LANES), lambda i: (0, 0, i, 0)),
                  row(g), row(bonus), vec(gn_w), vec(gn_b), full(ones_bd),
                  pl.BlockSpec((n_batch, n_slabs, tm // n_batch, LANES), lambda i: (0, 0, i, 0)),
                  pl.BlockSpec((2, tm, u.shape[1]), lambda i: (0, i, 0)), row(u), vec(d_skip), full(glu_w), vec(glu_b),
                  row(pg), row(x_all),
                  pl.BlockSpec((1, SUBLANES, d), lambda i: (seg(i), 0, 2)),
                  full(proj_a), full(proj_b), full(proj_c), full(w_out), vec(ln_g), vec(ln_b)],
        out_specs=pl.BlockSpec((tm, d), lambda i: (i, 0)),
        compiler_params=_cparams("parallel"),
        name="merge_ln",
    )(ysa, g, bonus, gn_w, gn_b, ones_bd, yb, ysc, u, d_skip, glu_w, glu_b,
      pg, x_all, modtab, proj_a, proj_b, proj_c, w_out, ln_g, ln_b)


def _mlp_kernel(alpha, x_ref, sh_ref, sc_ref, gt_ref, w1_ref, w2_ref, g_ref, b_ref, o_ref, h_ref, acc_ref):
    j = pl.program_id(1)

    @pl.when(j == 0)
    def _():
        h_ref[...] = _rows_add(_rows_mod(x_ref[...], 1.0 + sc_ref[0]), sh_ref[0]).astype(BF16)
        acc_ref[...] = jnp.zeros_like(acc_ref)

    a = jnp.maximum(jnp.dot(h_ref[...], w1_ref[...], preferred_element_type=F32), 0.0)
    acc_ref[...] += jnp.dot((a * a).astype(BF16), w2_ref[...], preferred_element_type=F32)

    @pl.when(j == pl.num_programs(1) - 1)
    def _():
        o_ref[...] = _layer_norm(alpha * x_ref[...] + _rows_mod(acc_ref[...], gt_ref[0]),
                                 g_ref[...], b_ref[...])


def _mlp(x_mid, modtab, w1, w2, ln_g, ln_b, alpha, n_ctx_rows):
    rows, d = x_mid.shape
    ff = w1.shape[1]
    tm = min(MLP_ROW_TILE, n_ctx_rows)
    tf = min(MLP_FF_TILE, ff)
    nctx = n_ctx_rows // tm
    seg = lambda i: (i >= nctx).astype(jnp.int32)
    mod = lambda col: pl.BlockSpec((1, SUBLANES, d), lambda i, j: (seg(i), 0, col))
    vec = pl.BlockSpec((1, d), lambda i, j: (0, 0))
    return pl.pallas_call(
        functools.partial(_mlp_kernel, alpha),
        out_shape=jax.ShapeDtypeStruct((rows, d), F32),
        grid=(rows // tm, ff // tf),
        in_specs=[pl.BlockSpec((tm, d), lambda i, j: (i, 0)), mod(3), mod(4), mod(5),
                  pl.BlockSpec((d, tf), lambda i, j: (0, j)),
                  pl.BlockSpec((tf, d), lambda i, j: (j, 0)), vec, vec],
        out_specs=pl.BlockSpec((tm, d), lambda i, j: (i, 0)),
        scratch_shapes=[pltpu.VMEM((tm, d), BF16), pltpu.VMEM((tm, d), F32)],
        compiler_params=_cparams("parallel", "arbitrary"),
        name="mlp_ln",
    )(x_mid, modtab, modtab, modtab, w1, w2, ln_g, ln_b)


def _block_diag_ones(n, blk):
    i = jnp.arange(n) // blk
    return (i[:, None] == i[None, :]).astype(BF16)


def _rope_tables(n_ctx, n_lat):
    pairs = HEAD_DIM // 4
    rows = n_lat // GRID_W
    row = jnp.repeat(jnp.arange(rows, dtype=F32), GRID_W)
    col = jnp.tile(jnp.arange(GRID_W, dtype=F32), rows)
    inv = ROPE_THETA ** (-jnp.arange(pairs, dtype=F32) / pairs)
    ang = jnp.stack([row, col], axis=-1)[:, :, None] * inv
    ang = jnp.broadcast_to(ang[:, :, None, :], (n_lat, 2, 2, pairs)).reshape(n_lat, HEAD_DIM)
    cos = jnp.concatenate([jnp.ones((n_ctx, HEAD_DIM), F32), jnp.cos(ang)], axis=0)
    sin = jnp.concatenate([jnp.zeros((n_ctx, HEAD_DIM), F32), jnp.sin(ang)], axis=0)
    first = (jnp.arange(HEAD_DIM) % (2 * pairs)) < pairs
    sin_a = jnp.where(first, -sin, 0.0)
    sin_b = jnp.where(first, 0.0, sin)
    two = lambda a: jnp.concatenate([a, a], axis=1)
    return two(cos), two(sin_a), two(sin_b)


def _block_diag(blocks):
    g, a, b = blocks.shape
    eye = jnp.eye(g, dtype=blocks.dtype)
    return (eye[:, None, :, None] * blocks[:, :, None, :]).reshape(g * a, g * b)


def kernel(x, c, ctx, c_ctx, mod_w, mod_b, w_in, rwkv_mu, rwkv_w0, rwkv_w_up, rwkv_a0, rwkv_a_up, rwkv_g_up, rwkv_k_k, rwkv_k_a, rwkv_r_k, rwkv_gn_w, rwkv_gn_b, attn_q_gain, attn_k_gain, ssm_a_re, ssm_a_im, ssm_log_dt, ssm_b_re, ssm_b_im, ssm_c_re, ssm_c_im, ssm_d, ssm_glu_w, ssm_glu_b, proj_a, proj_b, proj_c, w_out, ln1_g, ln1_b, ln2_g, ln2_b, mlp_w1, mlp_w2):
    n_batch, n_lat, d_model = x.shape
    n_ctx = ctx.shape[1]
    depth = mod_w.shape[0]
    assert n_batch == SUBLANES, "row layout packs the batch into one sublane tile"
    alpha = (2 * depth) ** 0.25
    n_ctx_rows = n_ctx * n_batch

    rwkv_w = rwkv_k_k.shape[1]
    q_w = proj_b.shape[1]
    kv_w = q_w // GQA_GROUP
    ssm_w = ssm_d.shape[1]
    rwkv_cols = rwkv_mu.shape[1]
    edges = [0, rwkv_cols, rwkv_cols + q_w, rwkv_cols + q_w + kv_w, rwkv_cols + q_w + 2 * kv_w,
             rwkv_cols + q_w + 2 * kv_w + ssm_w, w_in.shape[2]]
    slabbed = (False, True, True, True, False, False)
    splits = tuple(zip(edges[:-1], edges[1:], slabbed))

    to_rows = lambda a: a.transpose(1, 0, 2).reshape(a.shape[1] * n_batch, d_model)
    x_all = jnp.concatenate([to_rows(ctx), to_rows(x)], axis=0)

    cvec = jnp.concatenate([jnp.broadcast_to(c_ctx[None], (n_batch, d_model)), c], axis=0)
    modtab = _modulation(cvec, mod_w, mod_b).reshape(depth, 2, n_batch, 6 * d_model)

    ones_rwkv = _block_diag_ones(rwkv_w, HEAD_DIM)
    ones_pair = _block_diag_ones(LANES, HEAD_DIM)
    cos, sin_a, sin_b = (jnp.repeat(t, n_batch, axis=0) for t in _rope_tables(n_ctx, n_lat))
    rank = rwkv_w_up.shape[2]
    zpad = jnp.zeros((depth, 2, rank, rwkv_w), F32)

    for l in range(depth):
        mt = modtab[l]
        prm = dict(mu=rwkv_mu[l][None], w0=rwkv_w0[l][:, None], a0=rwkv_a0[l][:, None],
                   w_up=jnp.concatenate([rwkv_w_up[l], zpad[l]], axis=1).astype(BF16),
                   a_up=jnp.concatenate([zpad[l], rwkv_a_up[l]], axis=1).astype(BF16),
                   g_up=rwkv_g_up[l].astype(BF16), k_k=rwkv_k_k[l][None], k_a=rwkv_k_a[l][None],
                   r_k=rwkv_r_k[l][None], ones_bd=ones_rwkv)
        r, v, kk, g, bonus, lw, kd, bv, pq, pk, pv, pc, pg = _in_projection(
            x_all, mt, w_in[l].astype(BF16), splits, prm, n_ctx_rows)
        ysa = _rwkv_scan(r, v, kk, lw, kd, bv, n_batch, n_ctx)
        rwkv = (ysa, g, bonus, rwkv_gn_w[l][None], rwkv_gn_b[l][None], ones_rwkv)

        tile2 = lambda a: jnp.tile(a, 2)[None]
        qh, kh, vh = _attn_prepare(pq, pk, pv, cos, sin_a, sin_b, tile2(attn_q_gain[l]), tile2(attn_k_gain[l]),
                                   ones_pair, n_batch)
        yb = _flash_attention(qh, kh, vh, n_ctx)

        abar_re, abar_im, bf_re, bf_im = _s5_params(ssm_a_re[l], ssm_a_im[l], ssm_log_dt[l], ssm_b_re[l], ssm_b_im[l])
        n_g = ssm_a_re.shape[2]
        n_split = max(1, ssm_w // MXU_DIM)
        gs = n_g // n_split
        diag = lambda blocks: jnp.stack([_block_diag(blocks[h * gs:(h + 1) * gs]) for h in range(n_split)])
        bmat = lambda bf: jnp.stack([diag(bf[d].reshape(SSM_GROUP, n_g, SSM_STATE).transpose(1, 0, 2))
                                     for d in range(2)]).astype(BF16)
        cmat = lambda cc: diag(cc.transpose(0, 2, 1)).astype(BF16)
        ysc = _s5_scan(pc, abar_re, abar_im, bmat(bf_re), bmat(bf_im), cmat(ssm_c_re[l]), cmat(ssm_c_im[l]),
                       n_batch, n_ctx)
        s5 = (ysc, pc, ssm_d[l][None], ssm_glu_w[l].astype(BF16), ssm_glu_b[l][None])

        x_mid = _merge(rwkv, yb, s5, pg, x_all, mt, proj_a[l].astype(BF16), proj_b[l].astype(BF16),
                       proj_c[l].astype(BF16), w_out[l].astype(BF16), ln1_g[l][None], ln1_b[l][None],
                       alpha, n_ctx_rows)
        x_all = _mlp(x_mid, mt, mlp_w1[l].astype(BF16), mlp_w2[l].astype(BF16), ln2_g[l][None], ln2_b[l][None],
                     alpha, n_ctx_rows)

    out = x_all[n_ctx_rows:].reshape(n_lat, n_batch, d_model).transpose(1, 0, 2)
    return out
```

```python
import functools
import math

import jax
import jax.numpy as jnp
from jax import lax
from jax.experimental import pallas as pl
from jax.experimental.pallas import tpu as pltpu

HEAD_DIM = 64
GRID_W = 64
ROPE_THETA = 10000.0
LN_EPS = 1e-5
RMS_EPS = 1e-6
GN_EPS = 64e-5
KK_EPS = 1e-24
LAMBDA_RE_MAX = -1e-4
GQA_GROUP = 4
SSM_GROUP = 16
SSM_STATE = 64

LANES = 128
SUBLANES = 8
MXU_DIM = 256
VMEM_LIMIT = 56 * 1024 * 1024

ROW_TILE = 256
MERGE_ROW_TILE = 512
MLP_ROW_TILE = 1024
MLP_FF_TILE = 1024
SCAN_CHUNK = 64
SCAN_PAIRS = 2
SSM_CHUNK = 64
SSM_LANE_TILE = 512
ATTN_TQ = 256
ATTN_SUB = 256
ATTN_AHEAD = 1
PREP_TT = 128
VT_ROWS = 80

F32 = jnp.float32
BF16 = jnp.bfloat16


def _cparams(*sem):
    return pltpu.CompilerParams(dimension_semantics=sem, vmem_limit_bytes=VMEM_LIMIT)


def _mm(a, b):
    return jnp.dot(a.astype(BF16), b.astype(BF16), preferred_element_type=F32)


def _mm_nt(a, b):
    return lax.dot_general(a.astype(BF16), b.astype(BF16), (((1,), (1,)), ((), ())),
                           preferred_element_type=F32)


def _mm_tn(a, b):
    return lax.dot_general(a.astype(BF16), b.astype(BF16), (((0,), (0,)), ((), ())),
                           preferred_element_type=F32)


def _segsum(x, ones_bd):
    w = ones_bd.shape[0]
    hi = x.astype(BF16)
    lo = (x - hi.astype(F32)).astype(BF16)
    parts = [jnp.dot(hi[:, o:o + w], ones_bd, preferred_element_type=F32)
             + jnp.dot(lo[:, o:o + w], ones_bd, preferred_element_type=F32)
             for o in range(0, x.shape[1], w)]
    return parts[0] if len(parts) == 1 else jnp.concatenate(parts, axis=1)


def _sigmoid(x):
    return 1.0 / (1.0 + jnp.exp(-x))


def _softplus(x):
    return jnp.maximum(x, 0.0) + jnp.log(1.0 + jnp.exp(-jnp.abs(x)))


def _layer_norm(z, g, b):
    mu = jnp.mean(z, axis=-1, keepdims=True)
    zc = z - mu
    var = jnp.mean(zc * zc, axis=-1, keepdims=True)
    return zc * lax.rsqrt(var + LN_EPS) * g + b


def _rows_mod(x, m):
    tm, n = x.shape
    return (x.reshape(tm // SUBLANES, SUBLANES, n) * m[None]).reshape(tm, n)


def _rows_add(x, m):
    tm, n = x.shape
    return (x.reshape(tm // SUBLANES, SUBLANES, n) + m[None]).reshape(tm, n)


def _mod_kernel(c_ref, w_ref, b_ref, o_ref):
    c = c_ref[...]
    s = c * _sigmoid(c)
    o_ref[0] = jnp.dot(s, w_ref[0], preferred_element_type=F32,
                       precision=lax.Precision.HIGHEST) + b_ref[0]


def _modulation(cvec, mod_w, mod_b):
    n_layers, d, n6 = mod_w.shape
    tn = 1024 if n6 % 1024 == 0 else n6
    return pl.pallas_call(
        _mod_kernel,
        out_shape=jax.ShapeDtypeStruct((n_layers, cvec.shape[0], n6), F32),
        grid=(n_layers, n6 // tn),
        in_specs=[pl.BlockSpec((cvec.shape[0], d), lambda l, j: (0, 0)),
                  pl.BlockSpec((1, d, tn), lambda l, j: (l, 0, j)),
                  pl.BlockSpec((1, 1, tn), lambda l, j: (l, 0, j))],
        out_specs=pl.BlockSpec((1, cvec.shape[0], tn), lambda l, j: (l, 0, j)),
        compiler_params=_cparams("parallel", "parallel"),
        name="modulation",
    )(cvec, mod_w, mod_b.reshape(n_layers, 1, n6))


def _inproj_kernel(splits, n_ctx_rows, n_rows, x_ref, xp_ref, xn_ref, sh_ref, sc_ref, w_ref,
                   mu_ref, w0_ref, wup_ref, a0_ref, aup_ref, gup_ref, kk_ref, ka_ref, rk_ref, ones_ref, *o_refs):
    i = pl.program_id(0)
    tm = x_ref.shape[0]
    r0 = i * tm
    seg_start = jnp.logical_or(r0 == 0, r0 == n_ctx_rows)
    seg_end = jnp.logical_or(r0 + tm == n_ctx_rows, r0 + tm == n_rows)
    scale, shift = 1.0 + sc_ref[0], sh_ref[0]
    mod = lambda x: _rows_add(_rows_mod(x, scale), shift)
    xm = mod(x_ref[...])
    x_ext = jnp.concatenate([jnp.where(seg_start, 0.0, mod(xp_ref[...])), xm,
                             jnp.where(seg_end, 0.0, mod(xn_ref[...]))], axis=0).astype(BF16)
    a, b, _ = splits[0]
    n_rwkv = 8
    _rwkv_prep(jnp.dot(x_ext, w_ref[:, a:b], preferred_element_type=F32), mu_ref[...], w0_ref, wup_ref,
               a0_ref, aup_ref, gup_ref[...], kk_ref[...], ka_ref[...], rk_ref[...], ones_ref[...],
               o_refs[:n_rwkv])
    xm = xm.astype(BF16)
    for (a, b, slabbed), o_ref in zip(splits[1:], o_refs[n_rwkv:]):
        y = jnp.dot(xm, w_ref[:, a:b], preferred_element_type=F32)
        if slabbed:
            for j in range((b - a) // LANES):
                o_ref[j] = y[:, j * LANES:(j + 1) * LANES]
        else:
            o_ref[...] = y.astype(o_ref.dtype)


def _in_projection(x_all, modtab, w_bf16, splits, prm, n_ctx_rows):
    rows, d = x_all.shape
    width = prm["k_k"].shape[-1]
    tm = ROW_TILE
    nctx = n_ctx_rows // tm
    hb = tm // SUBLANES
    last = rows // SUBLANES - 1
    seg = lambda i: (i >= nctx).astype(jnp.int32)
    n_slabs = width // LANES
    slab = pl.BlockSpec((n_slabs, tm, LANES), lambda i: (0, i, 0))
    slab2 = pl.BlockSpec((2, n_slabs, tm, LANES), lambda i: (0, 0, i, 0))
    row = pl.BlockSpec((tm, width), lambda i: (i, 0))
    flat = jax.ShapeDtypeStruct((rows, width), BF16)
    one = jax.ShapeDtypeStruct((n_slabs, rows, LANES), F32)
    two = jax.ShapeDtypeStruct((2, n_slabs, rows, LANES), F32)
    shapes = [one, one, one, flat, flat, two, two, two]
    specs = [slab] * 3 + [row] * 2 + [slab2] * 3
    for a, b, slabbed in splits[1:]:
        if slabbed:
            shapes.append(jax.ShapeDtypeStruct(((b - a) // LANES, rows, LANES), F32))
            specs.append(pl.BlockSpec(((b - a) // LANES, tm, LANES), lambda i: (0, i, 0)))
        else:
            shapes.append(jax.ShapeDtypeStruct((rows, b - a), BF16))
            specs.append(pl.BlockSpec((tm, b - a), lambda i: (i, 0)))
    consts = [prm["mu"], prm["w0"], prm["w_up"], prm["a0"], prm["a_up"], prm["g_up"],
              prm["k_k"], prm["k_a"], prm["r_k"], prm["ones_bd"]]
    full = lambda a: pl.BlockSpec(a.shape, lambda i: (0,) * a.ndim)
    return pl.pallas_call(
        functools.partial(_inproj_kernel, splits, n_ctx_rows, rows),
        out_shape=shapes,
        grid=(rows // tm,),
        in_specs=[pl.BlockSpec((tm, d), lambda i: (i, 0)),
                  pl.BlockSpec((SUBLANES, d), lambda i: (jnp.maximum(i * hb - 1, 0), 0)),
                  pl.BlockSpec((SUBLANES, d), lambda i: (jnp.minimum((i + 1) * hb, last), 0)),
                  pl.BlockSpec((1, SUBLANES, d), lambda i: (seg(i), 0, 0)),
                  pl.BlockSpec((1, SUBLANES, d), lambda i: (seg(i), 0, 1)),
                  pl.BlockSpec(w_bf16.shape, lambda i: (0, 0), pipeline_mode=pl.Buffered(1))]
                 + [full(a) for a in consts],
        out_specs=specs,
        compiler_params=_cparams("parallel"),
        name="in_projection",
    )(x_all, x_all, x_all, modtab, modtab, w_bf16, *consts)


def _rwkv_prep(p_ext, mu, w0_ref, wup_ref, a0_ref, aup_ref, g_up, k_k, k_a, r_k, ones_bd, outs):
    r_out, v_out, kkn_out, g_out, bonus_out, lw_out, kd_out, bv_out = outs
    tm = p_ext.shape[0] - 2 * SUBLANES
    p = p_ext[SUBLANES:SUBLANES + tm]
    p = p + mu * (0.5 * (p_ext[0:tm] + p_ext[2 * SUBLANES:]) - p)

    w = k_k.shape[-1]
    r, k, v = p[:, 0:w], p[:, w:2 * w], p[:, 2 * w:3 * w]
    wa = p[:, 3 * w:3 * w + LANES]
    gd = p[:, 3 * w + LANES:3 * w + 2 * LANES]

    def put(o_ref, y, *lead):
        for j in range(w // LANES):
            o_ref[lead + (j,)] = y[:, j * LANES:(j + 1) * LANES]

    g_out[...] = _mm(_sigmoid(gd), g_up).astype(g_out.dtype)
    kk = k * k_k
    kk = kk * lax.rsqrt(jnp.maximum(_segsum(kk * kk, ones_bd), KK_EPS))
    put(r_out, r)
    put(v_out, v)
    put(kkn_out, kk)
    tanh_wa = jnp.tanh(wa)
    rk_sum = jnp.zeros_like(r)
    for d in range(2):
        w_pre = w0_ref[d] + _mm(tanh_wa, wup_ref[d])
        put(lw_out, -jnp.exp(-_softplus(-w_pre) - 0.5), d)
        a = _sigmoid(a0_ref[d] + _mm(wa, aup_ref[d]))
        k_dir = k * (1.0 + (a - 1.0) * k_a)
        put(kd_out, k_dir, d)
        put(bv_out, kk * a, d)
        rk_sum = rk_sum + r * k_dir
    bonus_out[...] = (_segsum(rk_sum * r_k, ones_bd) * v).astype(bonus_out.dtype)


def _scan_block(d, c, n_ctx_blk, n_blk):
    bwd = jnp.where(c < n_ctx_blk, n_ctx_blk - 1 - c, n_ctx_blk + (n_blk - 1 - c))
    return jnp.where(d == 0, c, bwd)


def _tri_inverse(a_list, ri, ci, n):
    eye = (ri == ci).astype(F32)
    same = (ri // SUBLANES) == (ci // SUBLANES)
    xs = [-jnp.where(same, a, 0.0) for a in a_list]
    x2 = [_mm(x, x) for x in xs]
    x4 = [_mm(x, x) for x in x2]
    ts = [eye + x for x in xs]
    ts = [t + _mm(t, x) for t, x in zip(ts, x2)]
    ts = [t + _mm(t, x) for t, x in zip(ts, x4)]
    size = SUBLANES
    while size < n:
        pick = jnp.logical_and((ri // (2 * size)) == (ci // (2 * size)), (ri // size) != (ci // size))
        lows = [_mm(jnp.where(pick, a, 0.0), t) for a, t in zip(a_list, ts)]
        ts = [t - _mm(t, low) for t, low in zip(ts, lows)]
        size *= 2
    return ts


def _rwkv_scan_kernel(n_batch, r_ref, v_ref, kk_ref, lw_ref, kd_ref, bv_ref, y_ref, state_ref):
    d = pl.program_id(0)
    c = pl.program_id(2)
    n = r_ref.shape[1] // n_batch
    m = 2 * n
    chains = [(p, b) for p in range(r_ref.shape[0]) for b in range(n_batch)]
    batches = range(len(chains))

    @pl.when(c == 0)
    def _():
        state_ref[...] = jnp.zeros_like(state_ref)

    sign = 1 - 2 * d
    ri = lax.broadcasted_iota(jnp.int32, (m, m), 0)
    ci = lax.broadcasted_iota(jnp.int32, (m, m), 1)
    same_head = (ri // n) == (ci // n)
    order = (ri - ci) * sign
    strict = jnp.logical_and(same_head, order > 0)
    incl = jnp.logical_and(same_head, order >= 0)
    ti = lax.broadcasted_iota(jnp.int32, (n, n), 0)
    si = lax.broadcasted_iota(jnp.int32, (n, n), 1)
    tri = ((ti - si) * sign >= 0).astype(BF16)
    head0 = lax.broadcasted_iota(jnp.int32, (n, LANES), 1) < HEAD_DIM
    blockdiag = (lax.broadcasted_iota(jnp.int32, (LANES, LANES), 0) // HEAD_DIM
                 == lax.broadcasted_iota(jnp.int32, (LANES, LANES), 1) // HEAD_DIM)

    def stack(x):
        return jnp.concatenate([jnp.where(head0, x, 0.0), jnp.where(head0, 0.0, x)], axis=0)

    def shared(ref, i):
        p, b = chains[i]
        return ref[p, pl.ds(b, n, stride=n_batch), :]

    def split(ref, i):
        p, b = chains[i]
        return ref[0, p, pl.ds(b, n, stride=n_batch), :]

    duos = range(0, len(chains), 2)
    cat = lambda xs, i: jnp.concatenate([xs[i], xs[i + 1]], axis=1)
    uncat = lambda xs: [x[:, o:o + LANES] for x in xs for o in (0, LANES)]

    lw = [split(lw_ref, b) for b in batches]
    hi = [x.astype(BF16) for x in lw]
    lo = [(x - h.astype(F32)).astype(BF16) for x, h in zip(lw, hi)]
    cum = uncat([jnp.dot(tri, cat(hi, i), preferred_element_type=F32)
                 + jnp.dot(tri, cat(lo, i), preferred_element_type=F32) for i in duos])
    e_cum = [jnp.exp(x) for x in cum]
    e_neg = [jnp.exp(-x) for x in cum]
    kap = [shared(kk_ref, b) * jnp.exp(cum[b] - lw[b]) for b in batches]
    rt = [shared(r_ref, b) * e_cum[b] for b in batches]
    bt = [split(bv_ref, b) * e_neg[b] for b in batches]
    kt = [split(kd_ref, b) * e_neg[b] for b in batches]
    v = [shared(v_ref, b) for b in batches]
    total = [jnp.where(d == 0, e[n - 1:n], e[0:1]) for e in e_cum]
    state = [state_ref[b] for b in batches]

    scores = [_mm_nt(jnp.concatenate([stack(kap[b]), stack(rt[b])], axis=0),
                     jnp.concatenate([stack(bt[b]), stack(kt[b])], axis=0)) for b in batches]
    a_b = [jnp.where(strict, s[0:m, 0:m], 0.0) for s in scores]
    a_k = [jnp.where(strict, s[0:m, m:2 * m], 0.0) for s in scores]
    a_r = [jnp.concatenate([jnp.where(incl, s[m:2 * m, m:2 * m], 0.0),
                            jnp.where(incl, -s[m:2 * m, 0:m], 0.0)], axis=1) for s in scores]
    g0 = [_mm_nt(jnp.concatenate([kap[b], rt[b]], axis=0), state[b]) for b in batches]
    t_inv = _tri_inverse(a_b, ri, ci, n)
    vs = [stack(x) for x in v]
    rhs = [stack(g0[b][0:n]) + _mm(a_k[b], vs[b]) for b in batches]
    us = [_mm(t_inv[b], rhs[b]) for b in batches]
    ys = [stack(g0[b][n:m]) + _mm(a_r[b], jnp.concatenate([vs[b], us[b]], axis=0)) for b in batches]
    for i in batches:
        p, b = chains[i]
        y_ref[0, p, pl.ds(b, n, stride=n_batch), :] = ys[i][0:n] + ys[i][n:m]
    ds = [_mm_tn(jnp.concatenate([-(us[b][0:n] + us[b][n:m]), v[b]], axis=0),
                 jnp.concatenate([bt[b], kt[b]], axis=0)) for b in batches]
    for b in batches:
        state_ref[b] = (state[b] + jnp.where(blockdiag, ds[b], 0.0)) * total[b]


def _rwkv_scan(r, v, kk, lw, kd, bv, n_batch, n_ctx):
    n_slabs, rows, _ = r.shape
    tm = SCAN_CHUNK * n_batch
    pp = min(SCAN_PAIRS, n_slabs)
    n_blk, n_ctx_blk = rows // tm, n_ctx * n_batch // tm
    blk = lambda d, c: _scan_block(d, c, n_ctx_blk, n_blk)
    shared = pl.BlockSpec((pp, tm, LANES), lambda d, p, c: (p, blk(d, c), 0))
    split = pl.BlockSpec((1, pp, tm, LANES), lambda d, p, c: (d, p, blk(d, c), 0))
    return pl.pallas_call(
        functools.partial(_rwkv_scan_kernel, n_batch),
        out_shape=jax.ShapeDtypeStruct((2, n_slabs, rows, LANES), F32),
        grid=(2, n_slabs // pp, n_blk),
        in_specs=[shared, shared, shared, split, split, split],
        out_specs=split,
        scratch_shapes=[pltpu.VMEM((pp * n_batch, LANES, LANES), F32)],
        compiler_params=_cparams("parallel", "parallel", "arbitrary"),
        name="rwkv_scan",
    )(r, v, kk, lw, kd, bv)


def _rwkv_readout(y_ref, g, bonus, gn_w, gn_b, ones_bd):
    y = jnp.concatenate([y_ref[0, j] + y_ref[1, j] for j in range(y_ref.shape[1])], axis=1)
    inv = 1.0 / HEAD_DIM
    mean = _segsum(y, ones_bd) * inv
    yc = y - mean
    var = _segsum(yc * yc, ones_bd) * inv
    yn = yc * lax.rsqrt(var + GN_EPS) * gn_w + gn_b
    return (yn + bonus) * g


def _norm_rope(x, gain, cos, sin_a, sin_b, ones_bd):
    ms = _segsum(x * x, ones_bd) * (1.0 / HEAD_DIM)
    xn = x * lax.rsqrt(ms + RMS_EPS) * gain
    return (xn * cos + pltpu.roll(xn, LANES - HEAD_DIM // 4, 1) * sin_a
            + pltpu.roll(xn, HEAD_DIM // 4, 1) * sin_b)


def _attn_prep_kernel(scale, n_batch, q_ref, k_ref, v_ref, cos_ref, sa_ref, sb_ref, qg_ref, kg_ref, ones_ref,
                      qo_ref, ko_ref, vo_ref, scr):
    cos, sin_a, sin_b = cos_ref[...], sa_ref[...], sb_ref[...]
    ones_bd = ones_ref[...]
    tm = cos.shape[0]
    tt = tm // n_batch
    first = lax.broadcasted_iota(jnp.int32, (tm, LANES), 1) < HEAD_DIM
    n_q, n_k = q_ref.shape[0], k_ref.shape[0]
    for j in range(n_q):
        scr[j] = _norm_rope(q_ref[j], qg_ref[...], cos, sin_a, sin_b, ones_bd) * scale
    for j in range(n_k):
        y = _norm_rope(k_ref[j], kg_ref[...], cos, sin_a, sin_b, ones_bd)
        swapped = pltpu.roll(y, HEAD_DIM, 1)
        scr[n_q + 2 * j] = jnp.where(first, y, swapped)
        scr[n_q + 2 * j + 1] = jnp.where(first, swapped, y)
    ones_rows = jnp.ones((vo_ref.shape[3] - HEAD_DIM, LANES), BF16)
    for b in range(n_batch):
        rows = pl.ds(b, tt, stride=n_batch)
        for j in range(n_q):
            qo_ref[b, j] = scr[j, rows, :].astype(BF16)
        for h in range(2 * n_k):
            ko_ref[b, h] = scr[n_q + h, rows, :].astype(BF16)
        for j in range(n_k):
            vb = v_ref[j, rows, :]
            for g in range(tt // LANES):
                vt = vb[g * LANES:(g + 1) * LANES].T.astype(BF16)
                for h in range(2):
                    vo_ref[b, 2 * j + h, g, 0:HEAD_DIM] = vt[h * HEAD_DIM:(h + 1) * HEAD_DIM]
                    vo_ref[b, 2 * j + h, g, HEAD_DIM:] = ones_rows


def _attn_prepare(pq, pk, pv, cos, sin_a, sin_b, q_gain, k_gain, ones_pair, n_batch):
    q_slabs, rows, _ = pq.shape
    k_slabs = pk.shape[0]
    tt_all = rows // n_batch
    tt = PREP_TT
    tm = tt * n_batch
    kh = 2 * k_slabs
    tab = pl.BlockSpec((tm, LANES), lambda i: (i, 0))
    vec = pl.BlockSpec((1, LANES), lambda i: (0, 0))
    return pl.pallas_call(
        functools.partial(_attn_prep_kernel, HEAD_DIM ** -0.5 * math.log2(math.e), n_batch),
        out_shape=[jax.ShapeDtypeStruct((n_batch, q_slabs, tt_all, LANES), BF16),
                   jax.ShapeDtypeStruct((n_batch, kh, tt_all, LANES), BF16),
                   jax.ShapeDtypeStruct((n_batch, kh, tt_all // LANES, VT_ROWS, LANES), BF16)],
        grid=(tt_all // tt,),
        in_specs=[pl.BlockSpec((q_slabs, tm, LANES), lambda i: (0, i, 0)),
                  pl.BlockSpec((k_slabs, tm, LANES), lambda i: (0, i, 0)),
                  pl.BlockSpec((k_slabs, tm, LANES), lambda i: (0, i, 0)),
                  tab, tab, tab, vec, vec,
                  pl.BlockSpec(ones_pair.shape, lambda i: (0, 0))],
        out_specs=[pl.BlockSpec((n_batch, q_slabs, tt, LANES), lambda i: (0, 0, i, 0)),
                   pl.BlockSpec((n_batch, kh, tt, LANES), lambda i: (0, 0, i, 0)),
                   pl.BlockSpec((n_batch, kh, tt // LANES, VT_ROWS, LANES), lambda i: (0, 0, i, 0, 0))],
        scratch_shapes=[pltpu.VMEM((q_slabs + kh, tm, LANES), F32)],
        compiler_params=_cparams("parallel"),
        name="attn_prepare",
    )(pq, pk, pv, cos, sin_a, sin_b, q_gain, k_gain, ones_pair)


def _flash_kernel(n_ctx, sub_ctx, sub_lat, q_ref, k_ref, vt_ref, o_ref):
    qi = pl.program_id(2)
    n_pairs, tq = q_ref.shape[1], q_ref.shape[2]
    n_keys = k_ref.shape[2]
    first = lax.broadcasted_iota(jnp.int32, (tq, LANES), 1) < HEAD_DIM
    zero = jnp.zeros((tq, LANES), BF16)
    q = jnp.concatenate([jnp.where(first if h == 0 else jnp.logical_not(first), q_ref[0, p], zero)
                         for p in range(n_pairs) for h in range(2)], axis=0)
    q_t = q.astype(F32).T.astype(BF16)

    def scores(blk):
        start, size = blk
        return jnp.dot(k_ref[0, 0, start:start + size, :], q_t, preferred_element_type=F32)

    def weighted_values(s, m_new, blk):
        start, size = blk
        p = jnp.exp2(s - m_new).astype(BF16)
        vt = jnp.concatenate([vt_ref[0, 0, start // LANES + g] for g in range(size // LANES)], axis=1)
        return jnp.dot(vt, p, preferred_element_type=F32)

    def attend(blocks):
        ahead = [scores(b) for b in blocks[:ATTN_AHEAD]]
        m_prev = acc = None
        for i, blk in enumerate(blocks):
            s = ahead.pop(0)
            if i + ATTN_AHEAD < len(blocks):
                ahead.append(scores(blocks[i + ATTN_AHEAD]))
            m_new = jnp.max(s, axis=0, keepdims=True)
            if m_prev is None:
                acc = weighted_values(s, m_new, blk)
            else:
                m_new = jnp.maximum(m_prev, m_new)
                acc = jnp.exp2(m_prev - m_new) * acc + weighted_values(s, m_new, blk)
            m_prev = m_new
        out_t = acc[0:HEAD_DIM] / acc[HEAD_DIM:HEAD_DIM + 1]
        for p in range(n_pairs):
            pair_t = jnp.concatenate([out_t[:, (2 * p) * tq:(2 * p + 1) * tq],
                                      out_t[:, (2 * p + 1) * tq:(2 * p + 2) * tq]], axis=0)
            o_ref[0, p] = pair_t.T.astype(o_ref.dtype)

    is_ctx = qi * tq < n_ctx
    ctx_blocks = [(s0, sub_ctx) for s0 in range(0, n_ctx, sub_ctx)]
    lat_blocks = [(s0, sub_lat) for s0 in range(n_ctx, n_keys, sub_lat)]

    @pl.when(is_ctx)
    def _():
        attend(ctx_blocks)

    @pl.when(jnp.logical_not(is_ctx))
    def _():
        attend(ctx_blocks + lat_blocks)


def _flash_attention(qh, kh, vt, n_ctx):
    n_batch, q_slabs, tt_all, _ = qh.shape
    n_kvh = kh.shape[1]
    n_pairs = q_slabs // n_kvh
    tq = min(ATTN_TQ, n_ctx)
    sub_ctx = min(ATTN_SUB, n_ctx)
    sub_lat = min(ATTN_SUB, tt_all - n_ctx)
    assert n_ctx % tq == 0 and n_ctx % sub_ctx == 0 and (tt_all - n_ctx) % sub_lat == 0
    assert sub_ctx % LANES == 0 and sub_lat % LANES == 0 and tq % LANES == 0
    return pl.pallas_call(
        functools.partial(_flash_kernel, n_ctx, sub_ctx, sub_lat),
        out_shape=jax.ShapeDtypeStruct((n_batch, q_slabs, tt_all, LANES), BF16),
        grid=(n_batch, n_kvh, tt_all // tq),
        in_specs=[pl.BlockSpec((1, n_pairs, tq, LANES), lambda b, h, qi: (b, h, qi, 0)),
                  pl.BlockSpec((1, 1, tt_all, LANES), lambda b, h, qi: (b, h, 0, 0)),
                  pl.BlockSpec((1, 1, tt_all // LANES, VT_ROWS, LANES), lambda b, h, qi: (b, h, 0, 0, 0))],
        out_specs=pl.BlockSpec((1, n_pairs, tq, LANES), lambda b, h, qi: (b, h, qi, 0)),
        compiler_params=_cparams("parallel", "parallel", "parallel"),
        name="flash_attention",
    )(qh, kh, vt)


def _s5_param_kernel(are_ref, aim_ref, ldt_ref, bre_ref, bim_ref, abar_re, abar_im, bf_re, bf_im):
    lam_re = jnp.minimum(are_ref[0], LAMBDA_RE_MAX)
    lam_im = aim_ref[0]
    dt = jnp.exp(ldt_ref[0])
    mag = jnp.exp(lam_re * dt)
    ar, ai = mag * jnp.cos(lam_im * dt), mag * jnp.sin(lam_im * dt)
    nr, ni = ar - 1.0, ai
    den = lam_re * lam_re + lam_im * lam_im
    cr = (nr * lam_re + ni * lam_im) / den
    cim = (ni * lam_re - nr * lam_im) / den
    abar_re[0], abar_im[0] = ar, ai
    bre, bim = bre_ref[...], bim_ref[...]
    bf_re[0] = cr * bre - cim * bim
    bf_im[0] = cr * bim + cim * bre


def _s5_params(a_re, a_im, log_dt, b_re, b_im):
    _, n_g, n_s = a_re.shape
    n_i = b_re.shape[-1]
    gn = n_g * n_s
    flat = lambda a: a.reshape(2, 1, gn)
    dt = jnp.broadcast_to(log_dt[:, :, None], (2, n_g, n_s))
    bt = lambda b: b.reshape(gn, n_i).T
    vec = pl.BlockSpec((1, 1, gn), lambda d: (d, 0, 0))
    mat = pl.BlockSpec((n_i, gn), lambda d: (0, 0))
    omat = pl.BlockSpec((1, n_i, gn), lambda d: (d, 0, 0))
    return pl.pallas_call(
        _s5_param_kernel,
        out_shape=[jax.ShapeDtypeStruct((2, 1, gn), F32)] * 2 + [jax.ShapeDtypeStruct((2, n_i, gn), F32)] * 2,
        grid=(2,),
        in_specs=[vec, vec, vec, mat, mat],
        out_specs=[vec, vec, omat, omat],
        compiler_params=_cparams("parallel"),
        name="s5_params",
    )(flat(a_re), flat(a_im), flat(dt), bt(b_re), bt(b_im))


def _s5_scan_kernel(u_ref, ar_ref, ai_ref, bre_ref, bim_ref, cre_ref, cim_ref, y_ref,
                    xr_ref, xi_ref, hr_ref, hi_ref):
    d = pl.program_id(0)
    c = pl.program_id(1)
    steps = u_ref.shape[0] // SUBLANES
    width = xr_ref.shape[1]

    @pl.when(c == 0)
    def _():
        hr_ref[...] = jnp.zeros_like(hr_ref)
        hi_ref[...] = jnp.zeros_like(hi_ref)

    n_split, kw, sw = bre_ref.shape[1], bre_ref.shape[2], bre_ref.shape[3]
    lt = min(SSM_LANE_TILE, sw)
    tiles = [(h, o) for h in range(n_split) for o in range(0, sw, lt)]
    u = u_ref[...].astype(BF16)

    def drive(tile):
        h, o = tile
        uh = u[:, h * kw:(h + 1) * kw]
        sl = slice(h * sw + o, h * sw + o + lt)
        xr_ref[:, sl] = jnp.dot(uh, bre_ref[0, h, :, o:o + lt], preferred_element_type=F32)
        xi_ref[:, sl] = jnp.dot(uh, bim_ref[0, h, :, o:o + lt], preferred_element_type=F32)

    def scan(tile, reverse):
        h, o = tile
        sl = slice(h * sw + o, h * sw + o + lt)
        ar = jnp.broadcast_to(ar_ref[0, :, sl], (SUBLANES, lt))
        ai = jnp.broadcast_to(ai_ref[0, :, sl], (SUBLANES, lt))
        hr, hi = hr_ref[:, sl], hi_ref[:, sl]
        for i in range(steps):
            t = steps - 1 - i if reverse else i
            rows = slice(t * SUBLANES, (t + 1) * SUBLANES)
            hr, hi = ar * hr - ai * hi + xr_ref[rows, sl], ar * hi + ai * hr + xi_ref[rows, sl]
            xr_ref[rows, sl] = hr
            xi_ref[rows, sl] = hi
        hr_ref[:, sl] = hr
        hi_ref[:, sl] = hi

    def readout(tile):
        h, o = tile
        sl = slice(h * sw + o, h * sw + o + lt)
        return (jnp.dot(xr_ref[:, sl].astype(BF16), cre_ref[h, o:o + lt, :], preferred_element_type=F32)
                - jnp.dot(xi_ref[:, sl].astype(BF16), cim_ref[h, o:o + lt, :], preferred_element_type=F32))

    def run(reverse):
        y = [None] * n_split
        drive(tiles[0])
        for n, tile in enumerate(tiles):
            if n + 1 < len(tiles):
                drive(tiles[n + 1])
            if n > 0:
                ph = tiles[n - 1][0]
                part = readout(tiles[n - 1])
                y[ph] = part if y[ph] is None else y[ph] + part
            scan(tile, reverse)
        ph = tiles[-1][0]
        part = readout(tiles[-1])
        y[ph] = part if y[ph] is None else y[ph] + part
        for h in range(n_split):
            y_ref[0, :, h * kw:(h + 1) * kw] = y[h].astype(y_ref.dtype)

    @pl.when(d == 0)
    def _():
        run(False)

    @pl.when(d == 1)
    def _():
        run(True)


def _s5_scan(u, abar_re, abar_im, b_re, b_im, c_re, c_im, n_batch, n_ctx):
    rows, width = u.shape
    gn = abar_re.shape[-1]
    tm = SSM_CHUNK * n_batch
    n_blk, n_ctx_blk = rows // tm, n_ctx * n_batch // tm
    blk = lambda d, c: _scan_block(d, c, n_ctx_blk, n_blk)
    vec = pl.BlockSpec((1, 1, gn), lambda d, c: (d, 0, 0))
    bmat = pl.BlockSpec((1,) + b_re.shape[1:], lambda d, c: (d, 0, 0, 0))
    cmat = pl.BlockSpec(c_re.shape, lambda d, c: (0, 0, 0))
    return pl.pallas_call(
        _s5_scan_kernel,
        out_shape=jax.ShapeDtypeStruct((2, rows, width), BF16),
        grid=(2, n_blk),
        in_specs=[pl.BlockSpec((tm, width), lambda d, c: (blk(d, c), 0)), vec, vec, bmat, bmat, cmat, cmat],
        out_specs=pl.BlockSpec((1, tm, width), lambda d, c: (d, blk(d, c), 0)),
        scratch_shapes=[pltpu.VMEM((tm, gn), F32), pltpu.VMEM((tm, gn), F32),
                        pltpu.VMEM((SUBLANES, gn), F32), pltpu.VMEM((SUBLANES, gn), F32)],
        compiler_params=_cparams("parallel", "arbitrary"),
        name="s5_scan",
    )(u, abar_re, abar_im, b_re, b_im, c_re, c_im)


def _gelu_tanh(x):
    return 0.5 * x * (1.0 + jnp.tanh(math.sqrt(2.0 / math.pi) * (x + 0.044715 * (x * x * x))))


def _s5_readout(y_ref, u, d_skip, glu_w, glu_b):
    y = _gelu_tanh(y_ref[0].astype(F32) + y_ref[1].astype(F32) + d_skip * u)
    return y * _sigmoid(jnp.dot(y.astype(BF16), glu_w, preferred_element_type=F32) + glu_b)


def _merge_kernel(alpha, ysa_ref, g_ref, bonus_ref, gnw_ref, gnb_ref, ones_ref,
                  yb_ref, ysc_ref, u_ref, dskip_ref, gluw_ref, glub_ref,
                  pg_ref, x_ref, gt_ref, pa_ref, pb_ref, pc_ref, wo_ref, lng_ref, lnb_ref, o_ref, yb_rows):
    d = x_ref.shape[1]
    ya = _rwkv_readout(ysa_ref, g_ref[...].astype(F32), bonus_ref[...].astype(F32), gnw_ref[...], gnb_ref[...],
                       ones_ref[...])
    yc = _s5_readout(ysc_ref, u_ref[...].astype(F32), dskip_ref[...], gluw_ref[...], glub_ref[...])
    n_batch, n_slabs, tt = yb_ref.shape[0], yb_ref.shape[1], yb_ref.shape[2]
    for b in range(n_batch):
        for j in range(n_slabs):
            yb_rows[j, pl.ds(b, tt, stride=n_batch), :] = yb_ref[b, j].astype(F32)
    yb = jnp.concatenate([yb_rows[j].astype(BF16) for j in range(n_slabs)], axis=1)
    pg = pg_ref[...].astype(F32)
    merged = (_sigmoid(pg[:, 0:d]) * _mm(ya, pa_ref[...])
              + _sigmoid(pg[:, d:2 * d]) * _mm(yb, pb_ref[...])
              + _sigmoid(pg[:, 2 * d:3 * d]) * _mm(yc, pc_ref[...]))
    mix = _mm(merged, wo_ref[...])
    o_ref[...] = _layer_norm(alpha * x_ref[...] + _rows_mod(mix, gt_ref[0]), lng_ref[...], lnb_ref[...])


def _merge(rwkv, yb, s5, pg, x_all, modtab, proj_a, proj_b, proj_c, w_out, ln_g, ln_b, alpha, n_ctx_rows):
    rows, d = x_all.shape
    n_batch, n_slabs = yb.shape[0], yb.shape[1]
    ysa, g, bonus, gn_w, gn_b, ones_bd = rwkv
    ysc, u, d_skip, glu_w, glu_b = s5
    tm = min(MERGE_ROW_TILE, n_ctx_rows)
    nctx = n_ctx_rows // tm
    seg = lambda i: (i >= nctx).astype(jnp.int32)
    row = lambda a: pl.BlockSpec((tm, a.shape[1]), lambda i: (i, 0))
    full = lambda a: pl.BlockSpec(a.shape, lambda i: (0, 0), pipeline_mode=pl.Buffered(1))
    vec = lambda a: pl.BlockSpec((1, a.shape[1]), lambda i: (0, 0))
    return pl.pallas_call(
        functools.partial(_merge_kernel, alpha),
        out_shape=jax.ShapeDtypeStruct((rows, d), F32),
        grid=(rows // tm,),
        scratch_shapes=[pltpu.VMEM((n_slabs, tm, LANES), F32)],
        in_specs=[pl.BlockSpec((2, ysa.shape[1], tm, LANES), lambda i: (0, 0, i, 0)),
                  row(g), row(bonus), vec(gn_w), vec(gn_b), full(ones_bd),
                  pl.BlockSpec((n_batch, n_slabs, tm // n_batch, LANES), lambda i: (0, 0, i, 0)),
                  pl.BlockSpec((2, tm, u.shape[1]), lambda i: (0, i, 0)), row(u), vec(d_skip), full(glu_w), vec(glu_b),
                  row(pg), row(x_all),
                  pl.BlockSpec((1, SUBLANES, d), lambda i: (seg(i), 0, 2)),
                  full(proj_a), full(proj_b), full(proj_c), full(w_out), vec(ln_g), vec(ln_b)],
        out_specs=pl.BlockSpec((tm, d), lambda i: (i, 0)),
        compiler_params=_cparams("parallel"),
        name="merge_ln",
    )(ysa, g, bonus, gn_w, gn_b, ones_bd, yb, ysc, u, d_skip, glu_w, glu_b,
      pg, x_all, modtab, proj_a, proj_b, proj_c, w_out, ln_g, ln_b)


def _mlp_kernel(alpha, x_ref, sh_ref, sc_ref, gt_ref, w1_ref, w2_ref, g_ref, b_ref, o_ref, h_ref, acc_ref):
    j = pl.program_id(1)

    @pl.when(j == 0)
    def _():
        h_ref[...] = _rows_add(_rows_mod(x_ref[...], 1.0 + sc_ref[0]), sh_ref[0]).astype(BF16)
        acc_ref[...] = jnp.zeros_like(acc_ref)

    a = jnp.maximum(jnp.dot(h_ref[...], w1_ref[...], preferred_element_type=F32), 0.0)
    acc_ref[...] += jnp.dot((a * a).astype(BF16), w2_ref[...], preferred_element_type=F32)

    @pl.when(j == pl.num_programs(1) - 1)
    def _():
        o_ref[...] = _layer_norm(alpha * x_ref[...] + _rows_mod(acc_ref[...], gt_ref[0]),
                                 g_ref[...], b_ref[...])


def _mlp(x_mid, modtab, w1, w2, ln_g, ln_b, alpha, n_ctx_rows):
    rows, d = x_mid.shape
    ff = w1.shape[1]
    tm = min(MLP_ROW_TILE, n_ctx_rows)
    tf = min(MLP_FF_TILE, ff)
    nctx = n_ctx_rows // tm
    seg = lambda i: (i >= nctx).astype(jnp.int32)
    mod = lambda col: pl.BlockSpec((1, SUBLANES, d), lambda i, j: (seg(i), 0, col))
    vec = pl.BlockSpec((1, d), lambda i, j: (0, 0))
    return pl.pallas_call(
        functools.partial(_mlp_kernel, alpha),
        out_shape=jax.ShapeDtypeStruct((rows, d), F32),
        grid=(rows // tm, ff // tf),
        in_specs=[pl.BlockSpec((tm, d), lambda i, j: (i, 0)), mod(3), mod(4), mod(5),
                  pl.BlockSpec((d, tf), lambda i, j: (0, j)),
                  pl.BlockSpec((tf, d), lambda i, j: (j, 0)), vec, vec],
        out_specs=pl.BlockSpec((tm, d), lambda i, j: (i, 0)),
        scratch_shapes=[pltpu.VMEM((tm, d), BF16), pltpu.VMEM((tm, d), F32)],
        compiler_params=_cparams("parallel", "arbitrary"),
        name="mlp_ln",
    )(x_mid, modtab, modtab, modtab, w1, w2, ln_g, ln_b)


def _block_diag_ones(n, blk):
    i = jnp.arange(n) // blk
    return (i[:, None] == i[None, :]).astype(BF16)


def _rope_tables(n_ctx, n_lat):
    pairs = HEAD_DIM // 4
    rows = n_lat // GRID_W
    row = jnp.repeat(jnp.arange(rows, dtype=F32), GRID_W)
    col = jnp.tile(jnp.arange(GRID_W, dtype=F32), rows)
    inv = ROPE_THETA ** (-jnp.arange(pairs, dtype=F32) / pairs)
    ang = jnp.stack([row, col], axis=-1)[:, :, None] * inv
    ang = jnp.broadcast_to(ang[:, :, None, :], (n_lat, 2, 2, pairs)).reshape(n_lat, HEAD_DIM)
    cos = jnp.concatenate([jnp.ones((n_ctx, HEAD_DIM), F32), jnp.cos(ang)], axis=0)
    sin = jnp.concatenate([jnp.zeros((n_ctx, HEAD_DIM), F32), jnp.sin(ang)], axis=0)
    first = (jnp.arange(HEAD_DIM) % (2 * pairs)) < pairs
    sin_a = jnp.where(first, -sin, 0.0)
    sin_b = jnp.where(first, 0.0, sin)
    two = lambda a: jnp.concatenate([a, a], axis=1)
    return two(cos), two(sin_a), two(sin_b)


def _block_diag(blocks):
    g, a, b = blocks.shape
    eye = jnp.eye(g, dtype=blocks.dtype)
    return (eye[:, None, :, None] * blocks[:, :, None, :]).reshape(g * a, g * b)


def kernel(x, c, ctx, c_ctx, mod_w, mod_b, w_in, rwkv_mu, rwkv_w0, rwkv_w_up, rwkv_a0, rwkv_a_up, rwkv_g_up, rwkv_k_k, rwkv_k_a, rwkv_r_k, rwkv_gn_w, rwkv_gn_b, attn_q_gain, attn_k_gain, ssm_a_re, ssm_a_im, ssm_log_dt, ssm_b_re, ssm_b_im, ssm_c_re, ssm_c_im, ssm_d, ssm_glu_w, ssm_glu_b, proj_a, proj_b, proj_c, w_out, ln1_g, ln1_b, ln2_g, ln2_b, mlp_w1, mlp_w2):
    n_batch, n_lat, d_model = x.shape
    n_ctx = ctx.shape[1]
    depth = mod_w.shape[0]
    assert n_batch == SUBLANES, "row layout packs the batch into one sublane tile"
    alpha = (2 * depth) ** 0.25
    n_ctx_rows = n_ctx * n_batch

    rwkv_w = rwkv_k_k.shape[1]
    q_w = proj_b.shape[1]
    kv_w = q_w // GQA_GROUP
    ssm_w = ssm_d.shape[1]
    rwkv_cols = rwkv_mu.shape[1]
    edges = [0, rwkv_cols, rwkv_cols + q_w, rwkv_cols + q_w + kv_w, rwkv_cols + q_w + 2 * kv_w,
             rwkv_cols + q_w + 2 * kv_w + ssm_w, w_in.shape[2]]
    slabbed = (False, True, True, True, False, False)
    splits = tuple(zip(edges[:-1], edges[1:], slabbed))

    x_all = jnp.concatenate([ctx, x], axis=1).transpose(1, 0, 2).reshape((n_ctx + n_lat) * n_batch, d_model)

    cvec = jnp.concatenate([jnp.broadcast_to(c_ctx[None], (n_batch, d_model)), c], axis=0)
    modtab = _modulation(cvec, mod_w, mod_b).reshape(depth, 2, n_batch, 6 * d_model)

    ones_rwkv = _block_diag_ones(min(MXU_DIM, rwkv_w), HEAD_DIM)
    ones_pair = _block_diag_ones(LANES, HEAD_DIM)
    cos, sin_a, sin_b = (jnp.repeat(t, n_batch, axis=0) for t in _rope_tables(n_ctx, n_lat))
    rank = rwkv_w_up.shape[2]
    zpad = jnp.zeros((depth, 2, rank, rwkv_w), F32)

    for l in range(depth):
        mt = modtab[l]
        prm = dict(mu=rwkv_mu[l][None], w0=rwkv_w0[l][:, None], a0=rwkv_a0[l][:, None],
                   w_up=jnp.concatenate([rwkv_w_up[l], zpad[l]], axis=1).astype(BF16),
                   a_up=jnp.concatenate([zpad[l], rwkv_a_up[l]], axis=1).astype(BF16),
                   g_up=rwkv_g_up[l].astype(BF16), k_k=rwkv_k_k[l][None], k_a=rwkv_k_a[l][None],
                   r_k=rwkv_r_k[l][None], ones_bd=ones_rwkv)
        r, v, kk, g, bonus, lw, kd, bv, pq, pk, pv, pc, pg = _in_projection(
            x_all, mt, w_in[l].astype(BF16), splits, prm, n_ctx_rows)
        ysa = _rwkv_scan(r, v, kk, lw, kd, bv, n_batch, n_ctx)
        rwkv = (ysa, g, bonus, rwkv_gn_w[l][None], rwkv_gn_b[l][None], ones_rwkv)

        tile2 = lambda a: jnp.tile(a, 2)[None]
        qh, kh, vh = _attn_prepare(pq, pk, pv, cos, sin_a, sin_b, tile2(attn_q_gain[l]), tile2(attn_k_gain[l]),
                                   ones_pair, n_batch)
        yb = _flash_attention(qh, kh, vh, n_ctx)

        abar_re, abar_im, bf_re, bf_im = _s5_params(ssm_a_re[l], ssm_a_im[l], ssm_log_dt[l], ssm_b_re[l], ssm_b_im[l])
        n_g = ssm_a_re.shape[2]
        n_split = max(1, ssm_w // MXU_DIM)
        gs = n_g // n_split
        diag = lambda blocks: jnp.stack([_block_diag(blocks[h * gs:(h + 1) * gs]) for h in range(n_split)])
        bmat = lambda bf: jnp.stack([diag(bf[d].reshape(SSM_GROUP, n_g, SSM_STATE).transpose(1, 0, 2))
                                     for d in range(2)]).astype(BF16)
        cmat = lambda cc: diag(cc.transpose(0, 2, 1)).astype(BF16)
        ysc = _s5_scan(pc, abar_re, abar_im, bmat(bf_re), bmat(bf_im), cmat(ssm_c_re[l]), cmat(ssm_c_im[l]),
                       n_batch, n_ctx)
        s5 = (ysc, pc, ssm_d[l][None], ssm_glu_w[l].astype(BF16), ssm_glu_b[l][None])

        x_mid = _merge(rwkv, yb, s5, pg, x_all, mt, proj_a[l].astype(BF16), proj_b[l].astype(BF16),
                       proj_c[l].astype(BF16), w_out[l].astype(BF16), ln1_g[l][None], ln1_b[l][None],
                       alpha, n_ctx_rows)
        x_all = _mlp(x_mid, mt, mlp_w1[l].astype(BF16), mlp_w2[l].astype(BF16), ln2_g[l][None], ln2_b[l][None],
                     alpha, n_ctx_rows)

    out = x_all[n_ctx_rows:].reshape(n_lat, n_batch, d_model).transpose(1, 0, 2)
    return out
```

```python
import functools
import math

import jax
import jax.numpy as jnp
from jax import lax
from jax.experimental import pallas as pl
from jax.experimental.pallas import tpu as pltpu

HEAD_DIM = 64
GRID_W = 64
ROPE_THETA = 10000.0
LN_EPS = 1e-5
RMS_EPS = 1e-6
GN_EPS = 64e-5
KK_EPS = 1e-24
LAMBDA_RE_MAX = -1e-4
GQA_GROUP = 4
SSM_GROUP = 16
SSM_STATE = 64

LANES = 128
SUBLANES = 8
MXU_DIM = 256
VMEM_LIMIT = 56 * 1024 * 1024

ROW_TILE = 256
MERGE_ROW_TILE = 512
LAYOUT_TT = 64
MLP_ROW_TILE = 1024
MLP_FF_TILE = 1024
SCAN_CHUNK = 64
SCAN_PAIRS = 2
SSM_CHUNK = 64
SSM_LANE_TILE = 512
ATTN_TQ = 256
ATTN_SUB = 256
ATTN_AHEAD = 1
PREP_TT = 128
VT_ROWS = 80

F32 = jnp.float32
BF16 = jnp.bfloat16


def _cparams(*sem):
    return pltpu.CompilerParams(dimension_semantics=sem, vmem_limit_bytes=VMEM_LIMIT)


def _mm(a, b):
    return jnp.dot(a.astype(BF16), b.astype(BF16), preferred_element_type=F32)


def _mm_nt(a, b):
    return lax.dot_general(a.astype(BF16), b.astype(BF16), (((1,), (1,)), ((), ())),
                           preferred_element_type=F32)


def _mm_tn(a, b):
    return lax.dot_general(a.astype(BF16), b.astype(BF16), (((0,), (0,)), ((), ())),
                           preferred_element_type=F32)


def _segsum(x, ones_bd):
    w = ones_bd.shape[0]
    hi = x.astype(BF16)
    lo = (x - hi.astype(F32)).astype(BF16)
    parts = [jnp.dot(hi[:, o:o + w], ones_bd, preferred_element_type=F32)
             + jnp.dot(lo[:, o:o + w], ones_bd, preferred_element_type=F32)
             for o in range(0, x.shape[1], w)]
    return parts[0] if len(parts) == 1 else jnp.concatenate(parts, axis=1)


def _sigmoid(x):
    return 1.0 / (1.0 + jnp.exp(-x))


def _softplus(x):
    return jnp.maximum(x, 0.0) + jnp.log(1.0 + jnp.exp(-jnp.abs(x)))


def _layer_norm(z, g, b):
    mu = jnp.mean(z, axis=-1, keepdims=True)
    zc = z - mu
    var = jnp.mean(zc * zc, axis=-1, keepdims=True)
    return zc * lax.rsqrt(var + LN_EPS) * g + b


def _rows_mod(x, m):
    tm, n = x.shape
    return (x.reshape(tm // SUBLANES, SUBLANES, n) * m[None]).reshape(tm, n)


def _rows_add(x, m):
    tm, n = x.shape
    return (x.reshape(tm // SUBLANES, SUBLANES, n) + m[None]).reshape(tm, n)


def _to_rows_kernel(n_ctx_tiles, ctx_ref, x_ref, o_ref, rows_ref):
    i = pl.program_id(0)
    n_batch, tt, d = x_ref.shape

    def interleave(src_ref):
        for b in range(n_batch):
            for s in range(d // LANES):
                rows_ref[s, pl.ds(b, tt, stride=n_batch), :] = src_ref[b, :, s * LANES:(s + 1) * LANES]
        for s in range(d // LANES):
            o_ref[:, s * LANES:(s + 1) * LANES] = rows_ref[s]

    @pl.when(i < n_ctx_tiles)
    def _():
        interleave(ctx_ref)

    @pl.when(i >= n_ctx_tiles)
    def _():
        interleave(x_ref)


def _to_rows(ctx, x):
    n_batch, n_ctx, d = ctx.shape
    n_lat = x.shape[1]
    tt = math.gcd(LAYOUT_TT, n_ctx)
    n_ctx_tiles = n_ctx // tt
    return pl.pallas_call(
        functools.partial(_to_rows_kernel, n_ctx_tiles),
        out_shape=jax.ShapeDtypeStruct(((n_ctx + n_lat) * n_batch, d), F32),
        grid=((n_ctx + n_lat) // tt,),
        in_specs=[pl.BlockSpec((n_batch, tt, d), lambda i: (0, jnp.minimum(i, n_ctx_tiles - 1), 0)),
                  pl.BlockSpec((n_batch, tt, d), lambda i: (0, jnp.maximum(i - n_ctx_tiles, 0), 0))],
        out_specs=pl.BlockSpec((tt * n_batch, d), lambda i: (i, 0)),
        scratch_shapes=[pltpu.VMEM((d // LANES, tt * n_batch, LANES), F32)],
        compiler_params=_cparams("parallel"),
        name="to_rows",
    )(ctx, x)


def _mod_kernel(c_ref, w_ref, b_ref, o_ref):
    c = c_ref[...]
    s = c * _sigmoid(c)
    o_ref[0] = jnp.dot(s, w_ref[0], preferred_element_type=F32,
                       precision=lax.Precision.HIGHEST) + b_ref[0]


def _modulation(cvec, mod_w, mod_b):
    n_layers, d, n6 = mod_w.shape
    tn = 1024 if n6 % 1024 == 0 else n6
    return pl.pallas_call(
        _mod_kernel,
        out_shape=jax.ShapeDtypeStruct((n_layers, cvec.shape[0], n6), F32),
        grid=(n_layers, n6 // tn),
        in_specs=[pl.BlockSpec((cvec.shape[0], d), lambda l, j: (0, 0)),
                  pl.BlockSpec((1, d, tn), lambda l, j: (l, 0, j)),
                  pl.BlockSpec((1, 1, tn), lambda l, j: (l, 0, j))],
        out_specs=pl.BlockSpec((1, cvec.shape[0], tn), lambda l, j: (l, 0, j)),
        compiler_params=_cparams("parallel", "parallel"),
        name="modulation",
    )(cvec, mod_w, mod_b.reshape(n_layers, 1, n6))


def _inproj_kernel(splits, n_ctx_rows, n_rows, x_ref, xp_ref, xn_ref, sh_ref, sc_ref, w_ref,
                   mu_ref, w0_ref, wup_ref, a0_ref, aup_ref, gup_ref, kk_ref, ka_ref, rk_ref, ones_ref, *o_refs):
    i = pl.program_id(0)
    tm = x_ref.shape[0]
    r0 = i * tm
    seg_start = jnp.logical_or(r0 == 0, r0 == n_ctx_rows)
    seg_end = jnp.logical_or(r0 + tm == n_ctx_rows, r0 + tm == n_rows)
    scale, shift = 1.0 + sc_ref[0], sh_ref[0]
    mod = lambda x: _rows_add(_rows_mod(x, scale), shift)
    xm = mod(x_ref[...])
    x_ext = jnp.concatenate([jnp.where(seg_start, 0.0, mod(xp_ref[...])), xm,
                             jnp.where(seg_end, 0.0, mod(xn_ref[...]))], axis=0).astype(BF16)
    a, b, _ = splits[0]
    n_rwkv = 8
    _rwkv_prep(jnp.dot(x_ext, w_ref[:, a:b], preferred_element_type=F32), mu_ref[...], w0_ref, wup_ref,
               a0_ref, aup_ref, gup_ref[...], kk_ref[...], ka_ref[...], rk_ref[...], ones_ref[...],
               o_refs[:n_rwkv])
    xm = xm.astype(BF16)
    for (a, b, slabbed), o_ref in zip(splits[1:], o_refs[n_rwkv:]):
        y = jnp.dot(xm, w_ref[:, a:b], preferred_element_type=F32)
        if slabbed:
            for j in range((b - a) // LANES):
                o_ref[j] = y[:, j * LANES:(j + 1) * LANES]
        else:
            o_ref[...] = y.astype(o_ref.dtype)


def _in_projection(x_all, modtab, w_bf16, splits, prm, n_ctx_rows):
    rows, d = x_all.shape
    width = prm["k_k"].shape[-1]
    tm = ROW_TILE
    nctx = n_ctx_rows // tm
    hb = tm // SUBLANES
    last = rows // SUBLANES - 1
    seg = lambda i: (i >= nctx).astype(jnp.int32)
    n_slabs = width // LANES
    slab = pl.BlockSpec((n_slabs, tm, LANES), lambda i: (0, i, 0))
    slab2 = pl.BlockSpec((2, n_slabs, tm, LANES), lambda i: (0, 0, i, 0))
    row = pl.BlockSpec((tm, width), lambda i: (i, 0))
    flat = jax.ShapeDtypeStruct((rows, width), BF16)
    one = jax.ShapeDtypeStruct((n_slabs, rows, LANES), F32)
    two = jax.ShapeDtypeStruct((2, n_slabs, rows, LANES), F32)
    shapes = [one, one, one, flat, flat, two, two, two]
    specs = [slab] * 3 + [row] * 2 + [slab2] * 3
    for a, b, slabbed in splits[1:]:
        if slabbed:
            shapes.append(jax.ShapeDtypeStruct(((b - a) // LANES, rows, LANES), F32))
            specs.append(pl.BlockSpec(((b - a) // LANES, tm, LANES), lambda i: (0, i, 0)))
        else:
            shapes.append(jax.ShapeDtypeStruct((rows, b - a), BF16))
            specs.append(pl.BlockSpec((tm, b - a), lambda i: (i, 0)))
    consts = [prm["mu"], prm["w0"], prm["w_up"], prm["a0"], prm["a_up"], prm["g_up"],
              prm["k_k"], prm["k_a"], prm["r_k"], prm["ones_bd"]]
    full = lambda a: pl.BlockSpec(a.shape, lambda i: (0,) * a.ndim)
    return pl.pallas_call(
        functools.partial(_inproj_kernel, splits, n_ctx_rows, rows),
        out_shape=shapes,
        grid=(rows // tm,),
        in_specs=[pl.BlockSpec((tm, d), lambda i: (i, 0)),
                  pl.BlockSpec((SUBLANES, d), lambda i: (jnp.maximum(i * hb - 1, 0), 0)),
                  pl.BlockSpec((SUBLANES, d), lambda i: (jnp.minimum((i + 1) * hb, last), 0)),
                  pl.BlockSpec((1, SUBLANES, d), lambda i: (seg(i), 0, 0)),
                  pl.BlockSpec((1, SUBLANES, d), lambda i: (seg(i), 0, 1)),
                  pl.BlockSpec(w_bf16.shape, lambda i: (0, 0), pipeline_mode=pl.Buffered(1))]
                 + [full(a) for a in consts],
        out_specs=specs,
        compiler_params=_cparams("parallel"),
        name="in_projection",
    )(x_all, x_all, x_all, modtab, modtab, w_bf16, *consts)


def _rwkv_prep(p_ext, mu, w0_ref, wup_ref, a0_ref, aup_ref, g_up, k_k, k_a, r_k, ones_bd, outs):
    r_out, v_out, kkn_out, g_out, bonus_out, lw_out, kd_out, bv_out = outs
    tm = p_ext.shape[0] - 2 * SUBLANES
    p = p_ext[SUBLANES:SUBLANES + tm]
    p = p + mu * (0.5 * (p_ext[0:tm] + p_ext[2 * SUBLANES:]) - p)

    w = k_k.shape[-1]
    r, k, v = p[:, 0:w], p[:, w:2 * w], p[:, 2 * w:3 * w]
    wa = p[:, 3 * w:3 * w + LANES]
    gd = p[:, 3 * w + LANES:3 * w + 2 * LANES]

    def put(o_ref, y, *lead):
        for j in range(w // LANES):
            o_ref[lead + (j,)] = y[:, j * LANES:(j + 1) * LANES]

    g_out[...] = _mm(_sigmoid(gd), g_up).astype(g_out.dtype)
    kk = k * k_k
    kk = kk * lax.rsqrt(jnp.maximum(_segsum(kk * kk, ones_bd), KK_EPS))
    put(r_out, r)
    put(v_out, v)
    put(kkn_out, kk)
    tanh_wa = jnp.tanh(wa)
    rk_sum = jnp.zeros_like(r)
    for d in range(2):
        w_pre = w0_ref[d] + _mm(tanh_wa, wup_ref[d])
        put(lw_out, -jnp.exp(-_softplus(-w_pre) - 0.5), d)
        a = _sigmoid(a0_ref[d] + _mm(wa, aup_ref[d]))
        k_dir = k * (1.0 + (a - 1.0) * k_a)
        put(kd_out, k_dir, d)
        put(bv_out, kk * a, d)
        rk_sum = rk_sum + r * k_dir
    bonus_out[...] = (_segsum(rk_sum * r_k, ones_bd) * v).astype(bonus_out.dtype)


def _scan_block(d, c, n_ctx_blk, n_blk):
    bwd = jnp.where(c < n_ctx_blk, n_ctx_blk - 1 - c, n_ctx_blk + (n_blk - 1 - c))
    return jnp.where(d == 0, c, bwd)


def _tri_inverse(a_list, ri, ci, n):
    eye = (ri == ci).astype(F32)
    same = (ri // SUBLANES) == (ci // SUBLANES)
    xs = [-jnp.where(same, a, 0.0) for a in a_list]
    x2 = [_mm(x, x) for x in xs]
    x4 = [_mm(x, x) for x in x2]
    ts = [eye + x for x in xs]
    ts = [t + _mm(t, x) for t, x in zip(ts, x2)]
    ts = [t + _mm(t, x) for t, x in zip(ts, x4)]
    size = SUBLANES
    while size < n:
        pick = jnp.logical_and((ri // (2 * size)) == (ci // (2 * size)), (ri // size) != (ci // size))
        lows = [_mm(jnp.where(pick, a, 0.0), t) for a, t in zip(a_list, ts)]
        ts = [t - _mm(t, low) for t, low in zip(ts, lows)]
        size *= 2
    return ts


def _rwkv_scan_kernel(n_batch, r_ref, v_ref, kk_ref, lw_ref, kd_ref, bv_ref, y_ref, state_ref):
    d = pl.program_id(0)
    c = pl.program_id(2)
    n = r_ref.shape[1] // n_batch
    m = 2 * n
    chains = [(p, b) for p in range(r_ref.shape[0]) for b in range(n_batch)]
    batches = range(len(chains))

    @pl.when(c == 0)
    def _():
        state_ref[...] = jnp.zeros_like(state_ref)

    sign = 1 - 2 * d
    ri = lax.broadcasted_iota(jnp.int32, (m, m), 0)
    ci = lax.broadcasted_iota(jnp.int32, (m, m), 1)
    same_head = (ri // n) == (ci // n)
    order = (ri - ci) * sign
    strict = jnp.logical_and(same_head, order > 0)
    incl = jnp.logical_and(same_head, order >= 0)
    ti = lax.broadcasted_iota(jnp.int32, (n, n), 0)
    si = lax.broadcasted_iota(jnp.int32, (n, n), 1)
    tri = ((ti - si) * sign >= 0).astype(BF16)
    head0 = lax.broadcasted_iota(jnp.int32, (n, LANES), 1) < HEAD_DIM
    blockdiag = (lax.broadcasted_iota(jnp.int32, (LANES, LANES), 0) // HEAD_DIM
                 == lax.broadcasted_iota(jnp.int32, (LANES, LANES), 1) // HEAD_DIM)

    def stack(x):
        return jnp.concatenate([jnp.where(head0, x, 0.0), jnp.where(head0, 0.0, x)], axis=0)

    def shared(ref, i):
        p, b = chains[i]
        return ref[p, pl.ds(b, n, stride=n_batch), :]

    def split(ref, i):
        p, b = chains[i]
        return ref[0, p, pl.ds(b, n, stride=n_batch), :]

    duos = range(0, len(chains), 2)
    cat = lambda xs, i: jnp.concatenate([xs[i], xs[i + 1]], axis=1)
    uncat = lambda xs: [x[:, o:o + LANES] for x in xs for o in (0, LANES)]

    lw = [split(lw_ref, b) for b in batches]
    hi = [x.astype(BF16) for x in lw]
    lo = [(x - h.astype(F32)).astype(BF16) for x, h in zip(lw, hi)]
    cum = uncat([jnp.dot(tri, cat(hi, i), preferred_element_type=F32)
                 + jnp.dot(tri, cat(lo, i), preferred_element_type=F32) for i in duos])
    e_cum = [jnp.exp(x) for x in cum]
    e_neg = [jnp.exp(-x) for x in cum]
    kap = [shared(kk_ref, b) * jnp.exp(cum[b] - lw[b]) for b in batches]
    rt = [shared(r_ref, b) * e_cum[b] for b in batches]
    bt = [split(bv_ref, b) * e_neg[b] for b in batches]
    kt = [split(kd_ref, b) * e_neg[b] for b in batches]
    v = [shared(v_ref, b) for b in batches]
    total = [jnp.where(d == 0, e[n - 1:n], e[0:1]) for e in e_cum]
    state = [state_ref[b] for b in batches]

    scores = [_mm_nt(jnp.concatenate([stack(kap[b]), stack(rt[b])], axis=0),
                     jnp.concatenate([stack(bt[b]), stack(kt[b])], axis=0)) for b in batches]
    a_b = [jnp.where(strict, s[0:m, 0:m], 0.0) for s in scores]
    a_k = [jnp.where(strict, s[0:m, m:2 * m], 0.0) for s in scores]
    a_r = [jnp.concatenate([jnp.where(incl, s[m:2 * m, m:2 * m], 0.0),
                            jnp.where(incl, -s[m:2 * m, 0:m], 0.0)], axis=1) for s in scores]
    g0 = [_mm_nt(jnp.concatenate([kap[b], rt[b]], axis=0), state[b]) for b in batches]
    t_inv = _tri_inverse(a_b, ri, ci, n)
    vs = [stack(x) for x in v]
    rhs = [stack(g0[b][0:n]) + _mm(a_k[b], vs[b]) for b in batches]
    us = [_mm(t_inv[b], rhs[b]) for b in batches]
    ys = [stack(g0[b][n:m]) + _mm(a_r[b], jnp.concatenate([vs[b], us[b]], axis=0)) for b in batches]
    for i in batches:
        p, b = chains[i]
        y_ref[0, p, pl.ds(b, n, stride=n_batch), :] = ys[i][0:n] + ys[i][n:m]
    ds = [_mm_tn(jnp.concatenate([-(us[b][0:n] + us[b][n:m]), v[b]], axis=0),
                 jnp.concatenate([bt[b], kt[b]], axis=0)) for b in batches]
    for b in batches:
        state_ref[b] = (state[b] + jnp.where(blockdiag, ds[b], 0.0)) * total[b]


def _rwkv_scan(r, v, kk, lw, kd, bv, n_batch, n_ctx):
    n_slabs, rows, _ = r.shape
    tm = SCAN_CHUNK * n_batch
    pp = min(SCAN_PAIRS, n_slabs)
    n_blk, n_ctx_blk = rows // tm, n_ctx * n_batch // tm
    blk = lambda d, c: _scan_block(d, c, n_ctx_blk, n_blk)
    shared = pl.BlockSpec((pp, tm, LANES), lambda d, p, c: (p, blk(d, c), 0))
    split = pl.BlockSpec((1, pp, tm, LANES), lambda d, p, c: (d, p, blk(d, c), 0))
    return pl.pallas_call(
        functools.partial(_rwkv_scan_kernel, n_batch),
        out_shape=jax.ShapeDtypeStruct((2, n_slabs, rows, LANES), F32),
        grid=(2, n_slabs // pp, n_blk),
        in_specs=[shared, shared, shared, split, split, split],
        out_specs=split,
        scratch_shapes=[pltpu.VMEM((pp * n_batch, LANES, LANES), F32)],
        compiler_params=_cparams("parallel", "parallel", "arbitrary"),
        name="rwkv_scan",
    )(r, v, kk, lw, kd, bv)


def _rwkv_readout(y_ref, g, bonus, gn_w, gn_b, ones_bd):
    y = jnp.concatenate([y_ref[0, j] + y_ref[1, j] for j in range(y_ref.shape[1])], axis=1)
    inv = 1.0 / HEAD_DIM
    mean = _segsum(y, ones_bd) * inv
    yc = y - mean
    var = _segsum(yc * yc, ones_bd) * inv
    yn = yc * lax.rsqrt(var + GN_EPS) * gn_w + gn_b
    return (yn + bonus) * g


def _norm_rope(x, gain, cos, sin_a, sin_b, ones_bd):
    ms = _segsum(x * x, ones_bd) * (1.0 / HEAD_DIM)
    xn = x * lax.rsqrt(ms + RMS_EPS) * gain
    return (xn * cos + pltpu.roll(xn, LANES - HEAD_DIM // 4, 1) * sin_a
            + pltpu.roll(xn, HEAD_DIM // 4, 1) * sin_b)


def _attn_prep_kernel(scale, n_batch, q_ref, k_ref, v_ref, cos_ref, sa_ref, sb_ref, qg_ref, kg_ref, ones_ref,
                      qo_ref, ko_ref, vo_ref, scr):
    cos, sin_a, sin_b = cos_ref[...], sa_ref[...], sb_ref[...]
    ones_bd = ones_ref[...]
    tm = cos.shape[0]
    tt = tm // n_batch
    first = lax.broadcasted_iota(jnp.int32, (tm, LANES), 1) < HEAD_DIM
    n_q, n_k = q_ref.shape[0], k_ref.shape[0]
    for j in range(n_q):
        scr[j] = _norm_rope(q_ref[j], qg_ref[...], cos, sin_a, sin_b, ones_bd) * scale
    for j in range(n_k):
        y = _norm_rope(k_ref[j], kg_ref[...], cos, sin_a, sin_b, ones_bd)
        swapped = pltpu.roll(y, HEAD_DIM, 1)
        scr[n_q + 2 * j] = jnp.where(first, y, swapped)
        scr[n_q + 2 * j + 1] = jnp.where(first, swapped, y)
    ones_rows = jnp.ones((vo_ref.shape[3] - HEAD_DIM, LANES), BF16)
    for b in range(n_batch):
        rows = pl.ds(b, tt, stride=n_batch)
        for j in range(n_q):
            qo_ref[b, j] = scr[j, rows, :].astype(BF16)
        for h in range(2 * n_k):
            ko_ref[b, h] = scr[n_q + h, rows, :].astype(BF16)
        for j in range(n_k):
            vb = v_ref[j, rows, :]
            for g in range(tt // LANES):
                vt = vb[g * LANES:(g + 1) * LANES].T.astype(BF16)
                for h in range(2):
                    vo_ref[b, 2 * j + h, g, 0:HEAD_DIM] = vt[h * HEAD_DIM:(h + 1) * HEAD_DIM]
                    vo_ref[b, 2 * j + h, g, HEAD_DIM:] = ones_rows


def _attn_prepare(pq, pk, pv, cos, sin_a, sin_b, q_gain, k_gain, ones_pair, n_batch):
    q_slabs, rows, _ = pq.shape
    k_slabs = pk.shape[0]
    tt_all = rows // n_batch
    tt = PREP_TT
    tm = tt * n_batch
    kh = 2 * k_slabs
    tab = pl.BlockSpec((tm, LANES), lambda i: (i, 0))
    vec = pl.BlockSpec((1, LANES), lambda i: (0, 0))
    return pl.pallas_call(
        functools.partial(_attn_prep_kernel, HEAD_DIM ** -0.5 * math.log2(math.e), n_batch),
        out_shape=[jax.ShapeDtypeStruct((n_batch, q_slabs, tt_all, LANES), BF16),
                   jax.ShapeDtypeStruct((n_batch, kh, tt_all, LANES), BF16),
                   jax.ShapeDtypeStruct((n_batch, kh, tt_all // LANES, VT_ROWS, LANES), BF16)],
        grid=(tt_all // tt,),
        in_specs=[pl.BlockSpec((q_slabs, tm, LANES), lambda i: (0, i, 0)),
                  pl.BlockSpec((k_slabs, tm, LANES), lambda i: (0, i, 0)),
                  pl.BlockSpec((k_slabs, tm, LANES), lambda i: (0, i, 0)),
                  tab, tab, tab, vec, vec,
                  pl.BlockSpec(ones_pair.shape, lambda i: (0, 0))],
        out_specs=[pl.BlockSpec((n_batch, q_slabs, tt, LANES), lambda i: (0, 0, i, 0)),
                   pl.BlockSpec((n_batch, kh, tt, LANES), lambda i: (0, 0, i, 0)),
                   pl.BlockSpec((n_batch, kh, tt // LANES, VT_ROWS, LANES), lambda i: (0, 0, i, 0, 0))],
        scratch_shapes=[pltpu.VMEM((q_slabs + kh, tm, LANES), F32)],
        compiler_params=_cparams("parallel"),
        name="attn_prepare",
    )(pq, pk, pv, cos, sin_a, sin_b, q_gain, k_gain, ones_pair)


def _flash_kernel(n_ctx, sub_ctx, sub_lat, q_ref, k_ref, vt_ref, o_ref):
    qi = pl.program_id(2)
    n_pairs, tq = q_ref.shape[1], q_ref.shape[2]
    n_keys = k_ref.shape[2]
    first = lax.broadcasted_iota(jnp.int32, (tq, LANES), 1) < HEAD_DIM
    zero = jnp.zeros((tq, LANES), BF16)
    q = jnp.concatenate([jnp.where(first if h == 0 else jnp.logical_not(first), q_ref[0, p], zero)
                         for p in range(n_pairs) for h in range(2)], axis=0)
    q_t = q.astype(F32).T.astype(BF16)

    def scores(blk):
        start, size = blk
        return jnp.dot(k_ref[0, 0, start:start + size, :], q_t, preferred_element_type=F32)

    def weighted_values(s, m_new, blk):
        start, size = blk
        p = jnp.exp2(s - m_new).astype(BF16)
        vt = jnp.concatenate([vt_ref[0, 0, start // LANES + g] for g in range(size // LANES)], axis=1)
        return jnp.dot(vt, p, preferred_element_type=F32)

    def attend(blocks):
        ahead = [scores(b) for b in blocks[:ATTN_AHEAD]]
        m_prev = acc = None
        for i, blk in enumerate(blocks):
            s = ahead.pop(0)
            if i + ATTN_AHEAD < len(blocks):
                ahead.append(scores(blocks[i + ATTN_AHEAD]))
            m_new = jnp.max(s, axis=0, keepdims=True)
            if m_prev is None:
                acc = weighted_values(s, m_new, blk)
            else:
                m_new = jnp.maximum(m_prev, m_new)
                acc = jnp.exp2(m_prev - m_new) * acc + weighted_values(s, m_new, blk)
            m_prev = m_new
        out_t = acc[0:HEAD_DIM] / acc[HEAD_DIM:HEAD_DIM + 1]
        for p in range(n_pairs):
            pair_t = jnp.concatenate([out_t[:, (2 * p) * tq:(2 * p + 1) * tq],
                                      out_t[:, (2 * p + 1) * tq:(2 * p + 2) * tq]], axis=0)
            o_ref[0, p] = pair_t.T.astype(o_ref.dtype)

    is_ctx = qi * tq < n_ctx
    ctx_blocks = [(s0, sub_ctx) for s0 in range(0, n_ctx, sub_ctx)]
    lat_blocks = [(s0, sub_lat) for s0 in range(n_ctx, n_keys, sub_lat)]

    @pl.when(is_ctx)
    def _():
        attend(ctx_blocks)

    @pl.when(jnp.logical_not(is_ctx))
    def _():
        attend(ctx_blocks + lat_blocks)


def _flash_attention(qh, kh, vt, n_ctx):
    n_batch, q_slabs, tt_all, _ = qh.shape
    n_kvh = kh.shape[1]
    n_pairs = q_slabs // n_kvh
    tq = min(ATTN_TQ, n_ctx)
    sub_ctx = min(ATTN_SUB, n_ctx)
    sub_lat = min(ATTN_SUB, tt_all - n_ctx)
    assert n_ctx % tq == 0 and n_ctx % sub_ctx == 0 and (tt_all - n_ctx) % sub_lat == 0
    assert sub_ctx % LANES == 0 and sub_lat % LANES == 0 and tq % LANES == 0
    return pl.pallas_call(
        functools.partial(_flash_kernel, n_ctx, sub_ctx, sub_lat),
        out_shape=jax.ShapeDtypeStruct((n_batch, q_slabs, tt_all, LANES), BF16),
        grid=(n_batch, n_kvh, tt_all // tq),
        in_specs=[pl.BlockSpec((1, n_pairs, tq, LANES), lambda b, h, qi: (b, h, qi, 0)),
                  pl.BlockSpec((1, 1, tt_all, LANES), lambda b, h, qi: (b, h, 0, 0)),
                  pl.BlockSpec((1, 1, tt_all // LANES, VT_ROWS, LANES), lambda b, h, qi: (b, h, 0, 0, 0))],
        out_specs=pl.BlockSpec((1, n_pairs, tq, LANES), lambda b, h, qi: (b, h, qi, 0)),
        compiler_params=_cparams("parallel", "parallel", "parallel"),
        name="flash_attention",
    )(qh, kh, vt)


def _s5_param_kernel(are_ref, aim_ref, ldt_ref, bre_ref, bim_ref, abar_re, abar_im, bf_re, bf_im):
    lam_re = jnp.minimum(are_ref[0], LAMBDA_RE_MAX)
    lam_im = aim_ref[0]
    dt = jnp.exp(ldt_ref[0])
    mag = jnp.exp(lam_re * dt)
    ar, ai = mag * jnp.cos(lam_im * dt), mag * jnp.sin(lam_im * dt)
    nr, ni = ar - 1.0, ai
    den = lam_re * lam_re + lam_im * lam_im
    cr = (nr * lam_re + ni * lam_im) / den
    cim = (ni * lam_re - nr * lam_im) / den
    abar_re[0], abar_im[0] = ar, ai
    bre, bim = bre_ref[...], bim_ref[...]
    bf_re[0] = cr * bre - cim * bim
    bf_im[0] = cr * bim + cim * bre


def _s5_params(a_re, a_im, log_dt, b_re, b_im):
    _, n_g, n_s = a_re.shape
    n_i = b_re.shape[-1]
    gn = n_g * n_s
    flat = lambda a: a.reshape(2, 1, gn)
    dt = jnp.broadcast_to(log_dt[:, :, None], (2, n_g, n_s))
    bt = lambda b: b.reshape(gn, n_i).T
    vec = pl.BlockSpec((1, 1, gn), lambda d: (d, 0, 0))
    mat = pl.BlockSpec((n_i, gn), lambda d: (0, 0))
    omat = pl.BlockSpec((1, n_i, gn), lambda d: (d, 0, 0))
    return pl.pallas_call(
        _s5_param_kernel,
        out_shape=[jax.ShapeDtypeStruct((2, 1, gn), F32)] * 2 + [jax.ShapeDtypeStruct((2, n_i, gn), F32)] * 2,
        grid=(2,),
        in_specs=[vec, vec, vec, mat, mat],
        out_specs=[vec, vec, omat, omat],
        compiler_params=_cparams("parallel"),
        name="s5_params",
    )(flat(a_re), flat(a_im), flat(dt), bt(b_re), bt(b_im))


def _s5_scan_kernel(u_ref, ar_ref, ai_ref, bre_ref, bim_ref, cre_ref, cim_ref, y_ref,
                    xr_ref, xi_ref, hr_ref, hi_ref):
    d = pl.program_id(0)
    c = pl.program_id(1)
    steps = u_ref.shape[0] // SUBLANES
    width = xr_ref.shape[1]

    @pl.when(c == 0)
    def _():
        hr_ref[...] = jnp.zeros_like(hr_ref)
        hi_ref[...] = jnp.zeros_like(hi_ref)

    n_split, kw, sw = bre_ref.shape[1], bre_ref.shape[2], bre_ref.shape[3]
    lt = min(SSM_LANE_TILE, sw)
    tiles = [(h, o) for h in range(n_split) for o in range(0, sw, lt)]
    u = u_ref[...].astype(BF16)

    def drive(tile):
        h, o = tile
        uh = u[:, h * kw:(h + 1) * kw]
        sl = slice(h * sw + o, h * sw + o + lt)
        xr_ref[:, sl] = jnp.dot(uh, bre_ref[0, h, :, o:o + lt], preferred_element_type=F32)
        xi_ref[:, sl] = jnp.dot(uh, bim_ref[0, h, :, o:o + lt], preferred_element_type=F32)

    def scan(tile, reverse):
        h, o = tile
        sl = slice(h * sw + o, h * sw + o + lt)
        ar = jnp.broadcast_to(ar_ref[0, :, sl], (SUBLANES, lt))
        ai = jnp.broadcast_to(ai_ref[0, :, sl], (SUBLANES, lt))
        hr, hi = hr_ref[:, sl], hi_ref[:, sl]
        for i in range(steps):
            t = steps - 1 - i if reverse else i
            rows = slice(t * SUBLANES, (t + 1) * SUBLANES)
            hr, hi = ar * hr - ai * hi + xr_ref[rows, sl], ar * hi + ai * hr + xi_ref[rows, sl]
            xr_ref[rows, sl] = hr
            xi_ref[rows, sl] = hi
        hr_ref[:, sl] = hr
        hi_ref[:, sl] = hi

    def readout(tile):
        h, o = tile
        sl = slice(h * sw + o, h * sw + o + lt)
        return (jnp.dot(xr_ref[:, sl].astype(BF16), cre_ref[h, o:o + lt, :], preferred_element_type=F32)
                - jnp.dot(xi_ref[:, sl].astype(BF16), cim_ref[h, o:o + lt, :], preferred_element_type=F32))

    def run(reverse):
        y = [None] * n_split
        drive(tiles[0])
        for n, tile in enumerate(tiles):
            if n + 1 < len(tiles):
                drive(tiles[n + 1])
            if n > 0:
                ph = tiles[n - 1][0]
                part = readout(tiles[n - 1])
                y[ph] = part if y[ph] is None else y[ph] + part
            scan(tile, reverse)
        ph = tiles[-1][0]
        part = readout(tiles[-1])
        y[ph] = part if y[ph] is None else y[ph] + part
        for h in range(n_split):
            y_ref[0, :, h * kw:(h + 1) * kw] = y[h].astype(y_ref.dtype)

    @pl.when(d == 0)
    def _():
        run(False)

    @pl.when(d == 1)
    def _():
        run(True)


def _s5_scan(u, abar_re, abar_im, b_re, b_im, c_re, c_im, n_batch, n_ctx):
    rows, width = u.shape
    gn = abar_re.shape[-1]
    tm = SSM_CHUNK * n_batch
    n_blk, n_ctx_blk = rows // tm, n_ctx * n_batch // tm
    blk = lambda d, c: _scan_block(d, c, n_ctx_blk, n_blk)
    vec = pl.BlockSpec((1, 1, gn), lambda d, c: (d, 0, 0))
    bmat = pl.BlockSpec((1,) + b_re.shape[1:], lambda d, c: (d, 0, 0, 0))
    cmat = pl.BlockSpec(c_re.shape, lambda d, c: (0, 0, 0))
    return pl.pallas_call(
        _s5_scan_kernel,
        out_shape=jax.ShapeDtypeStruct((2, rows, width), BF16),
        grid=(2, n_blk),
        in_specs=[pl.BlockSpec((tm, width), lambda d, c: (blk(d, c), 0)), vec, vec, bmat, bmat, cmat, cmat],
        out_specs=pl.BlockSpec((1, tm, width), lambda d, c: (d, blk(d, c), 0)),
        scratch_shapes=[pltpu.VMEM((tm, gn), F32), pltpu.VMEM((tm, gn), F32),
                        pltpu.VMEM((SUBLANES, gn), F32), pltpu.VMEM((SUBLANES, gn), F32)],
        compiler_params=_cparams("parallel", "arbitrary"),
        name="s5_scan",
    )(u, abar_re, abar_im, b_re, b_im, c_re, c_im)


def _gelu_tanh(x):
    return 0.5 * x * (1.0 + jnp.tanh(math.sqrt(2.0 / math.pi) * (x + 0.044715 * (x * x * x))))


def _s5_readout(y_ref, u, d_skip, glu_w, glu_b):
    y = _gelu_tanh(y_ref[0].astype(F32) + y_ref[1].astype(F32) + d_skip * u)
    return y * _sigmoid(jnp.dot(y.astype(BF16), glu_w, preferred_element_type=F32) + glu_b)


def _merge_kernel(alpha, ysa_ref, g_ref, bonus_ref, gnw_ref, gnb_ref, ones_ref,
                  yb_ref, ysc_ref, u_ref, dskip_ref, gluw_ref, glub_ref,
                  pg_ref, x_ref, gt_ref, pa_ref, pb_ref, pc_ref, wo_ref, lng_ref, lnb_ref, o_ref, yb_rows):
    d = x_ref.shape[1]
    ya = _rwkv_readout(ysa_ref, g_ref[...].astype(F32), bonus_ref[...].astype(F32), gnw_ref[...], gnb_ref[...],
                       ones_ref[...])
    yc = _s5_readout(ysc_ref, u_ref[...].astype(F32), dskip_ref[...], gluw_ref[...], glub_ref[...])
    n_batch, n_slabs, tt = yb_ref.shape[0], yb_ref.shape[1], yb_ref.shape[2]
    for b in range(n_batch):
        for j in range(n_slabs):
            yb_rows[j, pl.ds(b, tt, stride=n_batch), :] = yb_ref[b, j].astype(F32)
    yb = jnp.concatenate([yb_rows[j].astype(BF16) for j in range(n_slabs)], axis=1)
    pg = pg_ref[...].astype(F32)
    merged = (_sigmoid(pg[:, 0:d]) * _mm(ya, pa_ref[...])
              + _sigmoid(pg[:, d:2 * d]) * _mm(yb, pb_ref[...])
              + _sigmoid(pg[:, 2 * d:3 * d]) * _mm(yc, pc_ref[...]))
    mix = _mm(merged, wo_ref[...])
    o_ref[...] = _layer_norm(alpha * x_ref[...] + _rows_mod(mix, gt_ref[0]), lng_ref[...], lnb_ref[...])


def _merge(rwkv, yb, s5, pg, x_all, modtab, proj_a, proj_b, proj_c, w_out, ln_g, ln_b, alpha, n_ctx_rows):
    rows, d = x_all.shape
    n_batch, n_slabs = yb.shape[0], yb.shape[1]
    ysa, g, bonus, gn_w, gn_b, ones_bd = rwkv
    ysc, u, d_skip, glu_w, glu_b = s5
    tm = min(MERGE_ROW_TILE, n_ctx_rows)
    nctx = n_ctx_rows // tm
    seg = lambda i: (i >= nctx).astype(jnp.int32)
    row = lambda a: pl.BlockSpec((tm, a.shape[1]), lambda i: (i, 0))
    full = lambda a: pl.BlockSpec(a.shape, lambda i: (0, 0), pipeline_mode=pl.Buffered(1))
    vec = lambda a: pl.BlockSpec((1, a.shape[1]), lambda i: (0, 0))
    return pl.pallas_call(
        functools.partial(_merge_kernel, alpha),
        out_shape=jax.ShapeDtypeStruct((rows, d), F32),
        grid=(rows // tm,),
        scratch_shapes=[pltpu.VMEM((n_slabs, tm, LANES), F32)],
        in_specs=[pl.BlockSpec((2, ysa.shape[1], tm, LANES), lambda i: (0, 0, i, 0)),
                  row(g), row(bonus), vec(gn_w), vec(gn_b), full(ones_bd),
                  pl.BlockSpec((n_batch, n_slabs, tm // n_batch, LANES), lambda i: (0, 0, i, 0)),
                  pl.BlockSpec((2, tm, u.shape[1]), lambda i: (0, i, 0)), row(u), vec(d_skip), full(glu_w), vec(glu_b),
                  row(pg), row(x_all),
                  pl.BlockSpec((1, SUBLANES, d), lambda i: (seg(i), 0, 2)),
                  full(proj_a), full(proj_b), full(proj_c), full(w_out), vec(ln_g), vec(ln_b)],
        out_specs=pl.BlockSpec((tm, d), lambda i: (i, 0)),
        compiler_params=_cparams("parallel"),
        name="merge_ln",
    )(ysa, g, bonus, gn_w, gn_b, ones_bd, yb, ysc, u, d_skip, glu_w, glu_b,
      pg, x_all, modtab, proj_a, proj_b, proj_c, w_out, ln_g, ln_b)


def _mlp_kernel(alpha, x_ref, sh_ref, sc_ref, gt_ref, w1_ref, w2_ref, g_ref, b_ref, o_ref, h_ref, acc_ref,
                *split_ref):
    j = pl.program_id(1)

    @pl.when(j == 0)
    def _():
        h_ref[...] = _rows_add(_rows_mod(x_ref[...], 1.0 + sc_ref[0]), sh_ref[0]).astype(BF16)
        acc_ref[...] = jnp.zeros_like(acc_ref)

    a = jnp.maximum(jnp.dot(h_ref[...], w1_ref[...], preferred_element_type=F32), 0.0)
    acc_ref[...] += jnp.dot((a * a).astype(BF16), w2_ref[...], preferred_element_type=F32)

    @pl.when(j == pl.num_programs(1) - 1)
    def _():
        y = _layer_norm(alpha * x_ref[...] + _rows_mod(acc_ref[...], gt_ref[0]), g_ref[...], b_ref[...])
        if not split_ref:
            o_ref[...] = y
        else:
            rows_ref, = split_ref
            n_batch, tt, d = o_ref.shape
            for s in range(d // LANES):
                rows_ref[s] = y[:, s * LANES:(s + 1) * LANES]
            for b in range(n_batch):
                for s in range(d // LANES):
                    o_ref[b, :, s * LANES:(s + 1) * LANES] = rows_ref[s, pl.ds(b, tt, stride=n_batch), :]


def _mlp(x_mid, modtab, w1, w2, ln_g, ln_b, alpha, n_ctx_rows, final_batch=None):
    rows, d = x_mid.shape
    ff = w1.shape[1]
    tm = min(MLP_ROW_TILE, n_ctx_rows)
    tf = min(MLP_FF_TILE, ff)
    nctx = n_ctx_rows // tm
    first = nctx if final_batch else 0
    seg = lambda i: (i + first >= nctx).astype(jnp.int32)
    mod = lambda col: pl.BlockSpec((1, SUBLANES, d), lambda i, j: (seg(i), 0, col))
    vec = pl.BlockSpec((1, d), lambda i, j: (0, 0))
    scratch = [pltpu.VMEM((tm, d), BF16), pltpu.VMEM((tm, d), F32)]
    if final_batch:
        out_shape = jax.ShapeDtypeStruct((final_batch, (rows - n_ctx_rows) // final_batch, d), F32)
        out_spec = pl.BlockSpec((final_batch, tm // final_batch, d), lambda i, j: (0, i, 0))
        scratch.append(pltpu.VMEM((d // LANES, tm, LANES), F32))
    else:
        out_shape = jax.ShapeDtypeStruct((rows, d), F32)
        out_spec = pl.BlockSpec((tm, d), lambda i, j: (i, 0))
    return pl.pallas_call(
        functools.partial(_mlp_kernel, alpha),
        out_shape=out_shape,
        grid=(rows // tm - first, ff // tf),
        in_specs=[pl.BlockSpec((tm, d), lambda i, j: (i + first, 0)), mod(3), mod(4), mod(5),
                  pl.BlockSpec((d, tf), lambda i, j: (0, j)),
                  pl.BlockSpec((tf, d), lambda i, j: (j, 0)), vec, vec],
        out_specs=out_spec,
        scratch_shapes=scratch,
        compiler_params=_cparams("parallel", "arbitrary"),
        name="mlp_ln",
    )(x_mid, modtab, modtab, modtab, w1, w2, ln_g, ln_b)


def _block_diag_ones(n, blk):
    i = jnp.arange(n) // blk
    return (i[:, None] == i[None, :]).astype(BF16)


def _rope_tables(n_ctx, n_lat):
    pairs = HEAD_DIM // 4
    rows = n_lat // GRID_W
    row = jnp.repeat(jnp.arange(rows, dtype=F32), GRID_W)
    col = jnp.tile(jnp.arange(GRID_W, dtype=F32), rows)
    inv = ROPE_THETA ** (-jnp.arange(pairs, dtype=F32) / pairs)
    ang = jnp.stack([row, col], axis=-1)[:, :, None] * inv
    ang = jnp.broadcast_to(ang[:, :, None, :], (n_lat, 2, 2, pairs)).reshape(n_lat, HEAD_DIM)
    cos = jnp.concatenate([jnp.ones((n_ctx, HEAD_DIM), F32), jnp.cos(ang)], axis=0)
    sin = jnp.concatenate([jnp.zeros((n_ctx, HEAD_DIM), F32), jnp.sin(ang)], axis=0)
    first = (jnp.arange(HEAD_DIM) % (2 * pairs)) < pairs
    sin_a = jnp.where(first, -sin, 0.0)
    sin_b = jnp.where(first, 0.0, sin)
    two = lambda a: jnp.concatenate([a, a], axis=1)
    return two(cos), two(sin_a), two(sin_b)


def _block_diag(blocks):
    g, a, b = blocks.shape
    eye = jnp.eye(g, dtype=blocks.dtype)
    return (eye[:, None, :, None] * blocks[:, :, None, :]).reshape(g * a, g * b)


def kernel(x, c, ctx, c_ctx, mod_w, mod_b, w_in, rwkv_mu, rwkv_w0, rwkv_w_up, rwkv_a0, rwkv_a_up, rwkv_g_up, rwkv_k_k, rwkv_k_a, rwkv_r_k, rwkv_gn_w, rwkv_gn_b, attn_q_gain, attn_k_gain, ssm_a_re, ssm_a_im, ssm_log_dt, ssm_b_re, ssm_b_im, ssm_c_re, ssm_c_im, ssm_d, ssm_glu_w, ssm_glu_b, proj_a, proj_b, proj_c, w_out, ln1_g, ln1_b, ln2_g, ln2_b, mlp_w1, mlp_w2):
    n_batch, n_lat, d_model = x.shape
    n_ctx = ctx.shape[1]
    depth = mod_w.shape[0]
    assert n_batch == SUBLANES, "row layout packs the batch into one sublane tile"
    alpha = (2 * depth) ** 0.25
    n_ctx_rows = n_ctx * n_batch

    rwkv_w = rwkv_k_k.shape[1]
    q_w = proj_b.shape[1]
    kv_w = q_w // GQA_GROUP
    ssm_w = ssm_d.shape[1]
    rwkv_cols = rwkv_mu.shape[1]
    edges = [0, rwkv_cols, rwkv_cols + q_w, rwkv_cols + q_w + kv_w, rwkv_cols + q_w + 2 * kv_w,
             rwkv_cols + q_w + 2 * kv_w + ssm_w, w_in.shape[2]]
    slabbed = (False, True, True, True, False, False)
    splits = tuple(zip(edges[:-1], edges[1:], slabbed))

    x_all = _to_rows(ctx, x)

    cvec = jnp.concatenate([jnp.broadcast_to(c_ctx[None], (n_batch, d_model)), c], axis=0)
    modtab = _modulation(cvec, mod_w, mod_b).reshape(depth, 2, n_batch, 6 * d_model)

    ones_rwkv = _block_diag_ones(min(MXU_DIM, rwkv_w), HEAD_DIM)
    ones_pair = _block_diag_ones(LANES, HEAD_DIM)
    cos, sin_a, sin_b = (jnp.repeat(t, n_batch, axis=0) for t in _rope_tables(n_ctx, n_lat))
    rank = rwkv_w_up.shape[2]
    zpad = jnp.zeros((depth, 2, rank, rwkv_w), F32)

    for l in range(depth):
        mt = modtab[l]
        prm = dict(mu=rwkv_mu[l][None], w0=rwkv_w0[l][:, None], a0=rwkv_a0[l][:, None],
                   w_up=jnp.concatenate([rwkv_w_up[l], zpad[l]], axis=1).astype(BF16),
                   a_up=jnp.concatenate([zpad[l], rwkv_a_up[l]], axis=1).astype(BF16),
                   g_up=rwkv_g_up[l].astype(BF16), k_k=rwkv_k_k[l][None], k_a=rwkv_k_a[l][None],
                   r_k=rwkv_r_k[l][None], ones_bd=ones_rwkv)
        r, v, kk, g, bonus, lw, kd, bv, pq, pk, pv, pc, pg = _in_projection(
            x_all, mt, w_in[l].astype(BF16), splits, prm, n_ctx_rows)
        ysa = _rwkv_scan(r, v, kk, lw, kd, bv, n_batch, n_ctx)
        rwkv = (ysa, g, bonus, rwkv_gn_w[l][None], rwkv_gn_b[l][None], ones_rwkv)

        tile2 = lambda a: jnp.tile(a, 2)[None]
        qh, kh, vh = _attn_prepare(pq, pk, pv, cos, sin_a, sin_b, tile2(attn_q_gain[l]), tile2(attn_k_gain[l]),
                                   ones_pair, n_batch)
        yb = _flash_attention(qh, kh, vh, n_ctx)

        abar_re, abar_im, bf_re, bf_im = _s5_params(ssm_a_re[l], ssm_a_im[l], ssm_log_dt[l], ssm_b_re[l], ssm_b_im[l])
        n_g = ssm_a_re.shape[2]
        n_split = max(1, ssm_w // MXU_DIM)
        gs = n_g // n_split
        diag = lambda blocks: jnp.stack([_block_diag(blocks[h * gs:(h + 1) * gs]) for h in range(n_split)])
        bmat = lambda bf: jnp.stack([diag(bf[d].reshape(SSM_GROUP, n_g, SSM_STATE).transpose(1, 0, 2))
                                     for d in range(2)]).astype(BF16)
        cmat = lambda cc: diag(cc.transpose(0, 2, 1)).astype(BF16)
        ysc = _s5_scan(pc, abar_re, abar_im, bmat(bf_re), bmat(bf_im), cmat(ssm_c_re[l]), cmat(ssm_c_im[l]),
                       n_batch, n_ctx)
        s5 = (ysc, pc, ssm_d[l][None], ssm_glu_w[l].astype(BF16), ssm_glu_b[l][None])

        x_mid = _merge(rwkv, yb, s5, pg, x_all, mt, proj_a[l].astype(BF16), proj_b[l].astype(BF16),
                       proj_c[l].astype(BF16), w_out[l].astype(BF16), ln1_g[l][None], ln1_b[l][None],
                       alpha, n_ctx_rows)
        x_all = _mlp(x_mid, mt, mlp_w1[l].astype(BF16), mlp_w2[l].astype(BF16), ln2_g[l][None], ln2_b[l][None],
                     alpha, n_ctx_rows, final_batch=n_batch if l == depth - 1 else None)

    return x_all
```

```python
import functools
import math

import jax
import jax.numpy as jnp
from jax import lax
from jax.experimental import pallas as pl
from jax.experimental.pallas import tpu as pltpu

HEAD_DIM = 64
GRID_W = 64
ROPE_THETA = 10000.0
LN_EPS = 1e-5
RMS_EPS = 1e-6
GN_EPS = 64e-5
KK_EPS = 1e-24
LAMBDA_RE_MAX = -1e-4
GQA_GROUP = 4
SSM_GROUP = 16
SSM_STATE = 64

LANES = 128
SUBLANES = 8
MXU_DIM = 256
VMEM_LIMIT = 56 * 1024 * 1024

ROW_TILE = 256
MERGE_ROW_TILE = 512
LAYOUT_TT = 64
MLP_ROW_TILE = 1024
MLP_FF_TILE = 2048
SCAN_CHUNK = 64
SCAN_PAIRS = 2
SSM_CHUNK = 64
SSM_LANE_TILE = 512
ATTN_TQ = 256
ATTN_SUB = 256
ATTN_AHEAD = 1
PREP_TT = 128
VT_ROWS = 80

F32 = jnp.float32
BF16 = jnp.bfloat16


def _cparams(*sem):
    return pltpu.CompilerParams(dimension_semantics=sem, vmem_limit_bytes=VMEM_LIMIT)


def _mm(a, b):
    return jnp.dot(a.astype(BF16), b.astype(BF16), preferred_element_type=F32)


def _mm_nt(a, b):
    return lax.dot_general(a.astype(BF16), b.astype(BF16), (((1,), (1,)), ((), ())),
                           preferred_element_type=F32)


def _mm_tn(a, b):
    return lax.dot_general(a.astype(BF16), b.astype(BF16), (((0,), (0,)), ((), ())),
                           preferred_element_type=F32)


def _segsum(x, ones_bd):
    w = ones_bd.shape[0]
    hi = x.astype(BF16)
    lo = (x - hi.astype(F32)).astype(BF16)
    parts = [jnp.dot(hi[:, o:o + w], ones_bd, preferred_element_type=F32)
             + jnp.dot(lo[:, o:o + w], ones_bd, preferred_element_type=F32)
             for o in range(0, x.shape[1], w)]
    return parts[0] if len(parts) == 1 else jnp.concatenate(parts, axis=1)


def _sigmoid(x):
    return 1.0 / (1.0 + jnp.exp(-x))


def _softplus(x):
    return jnp.maximum(x, 0.0) + jnp.log(1.0 + jnp.exp(-jnp.abs(x)))


def _layer_norm(z, g, b):
    mu = jnp.mean(z, axis=-1, keepdims=True)
    zc = z - mu
    var = jnp.mean(zc * zc, axis=-1, keepdims=True)
    return zc * lax.rsqrt(var + LN_EPS) * g + b


def _rows_mod(x, m):
    tm, n = x.shape
    return (x.reshape(tm // SUBLANES, SUBLANES, n) * m[None]).reshape(tm, n)


def _rows_add(x, m):
    tm, n = x.shape
    return (x.reshape(tm // SUBLANES, SUBLANES, n) + m[None]).reshape(tm, n)


def _to_rows_kernel(n_ctx_tiles, ctx_ref, x_ref, o_ref, rows_ref):
    i = pl.program_id(0)
    n_batch, tt, d = x_ref.shape

    def interleave(src_ref):
        for b in range(n_batch):
            for s in range(d // LANES):
                rows_ref[s, pl.ds(b, tt, stride=n_batch), :] = src_ref[b, :, s * LANES:(s + 1) * LANES]
        for s in range(d // LANES):
            o_ref[:, s * LANES:(s + 1) * LANES] = rows_ref[s]

    @pl.when(i < n_ctx_tiles)
    def _():
        interleave(ctx_ref)

    @pl.when(i >= n_ctx_tiles)
    def _():
        interleave(x_ref)


def _to_rows(ctx, x):
    n_batch, n_ctx, d = ctx.shape
    n_lat = x.shape[1]
    tt = math.gcd(LAYOUT_TT, n_ctx)
    n_ctx_tiles = n_ctx // tt
    return pl.pallas_call(
        functools.partial(_to_rows_kernel, n_ctx_tiles),
        out_shape=jax.ShapeDtypeStruct(((n_ctx + n_lat) * n_batch, d), F32),
        grid=((n_ctx + n_lat) // tt,),
        in_specs=[pl.BlockSpec((n_batch, tt, d), lambda i: (0, jnp.minimum(i, n_ctx_tiles - 1), 0)),
                  pl.BlockSpec((n_batch, tt, d), lambda i: (0, jnp.maximum(i - n_ctx_tiles, 0), 0))],
        out_specs=pl.BlockSpec((tt * n_batch, d), lambda i: (i, 0)),
        scratch_shapes=[pltpu.VMEM((d // LANES, tt * n_batch, LANES), F32)],
        compiler_params=_cparams("parallel"),
        name="to_rows",
    )(ctx, x)


def _mod_kernel(c_ref, w_ref, b_ref, o_ref):
    c = c_ref[...]
    s = c * _sigmoid(c)
    o_ref[0] = jnp.dot(s, w_ref[0], preferred_element_type=F32,
                       precision=lax.Precision.HIGHEST) + b_ref[0]


def _modulation(cvec, mod_w, mod_b):
    n_layers, d, n6 = mod_w.shape
    tn = 1024 if n6 % 1024 == 0 else n6
    return pl.pallas_call(
        _mod_kernel,
        out_shape=jax.ShapeDtypeStruct((n_layers, cvec.shape[0], n6), F32),
        grid=(n_layers, n6 // tn),
        in_specs=[pl.BlockSpec((cvec.shape[0], d), lambda l, j: (0, 0)),
                  pl.BlockSpec((1, d, tn), lambda l, j: (l, 0, j)),
                  pl.BlockSpec((1, 1, tn), lambda l, j: (l, 0, j))],
        out_specs=pl.BlockSpec((1, cvec.shape[0], tn), lambda l, j: (l, 0, j)),
        compiler_params=_cparams("parallel", "parallel"),
        name="modulation",
    )(cvec, mod_w, mod_b.reshape(n_layers, 1, n6))


def _inproj_kernel(splits, n_ctx_rows, n_rows, x_ref, xp_ref, xn_ref, sh_ref, sc_ref, w_ref,
                   mu_ref, w0_ref, wup_ref, a0_ref, aup_ref, gup_ref, kk_ref, ka_ref, rk_ref, ones_ref, *o_refs):
    i = pl.program_id(0)
    tm = x_ref.shape[0]
    r0 = i * tm
    seg_start = jnp.logical_or(r0 == 0, r0 == n_ctx_rows)
    seg_end = jnp.logical_or(r0 + tm == n_ctx_rows, r0 + tm == n_rows)
    scale, shift = 1.0 + sc_ref[0], sh_ref[0]
    mod = lambda x: _rows_add(_rows_mod(x, scale), shift)
    xm = mod(x_ref[...])
    x_ext = jnp.concatenate([jnp.where(seg_start, 0.0, mod(xp_ref[...])), xm,
                             jnp.where(seg_end, 0.0, mod(xn_ref[...]))], axis=0).astype(BF16)
    a, b, _ = splits[0]
    n_rwkv = 8
    _rwkv_prep(jnp.dot(x_ext, w_ref[:, a:b], preferred_element_type=F32), mu_ref[...], w0_ref, wup_ref,
               a0_ref, aup_ref, gup_ref[...], kk_ref[...], ka_ref[...], rk_ref[...], ones_ref[...],
               o_refs[:n_rwkv])
    xm = xm.astype(BF16)
    for (a, b, slabbed), o_ref in zip(splits[1:], o_refs[n_rwkv:]):
        y = jnp.dot(xm, w_ref[:, a:b], preferred_element_type=F32)
        if slabbed:
            for j in range((b - a) // LANES):
                o_ref[j] = y[:, j * LANES:(j + 1) * LANES]
        else:
            o_ref[...] = y.astype(o_ref.dtype)


def _in_projection(x_all, modtab, w_bf16, splits, prm, n_ctx_rows):
    rows, d = x_all.shape
    width = prm["k_k"].shape[-1]
    tm = ROW_TILE
    nctx = n_ctx_rows // tm
    hb = tm // SUBLANES
    last = rows // SUBLANES - 1
    seg = lambda i: (i >= nctx).astype(jnp.int32)
    n_slabs = width // LANES
    slab = pl.BlockSpec((n_slabs, tm, LANES), lambda i: (0, i, 0))
    slab2 = pl.BlockSpec((2, n_slabs, tm, LANES), lambda i: (0, 0, i, 0))
    row = pl.BlockSpec((tm, width), lambda i: (i, 0))
    flat = jax.ShapeDtypeStruct((rows, width), BF16)
    one = jax.ShapeDtypeStruct((n_slabs, rows, LANES), F32)
    two = jax.ShapeDtypeStruct((2, n_slabs, rows, LANES), F32)
    shapes = [one, one, one, flat, flat, two, two, two]
    specs = [slab] * 3 + [row] * 2 + [slab2] * 3
    for a, b, slabbed in splits[1:]:
        if slabbed:
            shapes.append(jax.ShapeDtypeStruct(((b - a) // LANES, rows, LANES), F32))
            specs.append(pl.BlockSpec(((b - a) // LANES, tm, LANES), lambda i: (0, i, 0)))
        else:
            shapes.append(jax.ShapeDtypeStruct((rows, b - a), BF16))
            specs.append(pl.BlockSpec((tm, b - a), lambda i: (i, 0)))
    consts = [prm["mu"], prm["w0"], prm["w_up"], prm["a0"], prm["a_up"], prm["g_up"],
              prm["k_k"], prm["k_a"], prm["r_k"], prm["ones_bd"]]
    full = lambda a: pl.BlockSpec(a.shape, lambda i: (0,) * a.ndim)
    return pl.pallas_call(
        functools.partial(_inproj_kernel, splits, n_ctx_rows, rows),
        out_shape=shapes,
        grid=(rows // tm,),
        in_specs=[pl.BlockSpec((tm, d), lambda i: (i, 0)),
                  pl.BlockSpec((SUBLANES, d), lambda i: (jnp.maximum(i * hb - 1, 0), 0)),
                  pl.BlockSpec((SUBLANES, d), lambda i: (jnp.minimum((i + 1) * hb, last), 0)),
                  pl.BlockSpec((1, SUBLANES, d), lambda i: (seg(i), 0, 0)),
                  pl.BlockSpec((1, SUBLANES, d), lambda i: (seg(i), 0, 1)),
                  pl.BlockSpec(w_bf16.shape, lambda i: (0, 0), pipeline_mode=pl.Buffered(1))]
                 + [full(a) for a in consts],
        out_specs=specs,
        compiler_params=_cparams("parallel"),
        name="in_projection",
    )(x_all, x_all, x_all, modtab, modtab, w_bf16, *consts)


def _rwkv_prep(p_ext, mu, w0_ref, wup_ref, a0_ref, aup_ref, g_up, k_k, k_a, r_k, ones_bd, outs):
    r_out, v_out, kkn_out, g_out, bonus_out, lw_out, kd_out, bv_out = outs
    tm = p_ext.shape[0] - 2 * SUBLANES
    p = p_ext[SUBLANES:SUBLANES + tm]
    p = p + mu * (0.5 * (p_ext[0:tm] + p_ext[2 * SUBLANES:]) - p)

    w = k_k.shape[-1]
    r, k, v = p[:, 0:w], p[:, w:2 * w], p[:, 2 * w:3 * w]
    wa = p[:, 3 * w:3 * w + LANES]
    gd = p[:, 3 * w + LANES:3 * w + 2 * LANES]

    def put(o_ref, y, *lead):
        for j in range(w // LANES):
            o_ref[lead + (j,)] = y[:, j * LANES:(j + 1) * LANES]

    g_out[...] = _mm(_sigmoid(gd), g_up).astype(g_out.dtype)
    kk = k * k_k
    kk = kk * lax.rsqrt(jnp.maximum(_segsum(kk * kk, ones_bd), KK_EPS))
    put(r_out, r)
    put(v_out, v)
    put(kkn_out, kk)
    tanh_wa = jnp.tanh(wa)
    rk_sum = jnp.zeros_like(r)
    for d in range(2):
        w_pre = w0_ref[d] + _mm(tanh_wa, wup_ref[d])
        put(lw_out, -jnp.exp(-_softplus(-w_pre) - 0.5), d)
        a = _sigmoid(a0_ref[d] + _mm(wa, aup_ref[d]))
        k_dir = k * (1.0 + (a - 1.0) * k_a)
        put(kd_out, k_dir, d)
        put(bv_out, kk * a, d)
        rk_sum = rk_sum + r * k_dir
    bonus_out[...] = (_segsum(rk_sum * r_k, ones_bd) * v).astype(bonus_out.dtype)


def _scan_block(d, c, n_ctx_blk, n_blk):
    bwd = jnp.where(c < n_ctx_blk, n_ctx_blk - 1 - c, n_ctx_blk + (n_blk - 1 - c))
    return jnp.where(d == 0, c, bwd)


def _tri_inverse(a_list, ri, ci, n):
    eye = (ri == ci).astype(F32)
    same = (ri // SUBLANES) == (ci // SUBLANES)
    xs = [-jnp.where(same, a, 0.0) for a in a_list]
    x2 = [_mm(x, x) for x in xs]
    x4 = [_mm(x, x) for x in x2]
    ts = [eye + x for x in xs]
    ts = [t + _mm(t, x) for t, x in zip(ts, x2)]
    ts = [t + _mm(t, x) for t, x in zip(ts, x4)]
    size = SUBLANES
    while size < n:
        pick = jnp.logical_and((ri // (2 * size)) == (ci // (2 * size)), (ri // size) != (ci // size))
        lows = [_mm(jnp.where(pick, a, 0.0), t) for a, t in zip(a_list, ts)]
        ts = [t - _mm(t, low) for t, low in zip(ts, lows)]
        size *= 2
    return ts


def _rwkv_scan_kernel(n_batch, r_ref, v_ref, kk_ref, lw_ref, kd_ref, bv_ref, y_ref, state_ref):
    d = pl.program_id(0)
    c = pl.program_id(2)
    n = r_ref.shape[1] // n_batch
    m = 2 * n
    chains = [(p, b) for p in range(r_ref.shape[0]) for b in range(n_batch)]
    batches = range(len(chains))

    @pl.when(c == 0)
    def _():
        state_ref[...] = jnp.zeros_like(state_ref)

    sign = 1 - 2 * d
    ri = lax.broadcasted_iota(jnp.int32, (m, m), 0)
    ci = lax.broadcasted_iota(jnp.int32, (m, m), 1)
    same_head = (ri // n) == (ci // n)
    order = (ri - ci) * sign
    strict = jnp.logical_and(same_head, order > 0)
    incl = jnp.logical_and(same_head, order >= 0)
    ti = lax.broadcasted_iota(jnp.int32, (n, n), 0)
    si = lax.broadcasted_iota(jnp.int32, (n, n), 1)
    tri = ((ti - si) * sign >= 0).astype(BF16)
    head0 = lax.broadcasted_iota(jnp.int32, (n, LANES), 1) < HEAD_DIM
    blockdiag = (lax.broadcasted_iota(jnp.int32, (LANES, LANES), 0) // HEAD_DIM
                 == lax.broadcasted_iota(jnp.int32, (LANES, LANES), 1) // HEAD_DIM)

    def stack(x):
        return jnp.concatenate([jnp.where(head0, x, 0.0), jnp.where(head0, 0.0, x)], axis=0)

    def shared(ref, i):
        p, b = chains[i]
        return ref[p, pl.ds(b, n, stride=n_batch), :]

    def split(ref, i):
        p, b = chains[i]
        return ref[0, p, pl.ds(b, n, stride=n_batch), :]

    duos = range(0, len(chains), 2)
    cat = lambda xs, i: jnp.concatenate([xs[i], xs[i + 1]], axis=1)
    uncat = lambda xs: [x[:, o:o + LANES] for x in xs for o in (0, LANES)]

    lw = [split(lw_ref, b) for b in batches]
    hi = [x.astype(BF16) for x in lw]
    lo = [(x - h.astype(F32)).astype(BF16) for x, h in zip(lw, hi)]
    cum = uncat([jnp.dot(tri, cat(hi, i), preferred_element_type=F32)
                 + jnp.dot(tri, cat(lo, i), preferred_element_type=F32) for i in duos])
    e_cum = [jnp.exp(x) for x in cum]
    e_neg = [jnp.exp(-x) for x in cum]
    kap = [shared(kk_ref, b) * jnp.exp(cum[b] - lw[b]) for b in batches]
    rt = [shared(r_ref, b) * e_cum[b] for b in batches]
    bt = [split(bv_ref, b) * e_neg[b] for b in batches]
    kt = [split(kd_ref, b) * e_neg[b] for b in batches]
    v = [shared(v_ref, b) for b in batches]
    total = [jnp.where(d == 0, e[n - 1:n], e[0:1]) for e in e_cum]
    state = [state_ref[b] for b in batches]

    scores = [_mm_nt(jnp.concatenate([stack(kap[b]), stack(rt[b])], axis=0),
                     jnp.concatenate([stack(bt[b]), stack(kt[b])], axis=0)) for b in batches]
    a_b = [jnp.where(strict, s[0:m, 0:m], 0.0) for s in scores]
    a_k = [jnp.where(strict, s[0:m, m:2 * m], 0.0) for s in scores]
    a_r = [jnp.concatenate([jnp.where(incl, s[m:2 * m, m:2 * m], 0.0),
                            jnp.where(incl, -s[m:2 * m, 0:m], 0.0)], axis=1) for s in scores]
    g0 = [_mm_nt(jnp.concatenate([kap[b], rt[b]], axis=0), state[b]) for b in batches]
    t_inv = _tri_inverse(a_b, ri, ci, n)
    vs = [stack(x) for x in v]
    rhs = [stack(g0[b][0:n]) + _mm(a_k[b], vs[b]) for b in batches]
    us = [_mm(t_inv[b], rhs[b]) for b in batches]
    ys = [stack(g0[b][n:m]) + _mm(a_r[b], jnp.concatenate([vs[b], us[b]], axis=0)) for b in batches]
    for i in batches:
        p, b = chains[i]
        y_ref[0, p, pl.ds(b, n, stride=n_batch), :] = ys[i][0:n] + ys[i][n:m]
    ds = [_mm_tn(jnp.concatenate([-(us[b][0:n] + us[b][n:m]), v[b]], axis=0),
                 jnp.concatenate([bt[b], kt[b]], axis=0)) for b in batches]
    for b in batches:
        state_ref[b] = (state[b] + jnp.where(blockdiag, ds[b], 0.0)) * total[b]


def _rwkv_scan(r, v, kk, lw, kd, bv, n_batch, n_ctx):
    n_slabs, rows, _ = r.shape
    tm = SCAN_CHUNK * n_batch
    pp = min(SCAN_PAIRS, n_slabs)
    n_blk, n_ctx_blk = rows // tm, n_ctx * n_batch // tm
    blk = lambda d, c: _scan_block(d, c, n_ctx_blk, n_blk)
    shared = pl.BlockSpec((pp, tm, LANES), lambda d, p, c: (p, blk(d, c), 0))
    split = pl.BlockSpec((1, pp, tm, LANES), lambda d, p, c: (d, p, blk(d, c), 0))
    return pl.pallas_call(
        functools.partial(_rwkv_scan_kernel, n_batch),
        out_shape=jax.ShapeDtypeStruct((2, n_slabs, rows, LANES), F32),
        grid=(2, n_slabs // pp, n_blk),
        in_specs=[shared, shared, shared, split, split, split],
        out_specs=split,
        scratch_shapes=[pltpu.VMEM((pp * n_batch, LANES, LANES), F32)],
        compiler_params=_cparams("parallel", "parallel", "arbitrary"),
        name="rwkv_scan",
    )(r, v, kk, lw, kd, bv)


def _rwkv_readout(y_ref, g, bonus, gn_w, gn_b, ones_bd):
    y = jnp.concatenate([y_ref[0, j] + y_ref[1, j] for j in range(y_ref.shape[1])], axis=1)
    inv = 1.0 / HEAD_DIM
    mean = _segsum(y, ones_bd) * inv
    yc = y - mean
    var = _segsum(yc * yc, ones_bd) * inv
    yn = yc * lax.rsqrt(var + GN_EPS) * gn_w + gn_b
    return (yn + bonus) * g


def _norm_rope(x, gain, cos, sin_a, sin_b, ones_bd):
    ms = _segsum(x * x, ones_bd) * (1.0 / HEAD_DIM)
    xn = x * lax.rsqrt(ms + RMS_EPS) * gain
    return (xn * cos + pltpu.roll(xn, LANES - HEAD_DIM // 4, 1) * sin_a
            + pltpu.roll(xn, HEAD_DIM // 4, 1) * sin_b)


def _attn_prep_kernel(scale, n_batch, q_ref, k_ref, v_ref, cos_ref, sa_ref, sb_ref, qg_ref, kg_ref, ones_ref,
                      qo_ref, ko_ref, vo_ref, scr):
    cos, sin_a, sin_b = cos_ref[...], sa_ref[...], sb_ref[...]
    ones_bd = ones_ref[...]
    tm = cos.shape[0]
    tt = tm // n_batch
    first = lax.broadcasted_iota(jnp.int32, (tm, LANES), 1) < HEAD_DIM
    n_q, n_k = q_ref.shape[0], k_ref.shape[0]
    for j in range(n_q):
        scr[j] = _norm_rope(q_ref[j], qg_ref[...], cos, sin_a, sin_b, ones_bd) * scale
    for j in range(n_k):
        y = _norm_rope(k_ref[j], kg_ref[...], cos, sin_a, sin_b, ones_bd)
        swapped = pltpu.roll(y, HEAD_DIM, 1)
        scr[n_q + 2 * j] = jnp.where(first, y, swapped)
        scr[n_q + 2 * j + 1] = jnp.where(first, swapped, y)
    ones_rows = jnp.ones((vo_ref.shape[3] - HEAD_DIM, LANES), BF16)
    for b in range(n_batch):
        rows = pl.ds(b, tt, stride=n_batch)
        for j in range(n_q):
            qo_ref[b, j] = scr[j, rows, :].astype(BF16)
        for h in range(2 * n_k):
            ko_ref[b, h] = scr[n_q + h, rows, :].astype(BF16)
        for j in range(n_k):
            vb = v_ref[j, rows, :]
            for g in range(tt // LANES):
                vt = vb[g * LANES:(g + 1) * LANES].T.astype(BF16)
                for h in range(2):
                    vo_ref[b, 2 * j + h, g, 0:HEAD_DIM] = vt[h * HEAD_DIM:(h + 1) * HEAD_DIM]
                    vo_ref[b, 2 * j + h, g, HEAD_DIM:] = ones_rows


def _attn_prepare(pq, pk, pv, cos, sin_a, sin_b, q_gain, k_gain, ones_pair, n_batch):
    q_slabs, rows, _ = pq.shape
    k_slabs = pk.shape[0]
    tt_all = rows // n_batch
    tt = PREP_TT
    tm = tt * n_batch
    kh = 2 * k_slabs
    tab = pl.BlockSpec((tm, LANES), lambda i: (i, 0))
    vec = pl.BlockSpec((1, LANES), lambda i: (0, 0))
    return pl.pallas_call(
        functools.partial(_attn_prep_kernel, HEAD_DIM ** -0.5 * math.log2(math.e), n_batch),
        out_shape=[jax.ShapeDtypeStruct((n_batch, q_slabs, tt_all, LANES), BF16),
                   jax.ShapeDtypeStruct((n_batch, kh, tt_all, LANES), BF16),
                   jax.ShapeDtypeStruct((n_batch, kh, tt_all // LANES, VT_ROWS, LANES), BF16)],
        grid=(tt_all // tt,),
        in_specs=[pl.BlockSpec((q_slabs, tm, LANES), lambda i: (0, i, 0)),
                  pl.BlockSpec((k_slabs, tm, LANES), lambda i: (0, i, 0)),
                  pl.BlockSpec((k_slabs, tm, LANES), lambda i: (0, i, 0)),
                  tab, tab, tab, vec, vec,
                  pl.BlockSpec(ones_pair.shape, lambda i: (0, 0))],
        out_specs=[pl.BlockSpec((n_batch, q_slabs, tt, LANES), lambda i: (0, 0, i, 0)),
                   pl.BlockSpec((n_batch, kh, tt, LANES), lambda i: (0, 0, i, 0)),
                   pl.BlockSpec((n_batch, kh, tt // LANES, VT_ROWS, LANES), lambda i: (0, 0, i, 0, 0))],
        scratch_shapes=[pltpu.VMEM((q_slabs + kh, tm, LANES), F32)],
        compiler_params=_cparams("parallel"),
        name="attn_prepare",
    )(pq, pk, pv, cos, sin_a, sin_b, q_gain, k_gain, ones_pair)


def _flash_kernel(n_ctx, sub_ctx, sub_lat, q_ref, k_ref, vt_ref, o_ref):
    qi = pl.program_id(2)
    n_pairs, tq = q_ref.shape[1], q_ref.shape[2]
    n_keys = k_ref.shape[2]
    first = lax.broadcasted_iota(jnp.int32, (tq, LANES), 1) < HEAD_DIM
    zero = jnp.zeros((tq, LANES), BF16)
    q = jnp.concatenate([jnp.where(first if h == 0 else jnp.logical_not(first), q_ref[0, p], zero)
                         for p in range(n_pairs) for h in range(2)], axis=0)
    q_t = q.astype(F32).T.astype(BF16)

    def scores(blk):
        start, size = blk
        return jnp.dot(k_ref[0, 0, start:start + size, :], q_t, preferred_element_type=F32)

    def weighted_values(p, blk):
        start, size = blk
        vt = jnp.concatenate([vt_ref[0, 0, start // LANES + g] for g in range(size // LANES)], axis=1)
        return jnp.dot(vt, p, preferred_element_type=F32)

    def attend(blocks):
        ahead = [scores(b) for b in blocks[:ATTN_AHEAD]]
        m_prev = acc = pending = None
        for i, blk in enumerate(blocks):
            s = ahead.pop(0)
            if i + ATTN_AHEAD < len(blocks):
                ahead.append(scores(blocks[i + ATTN_AHEAD]))
            if pending is not None:
                alpha, p, pblk = pending
                pv = weighted_values(p, pblk)
                acc = pv if acc is None else alpha * acc + pv
            m_new = jnp.max(s, axis=0, keepdims=True)
            alpha = None
            if m_prev is not None:
                m_new = jnp.maximum(m_prev, m_new)
                alpha = jnp.exp2(m_prev - m_new)
            pending = (alpha, jnp.exp2(s - m_new).astype(BF16), blk)
            m_prev = m_new
        alpha, p, pblk = pending
        pv = weighted_values(p, pblk)
        acc = pv if acc is None else alpha * acc + pv
        out_t = acc[0:HEAD_DIM] / acc[HEAD_DIM:HEAD_DIM + 1]
        for p in range(n_pairs):
            pair_t = jnp.concatenate([out_t[:, (2 * p) * tq:(2 * p + 1) * tq],
                                      out_t[:, (2 * p + 1) * tq:(2 * p + 2) * tq]], axis=0)
            o_ref[0, p] = pair_t.T.astype(o_ref.dtype)

    is_ctx = qi * tq < n_ctx
    ctx_blocks = [(s0, sub_ctx) for s0 in range(0, n_ctx, sub_ctx)]
    lat_blocks = [(s0, sub_lat) for s0 in range(n_ctx, n_keys, sub_lat)]

    @pl.when(is_ctx)
    def _():
        attend(ctx_blocks)

    @pl.when(jnp.logical_not(is_ctx))
    def _():
        attend(ctx_blocks + lat_blocks)


def _flash_attention(qh, kh, vt, n_ctx):
    n_batch, q_slabs, tt_all, _ = qh.shape
    n_kvh = kh.shape[1]
    n_pairs = q_slabs // n_kvh
    tq = min(ATTN_TQ, n_ctx)
    sub_ctx = min(ATTN_SUB, n_ctx)
    sub_lat = min(ATTN_SUB, tt_all - n_ctx)
    assert n_ctx % tq == 0 and n_ctx % sub_ctx == 0 and (tt_all - n_ctx) % sub_lat == 0
    assert sub_ctx % LANES == 0 and sub_lat % LANES == 0 and tq % LANES == 0
    return pl.pallas_call(
        functools.partial(_flash_kernel, n_ctx, sub_ctx, sub_lat),
        out_shape=jax.ShapeDtypeStruct((n_batch, q_slabs, tt_all, LANES), BF16),
        grid=(n_batch, n_kvh, tt_all // tq),
        in_specs=[pl.BlockSpec((1, n_pairs, tq, LANES), lambda b, h, qi: (b, h, qi, 0)),
                  pl.BlockSpec((1, 1, tt_all, LANES), lambda b, h, qi: (b, h, 0, 0)),
                  pl.BlockSpec((1, 1, tt_all // LANES, VT_ROWS, LANES), lambda b, h, qi: (b, h, 0, 0, 0))],
        out_specs=pl.BlockSpec((1, n_pairs, tq, LANES), lambda b, h, qi: (b, h, qi, 0)),
        compiler_params=_cparams("parallel", "parallel", "parallel"),
        name="flash_attention",
    )(qh, kh, vt)


def _s5_param_kernel(are_ref, aim_ref, ldt_ref, bre_ref, bim_ref, abar_re, abar_im, bf_re, bf_im):
    lam_re = jnp.minimum(are_ref[0], LAMBDA_RE_MAX)
    lam_im = aim_ref[0]
    dt = jnp.exp(ldt_ref[0])
    mag = jnp.exp(lam_re * dt)
    ar, ai = mag * jnp.cos(lam_im * dt), mag * jnp.sin(lam_im * dt)
    nr, ni = ar - 1.0, ai
    den = lam_re * lam_re + lam_im * lam_im
    cr = (nr * lam_re + ni * lam_im) / den
    cim = (ni * lam_re - nr * lam_im) / den
    abar_re[0], abar_im[0] = ar, ai
    bre, bim = bre_ref[...], bim_ref[...]
    bf_re[0] = cr * bre - cim * bim
    bf_im[0] = cr * bim + cim * bre


def _s5_params(a_re, a_im, log_dt, b_re, b_im):
    _, n_g, n_s = a_re.shape
    n_i = b_re.shape[-1]
    gn = n_g * n_s
    flat = lambda a: a.reshape(2, 1, gn)
    dt = jnp.broadcast_to(log_dt[:, :, None], (2, n_g, n_s))
    bt = lambda b: b.reshape(gn, n_i).T
    vec = pl.BlockSpec((1, 1, gn), lambda d: (d, 0, 0))
    mat = pl.BlockSpec((n_i, gn), lambda d: (0, 0))
    omat = pl.BlockSpec((1, n_i, gn), lambda d: (d, 0, 0))
    return pl.pallas_call(
        _s5_param_kernel,
        out_shape=[jax.ShapeDtypeStruct((2, 1, gn), F32)] * 2 + [jax.ShapeDtypeStruct((2, n_i, gn), F32)] * 2,
        grid=(2,),
        in_specs=[vec, vec, vec, mat, mat],
        out_specs=[vec, vec, omat, omat],
        compiler_params=_cparams("parallel"),
        name="s5_params",
    )(flat(a_re), flat(a_im), flat(dt), bt(b_re), bt(b_im))


def _s5_scan_kernel(u_ref, ar_ref, ai_ref, bre_ref, bim_ref, cre_ref, cim_ref, y_ref,
                    xr_ref, xi_ref, hr_ref, hi_ref):
    d = pl.program_id(0)
    c = pl.program_id(1)
    steps = u_ref.shape[0] // SUBLANES
    width = xr_ref.shape[1]

    @pl.when(c == 0)
    def _():
        hr_ref[...] = jnp.zeros_like(hr_ref)
        hi_ref[...] = jnp.zeros_like(hi_ref)

    n_split, kw, sw = bre_ref.shape[1], bre_ref.shape[2], bre_ref.shape[3]
    lt = min(SSM_LANE_TILE, sw)
    tiles = [(h, o) for h in range(n_split) for o in range(0, sw, lt)]
    u = u_ref[...].astype(BF16)

    def drive(tile):
        h, o = tile
        uh = u[:, h * kw:(h + 1) * kw]
        sl = slice(h * sw + o, h * sw + o + lt)
        xr_ref[:, sl] = jnp.dot(uh, bre_ref[0, h, :, o:o + lt], preferred_element_type=F32)
        xi_ref[:, sl] = jnp.dot(uh, bim_ref[0, h, :, o:o + lt], preferred_element_type=F32)

    def scan(tile, reverse):
        h, o = tile
        sl = slice(h * sw + o, h * sw + o + lt)
        ar = jnp.broadcast_to(ar_ref[0, :, sl], (SUBLANES, lt))
        ai = jnp.broadcast_to(ai_ref[0, :, sl], (SUBLANES, lt))
        hr, hi = hr_ref[:, sl], hi_ref[:, sl]
        for i in range(steps):
            t = steps - 1 - i if reverse else i
            rows = slice(t * SUBLANES, (t + 1) * SUBLANES)
            hr, hi = ar * hr - ai * hi + xr_ref[rows, sl], ar * hi + ai * hr + xi_ref[rows, sl]
            xr_ref[rows, sl] = hr
            xi_ref[rows, sl] = hi
        hr_ref[:, sl] = hr
        hi_ref[:, sl] = hi

    def readout(tile):
        h, o = tile
        sl = slice(h * sw + o, h * sw + o + lt)
        return (jnp.dot(xr_ref[:, sl].astype(BF16), cre_ref[h, o:o + lt, :], preferred_element_type=F32)
                - jnp.dot(xi_ref[:, sl].astype(BF16), cim_ref[h, o:o + lt, :], preferred_element_type=F32))

    def run(reverse):
        y = [None] * n_split
        drive(tiles[0])
        for n, tile in enumerate(tiles):
            if n + 1 < len(tiles):
                drive(tiles[n + 1])
            if n > 0:
                ph = tiles[n - 1][0]
                part = readout(tiles[n - 1])
                y[ph] = part if y[ph] is None else y[ph] + part
            scan(tile, reverse)
        ph = tiles[-1][0]
        part = readout(tiles[-1])
        y[ph] = part if y[ph] is None else y[ph] + part
        for h in range(n_split):
            y_ref[0, :, h * kw:(h + 1) * kw] = y[h].astype(y_ref.dtype)

    @pl.when(d == 0)
    def _():
        run(False)

    @pl.when(d == 1)
    def _():
        run(True)


def _s5_scan(u, abar_re, abar_im, b_re, b_im, c_re, c_im, n_batch, n_ctx):
    rows, width = u.shape
    gn = abar_re.shape[-1]
    tm = SSM_CHUNK * n_batch
    n_blk, n_ctx_blk = rows // tm, n_ctx * n_batch // tm
    blk = lambda d, c: _scan_block(d, c, n_ctx_blk, n_blk)
    vec = pl.BlockSpec((1, 1, gn), lambda d, c: (d, 0, 0))
    bmat = pl.BlockSpec((1,) + b_re.shape[1:], lambda d, c: (d, 0, 0, 0))
    cmat = pl.BlockSpec(c_re.shape, lambda d, c: (0, 0, 0))
    return pl.pallas_call(
        _s5_scan_kernel,
        out_shape=jax.ShapeDtypeStruct((2, rows, width), BF16),
        grid=(2, n_blk),
        in_specs=[pl.BlockSpec((tm, width), lambda d, c: (blk(d, c), 0)), vec, vec, bmat, bmat, cmat, cmat],
        out_specs=pl.BlockSpec((1, tm, width), lambda d, c: (d, blk(d, c), 0)),
        scratch_shapes=[pltpu.VMEM((tm, gn), F32), pltpu.VMEM((tm, gn), F32),
                        pltpu.VMEM((SUBLANES, gn), F32), pltpu.VMEM((SUBLANES, gn), F32)],
        compiler_params=_cparams("parallel", "arbitrary"),
        name="s5_scan",
    )(u, abar_re, abar_im, b_re, b_im, c_re, c_im)


def _gelu_tanh(x):
    return 0.5 * x * (1.0 + jnp.tanh(math.sqrt(2.0 / math.pi) * (x + 0.044715 * (x * x * x))))


def _s5_readout(y_ref, u, d_skip, glu_w, glu_b):
    y = _gelu_tanh(y_ref[0].astype(F32) + y_ref[1].astype(F32) + d_skip * u)
    return y * _sigmoid(jnp.dot(y.astype(BF16), glu_w, preferred_element_type=F32) + glu_b)


def _merge_kernel(alpha, ysa_ref, g_ref, bonus_ref, gnw_ref, gnb_ref, ones_ref,
                  yb_ref, ysc_ref, u_ref, dskip_ref, gluw_ref, glub_ref,
                  pg_ref, x_ref, gt_ref, pa_ref, pb_ref, pc_ref, wo_ref, lng_ref, lnb_ref, o_ref, yb_rows):
    d = x_ref.shape[1]
    ya = _rwkv_readout(ysa_ref, g_ref[...].astype(F32), bonus_ref[...].astype(F32), gnw_ref[...], gnb_ref[...],
                       ones_ref[...])
    yc = _s5_readout(ysc_ref, u_ref[...].astype(F32), dskip_ref[...], gluw_ref[...], glub_ref[...])
    n_batch, n_slabs, tt = yb_ref.shape[0], yb_ref.shape[1], yb_ref.shape[2]
    for b in range(n_batch):
        for j in range(n_slabs):
            yb_rows[j, pl.ds(b, tt, stride=n_batch), :] = yb_ref[b, j].astype(F32)
    yb = jnp.concatenate([yb_rows[j].astype(BF16) for j in range(n_slabs)], axis=1)
    pg = pg_ref[...].astype(F32)
    merged = (_sigmoid(pg[:, 0:d]) * _mm(ya, pa_ref[...])
              + _sigmoid(pg[:, d:2 * d]) * _mm(yb, pb_ref[...])
              + _sigmoid(pg[:, 2 * d:3 * d]) * _mm(yc, pc_ref[...]))
    mix = _mm(merged, wo_ref[...])
    o_ref[...] = _layer_norm(alpha * x_ref[...] + _rows_mod(mix, gt_ref[0]), lng_ref[...], lnb_ref[...])


def _merge(rwkv, yb, s5, pg, x_all, modtab, proj_a, proj_b, proj_c, w_out, ln_g, ln_b, alpha, n_ctx_rows):
    rows, d = x_all.shape
    n_batch, n_slabs = yb.shape[0], yb.shape[1]
    ysa, g, bonus, gn_w, gn_b, ones_bd = rwkv
    ysc, u, d_skip, glu_w, glu_b = s5
    tm = min(MERGE_ROW_TILE, n_ctx_rows)
    nctx = n_ctx_rows // tm
    seg = lambda i: (i >= nctx).astype(jnp.int32)
    row = lambda a: pl.BlockSpec((tm, a.shape[1]), lambda i: (i, 0))
    full = lambda a: pl.BlockSpec(a.shape, lambda i: (0, 0), pipeline_mode=pl.Buffered(1))
    vec = lambda a: pl.BlockSpec((1, a.shape[1]), lambda i: (0, 0))
    return pl.pallas_call(
        functools.partial(_merge_kernel, alpha),
        out_shape=jax.ShapeDtypeStruct((rows, d), F32),
        grid=(rows // tm,),
        scratch_shapes=[pltpu.VMEM((n_slabs, tm, LANES), F32)],
        in_specs=[pl.BlockSpec((2, ysa.shape[1], tm, LANES), lambda i: (0, 0, i, 0)),
                  row(g), row(bonus), vec(gn_w), vec(gn_b), full(ones_bd),
                  pl.BlockSpec((n_batch, n_slabs, tm // n_batch, LANES), lambda i: (0, 0, i, 0)),
                  pl.BlockSpec((2, tm, u.shape[1]), lambda i: (0, i, 0)), row(u), vec(d_skip), full(glu_w), vec(glu_b),
                  row(pg), row(x_all),
                  pl.BlockSpec((1, SUBLANES, d), lambda i: (seg(i), 0, 2)),
                  full(proj_a), full(proj_b), full(proj_c), full(w_out), vec(ln_g), vec(ln_b)],
        out_specs=pl.BlockSpec((tm, d), lambda i: (i, 0)),
        compiler_params=_cparams("parallel"),
        name="merge_ln",
    )(ysa, g, bonus, gn_w, gn_b, ones_bd, yb, ysc, u, d_skip, glu_w, glu_b,
      pg, x_all, modtab, proj_a, proj_b, proj_c, w_out, ln_g, ln_b)


def _mlp_kernel(alpha, x_ref, sh_ref, sc_ref, gt_ref, w1_ref, w2_ref, g_ref, b_ref, o_ref, h_ref, acc_ref,
                *split_ref):
    j = pl.program_id(1)

    @pl.when(j == 0)
    def _():
        h_ref[...] = _rows_add(_rows_mod(x_ref[...], 1.0 + sc_ref[0]), sh_ref[0]).astype(BF16)
        acc_ref[...] = jnp.zeros_like(acc_ref)

    a = jnp.maximum(jnp.dot(h_ref[...], w1_ref[...], preferred_element_type=F32), 0.0)
    acc_ref[...] += jnp.dot((a * a).astype(BF16), w2_ref[...], preferred_element_type=F32)

    @pl.when(j == pl.num_programs(1) - 1)
    def _():
        y = _layer_norm(alpha * x_ref[...] + _rows_mod(acc_ref[...], gt_ref[0]), g_ref[...], b_ref[...])
        if not split_ref:
            o_ref[...] = y
        else:
            rows_ref, = split_ref
            n_batch, tt, d = o_ref.shape
            for s in range(d // LANES):
                rows_ref[s] = y[:, s * LANES:(s + 1) * LANES]
            for b in range(n_batch):
                for s in range(d // LANES):
                    o_ref[b, :, s * LANES:(s + 1) * LANES] = rows_ref[s, pl.ds(b, tt, stride=n_batch), :]


def _mlp(x_mid, modtab, w1, w2, ln_g, ln_b, alpha, n_ctx_rows, final_batch=None):
    rows, d = x_mid.shape
    ff = w1.shape[1]
    tm = min(MLP_ROW_TILE, n_ctx_rows)
    tf = min(MLP_FF_TILE, ff)
    nctx = n_ctx_rows // tm
    first = nctx if final_batch else 0
    seg = lambda i: (i + first >= nctx).astype(jnp.int32)
    mod = lambda col: pl.BlockSpec((1, SUBLANES, d), lambda i, j: (seg(i), 0, col))
    vec = pl.BlockSpec((1, d), lambda i, j: (0, 0))
    scratch = [pltpu.VMEM((tm, d), BF16), pltpu.VMEM((tm, d), F32)]
    if final_batch:
        out_shape = jax.ShapeDtypeStruct((final_batch, (rows - n_ctx_rows) // final_batch, d), F32)
        out_spec = pl.BlockSpec((final_batch, tm // final_batch, d), lambda i, j: (0, i, 0))
        scratch.append(pltpu.VMEM((d // LANES, tm, LANES), F32))
    else:
        out_shape = jax.ShapeDtypeStruct((rows, d), F32)
        out_spec = pl.BlockSpec((tm, d), lambda i, j: (i, 0))
    return pl.pallas_call(
        functools.partial(_mlp_kernel, alpha),
        out_shape=out_shape,
        grid=(rows // tm - first, ff // tf),
        in_specs=[pl.BlockSpec((tm, d), lambda i, j: (i + first, 0)), mod(3), mod(4), mod(5),
                  pl.BlockSpec((d, tf), lambda i, j: (0, j)),
                  pl.BlockSpec((tf, d), lambda i, j: (j, 0)), vec, vec],
        out_specs=out_spec,
        scratch_shapes=scratch,
        compiler_params=_cparams("parallel", "arbitrary"),
        name="mlp_ln",
    )(x_mid, modtab, modtab, modtab, w1, w2, ln_g, ln_b)


def _block_diag_ones(n, blk):
    i = jnp.arange(n) // blk
    return (i[:, None] == i[None, :]).astype(BF16)


def _rope_tables(n_ctx, n_lat):
    pairs = HEAD_DIM // 4
    rows = n_lat // GRID_W
    row = jnp.repeat(jnp.arange(rows, dtype=F32), GRID_W)
    col = jnp.tile(jnp.arange(GRID_W, dtype=F32), rows)
    inv = ROPE_THETA ** (-jnp.arange(pairs, dtype=F32) / pairs)
    ang = jnp.stack([row, col], axis=-1)[:, :, None] * inv
    ang = jnp.broadcast_to(ang[:, :, None, :], (n_lat, 2, 2, pairs)).reshape(n_lat, HEAD_DIM)
    cos = jnp.concatenate([jnp.ones((n_ctx, HEAD_DIM), F32), jnp.cos(ang)], axis=0)
    sin = jnp.concatenate([jnp.zeros((n_ctx, HEAD_DIM), F32), jnp.sin(ang)], axis=0)
    first = (jnp.arange(HEAD_DIM) % (2 * pairs)) < pairs
    sin_a = jnp.where(first, -sin, 0.0)
    sin_b = jnp.where(first, 0.0, sin)
    two = lambda a: jnp.concatenate([a, a], axis=1)
    return two(cos), two(sin_a), two(sin_b)


def _block_diag(blocks):
    g, a, b = blocks.shape
    eye = jnp.eye(g, dtype=blocks.dtype)
    return (eye[:, None, :, None] * blocks[:, :, None, :]).reshape(g * a, g * b)


def kernel(x, c, ctx, c_ctx, mod_w, mod_b, w_in, rwkv_mu, rwkv_w0, rwkv_w_up, rwkv_a0, rwkv_a_up, rwkv_g_up, rwkv_k_k, rwkv_k_a, rwkv_r_k, rwkv_gn_w, rwkv_gn_b, attn_q_gain, attn_k_gain, ssm_a_re, ssm_a_im, ssm_log_dt, ssm_b_re, ssm_b_im, ssm_c_re, ssm_c_im, ssm_d, ssm_glu_w, ssm_glu_b, proj_a, proj_b, proj_c, w_out, ln1_g, ln1_b, ln2_g, ln2_b, mlp_w1, mlp_w2):
    n_batch, n_lat, d_model = x.shape
    n_ctx = ctx.shape[1]
    depth = mod_w.shape[0]
    assert n_batch == SUBLANES, "row layout packs the batch into one sublane tile"
    alpha = (2 * depth) ** 0.25
    n_ctx_rows = n_ctx * n_batch

    rwkv_w = rwkv_k_k.shape[1]
    q_w = proj_b.shape[1]
    kv_w = q_w // GQA_GROUP
    ssm_w = ssm_d.shape[1]
    rwkv_cols = rwkv_mu.shape[1]
    edges = [0, rwkv_cols, rwkv_cols + q_w, rwkv_cols + q_w + kv_w, rwkv_cols + q_w + 2 * kv_w,
             rwkv_cols + q_w + 2 * kv_w + ssm_w, w_in.shape[2]]
    slabbed = (False, True, True, True, False, False)
    splits = tuple(zip(edges[:-1], edges[1:], slabbed))

    x_all = _to_rows(ctx, x)

    cvec = jnp.concatenate([jnp.broadcast_to(c_ctx[None], (n_batch, d_model)), c], axis=0)
    modtab = _modulation(cvec, mod_w, mod_b).reshape(depth, 2, n_batch, 6 * d_model)

    ones_rwkv = _block_diag_ones(min(MXU_DIM, rwkv_w), HEAD_DIM)
    ones_pair = _block_diag_ones(LANES, HEAD_DIM)
    cos, sin_a, sin_b = (jnp.repeat(t, n_batch, axis=0) for t in _rope_tables(n_ctx, n_lat))
    rank = rwkv_w_up.shape[2]
    zpad = jnp.zeros((depth, 2, rank, rwkv_w), F32)

    for l in range(depth):
        mt = modtab[l]
        prm = dict(mu=rwkv_mu[l][None], w0=rwkv_w0[l][:, None], a0=rwkv_a0[l][:, None],
                   w_up=jnp.concatenate([rwkv_w_up[l], zpad[l]], axis=1).astype(BF16),
                   a_up=jnp.concatenate([zpad[l], rwkv_a_up[l]], axis=1).astype(BF16),
                   g_up=rwkv_g_up[l].astype(BF16), k_k=rwkv_k_k[l][None], k_a=rwkv_k_a[l][None],
                   r_k=rwkv_r_k[l][None], ones_bd=ones_rwkv)
        r, v, kk, g, bonus, lw, kd, bv, pq, pk, pv, pc, pg = _in_projection(
            x_all, mt, w_in[l].astype(BF16), splits, prm, n_ctx_rows)
        ysa = _rwkv_scan(r, v, kk, lw, kd, bv, n_batch, n_ctx)
        rwkv = (ysa, g, bonus, rwkv_gn_w[l][None], rwkv_gn_b[l][None], ones_rwkv)

        tile2 = lambda a: jnp.tile(a, 2)[None]
        qh, kh, vh = _attn_prepare(pq, pk, pv, cos, sin_a, sin_b, tile2(attn_q_gain[l]), tile2(attn_k_gain[l]),
                                   ones_pair, n_batch)
        yb = _flash_attention(qh, kh, vh, n_ctx)

        abar_re, abar_im, bf_re, bf_im = _s5_params(ssm_a_re[l], ssm_a_im[l], ssm_log_dt[l], ssm_b_re[l], ssm_b_im[l])
        n_g = ssm_a_re.shape[2]
        n_split = max(1, ssm_w // MXU_DIM)
        gs = n_g // n_split
        diag = lambda blocks: jnp.stack([_block_diag(blocks[h * gs:(h + 1) * gs]) for h in range(n_split)])
        bmat = lambda bf: jnp.stack([diag(bf[d].reshape(SSM_GROUP, n_g, SSM_STATE).transpose(1, 0, 2))
                                     for d in range(2)]).astype(BF16)
        cmat = lambda cc: diag(cc.transpose(0, 2, 1)).astype(BF16)
        ysc = _s5_scan(pc, abar_re, abar_im, bmat(bf_re), bmat(bf_im), cmat(ssm_c_re[l]), cmat(ssm_c_im[l]),
                       n_batch, n_ctx)
        s5 = (ysc, pc, ssm_d[l][None], ssm_glu_w[l].astype(BF16), ssm_glu_b[l][None])

        x_mid = _merge(rwkv, yb, s5, pg, x_all, mt, proj_a[l].astype(BF16), proj_b[l].astype(BF16),
                       proj_c[l].astype(BF16), w_out[l].astype(BF16), ln1_g[l][None], ln1_b[l][None],
                       alpha, n_ctx_rows)
        x_all = _mlp(x_mid, mt, mlp_w1[l].astype(BF16), mlp_w2[l].astype(BF16), ln2_g[l][None], ln2_b[l][None],
                     alpha, n_ctx_rows, final_batch=n_batch if l == depth - 1 else None)

    return x_all
```

```python
import functools
import math

import jax
import jax.numpy as jnp
from jax import lax
from jax.experimental import pallas as pl
from jax.experimental.pallas import tpu as pltpu

HEAD_DIM = 64
GRID_W = 64
ROPE_THETA = 10000.0
LN_EPS = 1e-5
RMS_EPS = 1e-6
GN_EPS = 64e-5
KK_EPS = 1e-24
LAMBDA_RE_MAX = -1e-4
GQA_GROUP = 4
SSM_GROUP = 16
SSM_STATE = 64

LANES = 128
SUBLANES = 8
MXU_DIM = 256
VMEM_LIMIT = 56 * 1024 * 1024

ROW_TILE = 256
MERGE_ROW_TILE = 512
LAYOUT_TT = 64
MLP_ROW_TILE = 1024
MLP_FF_TILE = 2048
SCAN_CHUNK = 64
SCAN_PAIRS = 4
SSM_CHUNK = 64
SSM_LANE_TILE = 512
ATTN_TQ = 256
ATTN_SUB = 256
ATTN_AHEAD = 1
PREP_TT = 128
VT_ROWS = 80

F32 = jnp.float32
BF16 = jnp.bfloat16


def _cparams(*sem):
    return pltpu.CompilerParams(dimension_semantics=sem, vmem_limit_bytes=VMEM_LIMIT)


def _mm(a, b):
    return jnp.dot(a.astype(BF16), b.astype(BF16), preferred_element_type=F32)


def _mm_nt(a, b):
    return lax.dot_general(a.astype(BF16), b.astype(BF16), (((1,), (1,)), ((), ())),
                           preferred_element_type=F32)


def _mm_tn(a, b):
    return lax.dot_general(a.astype(BF16), b.astype(BF16), (((0,), (0,)), ((), ())),
                           preferred_element_type=F32)


def _segsum(x, ones_bd):
    w = ones_bd.shape[0]
    hi = x.astype(BF16)
    lo = (x - hi.astype(F32)).astype(BF16)
    parts = [jnp.dot(hi[:, o:o + w], ones_bd, preferred_element_type=F32)
             + jnp.dot(lo[:, o:o + w], ones_bd, preferred_element_type=F32)
             for o in range(0, x.shape[1], w)]
    return parts[0] if len(parts) == 1 else jnp.concatenate(parts, axis=1)


def _sigmoid(x):
    return 1.0 / (1.0 + jnp.exp(-x))


def _softplus(x):
    return jnp.maximum(x, 0.0) + jnp.log(1.0 + jnp.exp(-jnp.abs(x)))


def _layer_norm(z, g, b):
    mu = jnp.mean(z, axis=-1, keepdims=True)
    zc = z - mu
    var = jnp.mean(zc * zc, axis=-1, keepdims=True)
    return zc * lax.rsqrt(var + LN_EPS) * g + b


def _rows_mod(x, m):
    tm, n = x.shape
    return (x.reshape(tm // SUBLANES, SUBLANES, n) * m[None]).reshape(tm, n)


def _rows_add(x, m):
    tm, n = x.shape
    return (x.reshape(tm // SUBLANES, SUBLANES, n) + m[None]).reshape(tm, n)


def _to_rows_kernel(n_ctx_tiles, ctx_ref, x_ref, o_ref, rows_ref):
    i = pl.program_id(0)
    n_batch, tt, d = x_ref.shape

    def interleave(src_ref):
        for b in range(n_batch):
            for s in range(d // LANES):
                rows_ref[s, pl.ds(b, tt, stride=n_batch), :] = src_ref[b, :, s * LANES:(s + 1) * LANES]
        for s in range(d // LANES):
            o_ref[:, s * LANES:(s + 1) * LANES] = rows_ref[s]

    @pl.when(i < n_ctx_tiles)
    def _():
        interleave(ctx_ref)

    @pl.when(i >= n_ctx_tiles)
    def _():
        interleave(x_ref)


def _to_rows(ctx, x):
    n_batch, n_ctx, d = ctx.shape
    n_lat = x.shape[1]
    tt = math.gcd(LAYOUT_TT, n_ctx)
    n_ctx_tiles = n_ctx // tt
    return pl.pallas_call(
        functools.partial(_to_rows_kernel, n_ctx_tiles),
        out_shape=jax.ShapeDtypeStruct(((n_ctx + n_lat) * n_batch, d), F32),
        grid=((n_ctx + n_lat) // tt,),
        in_specs=[pl.BlockSpec((n_batch, tt, d), lambda i: (0, jnp.minimum(i, n_ctx_tiles - 1), 0)),
                  pl.BlockSpec((n_batch, tt, d), lambda i: (0, jnp.maximum(i - n_ctx_tiles, 0), 0))],
        out_specs=pl.BlockSpec((tt * n_batch, d), lambda i: (i, 0)),
        scratch_shapes=[pltpu.VMEM((d // LANES, tt * n_batch, LANES), F32)],
        compiler_params=_cparams("parallel"),
        name="to_rows",
    )(ctx, x)


def _mod_kernel(c_ref, w_ref, b_ref, o_ref):
    c = c_ref[...]
    s = c * _sigmoid(c)
    o_ref[0] = jnp.dot(s, w_ref[0], preferred_element_type=F32,
                       precision=lax.Precision.HIGHEST) + b_ref[0]


def _modulation(cvec, mod_w, mod_b):
    n_layers, d, n6 = mod_w.shape
    tn = 1024 if n6 % 1024 == 0 else n6
    return pl.pallas_call(
        _mod_kernel,
        out_shape=jax.ShapeDtypeStruct((n_layers, cvec.shape[0], n6), F32),
        grid=(n_layers, n6 // tn),
        in_specs=[pl.BlockSpec((cvec.shape[0], d), lambda l, j: (0, 0)),
                  pl.BlockSpec((1, d, tn), lambda l, j: (l, 0, j)),
                  pl.BlockSpec((1, 1, tn), lambda l, j: (l, 0, j))],
        out_specs=pl.BlockSpec((1, cvec.shape[0], tn), lambda l, j: (l, 0, j)),
        compiler_params=_cparams("parallel", "parallel"),
        name="modulation",
    )(cvec, mod_w, mod_b.reshape(n_layers, 1, n6))


def _inproj_kernel(splits, n_ctx_rows, n_rows, x_ref, xp_ref, xn_ref, sh_ref, sc_ref, w_ref,
                   mu_ref, w0_ref, wup_ref, a0_ref, aup_ref, gup_ref, kk_ref, ka_ref, rk_ref, ones_ref, *o_refs):
    i = pl.program_id(0)
    tm = x_ref.shape[0]
    r0 = i * tm
    seg_start = jnp.logical_or(r0 == 0, r0 == n_ctx_rows)
    seg_end = jnp.logical_or(r0 + tm == n_ctx_rows, r0 + tm == n_rows)
    scale, shift = 1.0 + sc_ref[0], sh_ref[0]
    mod = lambda x: _rows_add(_rows_mod(x, scale), shift)
    xm = mod(x_ref[...])
    x_ext = jnp.concatenate([jnp.where(seg_start, 0.0, mod(xp_ref[...])), xm,
                             jnp.where(seg_end, 0.0, mod(xn_ref[...]))], axis=0).astype(BF16)
    a, b, _ = splits[0]
    n_rwkv = 8
    _rwkv_prep(jnp.dot(x_ext, w_ref[:, a:b], preferred_element_type=F32), mu_ref[...], w0_ref, wup_ref,
               a0_ref, aup_ref, gup_ref[...], kk_ref[...], ka_ref[...], rk_ref[...], ones_ref[...],
               o_refs[:n_rwkv])
    xm = xm.astype(BF16)
    for (a, b, slabbed), o_ref in zip(splits[1:], o_refs[n_rwkv:]):
        y = jnp.dot(xm, w_ref[:, a:b], preferred_element_type=F32)
        if slabbed:
            for j in range((b - a) // LANES):
                o_ref[j] = y[:, j * LANES:(j + 1) * LANES]
        else:
            o_ref[...] = y.astype(o_ref.dtype)


def _in_projection(x_all, modtab, w_bf16, splits, prm, n_ctx_rows):
    rows, d = x_all.shape
    width = prm["k_k"].shape[-1]
    tm = ROW_TILE
    nctx = n_ctx_rows // tm
    hb = tm // SUBLANES
    last = rows // SUBLANES - 1
    seg = lambda i: (i >= nctx).astype(jnp.int32)
    n_slabs = width // LANES
    slab = pl.BlockSpec((n_slabs, tm, LANES), lambda i: (0, i, 0))
    slab2 = pl.BlockSpec((2, n_slabs, tm, LANES), lambda i: (0, 0, i, 0))
    row = pl.BlockSpec((tm, width), lambda i: (i, 0))
    flat = jax.ShapeDtypeStruct((rows, width), BF16)
    one = jax.ShapeDtypeStruct((n_slabs, rows, LANES), F32)
    two = jax.ShapeDtypeStruct((2, n_slabs, rows, LANES), F32)
    shapes = [one, one, one, flat, flat, two, two, two]
    specs = [slab] * 3 + [row] * 2 + [slab2] * 3
    for a, b, slabbed in splits[1:]:
        if slabbed:
            shapes.append(jax.ShapeDtypeStruct(((b - a) // LANES, rows, LANES), F32))
            specs.append(pl.BlockSpec(((b - a) // LANES, tm, LANES), lambda i: (0, i, 0)))
        else:
            shapes.append(jax.ShapeDtypeStruct((rows, b - a), BF16))
            specs.append(pl.BlockSpec((tm, b - a), lambda i: (i, 0)))
    consts = [prm["mu"], prm["w0"], prm["w_up"], prm["a0"], prm["a_up"], prm["g_up"],
              prm["k_k"], prm["k_a"], prm["r_k"], prm["ones_bd"]]
    full = lambda a: pl.BlockSpec(a.shape, lambda i: (0,) * a.ndim)
    return pl.pallas_call(
        functools.partial(_inproj_kernel, splits, n_ctx_rows, rows),
        out_shape=shapes,
        grid=(rows // tm,),
        in_specs=[pl.BlockSpec((tm, d), lambda i: (i, 0)),
                  pl.BlockSpec((SUBLANES, d), lambda i: (jnp.maximum(i * hb - 1, 0), 0)),
                  pl.BlockSpec((SUBLANES, d), lambda i: (jnp.minimum((i + 1) * hb, last), 0)),
                  pl.BlockSpec((1, SUBLANES, d), lambda i: (seg(i), 0, 0)),
                  pl.BlockSpec((1, SUBLANES, d), lambda i: (seg(i), 0, 1)),
                  pl.BlockSpec(w_bf16.shape, lambda i: (0, 0), pipeline_mode=pl.Buffered(1))]
                 + [full(a) for a in consts],
        out_specs=specs,
        compiler_params=_cparams("parallel"),
        name="in_projection",
    )(x_all, x_all, x_all, modtab, modtab, w_bf16, *consts)


def _rwkv_prep(p_ext, mu, w0_ref, wup_ref, a0_ref, aup_ref, g_up, k_k, k_a, r_k, ones_bd, outs):
    r_out, v_out, kkn_out, g_out, bonus_out, lw_out, kd_out, bv_out = outs
    tm = p_ext.shape[0] - 2 * SUBLANES
    p = p_ext[SUBLANES:SUBLANES + tm]
    p = p + mu * (0.5 * (p_ext[0:tm] + p_ext[2 * SUBLANES:]) - p)

    w = k_k.shape[-1]
    r, k, v = p[:, 0:w], p[:, w:2 * w], p[:, 2 * w:3 * w]
    wa = p[:, 3 * w:3 * w + LANES]
    gd = p[:, 3 * w + LANES:3 * w + 2 * LANES]

    def put(o_ref, y, *lead):
        for j in range(w // LANES):
            o_ref[lead + (j,)] = y[:, j * LANES:(j + 1) * LANES]

    g_out[...] = _mm(_sigmoid(gd), g_up).astype(g_out.dtype)
    kk = k * k_k
    kk = kk * lax.rsqrt(jnp.maximum(_segsum(kk * kk, ones_bd), KK_EPS))
    put(r_out, r)
    put(v_out, v)
    put(kkn_out, kk)
    tanh_wa = jnp.tanh(wa)
    rk_sum = jnp.zeros_like(r)
    for d in range(2):
        w_pre = w0_ref[d] + _mm(tanh_wa, wup_ref[d])
        put(lw_out, -jnp.exp(-_softplus(-w_pre) - 0.5), d)
        a = _sigmoid(a0_ref[d] + _mm(wa, aup_ref[d]))
        k_dir = k * (1.0 + (a - 1.0) * k_a)
        put(kd_out, k_dir, d)
        put(bv_out, kk * a, d)
        rk_sum = rk_sum + r * k_dir
    bonus_out[...] = (_segsum(rk_sum * r_k, ones_bd) * v).astype(bonus_out.dtype)


def _scan_block(d, c, n_ctx_blk, n_blk):
    bwd = jnp.where(c < n_ctx_blk, n_ctx_blk - 1 - c, n_ctx_blk + (n_blk - 1 - c))
    return jnp.where(d == 0, c, bwd)


def _tri_inverse(a_list, ri, ci, n):
    eye = (ri == ci).astype(F32)
    same = (ri // SUBLANES) == (ci // SUBLANES)
    xs = [-jnp.where(same, a, 0.0) for a in a_list]
    x2 = [_mm(x, x) for x in xs]
    x4 = [_mm(x, x) for x in x2]
    ts = [eye + x for x in xs]
    ts = [t + _mm(t, x) for t, x in zip(ts, x2)]
    ts = [t + _mm(t, x) for t, x in zip(ts, x4)]
    size = SUBLANES
    while size < n:
        pick = jnp.logical_and((ri // (2 * size)) == (ci // (2 * size)), (ri // size) != (ci // size))
        lows = [_mm(jnp.where(pick, a, 0.0), t) for a, t in zip(a_list, ts)]
        ts = [t - _mm(t, low) for t, low in zip(ts, lows)]
        size *= 2
    return ts


def _rwkv_scan_kernel(n_batch, r_ref, v_ref, kk_ref, lw_ref, kd_ref, bv_ref, y_ref, state_ref):
    d = pl.program_id(0)
    c = pl.program_id(2)
    n = r_ref.shape[1] // n_batch
    m = 2 * n
    chains = [(p, b) for p in range(r_ref.shape[0]) for b in range(n_batch)]
    batches = range(len(chains))

    @pl.when(c == 0)
    def _():
        state_ref[...] = jnp.zeros_like(state_ref)

    sign = 1 - 2 * d
    ri = lax.broadcasted_iota(jnp.int32, (m, m), 0)
    ci = lax.broadcasted_iota(jnp.int32, (m, m), 1)
    same_head = (ri // n) == (ci // n)
    order = (ri - ci) * sign
    strict = jnp.logical_and(same_head, order > 0)
    incl = jnp.logical_and(same_head, order >= 0)
    ti = lax.broadcasted_iota(jnp.int32, (n, n), 0)
    si = lax.broadcasted_iota(jnp.int32, (n, n), 1)
    tri = ((ti - si) * sign >= 0).astype(BF16)
    head0 = lax.broadcasted_iota(jnp.int32, (n, LANES), 1) < HEAD_DIM
    blockdiag = (lax.broadcasted_iota(jnp.int32, (LANES, LANES), 0) // HEAD_DIM
                 == lax.broadcasted_iota(jnp.int32, (LANES, LANES), 1) // HEAD_DIM)

    def stack(x):
        return jnp.concatenate([jnp.where(head0, x, 0.0), jnp.where(head0, 0.0, x)], axis=0)

    def shared(ref, i):
        p, b = chains[i]
        return ref[p, pl.ds(b, n, stride=n_batch), :]

    def split(ref, i):
        p, b = chains[i]
        return ref[0, p, pl.ds(b, n, stride=n_batch), :]

    duos = range(0, len(chains), 2)
    cat = lambda xs, i: jnp.concatenate([xs[i], xs[i + 1]], axis=1)
    uncat = lambda xs: [x[:, o:o + LANES] for x in xs for o in (0, LANES)]

    lw = [split(lw_ref, b) for b in batches]
    hi = [x.astype(BF16) for x in lw]
    lo = [(x - h.astype(F32)).astype(BF16) for x, h in zip(lw, hi)]
    cum = uncat([jnp.dot(tri, cat(hi, i), preferred_element_type=F32)
                 + jnp.dot(tri, cat(lo, i), preferred_element_type=F32) for i in duos])
    e_cum = [jnp.exp(x) for x in cum]
    e_neg = [jnp.exp(-x) for x in cum]
    kap = [shared(kk_ref, b) * jnp.exp(cum[b] - lw[b]) for b in batches]
    rt = [shared(r_ref, b) * e_cum[b] for b in batches]
    bt = [split(bv_ref, b) * e_neg[b] for b in batches]
    kt = [split(kd_ref, b) * e_neg[b] for b in batches]
    v = [shared(v_ref, b) for b in batches]
    total = [jnp.where(d == 0, e[n - 1:n], e[0:1]) for e in e_cum]
    state = [state_ref[b] for b in batches]

    scores = [_mm_nt(jnp.concatenate([stack(kap[b]), stack(rt[b])], axis=0),
                     jnp.concatenate([stack(bt[b]), stack(kt[b])], axis=0)) for b in batches]
    a_b = [jnp.where(strict, s[0:m, 0:m], 0.0) for s in scores]
    a_k = [jnp.where(strict, s[0:m, m:2 * m], 0.0) for s in scores]
    a_r = [jnp.concatenate([jnp.where(incl, s[m:2 * m, m:2 * m], 0.0),
                            jnp.where(incl, -s[m:2 * m, 0:m], 0.0)], axis=1) for s in scores]
    g0 = [_mm_nt(jnp.concatenate([kap[b], rt[b]], axis=0), state[b]) for b in batches]
    t_inv = _tri_inverse(a_b, ri, ci, n)
    vs = [stack(x) for x in v]
    rhs = [stack(g0[b][0:n]) + _mm(a_k[b], vs[b]) for b in batches]
    us = [_mm(t_inv[b], rhs[b]) for b in batches]
    ys = [stack(g0[b][n:m]) + _mm(a_r[b], jnp.concatenate([vs[b], us[b]], axis=0)) for b in batches]
    for i in batches:
        p, b = chains[i]
        y_ref[0, p, pl.ds(b, n, stride=n_batch), :] = ys[i][0:n] + ys[i][n:m]
    ds = [_mm_tn(jnp.concatenate([-(us[b][0:n] + us[b][n:m]), v[b]], axis=0),
                 jnp.concatenate([bt[b], kt[b]], axis=0)) for b in batches]
    for b in batches:
        state_ref[b] = (state[b] + jnp.where(blockdiag, ds[b], 0.0)) * total[b]


def _rwkv_scan(r, v, kk, lw, kd, bv, n_batch, n_ctx):
    n_slabs, rows, _ = r.shape
    tm = SCAN_CHUNK * n_batch
    pp = min(SCAN_PAIRS, n_slabs)
    n_blk, n_ctx_blk = rows // tm, n_ctx * n_batch // tm
    blk = lambda d, c: _scan_block(d, c, n_ctx_blk, n_blk)
    shared = pl.BlockSpec((pp, tm, LANES), lambda d, p, c: (p, blk(d, c), 0))
    split = pl.BlockSpec((1, pp, tm, LANES), lambda d, p, c: (d, p, blk(d, c), 0))
    return pl.pallas_call(
        functools.partial(_rwkv_scan_kernel, n_batch),
        out_shape=jax.ShapeDtypeStruct((2, n_slabs, rows, LANES), F32),
        grid=(2, n_slabs // pp, n_blk),
        in_specs=[shared, shared, shared, split, split, split],
        out_specs=split,
        scratch_shapes=[pltpu.VMEM((pp * n_batch, LANES, LANES), F32)],
        compiler_params=_cparams("parallel", "parallel", "arbitrary"),
        name="rwkv_scan",
    )(r, v, kk, lw, kd, bv)


def _rwkv_readout(y_ref, g, bonus, gn_w, gn_b, ones_bd):
    y = jnp.concatenate([y_ref[0, j] + y_ref[1, j] for j in range(y_ref.shape[1])], axis=1)
    inv = 1.0 / HEAD_DIM
    mean = _segsum(y, ones_bd) * inv
    yc = y - mean
    var = _segsum(yc * yc, ones_bd) * inv
    yn = yc * lax.rsqrt(var + GN_EPS) * gn_w + gn_b
    return (yn + bonus) * g


def _norm_rope(x, gain, cos, sin_a, sin_b, ones_bd):
    ms = _segsum(x * x, ones_bd) * (1.0 / HEAD_DIM)
    xn = x * lax.rsqrt(ms + RMS_EPS) * gain
    return (xn * cos + pltpu.roll(xn, LANES - HEAD_DIM // 4, 1) * sin_a
            + pltpu.roll(xn, HEAD_DIM // 4, 1) * sin_b)


def _attn_prep_kernel(scale, n_batch, q_ref, k_ref, v_ref, cos_ref, sa_ref, sb_ref, qg_ref, kg_ref, ones_ref,
                      qo_ref, ko_ref, vo_ref, scr):
    cos, sin_a, sin_b = cos_ref[...], sa_ref[...], sb_ref[...]
    ones_bd = ones_ref[...]
    tm = cos.shape[0]
    tt = tm // n_batch
    first = lax.broadcasted_iota(jnp.int32, (tm, LANES), 1) < HEAD_DIM
    n_q, n_k = q_ref.shape[0], k_ref.shape[0]
    for j in range(n_q):
        scr[j] = _norm_rope(q_ref[j], qg_ref[...], cos, sin_a, sin_b, ones_bd) * scale
    for j in range(n_k):
        y = _norm_rope(k_ref[j], kg_ref[...], cos, sin_a, sin_b, ones_bd)
        swapped = pltpu.roll(y, HEAD_DIM, 1)
        scr[n_q + 2 * j] = jnp.where(first, y, swapped)
        scr[n_q + 2 * j + 1] = jnp.where(first, swapped, y)
    ones_rows = jnp.ones((vo_ref.shape[3] - HEAD_DIM, LANES), BF16)
    for b in range(n_batch):
        rows = pl.ds(b, tt, stride=n_batch)
        for j in range(n_q):
            qo_ref[b, j] = scr[j, rows, :].astype(BF16)
        for h in range(2 * n_k):
            ko_ref[b, h] = scr[n_q + h, rows, :].astype(BF16)
        for j in range(n_k):
            vb = v_ref[j, rows, :]
            for g in range(tt // LANES):
                vt = vb[g * LANES:(g + 1) * LANES].T.astype(BF16)
                for h in range(2):
                    vo_ref[b, 2 * j + h, g, 0:HEAD_DIM] = vt[h * HEAD_DIM:(h + 1) * HEAD_DIM]
                    vo_ref[b, 2 * j + h, g, HEAD_DIM:] = ones_rows


def _attn_prepare(pq, pk, pv, cos, sin_a, sin_b, q_gain, k_gain, ones_pair, n_batch):
    q_slabs, rows, _ = pq.shape
    k_slabs = pk.shape[0]
    tt_all = rows // n_batch
    tt = PREP_TT
    tm = tt * n_batch
    kh = 2 * k_slabs
    tab = pl.BlockSpec((tm, LANES), lambda i: (i, 0))
    vec = pl.BlockSpec((1, LANES), lambda i: (0, 0))
    return pl.pallas_call(
        functools.partial(_attn_prep_kernel, HEAD_DIM ** -0.5 * math.log2(math.e), n_batch),
        out_shape=[jax.ShapeDtypeStruct((n_batch, q_slabs, tt_all, LANES), BF16),
                   jax.ShapeDtypeStruct((n_batch, kh, tt_all, LANES), BF16),
                   jax.ShapeDtypeStruct((n_batch, kh, tt_all // LANES, VT_ROWS, LANES), BF16)],
        grid=(tt_all // tt,),
        in_specs=[pl.BlockSpec((q_slabs, tm, LANES), lambda i: (0, i, 0)),
                  pl.BlockSpec((k_slabs, tm, LANES), lambda i: (0, i, 0)),
                  pl.BlockSpec((k_slabs, tm, LANES), lambda i: (0, i, 0)),
                  tab, tab, tab, vec, vec,
                  pl.BlockSpec(ones_pair.shape, lambda i: (0, 0))],
        out_specs=[pl.BlockSpec((n_batch, q_slabs, tt, LANES), lambda i: (0, 0, i, 0)),
                   pl.BlockSpec((n_batch, kh, tt, LANES), lambda i: (0, 0, i, 0)),
                   pl.BlockSpec((n_batch, kh, tt // LANES, VT_ROWS, LANES), lambda i: (0, 0, i, 0, 0))],
        scratch_shapes=[pltpu.VMEM((q_slabs + kh, tm, LANES), F32)],
        compiler_params=_cparams("parallel"),
        name="attn_prepare",
    )(pq, pk, pv, cos, sin_a, sin_b, q_gain, k_gain, ones_pair)


def _flash_kernel(n_ctx, sub_ctx, sub_lat, q_ref, k_ref, vt_ref, o_ref):
    qi = pl.program_id(2)
    n_pairs, tq = q_ref.shape[1], q_ref.shape[2]
    n_keys = k_ref.shape[2]
    first = lax.broadcasted_iota(jnp.int32, (tq, LANES), 1) < HEAD_DIM
    zero = jnp.zeros((tq, LANES), BF16)
    q = jnp.concatenate([jnp.where(first if h == 0 else jnp.logical_not(first), q_ref[0, p], zero)
                         for p in range(n_pairs) for h in range(2)], axis=0)
    q_t = q.astype(F32).T.astype(BF16)

    def scores(blk):
        start, size = blk
        return jnp.dot(k_ref[0, 0, start:start + size, :], q_t, preferred_element_type=F32)

    def weighted_values(p, blk):
        start, size = blk
        vt = jnp.concatenate([vt_ref[0, 0, start // LANES + g] for g in range(size // LANES)], axis=1)
        return jnp.dot(vt, p, preferred_element_type=F32)

    def attend(blocks):
        ahead = [scores(b) for b in blocks[:ATTN_AHEAD]]
        m_prev = acc = pending = None
        for i, blk in enumerate(blocks):
            s = ahead.pop(0)
            if i + ATTN_AHEAD < len(blocks):
                ahead.append(scores(blocks[i + ATTN_AHEAD]))
            if pending is not None:
                alpha, p, pblk = pending
                pv = weighted_values(p, pblk)
                acc = pv if acc is None else alpha * acc + pv
            m_new = jnp.max(s, axis=0, keepdims=True)
            alpha = None
            if m_prev is not None:
                m_new = jnp.maximum(m_prev, m_new)
                alpha = jnp.exp2(m_prev - m_new)
            pending = (alpha, jnp.exp2(s - m_new).astype(BF16), blk)
            m_prev = m_new
        alpha, p, pblk = pending
        pv = weighted_values(p, pblk)
        acc = pv if acc is None else alpha * acc + pv
        out_t = acc[0:HEAD_DIM] / acc[HEAD_DIM:HEAD_DIM + 1]
        for p in range(n_pairs):
            pair_t = jnp.concatenate([out_t[:, (2 * p) * tq:(2 * p + 1) * tq],
                                      out_t[:, (2 * p + 1) * tq:(2 * p + 2) * tq]], axis=0)
            o_ref[0, p] = pair_t.T.astype(o_ref.dtype)

    is_ctx = qi * tq < n_ctx
    ctx_blocks = [(s0, sub_ctx) for s0 in range(0, n_ctx, sub_ctx)]
    lat_blocks = [(s0, sub_lat) for s0 in range(n_ctx, n_keys, sub_lat)]

    @pl.when(is_ctx)
    def _():
        attend(ctx_blocks)

    @pl.when(jnp.logical_not(is_ctx))
    def _():
        attend(ctx_blocks + lat_blocks)


def _flash_attention(qh, kh, vt, n_ctx):
    n_batch, q_slabs, tt_all, _ = qh.shape
    n_kvh = kh.shape[1]
    n_pairs = q_slabs // n_kvh
    tq = min(ATTN_TQ, n_ctx)
    sub_ctx = min(ATTN_SUB, n_ctx)
    sub_lat = min(ATTN_SUB, tt_all - n_ctx)
    assert n_ctx % tq == 0 and n_ctx % sub_ctx == 0 and (tt_all - n_ctx) % sub_lat == 0
    assert sub_ctx % LANES == 0 and sub_lat % LANES == 0 and tq % LANES == 0
    return pl.pallas_call(
        functools.partial(_flash_kernel, n_ctx, sub_ctx, sub_lat),
        out_shape=jax.ShapeDtypeStruct((n_batch, q_slabs, tt_all, LANES), BF16),
        grid=(n_batch, n_kvh, tt_all // tq),
        in_specs=[pl.BlockSpec((1, n_pairs, tq, LANES), lambda b, h, qi: (b, h, qi, 0)),
                  pl.BlockSpec((1, 1, tt_all, LANES), lambda b, h, qi: (b, h, 0, 0)),
                  pl.BlockSpec((1, 1, tt_all // LANES, VT_ROWS, LANES), lambda b, h, qi: (b, h, 0, 0, 0))],
        out_specs=pl.BlockSpec((1, n_pairs, tq, LANES), lambda b, h, qi: (b, h, qi, 0)),
        compiler_params=_cparams("parallel", "parallel", "parallel"),
        name="flash_attention",
    )(qh, kh, vt)


def _s5_param_kernel(are_ref, aim_ref, ldt_ref, bre_ref, bim_ref, abar_re, abar_im, bf_re, bf_im):
    lam_re = jnp.minimum(are_ref[0], LAMBDA_RE_MAX)
    lam_im = aim_ref[0]
    dt = jnp.exp(ldt_ref[0])
    mag = jnp.exp(lam_re * dt)
    ar, ai = mag * jnp.cos(lam_im * dt), mag * jnp.sin(lam_im * dt)
    nr, ni = ar - 1.0, ai
    den = lam_re * lam_re + lam_im * lam_im
    cr = (nr * lam_re + ni * lam_im) / den
    cim = (ni * lam_re - nr * lam_im) / den
    abar_re[0], abar_im[0] = ar, ai
    bre, bim = bre_ref[...], bim_ref[...]
    bf_re[0] = cr * bre - cim * bim
    bf_im[0] = cr * bim + cim * bre


def _s5_params(a_re, a_im, log_dt, b_re, b_im):
    _, n_g, n_s = a_re.shape
    n_i = b_re.shape[-1]
    gn = n_g * n_s
    flat = lambda a: a.reshape(2, 1, gn)
    dt = jnp.broadcast_to(log_dt[:, :, None], (2, n_g, n_s))
    bt = lambda b: b.reshape(gn, n_i).T
    vec = pl.BlockSpec((1, 1, gn), lambda d: (d, 0, 0))
    mat = pl.BlockSpec((n_i, gn), lambda d: (0, 0))
    omat = pl.BlockSpec((1, n_i, gn), lambda d: (d, 0, 0))
    return pl.pallas_call(
        _s5_param_kernel,
        out_shape=[jax.ShapeDtypeStruct((2, 1, gn), F32)] * 2 + [jax.ShapeDtypeStruct((2, n_i, gn), F32)] * 2,
        grid=(2,),
        in_specs=[vec, vec, vec, mat, mat],
        out_specs=[vec, vec, omat, omat],
        compiler_params=_cparams("parallel"),
        name="s5_params",
    )(flat(a_re), flat(a_im), flat(dt), bt(b_re), bt(b_im))


def _s5_scan_kernel(u_ref, ar_ref, ai_ref, bre_ref, bim_ref, cre_ref, cim_ref, y_ref,
                    xr_ref, xi_ref, hr_ref, hi_ref):
    d = pl.program_id(0)
    c = pl.program_id(1)
    steps = u_ref.shape[0] // SUBLANES

    @pl.when(c == 0)
    def _():
        hr_ref[...] = jnp.zeros_like(hr_ref)
        hi_ref[...] = jnp.zeros_like(hi_ref)

    n_split, kw, sw = bre_ref.shape[1], bre_ref.shape[2], bre_ref.shape[3]
    lt = min(SSM_LANE_TILE, sw)
    tiles = [(h, o) for h in range(n_split) for o in range(0, sw, lt)]
    u = u_ref[...].astype(BF16)

    def drive(tile):
        h, o = tile
        uh = u[:, h * kw:(h + 1) * kw]
        sl = slice(h * sw + o, h * sw + o + lt)
        xr_ref[:, sl] = jnp.dot(uh, bre_ref[0, h, :, o:o + lt], preferred_element_type=F32)
        xi_ref[:, sl] = jnp.dot(uh, bim_ref[0, h, :, o:o + lt], preferred_element_type=F32)

    def scan(tile, reverse):
        h, o = tile
        sl = slice(h * sw + o, h * sw + o + lt)
        ar = jnp.broadcast_to(ar_ref[0, :, sl], (SUBLANES, lt))
        ai = jnp.broadcast_to(ai_ref[0, :, sl], (SUBLANES, lt))
        hr, hi = hr_ref[:, sl], hi_ref[:, sl]
        for i in range(steps):
            t = steps - 1 - i if reverse else i
            rows = slice(t * SUBLANES, (t + 1) * SUBLANES)
            hr, hi = ar * hr - ai * hi + xr_ref[rows, sl], ar * hi + ai * hr + xi_ref[rows, sl]
            xr_ref[rows, sl] = hr
            xi_ref[rows, sl] = hi
        hr_ref[:, sl] = hr
        hi_ref[:, sl] = hi

    def readout(tile):
        h, o = tile
        sl = slice(h * sw + o, h * sw + o + lt)
        return (jnp.dot(xr_ref[:, sl].astype(BF16), cre_ref[h, o:o + lt, :], preferred_element_type=F32)
                - jnp.dot(xi_ref[:, sl].astype(BF16), cim_ref[h, o:o + lt, :], preferred_element_type=F32))

    def run(reverse):
        y = [None] * n_split
        drive(tiles[0])
        for n, tile in enumerate(tiles):
            if n + 1 < len(tiles):
                drive(tiles[n + 1])
            if n > 0:
                ph = tiles[n - 1][0]
                part = readout(tiles[n - 1])
                y[ph] = part if y[ph] is None else y[ph] + part
            scan(tile, reverse)
        ph = tiles[-1][0]
        part = readout(tiles[-1])
        y[ph] = part if y[ph] is None else y[ph] + part
        for h in range(n_split):
            y_ref[0, :, h * kw:(h + 1) * kw] = y[h].astype(y_ref.dtype)

    @pl.when(d == 0)
    def _():
        run(False)

    @pl.when(d == 1)
    def _():
        run(True)


def _s5_scan(u, abar_re, abar_im, b_re, b_im, c_re, c_im, n_batch, n_ctx):
    rows, width = u.shape
    gn = abar_re.shape[-1]
    tm = SSM_CHUNK * n_batch
    n_blk, n_ctx_blk = rows // tm, n_ctx * n_batch // tm
    blk = lambda d, c: _scan_block(d, c, n_ctx_blk, n_blk)
    vec = pl.BlockSpec((1, 1, gn), lambda d, c: (d, 0, 0))
    bmat = pl.BlockSpec((1,) + b_re.shape[1:], lambda d, c: (d, 0, 0, 0))
    cmat = pl.BlockSpec(c_re.shape, lambda d, c: (0, 0, 0))
    return pl.pallas_call(
        _s5_scan_kernel,
        out_shape=jax.ShapeDtypeStruct((2, rows, width), BF16),
        grid=(2, n_blk),
        in_specs=[pl.BlockSpec((tm, width), lambda d, c: (blk(d, c), 0)), vec, vec, bmat, bmat, cmat, cmat],
        out_specs=pl.BlockSpec((1, tm, width), lambda d, c: (d, blk(d, c), 0)),
        scratch_shapes=[pltpu.VMEM((tm, gn), F32), pltpu.VMEM((tm, gn), F32),
                        pltpu.VMEM((SUBLANES, gn), F32), pltpu.VMEM((SUBLANES, gn), F32)],
        compiler_params=_cparams("parallel", "arbitrary"),
        name="s5_scan",
    )(u, abar_re, abar_im, b_re, b_im, c_re, c_im)


def _gelu_tanh(x):
    return 0.5 * x * (1.0 + jnp.tanh(math.sqrt(2.0 / math.pi) * (x + 0.044715 * (x * x * x))))


def _s5_readout(y_ref, u, d_skip, glu_w, glu_b):
    y = _gelu_tanh(y_ref[0].astype(F32) + y_ref[1].astype(F32) + d_skip * u)
    return y * _sigmoid(jnp.dot(y.astype(BF16), glu_w, preferred_element_type=F32) + glu_b)


def _merge_kernel(alpha, ysa_ref, g_ref, bonus_ref, gnw_ref, gnb_ref, ones_ref,
                  yb_ref, ysc_ref, u_ref, dskip_ref, gluw_ref, glub_ref,
                  pg_ref, x_ref, gt_ref, pa_ref, pb_ref, pc_ref, wo_ref, lng_ref, lnb_ref, o_ref, yb_rows):
    d = x_ref.shape[1]
    ya = _rwkv_readout(ysa_ref, g_ref[...].astype(F32), bonus_ref[...].astype(F32), gnw_ref[...], gnb_ref[...],
                       ones_ref[...])
    yc = _s5_readout(ysc_ref, u_ref[...].astype(F32), dskip_ref[...], gluw_ref[...], glub_ref[...])
    n_batch, n_slabs, tt = yb_ref.shape[0], yb_ref.shape[1], yb_ref.shape[2]
    for b in range(n_batch):
        for j in range(n_slabs):
            yb_rows[j, pl.ds(b, tt, stride=n_batch), :] = yb_ref[b, j].astype(F32)
    yb = jnp.concatenate([yb_rows[j].astype(BF16) for j in range(n_slabs)], axis=1)
    pg = pg_ref[...].astype(F32)
    merged = (_sigmoid(pg[:, 0:d]) * _mm(ya, pa_ref[...])
              + _sigmoid(pg[:, d:2 * d]) * _mm(yb, pb_ref[...])
              + _sigmoid(pg[:, 2 * d:3 * d]) * _mm(yc, pc_ref[...]))
    mix = _mm(merged, wo_ref[...])
    o_ref[...] = _layer_norm(alpha * x_ref[...] + _rows_mod(mix, gt_ref[0]), lng_ref[...], lnb_ref[...])


def _merge(rwkv, yb, s5, pg, x_all, modtab, proj_a, proj_b, proj_c, w_out, ln_g, ln_b, alpha, n_ctx_rows):
    rows, d = x_all.shape
    n_batch, n_slabs = yb.shape[0], yb.shape[1]
    ysa, g, bonus, gn_w, gn_b, ones_bd = rwkv
    ysc, u, d_skip, glu_w, glu_b = s5
    tm = min(MERGE_ROW_TILE, n_ctx_rows)
    nctx = n_ctx_rows // tm
    seg = lambda i: (i >= nctx).astype(jnp.int32)
    row = lambda a: pl.BlockSpec((tm, a.shape[1]), lambda i: (i, 0))
    full = lambda a: pl.BlockSpec(a.shape, lambda i: (0, 0), pipeline_mode=pl.Buffered(1))
    vec = lambda a: pl.BlockSpec((1, a.shape[1]), lambda i: (0, 0))
    return pl.pallas_call(
        functools.partial(_merge_kernel, alpha),
        out_shape=jax.ShapeDtypeStruct((rows, d), F32),
        grid=(rows // tm,),
        scratch_shapes=[pltpu.VMEM((n_slabs, tm, LANES), F32)],
        in_specs=[pl.BlockSpec((2, ysa.shape[1], tm, LANES), lambda i: (0, 0, i, 0)),
                  row(g), row(bonus), vec(gn_w), vec(gn_b), full(ones_bd),
                  pl.BlockSpec((n_batch, n_slabs, tm // n_batch, LANES), lambda i: (0, 0, i, 0)),
                  pl.BlockSpec((2, tm, u.shape[1]), lambda i: (0, i, 0)), row(u), vec(d_skip), full(glu_w), vec(glu_b),
                  row(pg), row(x_all),
                  pl.BlockSpec((1, SUBLANES, d), lambda i: (seg(i), 0, 2)),
                  full(proj_a), full(proj_b), full(proj_c), full(w_out), vec(ln_g), vec(ln_b)],
        out_specs=pl.BlockSpec((tm, d), lambda i: (i, 0)),
        compiler_params=_cparams("parallel"),
        name="merge_ln",
    )(ysa, g, bonus, gn_w, gn_b, ones_bd, yb, ysc, u, d_skip, glu_w, glu_b,
      pg, x_all, modtab, proj_a, proj_b, proj_c, w_out, ln_g, ln_b)


def _mlp_kernel(alpha, x_ref, sh_ref, sc_ref, gt_ref, w1_ref, w2_ref, g_ref, b_ref, o_ref, h_ref, acc_ref,
                *split_ref):
    j = pl.program_id(1)

    @pl.when(j == 0)
    def _():
        h_ref[...] = _rows_add(_rows_mod(x_ref[...], 1.0 + sc_ref[0]), sh_ref[0]).astype(BF16)
        acc_ref[...] = jnp.zeros_like(acc_ref)

    a = jnp.maximum(jnp.dot(h_ref[...], w1_ref[...], preferred_element_type=F32), 0.0)
    acc_ref[...] += jnp.dot((a * a).astype(BF16), w2_ref[...], preferred_element_type=F32)

    @pl.when(j == pl.num_programs(1) - 1)
    def _():
        y = _layer_norm(alpha * x_ref[...] + _rows_mod(acc_ref[...], gt_ref[0]), g_ref[...], b_ref[...])
        if not split_ref:
            o_ref[...] = y
        else:
            rows_ref, = split_ref
            n_batch, tt, d = o_ref.shape
            for s in range(d // LANES):
                rows_ref[s] = y[:, s * LANES:(s + 1) * LANES]
            for b in range(n_batch):
                for s in range(d // LANES):
                    o_ref[b, :, s * LANES:(s + 1) * LANES] = rows_ref[s, pl.ds(b, tt, stride=n_batch), :]


def _mlp(x_mid, modtab, w1, w2, ln_g, ln_b, alpha, n_ctx_rows, final_batch=None):
    rows, d = x_mid.shape
    ff = w1.shape[1]
    tm = min(MLP_ROW_TILE, n_ctx_rows)
    tf = min(MLP_FF_TILE, ff)
    nctx = n_ctx_rows // tm
    first = nctx if final_batch else 0
    seg = lambda i: (i + first >= nctx).astype(jnp.int32)
    mod = lambda col: pl.BlockSpec((1, SUBLANES, d), lambda i, j: (seg(i), 0, col))
    vec = pl.BlockSpec((1, d), lambda i, j: (0, 0))
    scratch = [pltpu.VMEM((tm, d), BF16), pltpu.VMEM((tm, d), F32)]
    if final_batch:
        out_shape = jax.ShapeDtypeStruct((final_batch, (rows - n_ctx_rows) // final_batch, d), F32)
        out_spec = pl.BlockSpec((final_batch, tm // final_batch, d), lambda i, j: (0, i, 0))
        scratch.append(pltpu.VMEM((d // LANES, tm, LANES), F32))
    else:
        out_shape = jax.ShapeDtypeStruct((rows, d), F32)
        out_spec = pl.BlockSpec((tm, d), lambda i, j: (i, 0))
    return pl.pallas_call(
        functools.partial(_mlp_kernel, alpha),
        out_shape=out_shape,
        grid=(rows // tm - first, ff // tf),
        in_specs=[pl.BlockSpec((tm, d), lambda i, j: (i + first, 0)), mod(3), mod(4), mod(5),
                  pl.BlockSpec((d, tf), lambda i, j: (0, j)),
                  pl.BlockSpec((tf, d), lambda i, j: (j, 0)), vec, vec],
        out_specs=out_spec,
        scratch_shapes=scratch,
        compiler_params=_cparams("parallel", "arbitrary"),
        name="mlp_ln",
    )(x_mid, modtab, modtab, modtab, w1, w2, ln_g, ln_b)


def _block_diag_ones(n, blk):
    i = jnp.arange(n) // blk
    return (i[:, None] == i[None, :]).astype(BF16)


def _rope_tables(n_ctx, n_lat):
    pairs = HEAD_DIM // 4
    rows = n_lat // GRID_W
    row = jnp.repeat(jnp.arange(rows, dtype=F32), GRID_W)
    col = jnp.tile(jnp.arange(GRID_W, dtype=F32), rows)
    inv = ROPE_THETA ** (-jnp.arange(pairs, dtype=F32) / pairs)
    ang = jnp.stack([row, col], axis=-1)[:, :, None] * inv
    ang = jnp.broadcast_to(ang[:, :, None, :], (n_lat, 2, 2, pairs)).reshape(n_lat, HEAD_DIM)
    cos = jnp.concatenate([jnp.ones((n_ctx, HEAD_DIM), F32), jnp.cos(ang)], axis=0)
    sin = jnp.concatenate([jnp.zeros((n_ctx, HEAD_DIM), F32), jnp.sin(ang)], axis=0)
    first = (jnp.arange(HEAD_DIM) % (2 * pairs)) < pairs
    sin_a = jnp.where(first, -sin, 0.0)
    sin_b = jnp.where(first, 0.0, sin)
    two = lambda a: jnp.concatenate([a, a], axis=1)
    return two(cos), two(sin_a), two(sin_b)


def _block_diag(blocks):
    g, a, b = blocks.shape
    eye = jnp.eye(g, dtype=blocks.dtype)
    return (eye[:, None, :, None] * blocks[:, :, None, :]).reshape(g * a, g * b)


def kernel(x, c, ctx, c_ctx, mod_w, mod_b, w_in, rwkv_mu, rwkv_w0, rwkv_w_up, rwkv_a0, rwkv_a_up, rwkv_g_up, rwkv_k_k, rwkv_k_a, rwkv_r_k, rwkv_gn_w, rwkv_gn_b, attn_q_gain, attn_k_gain, ssm_a_re, ssm_a_im, ssm_log_dt, ssm_b_re, ssm_b_im, ssm_c_re, ssm_c_im, ssm_d, ssm_glu_w, ssm_glu_b, proj_a, proj_b, proj_c, w_out, ln1_g, ln1_b, ln2_g, ln2_b, mlp_w1, mlp_w2):
    n_batch, n_lat, d_model = x.shape
    n_ctx = ctx.shape[1]
    depth = mod_w.shape[0]
    assert n_batch == SUBLANES, "row layout packs the batch into one sublane tile"
    alpha = (2 * depth) ** 0.25
    n_ctx_rows = n_ctx * n_batch

    rwkv_w = rwkv_k_k.shape[1]
    q_w = proj_b.shape[1]
    kv_w = q_w // GQA_GROUP
    ssm_w = ssm_d.shape[1]
    rwkv_cols = rwkv_mu.shape[1]
    edges = [0, rwkv_cols, rwkv_cols + q_w, rwkv_cols + q_w + kv_w, rwkv_cols + q_w + 2 * kv_w,
             rwkv_cols + q_w + 2 * kv_w + ssm_w, w_in.shape[2]]
    slabbed = (False, True, True, True, False, False)
    splits = tuple(zip(edges[:-1], edges[1:], slabbed))

    x_all = _to_rows(ctx, x)

    cvec = jnp.concatenate([jnp.broadcast_to(c_ctx[None], (n_batch, d_model)), c], axis=0)
    modtab = _modulation(cvec, mod_w, mod_b).reshape(depth, 2, n_batch, 6 * d_model)

    ones_rwkv = _block_diag_ones(min(MXU_DIM, rwkv_w), HEAD_DIM)
    ones_pair = _block_diag_ones(LANES, HEAD_DIM)
    cos, sin_a, sin_b = (jnp.repeat(t, n_batch, axis=0) for t in _rope_tables(n_ctx, n_lat))
    rank = rwkv_w_up.shape[2]
    zpad = jnp.zeros((depth, 2, rank, rwkv_w), F32)

    for l in range(depth):
        mt = modtab[l]
        prm = dict(mu=rwkv_mu[l][None], w0=rwkv_w0[l][:, None], a0=rwkv_a0[l][:, None],
                   w_up=jnp.concatenate([rwkv_w_up[l], zpad[l]], axis=1).astype(BF16),
                   a_up=jnp.concatenate([zpad[l], rwkv_a_up[l]], axis=1).astype(BF16),
                   g_up=rwkv_g_up[l].astype(BF16), k_k=rwkv_k_k[l][None], k_a=rwkv_k_a[l][None],
                   r_k=rwkv_r_k[l][None], ones_bd=ones_rwkv)
        r, v, kk, g, bonus, lw, kd, bv, pq, pk, pv, pc, pg = _in_projection(
            x_all, mt, w_in[l].astype(BF16), splits, prm, n_ctx_rows)
        ysa = _rwkv_scan(r, v, kk, lw, kd, bv, n_batch, n_ctx)
        rwkv = (ysa, g, bonus, rwkv_gn_w[l][None], rwkv_gn_b[l][None], ones_rwkv)

        tile2 = lambda a: jnp.tile(a, 2)[None]
        qh, kh, vh = _attn_prepare(pq, pk, pv, cos, sin_a, sin_b, tile2(attn_q_gain[l]), tile2(attn_k_gain[l]),
                                   ones_pair, n_batch)
        yb = _flash_attention(qh, kh, vh, n_ctx)

        abar_re, abar_im, bf_re, bf_im = _s5_params(ssm_a_re[l], ssm_a_im[l], ssm_log_dt[l], ssm_b_re[l], ssm_b_im[l])
        n_g = ssm_a_re.shape[2]
        n_split = max(1, ssm_w // MXU_DIM)
        gs = n_g // n_split
        diag = lambda blocks: jnp.stack([_block_diag(blocks[h * gs:(h + 1) * gs]) for h in range(n_split)])
        bmat = lambda bf: jnp.stack([diag(bf[d].reshape(SSM_GROUP, n_g, SSM_STATE).transpose(1, 0, 2))
                                     for d in range(2)]).astype(BF16)
        cmat = lambda cc: diag(cc.transpose(0, 2, 1)).astype(BF16)
        ysc = _s5_scan(pc, abar_re, abar_im, bmat(bf_re), bmat(bf_im), cmat(ssm_c_re[l]), cmat(ssm_c_im[l]),
                       n_batch, n_ctx)
        s5 = (ysc, pc, ssm_d[l][None], ssm_glu_w[l].astype(BF16), ssm_glu_b[l][None])

        x_mid = _merge(rwkv, yb, s5, pg, x_all, mt, proj_a[l].astype(BF16), proj_b[l].astype(BF16),
                       proj_c[l].astype(BF16), w_out[l].astype(BF16), ln1_g[l][None], ln1_b[l][None],
                       alpha, n_ctx_rows)
        x_all = _mlp(x_mid, mt, mlp_w1[l].astype(BF16), mlp_w2[l].astype(BF16), ln2_g[l][None], ln2_b[l][None],
                     alpha, n_ctx_rows, final_batch=n_batch if l == depth - 1 else None)

    return x_all
```

```python
import functools
import math

import jax
import jax.numpy as jnp
from jax import lax
from jax.experimental import pallas as pl
from jax.experimental.pallas import tpu as pltpu

HEAD_DIM = 64
GRID_W = 64
ROPE_THETA = 10000.0
LN_EPS = 1e-5
RMS_EPS = 1e-6
GN_EPS = 64e-5
KK_EPS = 1e-24
LAMBDA_RE_MAX = -1e-4
GQA_GROUP = 4
SSM_GROUP = 16
SSM_STATE = 64

LANES = 128
SUBLANES = 8
MXU_DIM = 256
VMEM_LIMIT = 56 * 1024 * 1024

ROW_TILE = 256
MERGE_ROW_TILE = 512
LAYOUT_TT = 64
MLP_ROW_TILE = 1024
MLP_FF_TILE = 2048
SCAN_CHUNK = 64
SCAN_PAIRS = 2
SSM_CHUNK = 64
SSM_LANE_TILE = 512
ATTN_TQ = 256
ATTN_SUB = 256
ATTN_AHEAD = 1
PREP_TT = 128
VT_ROWS = 80

F32 = jnp.float32
BF16 = jnp.bfloat16


def _cparams(*sem):
    return pltpu.CompilerParams(dimension_semantics=sem, vmem_limit_bytes=VMEM_LIMIT)


def _mm(a, b):
    return jnp.dot(a.astype(BF16), b.astype(BF16), preferred_element_type=F32)


def _mm_nt(a, b):
    return lax.dot_general(a.astype(BF16), b.astype(BF16), (((1,), (1,)), ((), ())),
                           preferred_element_type=F32)


def _mm_tn(a, b):
    return lax.dot_general(a.astype(BF16), b.astype(BF16), (((0,), (0,)), ((), ())),
                           preferred_element_type=F32)


def _segsum(x, ones_bd):
    w = ones_bd.shape[0]
    hi = x.astype(BF16)
    lo = (x - hi.astype(F32)).astype(BF16)
    parts = [jnp.dot(hi[:, o:o + w], ones_bd, preferred_element_type=F32)
             + jnp.dot(lo[:, o:o + w], ones_bd, preferred_element_type=F32)
             for o in range(0, x.shape[1], w)]
    return parts[0] if len(parts) == 1 else jnp.concatenate(parts, axis=1)


def _sigmoid(x):
    return 1.0 / (1.0 + jnp.exp(-x))


def _softplus(x):
    return jnp.maximum(x, 0.0) + jnp.log(1.0 + jnp.exp(-jnp.abs(x)))


def _layer_norm(z, g, b):
    mu = jnp.mean(z, axis=-1, keepdims=True)
    zc = z - mu
    var = jnp.mean(zc * zc, axis=-1, keepdims=True)
    return zc * lax.rsqrt(var + LN_EPS) * g + b


def _rows_mod(x, m):
    tm, n = x.shape
    return (x.reshape(tm // SUBLANES, SUBLANES, n) * m[None]).reshape(tm, n)


def _rows_add(x, m):
    tm, n = x.shape
    return (x.reshape(tm // SUBLANES, SUBLANES, n) + m[None]).reshape(tm, n)


def _to_rows_kernel(n_ctx_tiles, ctx_ref, x_ref, o_ref, rows_ref):
    i = pl.program_id(0)
    n_batch, tt, d = x_ref.shape

    def interleave(src_ref):
        for b in range(n_batch):
            for s in range(d // LANES):
                rows_ref[s, pl.ds(b, tt, stride=n_batch), :] = src_ref[b, :, s * LANES:(s + 1) * LANES]
        for s in range(d // LANES):
            o_ref[:, s * LANES:(s + 1) * LANES] = rows_ref[s]

    @pl.when(i < n_ctx_tiles)
    def _():
        interleave(ctx_ref)

    @pl.when(i >= n_ctx_tiles)
    def _():
        interleave(x_ref)


def _to_rows(ctx, x):
    n_batch, n_ctx, d = ctx.shape
    n_lat = x.shape[1]
    tt = math.gcd(LAYOUT_TT, n_ctx)
    n_ctx_tiles = n_ctx // tt
    return pl.pallas_call(
        functools.partial(_to_rows_kernel, n_ctx_tiles),
        out_shape=jax.ShapeDtypeStruct(((n_ctx + n_lat) * n_batch, d), F32),
        grid=((n_ctx + n_lat) // tt,),
        in_specs=[pl.BlockSpec((n_batch, tt, d), lambda i: (0, jnp.minimum(i, n_ctx_tiles - 1), 0)),
                  pl.BlockSpec((n_batch, tt, d), lambda i: (0, jnp.maximum(i - n_ctx_tiles, 0), 0))],
        out_specs=pl.BlockSpec((tt * n_batch, d), lambda i: (i, 0)),
        scratch_shapes=[pltpu.VMEM((d // LANES, tt * n_batch, LANES), F32)],
        compiler_params=_cparams("parallel"),
        name="to_rows",
    )(ctx, x)


def _mod_kernel(c_ref, w_ref, b_ref, o_ref):
    c = c_ref[...]
    s = c * _sigmoid(c)
    o_ref[0] = jnp.dot(s, w_ref[0], preferred_element_type=F32,
                       precision=lax.Precision.HIGHEST) + b_ref[0]


def _modulation(cvec, mod_w, mod_b):
    n_layers, d, n6 = mod_w.shape
    tn = 1024 if n6 % 1024 == 0 else n6
    return pl.pallas_call(
        _mod_kernel,
        out_shape=jax.ShapeDtypeStruct((n_layers, cvec.shape[0], n6), F32),
        grid=(n_layers, n6 // tn),
        in_specs=[pl.BlockSpec((cvec.shape[0], d), lambda l, j: (0, 0)),
                  pl.BlockSpec((1, d, tn), lambda l, j: (l, 0, j)),
                  pl.BlockSpec((1, 1, tn), lambda l, j: (l, 0, j))],
        out_specs=pl.BlockSpec((1, cvec.shape[0], tn), lambda l, j: (l, 0, j)),
        compiler_params=_cparams("parallel", "parallel"),
        name="modulation",
    )(cvec, mod_w, mod_b.reshape(n_layers, 1, n6))


def _inproj_kernel(splits, n_ctx_rows, n_rows, x_ref, xp_ref, xn_ref, sh_ref, sc_ref, w_ref,
                   mu_ref, w0_ref, wup_ref, a0_ref, aup_ref, gup_ref, kk_ref, ka_ref, rk_ref, ones_ref, *o_refs):
    i = pl.program_id(0)
    tm = x_ref.shape[0]
    r0 = i * tm
    seg_start = jnp.logical_or(r0 == 0, r0 == n_ctx_rows)
    seg_end = jnp.logical_or(r0 + tm == n_ctx_rows, r0 + tm == n_rows)
    scale, shift = 1.0 + sc_ref[0], sh_ref[0]
    mod = lambda x: _rows_add(_rows_mod(x, scale), shift)
    xm = mod(x_ref[...])
    x_ext = jnp.concatenate([jnp.where(seg_start, 0.0, mod(xp_ref[...])), xm,
                             jnp.where(seg_end, 0.0, mod(xn_ref[...]))], axis=0).astype(BF16)
    a, b, _ = splits[0]
    n_rwkv = 8
    _rwkv_prep(jnp.dot(x_ext, w_ref[:, a:b], preferred_element_type=F32), mu_ref[...], w0_ref, wup_ref,
               a0_ref, aup_ref, gup_ref[...], kk_ref[...], ka_ref[...], rk_ref[...], ones_ref[...],
               o_refs[:n_rwkv])
    xm = xm.astype(BF16)
    for (a, b, slabbed), o_ref in zip(splits[1:], o_refs[n_rwkv:]):
        y = jnp.dot(xm, w_ref[:, a:b], preferred_element_type=F32)
        if slabbed:
            for j in range((b - a) // LANES):
                o_ref[j] = y[:, j * LANES:(j + 1) * LANES]
        else:
            o_ref[...] = y.astype(o_ref.dtype)


def _in_projection(x_all, modtab, w_bf16, splits, prm, n_ctx_rows):
    rows, d = x_all.shape
    width = prm["k_k"].shape[-1]
    tm = ROW_TILE
    nctx = n_ctx_rows // tm
    hb = tm // SUBLANES
    last = rows // SUBLANES - 1
    seg = lambda i: (i >= nctx).astype(jnp.int32)
    n_slabs = width // LANES
    slab = pl.BlockSpec((n_slabs, tm, LANES), lambda i: (0, i, 0))
    slab2 = pl.BlockSpec((2, n_slabs, tm, LANES), lambda i: (0, 0, i, 0))
    row = pl.BlockSpec((tm, width), lambda i: (i, 0))
    flat = jax.ShapeDtypeStruct((rows, width), BF16)
    one = jax.ShapeDtypeStruct((n_slabs, rows, LANES), F32)
    two = jax.ShapeDtypeStruct((2, n_slabs, rows, LANES), F32)
    shapes = [one, one, one, flat, flat, two, two, two]
    specs = [slab] * 3 + [row] * 2 + [slab2] * 3
    for a, b, slabbed in splits[1:]:
        if slabbed:
            shapes.append(jax.ShapeDtypeStruct(((b - a) // LANES, rows, LANES), F32))
            specs.append(pl.BlockSpec(((b - a) // LANES, tm, LANES), lambda i: (0, i, 0)))
        else:
            shapes.append(jax.ShapeDtypeStruct((rows, b - a), BF16))
            specs.append(pl.BlockSpec((tm, b - a), lambda i: (i, 0)))
    consts = [prm["mu"], prm["w0"], prm["w_up"], prm["a0"], prm["a_up"], prm["g_up"],
              prm["k_k"], prm["k_a"], prm["r_k"], prm["ones_bd"]]
    full = lambda a: pl.BlockSpec(a.shape, lambda i: (0,) * a.ndim)
    return pl.pallas_call(
        functools.partial(_inproj_kernel, splits, n_ctx_rows, rows),
        out_shape=shapes,
        grid=(rows // tm,),
        in_specs=[pl.BlockSpec((tm, d), lambda i: (i, 0)),
                  pl.BlockSpec((SUBLANES, d), lambda i: (jnp.maximum(i * hb - 1, 0), 0)),
                  pl.BlockSpec((SUBLANES, d), lambda i: (jnp.minimum((i + 1) * hb, last), 0)),
                  pl.BlockSpec((1, SUBLANES, d), lambda i: (seg(i), 0, 0)),
                  pl.BlockSpec((1, SUBLANES, d), lambda i: (seg(i), 0, 1)),
                  pl.BlockSpec(w_bf16.shape, lambda i: (0, 0), pipeline_mode=pl.Buffered(1))]
                 + [full(a) for a in consts],
        out_specs=specs,
        compiler_params=_cparams("parallel"),
        name="in_projection",
    )(x_all, x_all, x_all, modtab, modtab, w_bf16, *consts)


def _rwkv_prep(p_ext, mu, w0_ref, wup_ref, a0_ref, aup_ref, g_up, k_k, k_a, r_k, ones_bd, outs):
    r_out, v_out, kkn_out, g_out, bonus_out, lw_out, kd_out, bv_out = outs
    tm = p_ext.shape[0] - 2 * SUBLANES
    p = p_ext[SUBLANES:SUBLANES + tm]
    p = p + mu * (0.5 * (p_ext[0:tm] + p_ext[2 * SUBLANES:]) - p)

    w = k_k.shape[-1]
    r, k, v = p[:, 0:w], p[:, w:2 * w], p[:, 2 * w:3 * w]
    wa = p[:, 3 * w:3 * w + LANES]
    gd = p[:, 3 * w + LANES:3 * w + 2 * LANES]

    def put(o_ref, y, *lead):
        for j in range(w // LANES):
            o_ref[lead + (j,)] = y[:, j * LANES:(j + 1) * LANES]

    g_out[...] = _mm(_sigmoid(gd), g_up).astype(g_out.dtype)
    kk = k * k_k
    kk = kk * lax.rsqrt(jnp.maximum(_segsum(kk * kk, ones_bd), KK_EPS))
    put(r_out, r)
    put(v_out, v)
    put(kkn_out, kk)
    tanh_wa = jnp.tanh(wa)
    rk_sum = jnp.zeros_like(r)
    for d in range(2):
        w_pre = w0_ref[d] + _mm(tanh_wa, wup_ref[d])
        put(lw_out, -jnp.exp(-_softplus(-w_pre) - 0.5), d)
        a = _sigmoid(a0_ref[d] + _mm(wa, aup_ref[d]))
        k_dir = k * (1.0 + (a - 1.0) * k_a)
        put(kd_out, k_dir, d)
        put(bv_out, kk * a, d)
        rk_sum = rk_sum + r * k_dir
    bonus_out[...] = (_segsum(rk_sum * r_k, ones_bd) * v).astype(bonus_out.dtype)


def _scan_block(d, c, n_ctx_blk, n_blk):
    bwd = jnp.where(c < n_ctx_blk, n_ctx_blk - 1 - c, n_ctx_blk + (n_blk - 1 - c))
    return jnp.where(d == 0, c, bwd)


def _tri_inverse(a_list, ri, ci, n):
    eye = (ri == ci).astype(F32)
    same = (ri // SUBLANES) == (ci // SUBLANES)
    xs = [-jnp.where(same, a, 0.0) for a in a_list]
    x2 = [_mm(x, x) for x in xs]
    x4 = [_mm(x, x) for x in x2]
    ts = [eye + x for x in xs]
    ts = [t + _mm(t, x) for t, x in zip(ts, x2)]
    ts = [t + _mm(t, x) for t, x in zip(ts, x4)]
    size = SUBLANES
    while size < n:
        pick = jnp.logical_and((ri // (2 * size)) == (ci // (2 * size)), (ri // size) != (ci // size))
        lows = [_mm(jnp.where(pick, a, 0.0), t) for a, t in zip(a_list, ts)]
        ts = [t - _mm(t, low) for t, low in zip(ts, lows)]
        size *= 2
    return ts


def _rwkv_scan_kernel(n_batch, r_ref, v_ref, kk_ref, lw_ref, kd_ref, bv_ref, y_ref, state_ref):
    d = pl.program_id(0)
    c = pl.program_id(2)
    n = r_ref.shape[1] // n_batch
    m = 2 * n
    chains = [(p, b) for p in range(r_ref.shape[0]) for b in range(n_batch)]
    batches = range(len(chains))

    @pl.when(c == 0)
    def _():
        state_ref[...] = jnp.zeros_like(state_ref)

    sign = 1 - 2 * d
    ri = lax.broadcasted_iota(jnp.int32, (m, m), 0)
    ci = lax.broadcasted_iota(jnp.int32, (m, m), 1)
    same_head = (ri // n) == (ci // n)
    order = (ri - ci) * sign
    strict = jnp.logical_and(same_head, order > 0)
    incl = jnp.logical_and(same_head, order >= 0)
    ti = lax.broadcasted_iota(jnp.int32, (n, n), 0)
    si = lax.broadcasted_iota(jnp.int32, (n, n), 1)
    tri = ((ti - si) * sign >= 0).astype(BF16)
    head0 = lax.broadcasted_iota(jnp.int32, (n, LANES), 1) < HEAD_DIM
    blockdiag = (lax.broadcasted_iota(jnp.int32, (LANES, LANES), 0) // HEAD_DIM
                 == lax.broadcasted_iota(jnp.int32, (LANES, LANES), 1) // HEAD_DIM)

    def stack(x):
        return jnp.concatenate([jnp.where(head0, x, 0.0), jnp.where(head0, 0.0, x)], axis=0)

    def shared(ref, i):
        p, b = chains[i]
        return ref[p, pl.ds(b, n, stride=n_batch), :]

    def split(ref, i):
        p, b = chains[i]
        return ref[0, p, pl.ds(b, n, stride=n_batch), :]

    duos = range(0, len(chains), 2)
    cat = lambda xs, i: jnp.concatenate([xs[i], xs[i + 1]], axis=1)
    uncat = lambda xs: [x[:, o:o + LANES] for x in xs for o in (0, LANES)]

    lw = [split(lw_ref, b) for b in batches]
    hi = [x.astype(BF16) for x in lw]
    lo = [(x - h.astype(F32)).astype(BF16) for x, h in zip(lw, hi)]
    cum = uncat([jnp.dot(tri, cat(hi, i), preferred_element_type=F32)
                 + jnp.dot(tri, cat(lo, i), preferred_element_type=F32) for i in duos])
    e_cum = [jnp.exp(x) for x in cum]
    e_neg = [jnp.exp(-x) for x in cum]
    kap = [shared(kk_ref, b) * jnp.exp(cum[b] - lw[b]) for b in batches]
    rt = [shared(r_ref, b) * e_cum[b] for b in batches]
    bt = [split(bv_ref, b) * e_neg[b] for b in batches]
    kt = [split(kd_ref, b) * e_neg[b] for b in batches]
    v = [shared(v_ref, b) for b in batches]
    total = [jnp.where(d == 0, e[n - 1:n], e[0:1]) for e in e_cum]
    state = [state_ref[b] for b in batches]

    scores = [_mm_nt(jnp.concatenate([stack(kap[b]), stack(rt[b])], axis=0),
                     jnp.concatenate([stack(bt[b]), stack(kt[b])], axis=0)) for b in batches]
    a_b = [jnp.where(strict, s[0:m, 0:m], 0.0) for s in scores]
    a_k = [jnp.where(strict, s[0:m, m:2 * m], 0.0) for s in scores]
    a_r = [jnp.concatenate([jnp.where(incl, s[m:2 * m, m:2 * m], 0.0),
                            jnp.where(incl, -s[m:2 * m, 0:m], 0.0)], axis=1) for s in scores]
    g0 = [_mm_nt(jnp.concatenate([kap[b], rt[b]], axis=0), state[b]) for b in batches]
    t_inv = _tri_inverse(a_b, ri, ci, n)
    vs = [stack(x) for x in v]
    rhs = [stack(g0[b][0:n]) + _mm(a_k[b], vs[b]) for b in batches]
    us = [_mm(t_inv[b], rhs[b]) for b in batches]
    ys = [stack(g0[b][n:m]) + _mm(a_r[b], jnp.concatenate([vs[b], us[b]], axis=0)) for b in batches]
    for i in batches:
        p, b = chains[i]
        y_ref[0, p, pl.ds(b, n, stride=n_batch), :] = ys[i][0:n] + ys[i][n:m]
    ds = [_mm_tn(jnp.concatenate([-(us[b][0:n] + us[b][n:m]), v[b]], axis=0),
                 jnp.concatenate([bt[b], kt[b]], axis=0)) for b in batches]
    for b in batches:
        state_ref[b] = (state[b] + jnp.where(blockdiag, ds[b], 0.0)) * total[b]


def _rwkv_scan(r, v, kk, lw, kd, bv, n_batch, n_ctx):
    n_slabs, rows, _ = r.shape
    tm = SCAN_CHUNK * n_batch
    pp = min(SCAN_PAIRS, n_slabs)
    n_blk, n_ctx_blk = rows // tm, n_ctx * n_batch // tm
    blk = lambda d, c: _scan_block(d, c, n_ctx_blk, n_blk)
    shared = pl.BlockSpec((pp, tm, LANES), lambda d, p, c: (p, blk(d, c), 0))
    split = pl.BlockSpec((1, pp, tm, LANES), lambda d, p, c: (d, p, blk(d, c), 0))
    return pl.pallas_call(
        functools.partial(_rwkv_scan_kernel, n_batch),
        out_shape=jax.ShapeDtypeStruct((2, n_slabs, rows, LANES), F32),
        grid=(2, n_slabs // pp, n_blk),
        in_specs=[shared, shared, shared, split, split, split],
        out_specs=split,
        scratch_shapes=[pltpu.VMEM((pp * n_batch, LANES, LANES), F32)],
        compiler_params=_cparams("parallel", "parallel", "arbitrary"),
        name="rwkv_scan",
    )(r, v, kk, lw, kd, bv)


def _rwkv_readout(y_ref, g, bonus, gn_w, gn_b, ones_bd):
    y = jnp.concatenate([y_ref[0, j] + y_ref[1, j] for j in range(y_ref.shape[1])], axis=1)
    inv = 1.0 / HEAD_DIM
    mean = _segsum(y, ones_bd) * inv
    yc = y - mean
    var = _segsum(yc * yc, ones_bd) * inv
    yn = yc * lax.rsqrt(var + GN_EPS) * gn_w + gn_b
    return (yn + bonus) * g


def _norm_rope(x, gain, cos, sin_a, sin_b, ones_bd):
    ms = _segsum(x * x, ones_bd) * (1.0 / HEAD_DIM)
    xn = x * lax.rsqrt(ms + RMS_EPS) * gain
    return (xn * cos + pltpu.roll(xn, LANES - HEAD_DIM // 4, 1) * sin_a
            + pltpu.roll(xn, HEAD_DIM // 4, 1) * sin_b)


def _attn_prep_kernel(scale, n_batch, q_ref, k_ref, v_ref, cos_ref, sa_ref, sb_ref, qg_ref, kg_ref, ones_ref,
                      qo_ref, ko_ref, vo_ref, scr):
    cos, sin_a, sin_b = cos_ref[...], sa_ref[...], sb_ref[...]
    ones_bd = ones_ref[...]
    tm = cos.shape[0]
    tt = tm // n_batch
    first = lax.broadcasted_iota(jnp.int32, (tm, LANES), 1) < HEAD_DIM
    n_q, n_k = q_ref.shape[0], k_ref.shape[0]
    for j in range(n_q):
        scr[j] = _norm_rope(q_ref[j], qg_ref[...], cos, sin_a, sin_b, ones_bd) * scale
    for j in range(n_k):
        y = _norm_rope(k_ref[j], kg_ref[...], cos, sin_a, sin_b, ones_bd)
        swapped = pltpu.roll(y, HEAD_DIM, 1)
        scr[n_q + 2 * j] = jnp.where(first, y, swapped)
        scr[n_q + 2 * j + 1] = jnp.where(first, swapped, y)
    ones_rows = jnp.ones((vo_ref.shape[3] - HEAD_DIM, LANES), BF16)
    for b in range(n_batch):
        rows = pl.ds(b, tt, stride=n_batch)
        for j in range(n_q):
            qo_ref[b, j] = scr[j, rows, :].astype(BF16)
        for h in range(2 * n_k):
            ko_ref[b, h] = scr[n_q + h, rows, :].astype(BF16)
        for j in range(n_k):
            vb = v_ref[j, rows, :]
            for g in range(tt // LANES):
                vt = vb[g * LANES:(g + 1) * LANES].T.astype(BF16)
                for h in range(2):
                    vo_ref[b, 2 * j + h, g, 0:HEAD_DIM] = vt[h * HEAD_DIM:(h + 1) * HEAD_DIM]
                    vo_ref[b, 2 * j + h, g, HEAD_DIM:] = ones_rows


def _attn_prepare(pq, pk, pv, cos, sin_a, sin_b, q_gain, k_gain, ones_pair, n_batch):
    q_slabs, rows, _ = pq.shape
    k_slabs = pk.shape[0]
    tt_all = rows // n_batch
    tt = PREP_TT
    tm = tt * n_batch
    kh = 2 * k_slabs
    tab = pl.BlockSpec((tm, LANES), lambda i: (i, 0))
    vec = pl.BlockSpec((1, LANES), lambda i: (0, 0))
    return pl.pallas_call(
        functools.partial(_attn_prep_kernel, HEAD_DIM ** -0.5 * math.log2(math.e), n_batch),
        out_shape=[jax.ShapeDtypeStruct((n_batch, q_slabs, tt_all, LANES), BF16),
                   jax.ShapeDtypeStruct((n_batch, kh, tt_all, LANES), BF16),
                   jax.ShapeDtypeStruct((n_batch, kh, tt_all // LANES, VT_ROWS, LANES), BF16)],
        grid=(tt_all // tt,),
        in_specs=[pl.BlockSpec((q_slabs, tm, LANES), lambda i: (0, i, 0)),
                  pl.BlockSpec((k_slabs, tm, LANES), lambda i: (0, i, 0)),
                  pl.BlockSpec((k_slabs, tm, LANES), lambda i: (0, i, 0)),
                  tab, tab, tab, vec, vec,
                  pl.BlockSpec(ones_pair.shape, lambda i: (0, 0))],
        out_specs=[pl.BlockSpec((n_batch, q_slabs, tt, LANES), lambda i: (0, 0, i, 0)),
                   pl.BlockSpec((n_batch, kh, tt, LANES), lambda i: (0, 0, i, 0)),
                   pl.BlockSpec((n_batch, kh, tt // LANES, VT_ROWS, LANES), lambda i: (0, 0, i, 0, 0))],
        scratch_shapes=[pltpu.VMEM((q_slabs + kh, tm, LANES), F32)],
        compiler_params=_cparams("parallel"),
        name="attn_prepare",
    )(pq, pk, pv, cos, sin_a, sin_b, q_gain, k_gain, ones_pair)


def _flash_kernel(n_ctx, sub_ctx, sub_lat, q_ref, k_ref, vt_ref, o_ref):
    qi = pl.program_id(2)
    n_pairs, tq = q_ref.shape[1], q_ref.shape[2]
    n_keys = k_ref.shape[2]
    first = lax.broadcasted_iota(jnp.int32, (tq, LANES), 1) < HEAD_DIM
    zero = jnp.zeros((tq, LANES), BF16)
    q = jnp.concatenate([jnp.where(first if h == 0 else jnp.logical_not(first), q_ref[0, p], zero)
                         for p in range(n_pairs) for h in range(2)], axis=0)
    q_t = q.astype(F32).T.astype(BF16)

    def scores(blk):
        start, size = blk
        return jnp.dot(k_ref[0, 0, start:start + size, :], q_t, preferred_element_type=F32)

    def weighted_values(p, blk):
        start, size = blk
        vt = jnp.concatenate([vt_ref[0, 0, start // LANES + g] for g in range(size // LANES)], axis=1)
        return jnp.dot(vt, p, preferred_element_type=F32)

    def attend(blocks):
        ahead = [scores(b) for b in blocks[:ATTN_AHEAD]]
        m_prev = acc = pending = None
        for i, blk in enumerate(blocks):
            s = ahead.pop(0)
            if i + ATTN_AHEAD < len(blocks):
                ahead.append(scores(blocks[i + ATTN_AHEAD]))
            if pending is not None:
                alpha, p, pblk = pending
                pv = weighted_values(p, pblk)
                acc = pv if acc is None else alpha * acc + pv
            m_new = jnp.max(s, axis=0, keepdims=True)
            alpha = None
            if m_prev is not None:
                m_new = jnp.maximum(m_prev, m_new)
                alpha = jnp.exp2(m_prev - m_new)
            pending = (alpha, jnp.exp2(s - m_new).astype(BF16), blk)
            m_prev = m_new
        alpha, p, pblk = pending
        pv = weighted_values(p, pblk)
        acc = pv if acc is None else alpha * acc + pv
        out_t = acc[0:HEAD_DIM] / acc[HEAD_DIM:HEAD_DIM + 1]
        for p in range(n_pairs):
            pair_t = jnp.concatenate([out_t[:, (2 * p) * tq:(2 * p + 1) * tq],
                                      out_t[:, (2 * p + 1) * tq:(2 * p + 2) * tq]], axis=0)
            o_ref[0, p] = pair_t.T.astype(o_ref.dtype)

    is_ctx = qi * tq < n_ctx
    ctx_blocks = [(s0, sub_ctx) for s0 in range(0, n_ctx, sub_ctx)]
    lat_blocks = [(s0, sub_lat) for s0 in range(n_ctx, n_keys, sub_lat)]

    @pl.when(is_ctx)
    def _():
        attend(ctx_blocks)

    @pl.when(jnp.logical_not(is_ctx))
    def _():
        attend(ctx_blocks + lat_blocks)


def _flash_attention(qh, kh, vt, n_ctx):
    n_batch, q_slabs, tt_all, _ = qh.shape
    n_kvh = kh.shape[1]
    n_pairs = q_slabs // n_kvh
    tq = min(ATTN_TQ, n_ctx)
    sub_ctx = min(ATTN_SUB, n_ctx)
    sub_lat = min(ATTN_SUB, tt_all - n_ctx)
    assert n_ctx % tq == 0 and n_ctx % sub_ctx == 0 and (tt_all - n_ctx) % sub_lat == 0
    assert sub_ctx % LANES == 0 and sub_lat % LANES == 0 and tq % LANES == 0
    return pl.pallas_call(
        functools.partial(_flash_kernel, n_ctx, sub_ctx, sub_lat),
        out_shape=jax.ShapeDtypeStruct((n_batch, q_slabs, tt_all, LANES), BF16),
        grid=(n_batch, n_kvh, tt_all // tq),
        in_specs=[pl.BlockSpec((1, n_pairs, tq, LANES), lambda b, h, qi: (b, h, qi, 0)),
                  pl.BlockSpec((1, 1, tt_all, LANES), lambda b, h, qi: (b, h, 0, 0)),
                  pl.BlockSpec((1, 1, tt_all // LANES, VT_ROWS, LANES), lambda b, h, qi: (b, h, 0, 0, 0))],
        out_specs=pl.BlockSpec((1, n_pairs, tq, LANES), lambda b, h, qi: (b, h, qi, 0)),
        compiler_params=_cparams("parallel", "parallel", "parallel"),
        name="flash_attention",
    )(qh, kh, vt)


def _s5_param_kernel(are_ref, aim_ref, ldt_ref, bre_ref, bim_ref, abar_re, abar_im, bf_re, bf_im):
    lam_re = jnp.minimum(are_ref[0], LAMBDA_RE_MAX)
    lam_im = aim_ref[0]
    dt = jnp.exp(ldt_ref[0])
    mag = jnp.exp(lam_re * dt)
    ar, ai = mag * jnp.cos(lam_im * dt), mag * jnp.sin(lam_im * dt)
    nr, ni = ar - 1.0, ai
    den = lam_re * lam_re + lam_im * lam_im
    cr = (nr * lam_re + ni * lam_im) / den
    cim = (ni * lam_re - nr * lam_im) / den
    abar_re[0], abar_im[0] = ar, ai
    bre, bim = bre_ref[...], bim_ref[...]
    bf_re[0] = cr * bre - cim * bim
    bf_im[0] = cr * bim + cim * bre


def _s5_params(a_re, a_im, log_dt, b_re, b_im):
    _, n_g, n_s = a_re.shape
    n_i = b_re.shape[-1]
    gn = n_g * n_s
    flat = lambda a: a.reshape(2, 1, gn)
    dt = jnp.broadcast_to(log_dt[:, :, None], (2, n_g, n_s))
    bt = lambda b: b.reshape(gn, n_i).T
    vec = pl.BlockSpec((1, 1, gn), lambda d: (d, 0, 0))
    mat = pl.BlockSpec((n_i, gn), lambda d: (0, 0))
    omat = pl.BlockSpec((1, n_i, gn), lambda d: (d, 0, 0))
    return pl.pallas_call(
        _s5_param_kernel,
        out_shape=[jax.ShapeDtypeStruct((2, 1, gn), F32)] * 2 + [jax.ShapeDtypeStruct((2, n_i, gn), F32)] * 2,
        grid=(2,),
        in_specs=[vec, vec, vec, mat, mat],
        out_specs=[vec, vec, omat, omat],
        compiler_params=_cparams("parallel"),
        name="s5_params",
    )(flat(a_re), flat(a_im), flat(dt), bt(b_re), bt(b_im))


def _s5_scan_kernel(u_ref, ar_ref, ai_ref, bre_ref, bim_ref, cre_ref, cim_ref, y_ref,
                    xr_ref, xi_ref, hr_ref, hi_ref):
    d = pl.program_id(0)
    c = pl.program_id(1)
    steps = u_ref.shape[0] // SUBLANES

    @pl.when(c == 0)
    def _():
        hr_ref[...] = jnp.zeros_like(hr_ref)
        hi_ref[...] = jnp.zeros_like(hi_ref)

    n_split, kw, sw = bre_ref.shape[1], bre_ref.shape[2], bre_ref.shape[3]
    lt = min(SSM_LANE_TILE, sw)
    tiles = [(h, o) for h in range(n_split) for o in range(0, sw, lt)]
    u = u_ref[...].astype(BF16)

    def drive(tile):
        h, o = tile
        uh = u[:, h * kw:(h + 1) * kw]
        sl = slice(h * sw + o, h * sw + o + lt)
        xr_ref[:, sl] = jnp.dot(uh, bre_ref[0, h, :, o:o + lt], preferred_element_type=F32)
        xi_ref[:, sl] = jnp.dot(uh, bim_ref[0, h, :, o:o + lt], preferred_element_type=F32)

    def scan(tile, reverse):
        h, o = tile
        sl = slice(h * sw + o, h * sw + o + lt)
        ar = jnp.broadcast_to(ar_ref[0, :, sl], (SUBLANES, lt))
        ai = jnp.broadcast_to(ai_ref[0, :, sl], (SUBLANES, lt))
        hr, hi = hr_ref[:, sl], hi_ref[:, sl]
        for i in range(steps):
            t = steps - 1 - i if reverse else i
            rows = slice(t * SUBLANES, (t + 1) * SUBLANES)
            hr, hi = ar * hr - ai * hi + xr_ref[rows, sl], ar * hi + ai * hr + xi_ref[rows, sl]
            xr_ref[rows, sl] = hr
            xi_ref[rows, sl] = hi
        hr_ref[:, sl] = hr
        hi_ref[:, sl] = hi

    def readout(tile):
        h, o = tile
        sl = slice(h * sw + o, h * sw + o + lt)
        return (jnp.dot(xr_ref[:, sl].astype(BF16), cre_ref[h, o:o + lt, :], preferred_element_type=F32)
                - jnp.dot(xi_ref[:, sl].astype(BF16), cim_ref[h, o:o + lt, :], preferred_element_type=F32))

    def run(reverse):
        y = [None] * n_split
        drive(tiles[0])
        for n, tile in enumerate(tiles):
            if n + 1 < len(tiles):
                drive(tiles[n + 1])
            if n > 0:
                ph = tiles[n - 1][0]
                part = readout(tiles[n - 1])
                y[ph] = part if y[ph] is None else y[ph] + part
            scan(tile, reverse)
        ph = tiles[-1][0]
        part = readout(tiles[-1])
        y[ph] = part if y[ph] is None else y[ph] + part
        for h in range(n_split):
            y_ref[0, :, h * kw:(h + 1) * kw] = y[h].astype(y_ref.dtype)

    @pl.when(d == 0)
    def _():
        run(False)

    @pl.when(d == 1)
    def _():
        run(True)


def _s5_scan(u, abar_re, abar_im, b_re, b_im, c_re, c_im, n_batch, n_ctx):
    rows, width = u.shape
    gn = abar_re.shape[-1]
    tm = SSM_CHUNK * n_batch
    n_blk, n_ctx_blk = rows // tm, n_ctx * n_batch // tm
    blk = lambda d, c: _scan_block(d, c, n_ctx_blk, n_blk)
    vec = pl.BlockSpec((1, 1, gn), lambda d, c: (d, 0, 0))
    bmat = pl.BlockSpec((1,) + b_re.shape[1:], lambda d, c: (d, 0, 0, 0))
    cmat = pl.BlockSpec(c_re.shape, lambda d, c: (0, 0, 0))
    return pl.pallas_call(
        _s5_scan_kernel,
        out_shape=jax.ShapeDtypeStruct((2, rows, width), BF16),
        grid=(2, n_blk),
        in_specs=[pl.BlockSpec((tm, width), lambda d, c: (blk(d, c), 0)), vec, vec, bmat, bmat, cmat, cmat],
        out_specs=pl.BlockSpec((1, tm, width), lambda d, c: (d, blk(d, c), 0)),
        scratch_shapes=[pltpu.VMEM((tm, gn), F32), pltpu.VMEM((tm, gn), F32),
                        pltpu.VMEM((SUBLANES, gn), F32), pltpu.VMEM((SUBLANES, gn), F32)],
        compiler_params=_cparams("parallel", "arbitrary"),
        name="s5_scan",
    )(u, abar_re, abar_im, b_re, b_im, c_re, c_im)


def _gelu_tanh(x):
    return 0.5 * x * (1.0 + jnp.tanh(math.sqrt(2.0 / math.pi) * (x + 0.044715 * (x * x * x))))


def _s5_readout(y_ref, u, d_skip, glu_w, glu_b):
    y = _gelu_tanh(y_ref[0].astype(F32) + y_ref[1].astype(F32) + d_skip * u)
    return y * _sigmoid(jnp.dot(y.astype(BF16), glu_w, preferred_element_type=F32) + glu_b)


def _merge_kernel(alpha, ysa_ref, g_ref, bonus_ref, gnw_ref, gnb_ref, ones_ref,
                  yb_ref, ysc_ref, u_ref, dskip_ref, gluw_ref, glub_ref,
                  pg_ref, x_ref, gt_ref, pa_ref, pb_ref, pc_ref, wo_ref, lng_ref, lnb_ref, o_ref, yb_rows):
    d = x_ref.shape[1]
    ya = _rwkv_readout(ysa_ref, g_ref[...].astype(F32), bonus_ref[...].astype(F32), gnw_ref[...], gnb_ref[...],
                       ones_ref[...])
    yc = _s5_readout(ysc_ref, u_ref[...].astype(F32), dskip_ref[...], gluw_ref[...], glub_ref[...])
    n_batch, n_slabs, tt = yb_ref.shape[0], yb_ref.shape[1], yb_ref.shape[2]
    for b in range(n_batch):
        for j in range(n_slabs):
            yb_rows[j, pl.ds(b, tt, stride=n_batch), :] = yb_ref[b, j].astype(F32)
    yb = jnp.concatenate([yb_rows[j].astype(BF16) for j in range(n_slabs)], axis=1)
    pg = pg_ref[...].astype(F32)
    merged = (_sigmoid(pg[:, 0:d]) * _mm(ya, pa_ref[...])
              + _sigmoid(pg[:, d:2 * d]) * _mm(yb, pb_ref[...])
              + _sigmoid(pg[:, 2 * d:3 * d]) * _mm(yc, pc_ref[...]))
    mix = _mm(merged, wo_ref[...])
    o_ref[...] = _layer_norm(alpha * x_ref[...] + _rows_mod(mix, gt_ref[0]), lng_ref[...], lnb_ref[...])


def _merge(rwkv, yb, s5, pg, x_all, modtab, proj_a, proj_b, proj_c, w_out, ln_g, ln_b, alpha, n_ctx_rows):
    rows, d = x_all.shape
    n_batch, n_slabs = yb.shape[0], yb.shape[1]
    ysa, g, bonus, gn_w, gn_b, ones_bd = rwkv
    ysc, u, d_skip, glu_w, glu_b = s5
    tm = min(MERGE_ROW_TILE, n_ctx_rows)
    nctx = n_ctx_rows // tm
    seg = lambda i: (i >= nctx).astype(jnp.int32)
    row = lambda a: pl.BlockSpec((tm, a.shape[1]), lambda i: (i, 0))
    full = lambda a: pl.BlockSpec(a.shape, lambda i: (0, 0), pipeline_mode=pl.Buffered(1))
    vec = lambda a: pl.BlockSpec((1, a.shape[1]), lambda i: (0, 0))
    return pl.pallas_call(
        functools.partial(_merge_kernel, alpha),
        out_shape=jax.ShapeDtypeStruct((rows, d), F32),
        grid=(rows // tm,),
        scratch_shapes=[pltpu.VMEM((n_slabs, tm, LANES), F32)],
        in_specs=[pl.BlockSpec((2, ysa.shape[1], tm, LANES), lambda i: (0, 0, i, 0)),
                  row(g), row(bonus), vec(gn_w), vec(gn_b), full(ones_bd),
                  pl.BlockSpec((n_batch, n_slabs, tm // n_batch, LANES), lambda i: (0, 0, i, 0)),
                  pl.BlockSpec((2, tm, u.shape[1]), lambda i: (0, i, 0)), row(u), vec(d_skip), full(glu_w), vec(glu_b),
                  row(pg), row(x_all),
                  pl.BlockSpec((1, SUBLANES, d), lambda i: (seg(i), 0, 2)),
                  full(proj_a), full(proj_b), full(proj_c), full(w_out), vec(ln_g), vec(ln_b)],
        out_specs=pl.BlockSpec((tm, d), lambda i: (i, 0)),
        compiler_params=_cparams("parallel"),
        name="merge_ln",
    )(ysa, g, bonus, gn_w, gn_b, ones_bd, yb, ysc, u, d_skip, glu_w, glu_b,
      pg, x_all, modtab, proj_a, proj_b, proj_c, w_out, ln_g, ln_b)


def _mlp_kernel(alpha, x_ref, sh_ref, sc_ref, gt_ref, w1_ref, w2_ref, g_ref, b_ref, o_ref, h_ref, acc_ref,
                *split_ref):
    j = pl.program_id(1)

    @pl.when(j == 0)
    def _():
        h_ref[...] = _rows_add(_rows_mod(x_ref[...], 1.0 + sc_ref[0]), sh_ref[0]).astype(BF16)
        acc_ref[...] = jnp.zeros_like(acc_ref)

    a = jnp.maximum(jnp.dot(h_ref[...], w1_ref[...], preferred_element_type=F32), 0.0)
    acc_ref[...] += jnp.dot((a * a).astype(BF16), w2_ref[...], preferred_element_type=F32)

    @pl.when(j == pl.num_programs(1) - 1)
    def _():
        y = _layer_norm(alpha * x_ref[...] + _rows_mod(acc_ref[...], gt_ref[0]), g_ref[...], b_ref[...])
        if not split_ref:
            o_ref[...] = y
        else:
            rows_ref, = split_ref
            n_batch, tt, d = o_ref.shape
            for s in range(d // LANES):
                rows_ref[s] = y[:, s * LANES:(s + 1) * LANES]
            for b in range(n_batch):
                for s in range(d // LANES):
                    o_ref[b, :, s * LANES:(s + 1) * LANES] = rows_ref[s, pl.ds(b, tt, stride=n_batch), :]


def _mlp(x_mid, modtab, w1, w2, ln_g, ln_b, alpha, n_ctx_rows, final_batch=None):
    rows, d = x_mid.shape
    ff = w1.shape[1]
    tm = min(MLP_ROW_TILE, n_ctx_rows)
    tf = min(MLP_FF_TILE, ff)
    nctx = n_ctx_rows // tm
    first = nctx if final_batch else 0
    seg = lambda i: (i + first >= nctx).astype(jnp.int32)
    mod = lambda col: pl.BlockSpec((1, SUBLANES, d), lambda i, j: (seg(i), 0, col))
    vec = pl.BlockSpec((1, d), lambda i, j: (0, 0))
    scratch = [pltpu.VMEM((tm, d), BF16), pltpu.VMEM((tm, d), F32)]
    if final_batch:
        out_shape = jax.ShapeDtypeStruct((final_batch, (rows - n_ctx_rows) // final_batch, d), F32)
        out_spec = pl.BlockSpec((final_batch, tm // final_batch, d), lambda i, j: (0, i, 0))
        scratch.append(pltpu.VMEM((d // LANES, tm, LANES), F32))
    else:
        out_shape = jax.ShapeDtypeStruct((rows, d), F32)
        out_spec = pl.BlockSpec((tm, d), lambda i, j: (i, 0))
    return pl.pallas_call(
        functools.partial(_mlp_kernel, alpha),
        out_shape=out_shape,
        grid=(rows // tm - first, ff // tf),
        in_specs=[pl.BlockSpec((tm, d), lambda i, j: (i + first, 0)), mod(3), mod(4), mod(5),
                  pl.BlockSpec((d, tf), lambda i, j: (0, j)),
                  pl.BlockSpec((tf, d), lambda i, j: (j, 0)), vec, vec],
        out_specs=out_spec,
        scratch_shapes=scratch,
        compiler_params=_cparams("parallel", "arbitrary"),
        name="mlp_ln",
    )(x_mid, modtab, modtab, modtab, w1, w2, ln_g, ln_b)


def _block_diag_ones(n, blk):
    i = jnp.arange(n) // blk
    return (i[:, None] == i[None, :]).astype(BF16)


def _rope_tables(n_ctx, n_lat):
    pairs = HEAD_DIM // 4
    rows = n_lat // GRID_W
    row = jnp.repeat(jnp.arange(rows, dtype=F32), GRID_W)
    col = jnp.tile(jnp.arange(GRID_W, dtype=F32), rows)
    inv = ROPE_THETA ** (-jnp.arange(pairs, dtype=F32) / pairs)
    ang = jnp.stack([row, col], axis=-1)[:, :, None] * inv
    ang = jnp.broadcast_to(ang[:, :, None, :], (n_lat, 2, 2, pairs)).reshape(n_lat, HEAD_DIM)
    cos = jnp.concatenate([jnp.ones((n_ctx, HEAD_DIM), F32), jnp.cos(ang)], axis=0)
    sin = jnp.concatenate([jnp.zeros((n_ctx, HEAD_DIM), F32), jnp.sin(ang)], axis=0)
    first = (jnp.arange(HEAD_DIM) % (2 * pairs)) < pairs
    sin_a = jnp.where(first, -sin, 0.0)
    sin_b = jnp.where(first, 0.0, sin)
    two = lambda a: jnp.concatenate([a, a], axis=1)
    return two(cos), two(sin_a), two(sin_b)


def _block_diag(blocks):
    g, a, b = blocks.shape
    eye = jnp.eye(g, dtype=blocks.dtype)
    return (eye[:, None, :, None] * blocks[:, :, None, :]).reshape(g * a, g * b)


def kernel(x, c, ctx, c_ctx, mod_w, mod_b, w_in, rwkv_mu, rwkv_w0, rwkv_w_up, rwkv_a0, rwkv_a_up, rwkv_g_up, rwkv_k_k, rwkv_k_a, rwkv_r_k, rwkv_gn_w, rwkv_gn_b, attn_q_gain, attn_k_gain, ssm_a_re, ssm_a_im, ssm_log_dt, ssm_b_re, ssm_b_im, ssm_c_re, ssm_c_im, ssm_d, ssm_glu_w, ssm_glu_b, proj_a, proj_b, proj_c, w_out, ln1_g, ln1_b, ln2_g, ln2_b, mlp_w1, mlp_w2):
    n_batch, n_lat, d_model = x.shape
    n_ctx = ctx.shape[1]
    depth = mod_w.shape[0]
    assert n_batch == SUBLANES, "row layout packs the batch into one sublane tile"
    alpha = (2 * depth) ** 0.25
    n_ctx_rows = n_ctx * n_batch

    rwkv_w = rwkv_k_k.shape[1]
    q_w = proj_b.shape[1]
    kv_w = q_w // GQA_GROUP
    ssm_w = ssm_d.shape[1]
    rwkv_cols = rwkv_mu.shape[1]
    edges = [0, rwkv_cols, rwkv_cols + q_w, rwkv_cols + q_w + kv_w, rwkv_cols + q_w + 2 * kv_w,
             rwkv_cols + q_w + 2 * kv_w + ssm_w, w_in.shape[2]]
    slabbed = (False, True, True, True, False, False)
    splits = tuple(zip(edges[:-1], edges[1:], slabbed))

    x_all = _to_rows(ctx, x)

    cvec = jnp.concatenate([jnp.broadcast_to(c_ctx[None], (n_batch, d_model)), c], axis=0)
    modtab = _modulation(cvec, mod_w, mod_b).reshape(depth, 2, n_batch, 6 * d_model)

    ones_rwkv = _block_diag_ones(min(MXU_DIM, rwkv_w), HEAD_DIM)
    ones_pair = _block_diag_ones(LANES, HEAD_DIM)
    cos, sin_a, sin_b = (jnp.repeat(t, n_batch, axis=0) for t in _rope_tables(n_ctx, n_lat))
    rank = rwkv_w_up.shape[2]
    zpad = jnp.zeros((depth, 2, rank, rwkv_w), F32)

    for l in range(depth):
        mt = modtab[l]
        prm = dict(mu=rwkv_mu[l][None], w0=rwkv_w0[l][:, None], a0=rwkv_a0[l][:, None],
                   w_up=jnp.concatenate([rwkv_w_up[l], zpad[l]], axis=1).astype(BF16),
                   a_up=jnp.concatenate([zpad[l], rwkv_a_up[l]], axis=1).astype(BF16),
                   g_up=rwkv_g_up[l].astype(BF16), k_k=rwkv_k_k[l][None], k_a=rwkv_k_a[l][None],
                   r_k=rwkv_r_k[l][None], ones_bd=ones_rwkv)
        r, v, kk, g, bonus, lw, kd, bv, pq, pk, pv, pc, pg = _in_projection(
            x_all, mt, w_in[l].astype(BF16), splits, prm, n_ctx_rows)
        ysa = _rwkv_scan(r, v, kk, lw, kd, bv, n_batch, n_ctx)
        rwkv = (ysa, g, bonus, rwkv_gn_w[l][None], rwkv_gn_b[l][None], ones_rwkv)

        tile2 = lambda a: jnp.tile(a, 2)[None]
        qh, kh, vh = _attn_prepare(pq, pk, pv, cos, sin_a, sin_b, tile2(attn_q_gain[l]), tile2(attn_k_gain[l]),
                                   ones_pair, n_batch)
        yb = _flash_attention(qh, kh, vh, n_ctx)

        abar_re, abar_im, bf_re, bf_im = _s5_params(ssm_a_re[l], ssm_a_im[l], ssm_log_dt[l], ssm_b_re[l], ssm_b_im[l])
        n_g = ssm_a_re.shape[2]
        n_split = max(1, ssm_w // MXU_DIM)
        gs = n_g // n_split
        diag = lambda blocks: jnp.stack([_block_diag(blocks[h * gs:(h + 1) * gs]) for h in range(n_split)])
        bmat = lambda bf: jnp.stack([diag(bf[d].reshape(SSM_GROUP, n_g, SSM_STATE).transpose(1, 0, 2))
                                     for d in range(2)]).astype(BF16)
        cmat = lambda cc: diag(cc.transpose(0, 2, 1)).astype(BF16)
        ysc = _s5_scan(pc, abar_re, abar_im, bmat(bf_re), bmat(bf_im), cmat(ssm_c_re[l]), cmat(ssm_c_im[l]),
                       n_batch, n_ctx)
        s5 = (ysc, pc, ssm_d[l][None], ssm_glu_w[l].astype(BF16), ssm_glu_b[l][None])

        x_mid = _merge(rwkv, yb, s5, pg, x_all, mt, proj_a[l].astype(BF16), proj_b[l].astype(BF16),
                       proj_c[l].astype(BF16), w_out[l].astype(BF16), ln1_g[l][None], ln1_b[l][None],
                       alpha, n_ctx_rows)
        x_all = _mlp(x_mid, mt, mlp_w1[l].astype(BF16), mlp_w2[l].astype(BF16), ln2_g[l][None], ln2_b[l][None],
                     alpha, n_ctx_rows, final_batch=n_batch if l == depth - 1 else None)

    return x_all
```

```python
import functools
import math

import jax
import jax.numpy as jnp
from jax import lax
from jax.experimental import pallas as pl
from jax.experimental.pallas import tpu as pltpu

HEAD_DIM = 64
GRID_W = 64
ROPE_THETA = 10000.0
LN_EPS = 1e-5
RMS_EPS = 1e-6
GN_EPS = 64e-5
KK_EPS = 1e-24
LAMBDA_RE_MAX = -1e-4
GQA_GROUP = 4
SSM_GROUP = 16
SSM_STATE = 64

LANES = 128
SUBLANES = 8
MXU_DIM = 256
VMEM_LIMIT = 56 * 1024 * 1024

ROW_TILE = 256
MERGE_ROW_TILE = 512
LAYOUT_TT = 64
MLP_ROW_TILE = 1024
MLP_FF_TILE = 2048
SCAN_CHUNK = 64
SCAN_PAIRS = 2
SSM_CHUNK = 64
SSM_LANE_TILE = 256
ATTN_TQ = 256
ATTN_SUB = 256
ATTN_AHEAD = 1
PREP_TT = 128
VT_ROWS = 80

F32 = jnp.float32
BF16 = jnp.bfloat16


def _cparams(*sem):
    return pltpu.CompilerParams(dimension_semantics=sem, vmem_limit_bytes=VMEM_LIMIT)


def _mm(a, b):
    return jnp.dot(a.astype(BF16), b.astype(BF16), preferred_element_type=F32)


def _mm_nt(a, b):
    return lax.dot_general(a.astype(BF16), b.astype(BF16), (((1,), (1,)), ((), ())),
                           preferred_element_type=F32)


def _mm_tn(a, b):
    return lax.dot_general(a.astype(BF16), b.astype(BF16), (((0,), (0,)), ((), ())),
                           preferred_element_type=F32)


def _segsum(x, ones_bd):
    w = ones_bd.shape[0]
    hi = x.astype(BF16)
    lo = (x - hi.astype(F32)).astype(BF16)
    parts = [jnp.dot(hi[:, o:o + w], ones_bd, preferred_element_type=F32)
             + jnp.dot(lo[:, o:o + w], ones_bd, preferred_element_type=F32)
             for o in range(0, x.shape[1], w)]
    return parts[0] if len(parts) == 1 else jnp.concatenate(parts, axis=1)


def _sigmoid(x):
    return 1.0 / (1.0 + jnp.exp(-x))


def _softplus(x):
    return jnp.maximum(x, 0.0) + jnp.log(1.0 + jnp.exp(-jnp.abs(x)))


def _layer_norm(z, g, b):
    mu = jnp.mean(z, axis=-1, keepdims=True)
    zc = z - mu
    var = jnp.mean(zc * zc, axis=-1, keepdims=True)
    return zc * lax.rsqrt(var + LN_EPS) * g + b


def _rows_mod(x, m):
    tm, n = x.shape
    return (x.reshape(tm // SUBLANES, SUBLANES, n) * m[None]).reshape(tm, n)


def _rows_add(x, m):
    tm, n = x.shape
    return (x.reshape(tm // SUBLANES, SUBLANES, n) + m[None]).reshape(tm, n)


def _to_rows_kernel(n_ctx_tiles, ctx_ref, x_ref, o_ref, rows_ref):
    i = pl.program_id(0)
    n_batch, tt, d = x_ref.shape

    def interleave(src_ref):
        for b in range(n_batch):
            for s in range(d // LANES):
                rows_ref[s, pl.ds(b, tt, stride=n_batch), :] = src_ref[b, :, s * LANES:(s + 1) * LANES]
        for s in range(d // LANES):
            o_ref[:, s * LANES:(s + 1) * LANES] = rows_ref[s]

    @pl.when(i < n_ctx_tiles)
    def _():
        interleave(ctx_ref)

    @pl.when(i >= n_ctx_tiles)
    def _():
        interleave(x_ref)


def _to_rows(ctx, x):
    n_batch, n_ctx, d = ctx.shape
    n_lat = x.shape[1]
    tt = math.gcd(LAYOUT_TT, n_ctx)
    n_ctx_tiles = n_ctx // tt
    return pl.pallas_call(
        functools.partial(_to_rows_kernel, n_ctx_tiles),
        out_shape=jax.ShapeDtypeStruct(((n_ctx + n_lat) * n_batch, d), F32),
        grid=((n_ctx + n_lat) // tt,),
        in_specs=[pl.BlockSpec((n_batch, tt, d), lambda i: (0, jnp.minimum(i, n_ctx_tiles - 1), 0)),
                  pl.BlockSpec((n_batch, tt, d), lambda i: (0, jnp.maximum(i - n_ctx_tiles, 0), 0))],
        out_specs=pl.BlockSpec((tt * n_batch, d), lambda i: (i, 0)),
        scratch_shapes=[pltpu.VMEM((d // LANES, tt * n_batch, LANES), F32)],
        compiler_params=_cparams("parallel"),
        name="to_rows",
    )(ctx, x)


def _mod_kernel(c_ref, w_ref, b_ref, o_ref):
    c = c_ref[...]
    s = c * _sigmoid(c)
    o_ref[0] = jnp.dot(s, w_ref[0], preferred_element_type=F32,
                       precision=lax.Precision.HIGHEST) + b_ref[0]


def _modulation(cvec, mod_w, mod_b):
    n_layers, d, n6 = mod_w.shape
    tn = 1024 if n6 % 1024 == 0 else n6
    return pl.pallas_call(
        _mod_kernel,
        out_shape=jax.ShapeDtypeStruct((n_layers, cvec.shape[0], n6), F32),
        grid=(n_layers, n6 // tn),
        in_specs=[pl.BlockSpec((cvec.shape[0], d), lambda l, j: (0, 0)),
                  pl.BlockSpec((1, d, tn), lambda l, j: (l, 0, j)),
                  pl.BlockSpec((1, 1, tn), lambda l, j: (l, 0, j))],
        out_specs=pl.BlockSpec((1, cvec.shape[0], tn), lambda l, j: (l, 0, j)),
        compiler_params=_cparams("parallel", "parallel"),
        name="modulation",
    )(cvec, mod_w, mod_b.reshape(n_layers, 1, n6))


def _inproj_kernel(splits, n_ctx_rows, n_rows, x_ref, xp_ref, xn_ref, sh_ref, sc_ref, w_ref,
                   mu_ref, w0_ref, wup_ref, a0_ref, aup_ref, gup_ref, kk_ref, ka_ref, rk_ref, ones_ref, *o_refs):
    i = pl.program_id(0)
    tm = x_ref.shape[0]
    r0 = i * tm
    seg_start = jnp.logical_or(r0 == 0, r0 == n_ctx_rows)
    seg_end = jnp.logical_or(r0 + tm == n_ctx_rows, r0 + tm == n_rows)
    scale, shift = 1.0 + sc_ref[0], sh_ref[0]
    mod = lambda x: _rows_add(_rows_mod(x, scale), shift)
    xm = mod(x_ref[...])
    x_ext = jnp.concatenate([jnp.where(seg_start, 0.0, mod(xp_ref[...])), xm,
                             jnp.where(seg_end, 0.0, mod(xn_ref[...]))], axis=0).astype(BF16)
    a, b, _ = splits[0]
    n_rwkv = 8
    _rwkv_prep(jnp.dot(x_ext, w_ref[:, a:b], preferred_element_type=F32), mu_ref[...], w0_ref, wup_ref,
               a0_ref, aup_ref, gup_ref[...], kk_ref[...], ka_ref[...], rk_ref[...], ones_ref[...],
               o_refs[:n_rwkv])
    xm = xm.astype(BF16)
    for (a, b, slabbed), o_ref in zip(splits[1:], o_refs[n_rwkv:]):
        y = jnp.dot(xm, w_ref[:, a:b], preferred_element_type=F32)
        if slabbed:
            for j in range((b - a) // LANES):
                o_ref[j] = y[:, j * LANES:(j + 1) * LANES]
        else:
            o_ref[...] = y.astype(o_ref.dtype)


def _in_projection(x_all, modtab, w_bf16, splits, prm, n_ctx_rows):
    rows, d = x_all.shape
    width = prm["k_k"].shape[-1]
    tm = ROW_TILE
    nctx = n_ctx_rows // tm
    hb = tm // SUBLANES
    last = rows // SUBLANES - 1
    seg = lambda i: (i >= nctx).astype(jnp.int32)
    n_slabs = width // LANES
    slab = pl.BlockSpec((n_slabs, tm, LANES), lambda i: (0, i, 0))
    slab2 = pl.BlockSpec((2, n_slabs, tm, LANES), lambda i: (0, 0, i, 0))
    row = pl.BlockSpec((tm, width), lambda i: (i, 0))
    flat = jax.ShapeDtypeStruct((rows, width), BF16)
    one = jax.ShapeDtypeStruct((n_slabs, rows, LANES), F32)
    two = jax.ShapeDtypeStruct((2, n_slabs, rows, LANES), F32)
    shapes = [one, one, one, flat, flat, two, two, two]
    specs = [slab] * 3 + [row] * 2 + [slab2] * 3
    for a, b, slabbed in splits[1:]:
        if slabbed:
            shapes.append(jax.ShapeDtypeStruct(((b - a) // LANES, rows, LANES), F32))
            specs.append(pl.BlockSpec(((b - a) // LANES, tm, LANES), lambda i: (0, i, 0)))
        else:
            shapes.append(jax.ShapeDtypeStruct((rows, b - a), BF16))
            specs.append(pl.BlockSpec((tm, b - a), lambda i: (i, 0)))
    consts = [prm["mu"], prm["w0"], prm["w_up"], prm["a0"], prm["a_up"], prm["g_up"],
              prm["k_k"], prm["k_a"], prm["r_k"], prm["ones_bd"]]
    full = lambda a: pl.BlockSpec(a.shape, lambda i: (0,) * a.ndim)
    return pl.pallas_call(
        functools.partial(_inproj_kernel, splits, n_ctx_rows, rows),
        out_shape=shapes,
        grid=(rows // tm,),
        in_specs=[pl.BlockSpec((tm, d), lambda i: (i, 0)),
                  pl.BlockSpec((SUBLANES, d), lambda i: (jnp.maximum(i * hb - 1, 0), 0)),
                  pl.BlockSpec((SUBLANES, d), lambda i: (jnp.minimum((i + 1) * hb, last), 0)),
                  pl.BlockSpec((1, SUBLANES, d), lambda i: (seg(i), 0, 0)),
                  pl.BlockSpec((1, SUBLANES, d), lambda i: (seg(i), 0, 1)),
                  pl.BlockSpec(w_bf16.shape, lambda i: (0, 0), pipeline_mode=pl.Buffered(1))]
                 + [full(a) for a in consts],
        out_specs=specs,
        compiler_params=_cparams("parallel"),
        name="in_projection",
    )(x_all, x_all, x_all, modtab, modtab, w_bf16, *consts)


def _rwkv_prep(p_ext, mu, w0_ref, wup_ref, a0_ref, aup_ref, g_up, k_k, k_a, r_k, ones_bd, outs):
    r_out, v_out, kkn_out, g_out, bonus_out, lw_out, kd_out, bv_out = outs
    tm = p_ext.shape[0] - 2 * SUBLANES
    p = p_ext[SUBLANES:SUBLANES + tm]
    p = p + mu * (0.5 * (p_ext[0:tm] + p_ext[2 * SUBLANES:]) - p)

    w = k_k.shape[-1]
    r, k, v = p[:, 0:w], p[:, w:2 * w], p[:, 2 * w:3 * w]
    wa = p[:, 3 * w:3 * w + LANES]
    gd = p[:, 3 * w + LANES:3 * w + 2 * LANES]

    def put(o_ref, y, *lead):
        for j in range(w // LANES):
            o_ref[lead + (j,)] = y[:, j * LANES:(j + 1) * LANES]

    g_out[...] = _mm(_sigmoid(gd), g_up).astype(g_out.dtype)
    kk = k * k_k
    kk = kk * lax.rsqrt(jnp.maximum(_segsum(kk * kk, ones_bd), KK_EPS))
    put(r_out, r)
    put(v_out, v)
    put(kkn_out, kk)
    tanh_wa = jnp.tanh(wa)
    rk_sum = jnp.zeros_like(r)
    for d in range(2):
        w_pre = w0_ref[d] + _mm(tanh_wa, wup_ref[d])
        put(lw_out, -jnp.exp(-_softplus(-w_pre) - 0.5), d)
        a = _sigmoid(a0_ref[d] + _mm(wa, aup_ref[d]))
        k_dir = k * (1.0 + (a - 1.0) * k_a)
        put(kd_out, k_dir, d)
        put(bv_out, kk * a, d)
        rk_sum = rk_sum + r * k_dir
    bonus_out[...] = (_segsum(rk_sum * r_k, ones_bd) * v).astype(bonus_out.dtype)


def _scan_block(d, c, n_ctx_blk, n_blk):
    bwd = jnp.where(c < n_ctx_blk, n_ctx_blk - 1 - c, n_ctx_blk + (n_blk - 1 - c))
    return jnp.where(d == 0, c, bwd)


def _tri_inverse(a_list, ri, ci, n):
    eye = (ri == ci).astype(F32)
    same = (ri // SUBLANES) == (ci // SUBLANES)
    xs = [-jnp.where(same, a, 0.0) for a in a_list]
    x2 = [_mm(x, x) for x in xs]
    x4 = [_mm(x, x) for x in x2]
    ts = [eye + x for x in xs]
    ts = [t + _mm(t, x) for t, x in zip(ts, x2)]
    ts = [t + _mm(t, x) for t, x in zip(ts, x4)]
    size = SUBLANES
    while size < n:
        pick = jnp.logical_and((ri // (2 * size)) == (ci // (2 * size)), (ri // size) != (ci // size))
        lows = [_mm(jnp.where(pick, a, 0.0), t) for a, t in zip(a_list, ts)]
        ts = [t - _mm(t, low) for t, low in zip(ts, lows)]
        size *= 2
    return ts


def _rwkv_scan_kernel(n_batch, r_ref, v_ref, kk_ref, lw_ref, kd_ref, bv_ref, y_ref, state_ref):
    d = pl.program_id(0)
    c = pl.program_id(2)
    n = r_ref.shape[1] // n_batch
    m = 2 * n
    chains = [(p, b) for p in range(r_ref.shape[0]) for b in range(n_batch)]
    batches = range(len(chains))

    @pl.when(c == 0)
    def _():
        state_ref[...] = jnp.zeros_like(state_ref)

    sign = 1 - 2 * d
    ri = lax.broadcasted_iota(jnp.int32, (m, m), 0)
    ci = lax.broadcasted_iota(jnp.int32, (m, m), 1)
    same_head = (ri // n) == (ci // n)
    order = (ri - ci) * sign
    strict = jnp.logical_and(same_head, order > 0)
    incl = jnp.logical_and(same_head, order >= 0)
    ti = lax.broadcasted_iota(jnp.int32, (n, n), 0)
    si = lax.broadcasted_iota(jnp.int32, (n, n), 1)
    tri = ((ti - si) * sign >= 0).astype(BF16)
    head0 = lax.broadcasted_iota(jnp.int32, (n, LANES), 1) < HEAD_DIM
    blockdiag = (lax.broadcasted_iota(jnp.int32, (LANES, LANES), 0) // HEAD_DIM
                 == lax.broadcasted_iota(jnp.int32, (LANES, LANES), 1) // HEAD_DIM)

    def stack(x):
        return jnp.concatenate([jnp.where(head0, x, 0.0), jnp.where(head0, 0.0, x)], axis=0)

    def shared(ref, i):
        p, b = chains[i]
        return ref[p, pl.ds(b, n, stride=n_batch), :]

    def split(ref, i):
        p, b = chains[i]
        return ref[0, p, pl.ds(b, n, stride=n_batch), :]

    duos = range(0, len(chains), 2)
    cat = lambda xs, i: jnp.concatenate([xs[i], xs[i + 1]], axis=1)
    uncat = lambda xs: [x[:, o:o + LANES] for x in xs for o in (0, LANES)]

    lw = [split(lw_ref, b) for b in batches]
    hi = [x.astype(BF16) for x in lw]
    lo = [(x - h.astype(F32)).astype(BF16) for x, h in zip(lw, hi)]
    cum = uncat([jnp.dot(tri, cat(hi, i), preferred_element_type=F32)
                 + jnp.dot(tri, cat(lo, i), preferred_element_type=F32) for i in duos])
    e_cum = [jnp.exp(x) for x in cum]
    e_neg = [jnp.exp(-x) for x in cum]
    kap = [shared(kk_ref, b) * jnp.exp(cum[b] - lw[b]) for b in batches]
    rt = [shared(r_ref, b) * e_cum[b] for b in batches]
    bt = [split(bv_ref, b) * e_neg[b] for b in batches]
    kt = [split(kd_ref, b) * e_neg[b] for b in batches]
    v = [shared(v_ref, b) for b in batches]
    total = [jnp.where(d == 0, e[n - 1:n], e[0:1]) for e in e_cum]
    state = [state_ref[b] for b in batches]

    scores = [_mm_nt(jnp.concatenate([stack(kap[b]), stack(rt[b])], axis=0),
                     jnp.concatenate([stack(bt[b]), stack(kt[b])], axis=0)) for b in batches]
    a_b = [jnp.where(strict, s[0:m, 0:m], 0.0) for s in scores]
    a_k = [jnp.where(strict, s[0:m, m:2 * m], 0.0) for s in scores]
    a_r = [jnp.concatenate([jnp.where(incl, s[m:2 * m, m:2 * m], 0.0),
                            jnp.where(incl, -s[m:2 * m, 0:m], 0.0)], axis=1) for s in scores]
    g0 = [_mm_nt(jnp.concatenate([kap[b], rt[b]], axis=0), state[b]) for b in batches]
    t_inv = _tri_inverse(a_b, ri, ci, n)
    vs = [stack(x) for x in v]
    rhs = [stack(g0[b][0:n]) + _mm(a_k[b], vs[b]) for b in batches]
    us = [_mm(t_inv[b], rhs[b]) for b in batches]
    ys = [stack(g0[b][n:m]) + _mm(a_r[b], jnp.concatenate([vs[b], us[b]], axis=0)) for b in batches]
    for i in batches:
        p, b = chains[i]
        y_ref[0, p, pl.ds(b, n, stride=n_batch), :] = ys[i][0:n] + ys[i][n:m]
    ds = [_mm_tn(jnp.concatenate([-(us[b][0:n] + us[b][n:m]), v[b]], axis=0),
                 jnp.concatenate([bt[b], kt[b]], axis=0)) for b in batches]
    for b in batches:
        state_ref[b] = (state[b] + jnp.where(blockdiag, ds[b], 0.0)) * total[b]


def _rwkv_scan(r, v, kk, lw, kd, bv, n_batch, n_ctx):
    n_slabs, rows, _ = r.shape
    tm = SCAN_CHUNK * n_batch
    pp = min(SCAN_PAIRS, n_slabs)
    n_blk, n_ctx_blk = rows // tm, n_ctx * n_batch // tm
    blk = lambda d, c: _scan_block(d, c, n_ctx_blk, n_blk)
    shared = pl.BlockSpec((pp, tm, LANES), lambda d, p, c: (p, blk(d, c), 0))
    split = pl.BlockSpec((1, pp, tm, LANES), lambda d, p, c: (d, p, blk(d, c), 0))
    return pl.pallas_call(
        functools.partial(_rwkv_scan_kernel, n_batch),
        out_shape=jax.ShapeDtypeStruct((2, n_slabs, rows, LANES), F32),
        grid=(2, n_slabs // pp, n_blk),
        in_specs=[shared, shared, shared, split, split, split],
        out_specs=split,
        scratch_shapes=[pltpu.VMEM((pp * n_batch, LANES, LANES), F32)],
        compiler_params=_cparams("parallel", "parallel", "arbitrary"),
        name="rwkv_scan",
    )(r, v, kk, lw, kd, bv)


def _rwkv_readout(y_ref, g, bonus, gn_w, gn_b, ones_bd):
    y = jnp.concatenate([y_ref[0, j] + y_ref[1, j] for j in range(y_ref.shape[1])], axis=1)
    inv = 1.0 / HEAD_DIM
    mean = _segsum(y, ones_bd) * inv
    yc = y - mean
    var = _segsum(yc * yc, ones_bd) * inv
    yn = yc * lax.rsqrt(var + GN_EPS) * gn_w + gn_b
    return (yn + bonus) * g


def _norm_rope(x, gain, cos, sin_a, sin_b, ones_bd):
    ms = _segsum(x * x, ones_bd) * (1.0 / HEAD_DIM)
    xn = x * lax.rsqrt(ms + RMS_EPS) * gain
    return (xn * cos + pltpu.roll(xn, LANES - HEAD_DIM // 4, 1) * sin_a
            + pltpu.roll(xn, HEAD_DIM // 4, 1) * sin_b)


def _attn_prep_kernel(scale, n_batch, q_ref, k_ref, v_ref, cos_ref, sa_ref, sb_ref, qg_ref, kg_ref, ones_ref,
                      qo_ref, ko_ref, vo_ref, scr):
    cos, sin_a, sin_b = cos_ref[...], sa_ref[...], sb_ref[...]
    ones_bd = ones_ref[...]
    tm = cos.shape[0]
    tt = tm // n_batch
    first = lax.broadcasted_iota(jnp.int32, (tm, LANES), 1) < HEAD_DIM
    n_q, n_k = q_ref.shape[0], k_ref.shape[0]
    for j in range(n_q):
        scr[j] = _norm_rope(q_ref[j], qg_ref[...], cos, sin_a, sin_b, ones_bd) * scale
    for j in range(n_k):
        y = _norm_rope(k_ref[j], kg_ref[...], cos, sin_a, sin_b, ones_bd)
        swapped = pltpu.roll(y, HEAD_DIM, 1)
        scr[n_q + 2 * j] = jnp.where(first, y, swapped)
        scr[n_q + 2 * j + 1] = jnp.where(first, swapped, y)
    ones_rows = jnp.ones((vo_ref.shape[3] - HEAD_DIM, LANES), BF16)
    for b in range(n_batch):
        rows = pl.ds(b, tt, stride=n_batch)
        for j in range(n_q):
            qo_ref[b, j] = scr[j, rows, :].astype(BF16)
        for h in range(2 * n_k):
            ko_ref[b, h] = scr[n_q + h, rows, :].astype(BF16)
        for j in range(n_k):
            vb = v_ref[j, rows, :]
            for g in range(tt // LANES):
                vt = vb[g * LANES:(g + 1) * LANES].T.astype(BF16)
                for h in range(2):
                    vo_ref[b, 2 * j + h, g, 0:HEAD_DIM] = vt[h * HEAD_DIM:(h + 1) * HEAD_DIM]
                    vo_ref[b, 2 * j + h, g, HEAD_DIM:] = ones_rows


def _attn_prepare(pq, pk, pv, cos, sin_a, sin_b, q_gain, k_gain, ones_pair, n_batch):
    q_slabs, rows, _ = pq.shape
    k_slabs = pk.shape[0]
    tt_all = rows // n_batch
    tt = PREP_TT
    tm = tt * n_batch
    kh = 2 * k_slabs
    tab = pl.BlockSpec((tm, LANES), lambda i: (i, 0))
    vec = pl.BlockSpec((1, LANES), lambda i: (0, 0))
    return pl.pallas_call(
        functools.partial(_attn_prep_kernel, HEAD_DIM ** -0.5 * math.log2(math.e), n_batch),
        out_shape=[jax.ShapeDtypeStruct((n_batch, q_slabs, tt_all, LANES), BF16),
                   jax.ShapeDtypeStruct((n_batch, kh, tt_all, LANES), BF16),
                   jax.ShapeDtypeStruct((n_batch, kh, tt_all // LANES, VT_ROWS, LANES), BF16)],
        grid=(tt_all // tt,),
        in_specs=[pl.BlockSpec((q_slabs, tm, LANES), lambda i: (0, i, 0)),
                  pl.BlockSpec((k_slabs, tm, LANES), lambda i: (0, i, 0)),
                  pl.BlockSpec((k_slabs, tm, LANES), lambda i: (0, i, 0)),
                  tab, tab, tab, vec, vec,
                  pl.BlockSpec(ones_pair.shape, lambda i: (0, 0))],
        out_specs=[pl.BlockSpec((n_batch, q_slabs, tt, LANES), lambda i: (0, 0, i, 0)),
                   pl.BlockSpec((n_batch, kh, tt, LANES), lambda i: (0, 0, i, 0)),
                   pl.BlockSpec((n_batch, kh, tt // LANES, VT_ROWS, LANES), lambda i: (0, 0, i, 0, 0))],
        scratch_shapes=[pltpu.VMEM((q_slabs + kh, tm, LANES), F32)],
        compiler_params=_cparams("parallel"),
        name="attn_prepare",
    )(pq, pk, pv, cos, sin_a, sin_b, q_gain, k_gain, ones_pair)


def _flash_kernel(n_ctx, sub_ctx, sub_lat, q_ref, k_ref, vt_ref, o_ref):
    qi = pl.program_id(2)
    n_pairs, tq = q_ref.shape[1], q_ref.shape[2]
    n_keys = k_ref.shape[2]
    first = lax.broadcasted_iota(jnp.int32, (tq, LANES), 1) < HEAD_DIM
    zero = jnp.zeros((tq, LANES), BF16)
    q = jnp.concatenate([jnp.where(first if h == 0 else jnp.logical_not(first), q_ref[0, p], zero)
                         for p in range(n_pairs) for h in range(2)], axis=0)
    q_t = q.astype(F32).T.astype(BF16)

    def scores(blk):
        start, size = blk
        return jnp.dot(k_ref[0, 0, start:start + size, :], q_t, preferred_element_type=F32)

    def weighted_values(p, blk):
        start, size = blk
        vt = jnp.concatenate([vt_ref[0, 0, start // LANES + g] for g in range(size // LANES)], axis=1)
        return jnp.dot(vt, p, preferred_element_type=F32)

    def attend(blocks):
        ahead = [scores(b) for b in blocks[:ATTN_AHEAD]]
        m_prev = acc = pending = None
        for i, blk in enumerate(blocks):
            s = ahead.pop(0)
            if i + ATTN_AHEAD < len(blocks):
                ahead.append(scores(blocks[i + ATTN_AHEAD]))
            if pending is not None:
                alpha, p, pblk = pending
                pv = weighted_values(p, pblk)
                acc = pv if acc is None else alpha * acc + pv
            m_new = jnp.max(s, axis=0, keepdims=True)
            alpha = None
            if m_prev is not None:
                m_new = jnp.maximum(m_prev, m_new)
                alpha = jnp.exp2(m_prev - m_new)
            pending = (alpha, jnp.exp2(s - m_new).astype(BF16), blk)
            m_prev = m_new
        alpha, p, pblk = pending
        pv = weighted_values(p, pblk)
        acc = pv if acc is None else alpha * acc + pv
        out_t = acc[0:HEAD_DIM] / acc[HEAD_DIM:HEAD_DIM + 1]
        for p in range(n_pairs):
            pair_t = jnp.concatenate([out_t[:, (2 * p) * tq:(2 * p + 1) * tq],
                                      out_t[:, (2 * p + 1) * tq:(2 * p + 2) * tq]], axis=0)
            o_ref[0, p] = pair_t.T.astype(o_ref.dtype)

    is_ctx = qi * tq < n_ctx
    ctx_blocks = [(s0, sub_ctx) for s0 in range(0, n_ctx, sub_ctx)]
    lat_blocks = [(s0, sub_lat) for s0 in range(n_ctx, n_keys, sub_lat)]

    @pl.when(is_ctx)
    def _():
        attend(ctx_blocks)

    @pl.when(jnp.logical_not(is_ctx))
    def _():
        attend(ctx_blocks + lat_blocks)


def _flash_attention(qh, kh, vt, n_ctx):
    n_batch, q_slabs, tt_all, _ = qh.shape
    n_kvh = kh.shape[1]
    n_pairs = q_slabs // n_kvh
    tq = min(ATTN_TQ, n_ctx)
    sub_ctx = min(ATTN_SUB, n_ctx)
    sub_lat = min(ATTN_SUB, tt_all - n_ctx)
    assert n_ctx % tq == 0 and n_ctx % sub_ctx == 0 and (tt_all - n_ctx) % sub_lat == 0
    assert sub_ctx % LANES == 0 and sub_lat % LANES == 0 and tq % LANES == 0
    return pl.pallas_call(
        functools.partial(_flash_kernel, n_ctx, sub_ctx, sub_lat),
        out_shape=jax.ShapeDtypeStruct((n_batch, q_slabs, tt_all, LANES), BF16),
        grid=(n_batch, n_kvh, tt_all // tq),
        in_specs=[pl.BlockSpec((1, n_pairs, tq, LANES), lambda b, h, qi: (b, h, qi, 0)),
                  pl.BlockSpec((1, 1, tt_all, LANES), lambda b, h, qi: (b, h, 0, 0)),
                  pl.BlockSpec((1, 1, tt_all // LANES, VT_ROWS, LANES), lambda b, h, qi: (b, h, 0, 0, 0))],
        out_specs=pl.BlockSpec((1, n_pairs, tq, LANES), lambda b, h, qi: (b, h, qi, 0)),
        compiler_params=_cparams("parallel", "parallel", "parallel"),
        name="flash_attention",
    )(qh, kh, vt)


def _s5_param_kernel(are_ref, aim_ref, ldt_ref, bre_ref, bim_ref, abar_re, abar_im, bf_re, bf_im):
    lam_re = jnp.minimum(are_ref[0], LAMBDA_RE_MAX)
    lam_im = aim_ref[0]
    dt = jnp.exp(ldt_ref[0])
    mag = jnp.exp(lam_re * dt)
    ar, ai = mag * jnp.cos(lam_im * dt), mag * jnp.sin(lam_im * dt)
    nr, ni = ar - 1.0, ai
    den = lam_re * lam_re + lam_im * lam_im
    cr = (nr * lam_re + ni * lam_im) / den
    cim = (ni * lam_re - nr * lam_im) / den
    abar_re[0], abar_im[0] = ar, ai
    bre, bim = bre_ref[...], bim_ref[...]
    bf_re[0] = cr * bre - cim * bim
    bf_im[0] = cr * bim + cim * bre


def _s5_params(a_re, a_im, log_dt, b_re, b_im):
    _, n_g, n_s = a_re.shape
    n_i = b_re.shape[-1]
    gn = n_g * n_s
    flat = lambda a: a.reshape(2, 1, gn)
    dt = jnp.broadcast_to(log_dt[:, :, None], (2, n_g, n_s))
    bt = lambda b: b.reshape(gn, n_i).T
    vec = pl.BlockSpec((1, 1, gn), lambda d: (d, 0, 0))
    mat = pl.BlockSpec((n_i, gn), lambda d: (0, 0))
    omat = pl.BlockSpec((1, n_i, gn), lambda d: (d, 0, 0))
    return pl.pallas_call(
        _s5_param_kernel,
        out_shape=[jax.ShapeDtypeStruct((2, 1, gn), F32)] * 2 + [jax.ShapeDtypeStruct((2, n_i, gn), F32)] * 2,
        grid=(2,),
        in_specs=[vec, vec, vec, mat, mat],
        out_specs=[vec, vec, omat, omat],
        compiler_params=_cparams("parallel"),
        name="s5_params",
    )(flat(a_re), flat(a_im), flat(dt), bt(b_re), bt(b_im))


def _s5_scan_kernel(u_ref, ar_ref, ai_ref, bre_ref, bim_ref, cre_ref, cim_ref, y_ref,
                    xr_ref, xi_ref, hr_ref, hi_ref):
    d = pl.program_id(0)
    c = pl.program_id(1)
    steps = u_ref.shape[0] // SUBLANES

    @pl.when(c == 0)
    def _():
        hr_ref[...] = jnp.zeros_like(hr_ref)
        hi_ref[...] = jnp.zeros_like(hi_ref)

    n_split, kw, sw = bre_ref.shape[1], bre_ref.shape[2], bre_ref.shape[3]
    lt = min(SSM_LANE_TILE, sw)
    tiles = [(h, o) for h in range(n_split) for o in range(0, sw, lt)]
    u = u_ref[...].astype(BF16)

    def drive(tile):
        h, o = tile
        uh = u[:, h * kw:(h + 1) * kw]
        sl = slice(h * sw + o, h * sw + o + lt)
        xr_ref[:, sl] = jnp.dot(uh, bre_ref[0, h, :, o:o + lt], preferred_element_type=F32)
        xi_ref[:, sl] = jnp.dot(uh, bim_ref[0, h, :, o:o + lt], preferred_element_type=F32)

    def scan(tile, reverse):
        h, o = tile
        sl = slice(h * sw + o, h * sw + o + lt)
        ar = jnp.broadcast_to(ar_ref[0, :, sl], (SUBLANES, lt))
        ai = jnp.broadcast_to(ai_ref[0, :, sl], (SUBLANES, lt))
        hr, hi = hr_ref[:, sl], hi_ref[:, sl]
        for i in range(steps):
            t = steps - 1 - i if reverse else i
            rows = slice(t * SUBLANES, (t + 1) * SUBLANES)
            hr, hi = ar * hr - ai * hi + xr_ref[rows, sl], ar * hi + ai * hr + xi_ref[rows, sl]
            xr_ref[rows, sl] = hr
            xi_ref[rows, sl] = hi
        hr_ref[:, sl] = hr
        hi_ref[:, sl] = hi

    def readout(tile):
        h, o = tile
        sl = slice(h * sw + o, h * sw + o + lt)
        return (jnp.dot(xr_ref[:, sl].astype(BF16), cre_ref[h, o:o + lt, :], preferred_element_type=F32)
                - jnp.dot(xi_ref[:, sl].astype(BF16), cim_ref[h, o:o + lt, :], preferred_element_type=F32))

    def run(reverse):
        y = [None] * n_split
        drive(tiles[0])
        for n, tile in enumerate(tiles):
            if n + 1 < len(tiles):
                drive(tiles[n + 1])
            if n > 0:
                ph = tiles[n - 1][0]
                part = readout(tiles[n - 1])
                y[ph] = part if y[ph] is None else y[ph] + part
            scan(tile, reverse)
        ph = tiles[-1][0]
        part = readout(tiles[-1])
        y[ph] = part if y[ph] is None else y[ph] + part
        for h in range(n_split):
            y_ref[0, :, h * kw:(h + 1) * kw] = y[h].astype(y_ref.dtype)

    @pl.when(d == 0)
    def _():
        run(False)

    @pl.when(d == 1)
    def _():
        run(True)


def _s5_scan(u, abar_re, abar_im, b_re, b_im, c_re, c_im, n_batch, n_ctx):
    rows, width = u.shape
    gn = abar_re.shape[-1]
    tm = SSM_CHUNK * n_batch
    n_blk, n_ctx_blk = rows // tm, n_ctx * n_batch // tm
    blk = lambda d, c: _scan_block(d, c, n_ctx_blk, n_blk)
    vec = pl.BlockSpec((1, 1, gn), lambda d, c: (d, 0, 0))
    bmat = pl.BlockSpec((1,) + b_re.shape[1:], lambda d, c: (d, 0, 0, 0))
    cmat = pl.BlockSpec(c_re.shape, lambda d, c: (0, 0, 0))
    return pl.pallas_call(
        _s5_scan_kernel,
        out_shape=jax.ShapeDtypeStruct((2, rows, width), BF16),
        grid=(2, n_blk),
        in_specs=[pl.BlockSpec((tm, width), lambda d, c: (blk(d, c), 0)), vec, vec, bmat, bmat, cmat, cmat],
        out_specs=pl.BlockSpec((1, tm, width), lambda d, c: (d, blk(d, c), 0)),
        scratch_shapes=[pltpu.VMEM((tm, gn), F32), pltpu.VMEM((tm, gn), F32),
                        pltpu.VMEM((SUBLANES, gn), F32), pltpu.VMEM((SUBLANES, gn), F32)],
        compiler_params=_cparams("parallel", "arbitrary"),
        name="s5_scan",
    )(u, abar_re, abar_im, b_re, b_im, c_re, c_im)


def _gelu_tanh(x):
    return 0.5 * x * (1.0 + jnp.tanh(math.sqrt(2.0 / math.pi) * (x + 0.044715 * (x * x * x))))


def _s5_readout(y_ref, u, d_skip, glu_w, glu_b):
    y = _gelu_tanh(y_ref[0].astype(F32) + y_ref[1].astype(F32) + d_skip * u)
    return y * _sigmoid(jnp.dot(y.astype(BF16), glu_w, preferred_element_type=F32) + glu_b)


def _merge_kernel(alpha, ysa_ref, g_ref, bonus_ref, gnw_ref, gnb_ref, ones_ref,
                  yb_ref, ysc_ref, u_ref, dskip_ref, gluw_ref, glub_ref,
                  pg_ref, x_ref, gt_ref, pa_ref, pb_ref, pc_ref, wo_ref, lng_ref, lnb_ref, o_ref, yb_rows):
    d = x_ref.shape[1]
    ya = _rwkv_readout(ysa_ref, g_ref[...].astype(F32), bonus_ref[...].astype(F32), gnw_ref[...], gnb_ref[...],
                       ones_ref[...])
    yc = _s5_readout(ysc_ref, u_ref[...].astype(F32), dskip_ref[...], gluw_ref[...], glub_ref[...])
    n_batch, n_slabs, tt = yb_ref.shape[0], yb_ref.shape[1], yb_ref.shape[2]
    for b in range(n_batch):
        for j in range(n_slabs):
            yb_rows[j, pl.ds(b, tt, stride=n_batch), :] = yb_ref[b, j].astype(F32)
    yb = jnp.concatenate([yb_rows[j].astype(BF16) for j in range(n_slabs)], axis=1)
    pg = pg_ref[...].astype(F32)
    merged = (_sigmoid(pg[:, 0:d]) * _mm(ya, pa_ref[...])
              + _sigmoid(pg[:, d:2 * d]) * _mm(yb, pb_ref[...])
              + _sigmoid(pg[:, 2 * d:3 * d]) * _mm(yc, pc_ref[...]))
    mix = _mm(merged, wo_ref[...])
    o_ref[...] = _layer_norm(alpha * x_ref[...] + _rows_mod(mix, gt_ref[0]), lng_ref[...], lnb_ref[...])


def _merge(rwkv, yb, s5, pg, x_all, modtab, proj_a, proj_b, proj_c, w_out, ln_g, ln_b, alpha, n_ctx_rows):
    rows, d = x_all.shape
    n_batch, n_slabs = yb.shape[0], yb.shape[1]
    ysa, g, bonus, gn_w, gn_b, ones_bd = rwkv
    ysc, u, d_skip, glu_w, glu_b = s5
    tm = min(MERGE_ROW_TILE, n_ctx_rows)
    nctx = n_ctx_rows // tm
    seg = lambda i: (i >= nctx).astype(jnp.int32)
    row = lambda a: pl.BlockSpec((tm, a.shape[1]), lambda i: (i, 0))
    full = lambda a: pl.BlockSpec(a.shape, lambda i: (0, 0), pipeline_mode=pl.Buffered(1))
    vec = lambda a: pl.BlockSpec((1, a.shape[1]), lambda i: (0, 0))
    return pl.pallas_call(
        functools.partial(_merge_kernel, alpha),
        out_shape=jax.ShapeDtypeStruct((rows, d), F32),
        grid=(rows // tm,),
        scratch_shapes=[pltpu.VMEM((n_slabs, tm, LANES), F32)],
        in_specs=[pl.BlockSpec((2, ysa.shape[1], tm, LANES), lambda i: (0, 0, i, 0)),
                  row(g), row(bonus), vec(gn_w), vec(gn_b), full(ones_bd),
                  pl.BlockSpec((n_batch, n_slabs, tm // n_batch, LANES), lambda i: (0, 0, i, 0)),
                  pl.BlockSpec((2, tm, u.shape[1]), lambda i: (0, i, 0)), row(u), vec(d_skip), full(glu_w), vec(glu_b),
                  row(pg), row(x_all),
                  pl.BlockSpec((1, SUBLANES, d), lambda i: (seg(i), 0, 2)),
                  full(proj_a), full(proj_b), full(proj_c), full(w_out), vec(ln_g), vec(ln_b)],
        out_specs=pl.BlockSpec((tm, d), lambda i: (i, 0)),
        compiler_params=_cparams("parallel"),
        name="merge_ln",
    )(ysa, g, bonus, gn_w, gn_b, ones_bd, yb, ysc, u, d_skip, glu_w, glu_b,
      pg, x_all, modtab, proj_a, proj_b, proj_c, w_out, ln_g, ln_b)


def _mlp_kernel(alpha, x_ref, sh_ref, sc_ref, gt_ref, w1_ref, w2_ref, g_ref, b_ref, o_ref, h_ref, acc_ref,
                *split_ref):
    j = pl.program_id(1)

    @pl.when(j == 0)
    def _():
        h_ref[...] = _rows_add(_rows_mod(x_ref[...], 1.0 + sc_ref[0]), sh_ref[0]).astype(BF16)
        acc_ref[...] = jnp.zeros_like(acc_ref)

    a = jnp.maximum(jnp.dot(h_ref[...], w1_ref[...], preferred_element_type=F32), 0.0)
    acc_ref[...] += jnp.dot((a * a).astype(BF16), w2_ref[...], preferred_element_type=F32)

    @pl.when(j == pl.num_programs(1) - 1)
    def _():
        y = _layer_norm(alpha * x_ref[...] + _rows_mod(acc_ref[...], gt_ref[0]), g_ref[...], b_ref[...])
        if not split_ref:
            o_ref[...] = y
        else:
            rows_ref, = split_ref
            n_batch, tt, d = o_ref.shape
            for s in range(d // LANES):
                rows_ref[s] = y[:, s * LANES:(s + 1) * LANES]
            for b in range(n_batch):
                for s in range(d // LANES):
                    o_ref[b, :, s * LANES:(s + 1) * LANES] = rows_ref[s, pl.ds(b, tt, stride=n_batch), :]


def _mlp(x_mid, modtab, w1, w2, ln_g, ln_b, alpha, n_ctx_rows, final_batch=None):
    rows, d = x_mid.shape
    ff = w1.shape[1]
    tm = min(MLP_ROW_TILE, n_ctx_rows)
    tf = min(MLP_FF_TILE, ff)
    nctx = n_ctx_rows // tm
    first = nctx if final_batch else 0
    seg = lambda i: (i + first >= nctx).astype(jnp.int32)
    mod = lambda col: pl.BlockSpec((1, SUBLANES, d), lambda i, j: (seg(i), 0, col))
    vec = pl.BlockSpec((1, d), lambda i, j: (0, 0))
    scratch = [pltpu.VMEM((tm, d), BF16), pltpu.VMEM((tm, d), F32)]
    if final_batch:
        out_shape = jax.ShapeDtypeStruct((final_batch, (rows - n_ctx_rows) // final_batch, d), F32)
        out_spec = pl.BlockSpec((final_batch, tm // final_batch, d), lambda i, j: (0, i, 0))
        scratch.append(pltpu.VMEM((d // LANES, tm, LANES), F32))
    else:
        out_shape = jax.ShapeDtypeStruct((rows, d), F32)
        out_spec = pl.BlockSpec((tm, d), lambda i, j: (i, 0))
    return pl.pallas_call(
        functools.partial(_mlp_kernel, alpha),
        out_shape=out_shape,
        grid=(rows // tm - first, ff // tf),
        in_specs=[pl.BlockSpec((tm, d), lambda i, j: (i + first, 0)), mod(3), mod(4), mod(5),
                  pl.BlockSpec((d, tf), lambda i, j: (0, j)),
                  pl.BlockSpec((tf, d), lambda i, j: (j, 0)), vec, vec],
        out_specs=out_spec,
        scratch_shapes=scratch,
        compiler_params=_cparams("parallel", "arbitrary"),
        name="mlp_ln",
    )(x_mid, modtab, modtab, modtab, w1, w2, ln_g, ln_b)


def _block_diag_ones(n, blk):
    i = jnp.arange(n) // blk
    return (i[:, None] == i[None, :]).astype(BF16)


def _rope_tables(n_ctx, n_lat):
    pairs = HEAD_DIM // 4
    rows = n_lat // GRID_W
    row = jnp.repeat(jnp.arange(rows, dtype=F32), GRID_W)
    col = jnp.tile(jnp.arange(GRID_W, dtype=F32), rows)
    inv = ROPE_THETA ** (-jnp.arange(pairs, dtype=F32) / pairs)
    ang = jnp.stack([row, col], axis=-1)[:, :, None] * inv
    ang = jnp.broadcast_to(ang[:, :, None, :], (n_lat, 2, 2, pairs)).reshape(n_lat, HEAD_DIM)
    cos = jnp.concatenate([jnp.ones((n_ctx, HEAD_DIM), F32), jnp.cos(ang)], axis=0)
    sin = jnp.concatenate([jnp.zeros((n_ctx, HEAD_DIM), F32), jnp.sin(ang)], axis=0)
    first = (jnp.arange(HEAD_DIM) % (2 * pairs)) < pairs
    sin_a = jnp.where(first, -sin, 0.0)
    sin_b = jnp.where(first, 0.0, sin)
    two = lambda a: jnp.concatenate([a, a], axis=1)
    return two(cos), two(sin_a), two(sin_b)


def _block_diag(blocks):
    g, a, b = blocks.shape
    eye = jnp.eye(g, dtype=blocks.dtype)
    return (eye[:, None, :, None] * blocks[:, :, None, :]).reshape(g * a, g * b)


def kernel(x, c, ctx, c_ctx, mod_w, mod_b, w_in, rwkv_mu, rwkv_w0, rwkv_w_up, rwkv_a0, rwkv_a_up, rwkv_g_up, rwkv_k_k, rwkv_k_a, rwkv_r_k, rwkv_gn_w, rwkv_gn_b, attn_q_gain, attn_k_gain, ssm_a_re, ssm_a_im, ssm_log_dt, ssm_b_re, ssm_b_im, ssm_c_re, ssm_c_im, ssm_d, ssm_glu_w, ssm_glu_b, proj_a, proj_b, proj_c, w_out, ln1_g, ln1_b, ln2_g, ln2_b, mlp_w1, mlp_w2):
    n_batch, n_lat, d_model = x.shape
    n_ctx = ctx.shape[1]
    depth = mod_w.shape[0]
    assert n_batch == SUBLANES, "row layout packs the batch into one sublane tile"
    alpha = (2 * depth) ** 0.25
    n_ctx_rows = n_ctx * n_batch

    rwkv_w = rwkv_k_k.shape[1]
    q_w = proj_b.shape[1]
    kv_w = q_w // GQA_GROUP
    ssm_w = ssm_d.shape[1]
    rwkv_cols = rwkv_mu.shape[1]
    edges = [0, rwkv_cols, rwkv_cols + q_w, rwkv_cols + q_w + kv_w, rwkv_cols + q_w + 2 * kv_w,
             rwkv_cols + q_w + 2 * kv_w + ssm_w, w_in.shape[2]]
    slabbed = (False, True, True, True, False, False)
    splits = tuple(zip(edges[:-1], edges[1:], slabbed))

    x_all = _to_rows(ctx, x)

    cvec = jnp.concatenate([jnp.broadcast_to(c_ctx[None], (n_batch, d_model)), c], axis=0)
    modtab = _modulation(cvec, mod_w, mod_b).reshape(depth, 2, n_batch, 6 * d_model)

    ones_rwkv = _block_diag_ones(min(MXU_DIM, rwkv_w), HEAD_DIM)
    ones_pair = _block_diag_ones(LANES, HEAD_DIM)
    cos, sin_a, sin_b = (jnp.repeat(t, n_batch, axis=0) for t in _rope_tables(n_ctx, n_lat))
    rank = rwkv_w_up.shape[2]
    zpad = jnp.zeros((depth, 2, rank, rwkv_w), F32)

    for l in range(depth):
        mt = modtab[l]
        prm = dict(mu=rwkv_mu[l][None], w0=rwkv_w0[l][:, None], a0=rwkv_a0[l][:, None],
                   w_up=jnp.concatenate([rwkv_w_up[l], zpad[l]], axis=1).astype(BF16),
                   a_up=jnp.concatenate([zpad[l], rwkv_a_up[l]], axis=1).astype(BF16),
                   g_up=rwkv_g_up[l].astype(BF16), k_k=rwkv_k_k[l][None], k_a=rwkv_k_a[l][None],
                   r_k=rwkv_r_k[l][None], ones_bd=ones_rwkv)
        r, v, kk, g, bonus, lw, kd, bv, pq, pk, pv, pc, pg = _in_projection(
            x_all, mt, w_in[l].astype(BF16), splits, prm, n_ctx_rows)
        ysa = _rwkv_scan(r, v, kk, lw, kd, bv, n_batch, n_ctx)
        rwkv = (ysa, g, bonus, rwkv_gn_w[l][None], rwkv_gn_b[l][None], ones_rwkv)

        tile2 = lambda a: jnp.tile(a, 2)[None]
        qh, kh, vh = _attn_prepare(pq, pk, pv, cos, sin_a, sin_b, tile2(attn_q_gain[l]), tile2(attn_k_gain[l]),
                                   ones_pair, n_batch)
        yb = _flash_attention(qh, kh, vh, n_ctx)

        abar_re, abar_im, bf_re, bf_im = _s5_params(ssm_a_re[l], ssm_a_im[l], ssm_log_dt[l], ssm_b_re[l], ssm_b_im[l])
        n_g = ssm_a_re.shape[2]
        n_split = max(1, ssm_w // MXU_DIM)
        gs = n_g // n_split
        diag = lambda blocks: jnp.stack([_block_diag(blocks[h * gs:(h + 1) * gs]) for h in range(n_split)])
        bmat = lambda bf: jnp.stack([diag(bf[d].reshape(SSM_GROUP, n_g, SSM_STATE).transpose(1, 0, 2))
                                     for d in range(2)]).astype(BF16)
        cmat = lambda cc: diag(cc.transpose(0, 2, 1)).astype(BF16)
        ysc = _s5_scan(pc, abar_re, abar_im, bmat(bf_re), bmat(bf_im), cmat(ssm_c_re[l]), cmat(ssm_c_im[l]),
                       n_batch, n_ctx)
        s5 = (ysc, pc, ssm_d[l][None], ssm_glu_w[l].astype(BF16), ssm_glu_b[l][None])

        x_mid = _merge(rwkv, yb, s5, pg, x_all, mt, proj_a[l].astype(BF16), proj_b[l].astype(BF16),
                       proj_c[l].astype(BF16), w_out[l].astype(BF16), ln1_g[l][None], ln1_b[l][None],
                       alpha, n_ctx_rows)
        x_all = _mlp(x_mid, mt, mlp_w1[l].astype(BF16), mlp_w2[l].astype(BF16), ln2_g[l][None], ln2_b[l][None],
                     alpha, n_ctx_rows, final_batch=n_batch if l == depth - 1 else None)

    return x_all
```
